```python
import math
import jax, jax.numpy as jnp
from jax import lax
import numpy as np

D_MODEL = 1024
BATCH = 4
SEQ = 4096
DEPTH = 2

RMS_EPS = 1e-6
NEG_BIG = -1e30
TINY = 1e-20
HG_HEADS = 4
HG_DK = 128
HG_DV = 128
HG_CHUNK = 64
MOBA_HEADS = 4
MOBA_DH = 128
MOBA_BLOCK = 256
MOBA_TOPK = 3
MOBA_Q_CHUNK = 32
DSA_HEADS = 4
DSA_DH = 128
DSA_KV_RANK = 256
IDX_HEADS = 8
IDX_DIM = 64
DSA_TOPK_MAX = 256
DSA_Q_BLOCK = 128
REL_BUCKETS = 32
REL_MAX_DIST = 2048
N_BRANCH = 3
D_FF = ((8 * D_MODEL // 3 + 255) // 256) * 256
HG_W = HG_HEADS * HG_DV
MOBA_W = MOBA_HEADS * MOBA_DH
DSA_W = DSA_HEADS * DSA_DH
IN_SIZES = (HG_HEADS * HG_DK, HG_HEADS * HG_DK, HG_W, HG_W,
            MOBA_W, MOBA_W, MOBA_W,
            DSA_W, DSA_KV_RANK,
            IDX_HEADS * IDX_DIM, IDX_DIM, IDX_HEADS,
            N_BRANCH * D_MODEL)
IN_WIDTH = sum(IN_SIZES)

kernel_name = 'hybrid_hgrn2_moba_dsa_gated_block'


def rms_norm(x, w):
    xf = x.astype(jnp.float32)
    y = xf * lax.rsqrt(jnp.mean(xf * xf, axis=-1, keepdims=True) + RMS_EPS)
    return (y * w.astype(jnp.float32)).astype(x.dtype)


def layer_norm(x, w, b):
    xf = x.astype(jnp.float32)
    mu = jnp.mean(xf, axis=-1, keepdims=True)
    xc = xf - mu
    y = xc * lax.rsqrt(jnp.mean(xc * xc, axis=-1, keepdims=True) + RMS_EPS)
    return (y * w.astype(jnp.float32) + b.astype(jnp.float32)).astype(x.dtype)


def rel_bucket(dist):
    n = jnp.maximum(dist, 0)
    exact = REL_BUCKETS // 2
    nf = jnp.maximum(n, exact).astype(jnp.float32)
    large = exact + (jnp.log(nf / exact) / math.log(REL_MAX_DIST / exact) * (REL_BUCKETS - exact)).astype(jnp.int32)
    large = jnp.minimum(large, REL_BUCKETS - 1)
    return jnp.where(n < exact, n, large)


def chunk_gated_recurrence(q, k, log_f, v):
    B, S, H, DK = q.shape
    DV = v.shape[-1]
    nc = S // HG_CHUNK

    def chunks(a):
        return a.reshape(B, nc, HG_CHUNK, H, a.shape[-1]).transpose(1, 0, 3, 2, 4)

    causal = jnp.tril(jnp.ones((HG_CHUNK, HG_CHUNK), dtype=bool))

    def step(state, inp):
        qc, kc, gc, vc = inp
        b = jnp.cumsum(gc, axis=2)
        o_inter = jnp.einsum('bhcd,bhde->bhce', qc * jnp.exp(b), state)
        diff = b[:, :, :, None, :] - b[:, :, None, :, :]
        decay = jnp.exp(jnp.where(causal[:, :, None], diff, NEG_BIG))
        scores = jnp.einsum('bhid,bhijd,bhjd->bhij', qc, decay, kc)
        o = o_inter + jnp.einsum('bhij,bhje->bhie', scores, vc)
        b_last = b[:, :, -1:, :]
        state = (jnp.exp(b_last[:, :, 0, :])[..., None] * state
                 + jnp.einsum('bhjd,bhje->bhde', kc * jnp.exp(b_last - b), vc))
        return state, o

    s0 = jnp.zeros((B, H, DK, DV), jnp.float32)
    _, o = lax.scan(step, s0, (chunks(q), chunks(k), chunks(log_f), chunks(v)))
    return o.transpose(1, 0, 3, 2, 4).reshape(B, S, H, DV)


def hgrn2_mixer(q, f_pre, i, g, lb, norm_w):
    B, S, _ = q.shape
    f32 = jnp.float32
    qf = q.reshape(B, S, HG_HEADS, HG_DK).astype(f32)
    fp = f_pre.reshape(B, S, HG_HEADS, HG_DK).astype(f32)
    v = i.reshape(B, S, HG_HEADS, HG_DV).astype(f32)
    lbh = lb.reshape(HG_HEADS, HG_DK).astype(f32)
    f = lbh + (1.0 - lbh) * jax.nn.sigmoid(fp)
    log_f = jnp.log(jnp.maximum(f, TINY))
    k = (1.0 - lbh) * jax.nn.sigmoid(-fp)
    o = chunk_gated_recurrence(qf, k, log_f, v)
    gate = jax.nn.silu(g.reshape(B, S, HG_HEADS, HG_DV).astype(f32))
    o = rms_norm(o, norm_w) * gate
    return o.reshape(B, S, HG_W).astype(q.dtype)


def moba_attention(q, k, v, rel_table):
    B, S, H, DH = q.shape
    f32 = jnp.float32
    nb = -(-S // MOBA_BLOCK)
    pad = nb * MOBA_BLOCK - S
    kp = jnp.pad(k, ((0, 0), (0, pad), (0, 0), (0, 0)))
    vp = jnp.pad(v, ((0, 0), (0, pad), (0, 0), (0, 0)))
    kb = kp.reshape(B, nb, MOBA_BLOCK, H, DH).transpose(0, 3, 1, 2, 4)
    vb = vp.reshape(B, nb, MOBA_BLOCK, H, DH).transpose(0, 3, 1, 2, 4)
    k_mean = jnp.mean(kb.astype(f32), axis=3).astype(q.dtype)
    n_sel = min(MOBA_TOPK, nb - 1)
    scale = DH ** -0.5
    nq = S // MOBA_Q_CHUNK
    q_chunks = q.reshape(B, nq, MOBA_Q_CHUNK, H, DH).transpose(1, 0, 2, 3, 4)
    starts = jnp.arange(nq, dtype=jnp.int32) * MOBA_Q_CHUNK
    b_ix = jnp.arange(B)[:, None, None, None]
    h_ix = jnp.arange(H)[None, None, :, None]
    h_ix5 = jnp.arange(H)[None, None, :, None, None]
    offs = jnp.arange(MOBA_BLOCK, dtype=jnp.int32)
    blk_ids = jnp.arange(nb, dtype=jnp.int32)

    def one_chunk(args):
        qc, start = args
        t = start + jnp.arange(MOBA_Q_CHUNK, dtype=jnp.int32)
        blk = start // MOBA_BLOCK
        k_own = lax.dynamic_slice_in_dim(kp, blk * MOBA_BLOCK, MOBA_BLOCK, axis=1)
        v_own = lax.dynamic_slice_in_dim(vp, blk * MOBA_BLOCK, MOBA_BLOCK, axis=1)
        dist_own = t[:, None] - (blk * MOBA_BLOCK + offs)[None, :]
        l_own = jnp.einsum('bqhd,bkhd->bqhk', qc, k_own).astype(f32) * scale
        l_own = l_own + rel_table[rel_bucket(dist_own)].astype(f32).transpose(0, 2, 1)
        l_own = jnp.where((dist_own >= 0)[:, None, :], l_own, NEG_BIG)
        if n_sel == 0:
            p = jax.nn.softmax(l_own, axis=-1).astype(v.dtype)
            return jnp.einsum('bqhk,bkhd->bqhd', p, v_own)
        gate = jnp.einsum('bqhd,bhnd->bqhn', qc, k_mean).astype(f32)
        gate = jnp.where(blk_ids < blk, gate, NEG_BIG)
        _, sel = lax.top_k(gate, n_sel)
        k_sel = kb[b_ix, h_ix, sel]
        v_sel = vb[b_ix, h_ix, sel]
        dist_sel = t[None, :, None, None, None] - (sel[..., None] * MOBA_BLOCK + offs)
        l_sel = jnp.einsum('bqhd,bqhknd->bqhkn', qc, k_sel).astype(f32) * scale
        l_sel = l_sel + rel_table[rel_bucket(dist_sel), h_ix5].astype(f32)
        l_sel = jnp.where((sel < blk)[..., None], l_sel, NEG_BIG)
        logits = jnp.concatenate([l_sel.reshape(B, MOBA_Q_CHUNK, H, n_sel * MOBA_BLOCK), l_own], axis=-1)
        p = jax.nn.softmax(logits, axis=-1).astype(v.dtype)
        p_sel = p[..., :n_sel * MOBA_BLOCK].reshape(B, MOBA_Q_CHUNK, H, n_sel, MOBA_BLOCK)
        p_own = p[..., n_sel * MOBA_BLOCK:]
        return (jnp.einsum('bqhkn,bqhknd->bqhd', p_sel, v_sel)
                + jnp.einsum('bqhk,bkhd->bqhd', p_own, v_own))

    out = lax.map(one_chunk, (q_chunks, starts))
    return out.transpose(1, 0, 2, 3, 4).reshape(B, S, H * DH)


def dsa_attention(q, c_kv, w_uk, w_uv, iq, ik, iw, rel_table, n_keep):
    B, S, H, DH = q.shape
    f32 = jnp.float32
    q_abs = jnp.einsum('bshd,rhd->bshr', q, w_uk)
    nblk = S // DSA_Q_BLOCK

    def blocks(a):
        return a.reshape((B, nblk, DSA_Q_BLOCK) + a.shape[2:]).swapaxes(0, 1)

    starts = jnp.arange(nblk, dtype=jnp.int32) * DSA_Q_BLOCK
    key_pos = jnp.arange(S, dtype=jnp.int32)
    scale = DH ** -0.5

    def one_block(args):
        qa, iqb, iwb, start = args
        t = start + jnp.arange(DSA_Q_BLOCK, dtype=jnp.int32)
        s_idx = jnp.einsum('bqhe,bse->bqhs', iqb, ik).astype(f32) * (IDX_DIM ** -0.5)
        score = jnp.einsum('bqh,bqhs->bqs', iwb.astype(f32), jax.nn.relu(s_idx))
        score = jnp.where(key_pos[None, :] <= t[:, None], score, NEG_BIG)
        _, idx = lax.top_k(score, n_keep)
        c_sel = jax.vmap(lambda cb, ib: cb[ib])(c_kv, idx)
        dist = t[None, :, None] - idx
        logits = jnp.einsum('bqhr,bqkr->bqhk', qa, c_sel).astype(f32) * scale
        logits = logits + rel_table[rel_bucket(dist)].astype(f32).transpose(0, 1, 3, 2)
        logits = jnp.where((dist >= 0)[:, :, None, :], logits, NEG_BIG)
        p = jax.nn.softmax(logits, axis=-1).astype(c_kv.dtype)
        return jnp.einsum('bqhk,bqkr->bqhr', p, c_sel)

    o_lat = lax.map(one_block, (blocks(q_abs), blocks(iq), blocks(iw), starts))
    o_lat = o_lat.swapaxes(0, 1).reshape(B, S, H, DSA_KV_RANK)
    return jnp.einsum('bshr,rhd->bshd', o_lat, w_uv).reshape(B, S, H * DH)


def setup_inputs(seed: int = 0) -> dict:
    key = jax.random.key(seed)
    ks = jax.random.split(key, 24)
    f32 = jnp.float32

    def nrm(k, shape, scale):
        return jax.random.normal(k, shape, f32) * scale

    def gain(k, shape):
        return 1.0 + 0.05 * jax.random.normal(k, shape, f32)

    return {
        'x': nrm(ks[0], (BATCH, SEQ, D_MODEL), 1.0),
        'w_in': nrm(ks[1], (DEPTH, D_MODEL, IN_WIDTH), D_MODEL ** -0.5),
        'pre_mix_norm': gain(ks[2], (DEPTH, D_MODEL)),
        'post_mix_norm': gain(ks[3], (DEPTH, D_MODEL)),
        'pre_ffn_norm': gain(ks[4], (DEPTH, D_MODEL)),
        'post_ffn_norm': gain(ks[5], (DEPTH, D_MODEL)),
        'hg_lb_logits': nrm(ks[6], (DEPTH, HG_HEADS * HG_DK), 0.5),
        'hg_norm_w': gain(ks[7], (DEPTH, HG_DV)),
        'dsa_kv_norm': gain(ks[8], (DEPTH, DSA_KV_RANK)),
        'dsa_w_uk': nrm(ks[9], (DEPTH, DSA_KV_RANK, DSA_HEADS, DSA_DH), DSA_KV_RANK ** -0.5),
        'dsa_w_uv': nrm(ks[10], (DEPTH, DSA_KV_RANK, DSA_HEADS, DSA_DH), DSA_KV_RANK ** -0.5),
        'idx_k_norm_w': gain(ks[11], (DEPTH, IDX_DIM)),
        'idx_k_norm_b': nrm(ks[12], (DEPTH, IDX_DIM), 0.02),
        'rel_bias': nrm(ks[13], (REL_BUCKETS, MOBA_HEADS + DSA_HEADS), 0.5),
        'w_branch_a': nrm(ks[14], (DEPTH, HG_W, D_MODEL), HG_W ** -0.5),
        'w_branch_b': nrm(ks[15], (DEPTH, MOBA_W, D_MODEL), MOBA_W ** -0.5),
        'w_branch_c': nrm(ks[16], (DEPTH, DSA_W, D_MODEL), DSA_W ** -0.5),
        'w_out': nrm(ks[17], (DEPTH, D_MODEL, D_MODEL), D_MODEL ** -0.5),
        'w_ffn_in': nrm(ks[18], (DEPTH, D_MODEL, 2 * D_FF), D_MODEL ** -0.5),
        'w_ffn_out': nrm(ks[19], (DEPTH, D_FF, D_MODEL), D_FF ** -0.5),
    }


def reference(x, w_in, pre_mix_norm, post_mix_norm, pre_ffn_norm, post_ffn_norm,
              hg_lb_logits, hg_norm_w, dsa_kv_norm, dsa_w_uk, dsa_w_uv,
              idx_k_norm_w, idx_k_norm_b, rel_bias, w_branch_a, w_branch_b, w_branch_c,
              w_out, w_ffn_in, w_ffn_out):
    f32 = jnp.float32
    B, S, _ = x.shape
    lbp = jax.nn.softmax(hg_lb_logits.astype(f32), axis=0)
    lower_bounds = jnp.cumsum(lbp, axis=0) - lbp[0]
    bias_b = rel_bias[:, :MOBA_HEADS]
    bias_c = rel_bias[:, MOBA_HEADS:]
    n_keep = min(DSA_TOPK_MAX, S // 4)
    splits = [int(s) for s in np.cumsum(IN_SIZES)[:-1]]
    for l in range(DEPTH):
        h = rms_norm(x, pre_mix_norm[l])
        proj = h @ w_in[l]
        (hq, hf, hi, hg, mq, mk, mv, dq, dkv, iq, ik, iw, gates) = jnp.split(proj, splits, axis=-1)
        y_a = hgrn2_mixer(hq, hf, hi, hg, lower_bounds[l], hg_norm_w[l])
        y_b = moba_attention(mq.reshape(B, S, MOBA_HEADS, MOBA_DH),
                             mk.reshape(B, S, MOBA_HEADS, MOBA_DH),
                             mv.reshape(B, S, MOBA_HEADS, MOBA_DH), bias_b)
        c_kv = rms_norm(dkv, dsa_kv_norm[l])
        ik_n = layer_norm(ik, idx_k_norm_w[l], idx_k_norm_b[l])
        y_c = dsa_attention(dq.reshape(B, S, DSA_HEADS, DSA_DH), c_kv, dsa_w_uk[l], dsa_w_uv[l],
                            iq.reshape(B, S, IDX_HEADS, IDX_DIM), ik_n, iw * (IDX_HEADS ** -0.5),
                            bias_c, n_keep)
        g = jax.nn.sigmoid(gates.reshape(B, S, N_BRANCH, D_MODEL))
        merged = (g[:, :, 0] * (y_a @ w_branch_a[l])
                  + g[:, :, 1] * (y_b @ w_branch_b[l])
                  + g[:, :, 2] * (y_c @ w_branch_c[l]))
        x = x + rms_norm(merged @ w_out[l], post_mix_norm[l])
        h2 = rms_norm(x, pre_ffn_norm[l])
        gate, up = jnp.split(h2 @ w_ffn_in[l], 2, axis=-1)
        x = x + rms_norm((jax.nn.silu(gate) * up) @ w_ffn_out[l], post_ffn_norm[l])
    return x
```

```python
import functools
import math

import numpy as np
import jax
import jax.numpy as jnp
from jax import lax
from jax.experimental import pallas as pl
from jax.experimental.pallas import tpu as pltpu

F32 = jnp.float32
BF16 = jnp.bfloat16

D_MODEL = 1024
RMS_EPS = 1e-6
NEG_BIG = -1e30
TINY = 1e-20
HG_HEADS = 4
HG_D = 128
HG_CHUNK = 64
MOBA_HEADS = 4
MOBA_DH = 128
MOBA_BLOCK = 256
MOBA_TOPK = 3
DSA_HEADS = 4
DSA_DH = 128
DSA_KV_RANK = 256
IDX_HEADS = 8
IDX_DIM = 64
DSA_TOPK_MAX = 256
DSA_QB = 128
DSA_KC = 512
REL_BUCKETS = 32
REL_MAX_DIST = 2048
N_BRANCH = 3
D_FF = 2816
HEAD_W = 512

COL_HQ, COL_HF, COL_HI, COL_HG = 0, 512, 1024, 1536
COL_MQ, COL_MK, COL_MV = 2048, 2560, 3072
COL_DQ, COL_IQ, COL_DKV, COL_IKW, COL_GATES = 3584, 4096, 4608, 4864, 5120
PROJ_W = 8192

VMEM_LIMIT = 56 * 1024 * 1024
LANES = 128

_REL_EXACT = REL_BUCKETS // 2
REL_FAR_DIST = int(math.ceil(_REL_EXACT * (REL_MAX_DIST / _REL_EXACT) ** ((REL_BUCKETS - _REL_EXACT - 1) / (REL_BUCKETS - _REL_EXACT)))) + 1

_NEG_BITS = int(np.array(NEG_BIG, np.float32).view(np.int32))
KEY_NEG = int(np.int32(_NEG_BITS ^ 0x7FFFFFFF))


def _cparams(sem):
    return pltpu.CompilerParams(dimension_semantics=sem, vmem_limit_bytes=VMEM_LIMIT)


def _sigmoid(x):
    return 1.0 / (1.0 + jnp.exp(-x))


def _dot(a, b):
    return jnp.dot(a, b, preferred_element_type=F32)


def _dot_nt(a, b):
    return lax.dot_general(a, b, (((1,), (1,)), ((), ())), preferred_element_type=F32)


def _proj_kernel(x_ref, g_ref, w_ref, o_ref, xn_ref):
    @pl.when(pl.program_id(1) == 0)
    def _():
        x = x_ref[...]
        ms = jnp.mean(x * x, axis=-1, keepdims=True)
        xn_ref[...] = (x * lax.rsqrt(ms + RMS_EPS) * g_ref[...]).astype(BF16)

    o_ref[...] = _dot(xn_ref[...], w_ref[...])


def _proj(x2, gain, w_packed, tm, tn):
    n = x2.shape[0]
    return pl.pallas_call(
        _proj_kernel,
        out_shape=jax.ShapeDtypeStruct((n, PROJ_W), F32),
        grid=(n // tm, PROJ_W // tn),
        in_specs=[
            pl.BlockSpec((tm, D_MODEL), lambda i, j: (i, 0)),
            pl.BlockSpec((1, D_MODEL), lambda i, j: (0, 0)),
            pl.BlockSpec((D_MODEL, tn), lambda i, j: (0, j)),
        ],
        out_specs=pl.BlockSpec((tm, tn), lambda i, j: (i, j)),
        scratch_shapes=[pltpu.VMEM((tm, D_MODEL), BF16)],
        compiler_params=_cparams(("parallel", "arbitrary")),
        name="proj",
    )(x2, gain, w_packed)


HG_LEVELS = (32, 16, 8, 4, 2, 1)


def _hgrn_constants():
    c = HG_CHUNK
    t = np.arange(c)
    i = t[:, None]
    tt = t[None, :]
    rows = [tt <= i]
    mq, mk, masks = [], [], []
    for m in HG_LEVELS:
        mid = ((t // (2 * m)) * (2 * m) + m)[:, None]
        mq.append((i >= mid) & (tt >= mid) & (tt <= i))
        mk.append((i < mid) & (tt > i) & (tt <= mid - 1))
        same = (t[:, None] // (2 * m)) == (t[None, :] // (2 * m))
        masks.append(same & (i >= mid) & (tt < mid))
    masks.append(np.eye(c, dtype=bool))
    m_all = np.concatenate(rows + mq + mk, axis=0).astype(np.float32)
    return m_all, np.stack(masks).astype(np.float32)


def _hgrn_kernel(q_ref, f_ref, i_ref, g_ref, lb_ref, nw_ref, mall_ref, mask_ref, o_ref, st_ref, e_ref, *, n_chunks):
    c = HG_CHUNK
    nl = len(HG_LEVELS)

    @pl.when(pl.program_id(1) == 0)
    def _():
        st_ref[...] = jnp.zeros_like(st_ref)

    lb = lb_ref[...]
    nw = nw_ref[...]

    def chunk(ci, carry):
        r0 = pl.multiple_of(ci * c, c)
        fp = f_ref[pl.ds(r0, c), :]
        f = lb + (1.0 - lb) * _sigmoid(fp)
        logf = jnp.log(jnp.maximum(f, TINY))
        kk = (1.0 - lb) * _sigmoid(-fp)
        g_hi = logf.astype(BF16)
        g_lo = (logf - g_hi.astype(F32)).astype(BF16)
        e_ref[...] = _dot(mall_ref[...], jnp.concatenate([g_hi, g_lo], axis=1))
        q = q_ref[pl.ds(r0, c), :]
        v = i_ref[pl.ds(r0, c), :]
        og = g_ref[pl.ds(r0, c), :]
        for h in range(HG_HEADS):
            lo, hi = h * HG_D, (h + 1) * HG_D

            def expo(row):
                return e_ref[row * c:(row + 1) * c, lo:hi] + e_ref[row * c:(row + 1) * c, HEAD_W + lo:HEAD_W + hi]

            qh, kh, vh = q[:, lo:hi], kk[:, lo:hi], v[:, lo:hi]
            b = expo(0)
            s = jnp.zeros((c, c), F32)
            for l in range(nl + 1):
                if l < nl:
                    ql = qh * jnp.exp(expo(1 + l))
                    kl = kh * jnp.exp(expo(1 + nl + l))
                else:
                    ql, kl = qh, kh
                sl = _dot_nt(ql.astype(BF16), kl.astype(BF16))
                s = s + jnp.where(mask_ref[l] > 0.0, sl, 0.0)
            st = st_ref[h]
            o = _dot_nt((qh * jnp.exp(b)).astype(BF16), st.astype(BF16)) + _dot(s.astype(BF16), vh.astype(BF16))
            b_last = b[c - 1:c, :]
            kdec = kh * jnp.exp(b_last - b)
            st_ref[h] = st * jnp.exp(b_last) + _dot(vh.T.astype(BF16), kdec.astype(BF16))
            y = o * lax.rsqrt(jnp.mean(o * o, axis=-1, keepdims=True) + RMS_EPS) * nw
            gh = og[:, lo:hi]
            o_ref[pl.ds(r0, c), lo:hi] = (y * (gh * _sigmoid(gh))).astype(o_ref.dtype)
        return carry

    lax.fori_loop(0, n_chunks, chunk, 0)


def _hgrn(proj, lb, norm_w, batch, seq, rows):
    m_all, masks = _hgrn_constants()
    n_rb = seq // rows
    n_chunks = rows // HG_CHUNK

    def col(cidx):
        return pl.BlockSpec((rows, HEAD_W), lambda b, r, cidx=cidx: (b * n_rb + r, cidx))

    return pl.pallas_call(
        functools.partial(_hgrn_kernel, n_chunks=n_chunks),
        out_shape=jax.ShapeDtypeStruct((batch * seq, HEAD_W), BF16),
        grid=(batch, n_rb),
        in_specs=[
            col(COL_HQ // HEAD_W), col(COL_HF // HEAD_W), col(COL_HI // HEAD_W), col(COL_HG // HEAD_W),
            pl.BlockSpec((1, HEAD_W), lambda b, r: (0, 0)),
            pl.BlockSpec((1, HG_D), lambda b, r: (0, 0)),
            pl.BlockSpec(m_all.shape, lambda b, r: (0, 0)),
            pl.BlockSpec(masks.shape, lambda b, r: (0, 0, 0)),
        ],
        out_specs=pl.BlockSpec((rows, HEAD_W), lambda b, r: (b * n_rb + r, 0)),
        scratch_shapes=[
            pltpu.VMEM((HG_HEADS, HG_D, HG_D), F32),
            pltpu.VMEM((m_all.shape[0], 2 * HEAD_W), F32),
        ],
        compiler_params=_cparams(("parallel", "arbitrary")),
        name="hgrn",
    )(proj, proj, proj, proj, lb, norm_w, jnp.asarray(m_all, BF16), jnp.asarray(masks, F32))


def _rel_bucket(dist):
    n = jnp.maximum(dist, 0)
    exact = REL_BUCKETS // 2
    nf = jnp.maximum(n, exact).astype(F32)
    large = exact + (jnp.log(nf / exact) / math.log(REL_MAX_DIST / exact) * (REL_BUCKETS - exact)).astype(jnp.int32)
    large = jnp.minimum(large, REL_BUCKETS - 1)
    return jnp.where(n < exact, n, large)


def _n_bias_tiles(tile):
    d = 0
    while d * tile - (tile - 1) < REL_FAR_DIST:
        d += 1
    return d + 1


def _bias_tiles(table, seq, tile):
    n = _n_bias_tiles(tile)
    by_dist = table[_rel_bucket(jnp.arange(max(seq, n * tile + tile), dtype=jnp.int32))]
    i = np.arange(tile)
    idx = np.arange(n)[:, None, None] * tile + i[None, :, None] - i[None, None, :]
    idx = np.clip(idx, 0, by_dist.shape[0] - 1)
    return jnp.transpose(by_dist[idx], (3, 0, 1, 2)).astype(F32)


def _moba_kernel(q_ref, k_ref, v_ref, bias_ref, o_ref, kb_ref, vb_ref, km_ref, sel_ref, m_ref, l_ref, acc_ref,
                 *, n_blocks, n_sel, n_bias):
    blk = MOBA_BLOCK
    qb = pl.program_id(2)
    scale = MOBA_DH ** -0.5

    @pl.when(qb == 0)
    def _():
        kb_ref[...] = k_ref[...].astype(BF16)
        vb_ref[...] = v_ref[...].astype(BF16)
        for n in range(n_blocks):
            km_ref[n:n + 1, :] = jnp.mean(k_ref[n * blk:(n + 1) * blk, :], axis=0, keepdims=True)

    q = q_ref[...].astype(BF16)

    gate = _dot_nt(q, km_ref[...].astype(BF16))
    n_iota = lax.broadcasted_iota(jnp.int32, gate.shape, 1)
    gate = jnp.where(n_iota < qb, gate, NEG_BIG)
    rank = jnp.zeros(gate.shape, F32)
    for n in range(n_blocks):
        col = gate[:, n:n + 1]
        beats = (col > gate) | ((col == gate) & (n < n_iota))
        rank = rank + jnp.where(beats, 1.0, 0.0)
    sel = jnp.where((rank < n_sel) & (n_iota < qb), 0.0, NEG_BIG)
    for n in range(n_blocks):
        sel_ref[n] = jnp.broadcast_to(sel[:, n:n + 1], (blk, LANES))

    m_ref[...] = jnp.full(m_ref.shape, -jnp.inf, F32)
    l_ref[...] = jnp.zeros(l_ref.shape, F32)
    acc_ref[...] = jnp.zeros(acc_ref.shape, F32)

    def update(s, vblk):
        m_old = m_ref[...]
        m_new = jnp.maximum(m_old, jnp.max(s, axis=-1, keepdims=True))
        alpha = jnp.exp(m_old - m_new)
        p = jnp.exp(s - m_new)
        l_ref[...] = alpha * l_ref[...] + jnp.sum(p, axis=-1, keepdims=True)
        acc_ref[...] = alpha * acc_ref[...] + _dot(p.astype(BF16), vblk)
        m_ref[...] = m_new

    def past(n, carry):
        r0 = pl.multiple_of(n * blk, blk)
        s = _dot_nt(q, kb_ref[pl.ds(r0, blk), :]) * scale
        s = s + bias_ref[0, jnp.minimum(qb - n, n_bias - 1)]
        negb = sel_ref[n]
        s = s + jnp.concatenate([negb] * (blk // LANES), axis=1)
        update(s, vb_ref[pl.ds(r0, blk), :])
        return carry

    lax.fori_loop(0, qb, past, 0)

    r0 = pl.multiple_of(qb * blk, blk)
    s = _dot_nt(q, kb_ref[pl.ds(r0, blk), :]) * scale + bias_ref[0, 0]
    ti = lax.broadcasted_iota(jnp.int32, (blk, blk), 0)
    si = lax.broadcasted_iota(jnp.int32, (blk, blk), 1)
    s = jnp.where(si <= ti, s, NEG_BIG)
    update(s, vb_ref[pl.ds(r0, blk), :])

    o_ref[...] = (acc_ref[...] / l_ref[...]).astype(o_ref.dtype)


def _moba(proj, bias_tiles, batch, seq):
    blk = MOBA_BLOCK
    nb = seq // blk
    n_sel = min(MOBA_TOPK, nb - 1)
    n_bias = bias_tiles.shape[1]
    cq, ck, cv = COL_MQ // MOBA_DH, COL_MK // MOBA_DH, COL_MV // MOBA_DH
    return pl.pallas_call(
        functools.partial(_moba_kernel, n_blocks=nb, n_sel=n_sel, n_bias=n_bias),
        out_shape=jax.ShapeDtypeStruct((batch * seq, HEAD_W), BF16),
        grid=(batch, MOBA_HEADS, nb),
        in_specs=[
            pl.BlockSpec((blk, MOBA_DH), lambda b, h, i: (b * nb + i, cq + h)),
            pl.BlockSpec((seq, MOBA_DH), lambda b, h, i: (b, ck + h)),
            pl.BlockSpec((seq, MOBA_DH), lambda b, h, i: (b, cv + h)),
            pl.BlockSpec((1, n_bias, blk, blk), lambda b, h, i: (h, 0, 0, 0)),
        ],
        out_specs=pl.BlockSpec((blk, MOBA_DH), lambda b, h, i: (b * nb + i, h)),
        scratch_shapes=[
            pltpu.VMEM((seq, MOBA_DH), BF16),
            pltpu.VMEM((seq, MOBA_DH), BF16),
            pltpu.VMEM((nb, MOBA_DH), F32),
            pltpu.VMEM((nb, blk, LANES), F32),
            pltpu.VMEM((blk, 1), F32),
            pltpu.VMEM((blk, 1), F32),
            pltpu.VMEM((blk, MOBA_DH), F32),
        ],
        compiler_params=_cparams(("parallel", "parallel", "arbitrary")),
        name="moba",
    )(proj, proj, proj, bias_tiles)


def _sort_key(x):
    bits = pltpu.bitcast(x, jnp.int32)
    return bits ^ ((bits >> 31) & 0x7FFFFFFF)


def _dsa_kernel(dq_ref, iq_ref, iwq_ref, dkv_ref, ikw_ref, kvg_ref, lnw_ref, lnb_ref, wuk_ref, wuv_ref, bias_ref,
                o_ref, ckv_ref, ikn_ref, key_ref, mask_ref, m_ref, l_ref, acc_ref, *, seq, n_keep, n_bias):
    tq, kc = DSA_QB, DSA_KC
    qb = pl.program_id(1)
    t0 = qb * tq
    n_kc = (t0 + tq - 1) // kc + 1
    n_rest = (seq - n_kc * kc).astype(F32)

    @pl.when(qb == 0)
    def _():
        c = dkv_ref[...]
        ckv_ref[...] = (c * lax.rsqrt(jnp.mean(c * c, axis=-1, keepdims=True) + RMS_EPS) * kvg_ref[...]).astype(BF16)
        ik = ikw_ref[:, 0:IDX_DIM]
        mu = jnp.mean(ik, axis=-1, keepdims=True)
        xc = ik - mu
        ikn = xc * lax.rsqrt(jnp.mean(xc * xc, axis=-1, keepdims=True) + RMS_EPS) * lnw_ref[...] + lnb_ref[...]
        ikn_ref[...] = ikn.astype(BF16)

    t_col = t0 + lax.broadcasted_iota(jnp.int32, (tq, 1), 0)
    iq = iq_ref[...].astype(BF16)
    iw = iwq_ref[:, IDX_DIM:IDX_DIM + IDX_HEADS] * (IDX_HEADS ** -0.5)

    def score_chunk(c, carry):
        c0 = pl.multiple_of(c * kc, kc)
        ikc = ikn_ref[pl.ds(c0, kc), :]
        sc = jnp.zeros((tq, kc), F32)
        for h in range(IDX_HEADS):
            sh = _dot_nt(iq[:, h * IDX_DIM:(h + 1) * IDX_DIM], ikc) * (IDX_DIM ** -0.5)
            sc = sc + iw[:, h:h + 1] * jnp.maximum(sh, 0.0)
        s_pos = c0 + lax.broadcasted_iota(jnp.int32, (tq, kc), 1)
        sc = jnp.where(s_pos <= t_col, sc, NEG_BIG)
        sc = jnp.where(sc == 0.0, 0.0, sc)
        key_ref[:, pl.ds(c0, kc)] = _sort_key(sc)
        return carry

    lax.fori_loop(0, n_kc, score_chunk, 0)

    def count(pred_fn):
        def body(c, tot):
            c0 = pl.multiple_of(c * kc, kc)
            hit = pred_fn(key_ref[:, pl.ds(c0, kc)], c0)
            return tot + jnp.sum(jnp.where(hit, 1.0, 0.0), axis=-1, keepdims=True)
        return lax.fori_loop(0, n_kc, body, jnp.zeros((tq, 1), F32))

    def bit_step(k, base):
        cand = base + jnp.left_shift(jnp.int32(1), 31 - k)
        cnt = count(lambda keys, c0: keys >= cand) + jnp.where(cand <= KEY_NEG, n_rest, 0.0)
        return jnp.where(cnt >= n_keep, cand, base)

    tau = lax.fori_loop(0, 32, bit_step, jnp.full((tq, 1), jnp.iinfo(jnp.int32).min, jnp.int32))

    c_gt = count(lambda keys, c0: keys > tau) + jnp.where(tau < KEY_NEG, n_rest, 0.0)
    need = n_keep - c_gt

    def causal_of(c0):
        return (c0 + lax.broadcasted_iota(jnp.int32, (tq, kc), 1)) <= t_col

    n_eq = count(lambda keys, c0: (keys == tau) & causal_of(c0))
    tie_break = jnp.max(n_eq - need) > 0.0

    @pl.when(jnp.logical_not(tie_break))
    def _():
        def body(c, carry):
            c0 = pl.multiple_of(c * kc, kc)
            keep = (key_ref[:, pl.ds(c0, kc)] >= tau) & causal_of(c0)
            mask_ref[:, pl.ds(c0, kc)] = jnp.where(keep, 0.0, NEG_BIG)
            return carry
        lax.fori_loop(0, n_kc, body, 0)

    @pl.when(tie_break)
    def _():
        upper = jnp.where(lax.broadcasted_iota(jnp.int32, (LANES, LANES), 0)
                          < lax.broadcasted_iota(jnp.int32, (LANES, LANES), 1), 1.0, 0.0).astype(BF16)

        def body(c, seen):
            c0 = pl.multiple_of(c * kc, kc)
            keys = key_ref[:, pl.ds(c0, kc)]
            eq = keys == tau
            eqf = jnp.where(eq, 1.0, 0.0)
            ranks = []
            for g in range(kc // LANES):
                eg = eqf[:, g * LANES:(g + 1) * LANES]
                ranks.append(seen + _dot(eg.astype(BF16), upper))
                seen = seen + jnp.sum(eg, axis=-1, keepdims=True)
            rank = jnp.concatenate(ranks, axis=1)
            keep = ((keys > tau) | (eq & (rank < need))) & causal_of(c0)
            mask_ref[:, pl.ds(c0, kc)] = jnp.where(keep, 0.0, NEG_BIG)
            return seen
        lax.fori_loop(0, n_kc, body, jnp.zeros((tq, 1), F32))

    scale = DSA_DH ** -0.5
    for h in range(DSA_HEADS):
        qh = dq_ref[:, h * DSA_DH:(h + 1) * DSA_DH].astype(BF16)
        q_abs = _dot(qh, wuk_ref[h]).astype(BF16)
        m_ref[...] = jnp.full(m_ref.shape, -jnp.inf, F32)
        l_ref[...] = jnp.zeros(l_ref.shape, F32)
        acc_ref[...] = jnp.zeros(acc_ref.shape, F32)

        def att_chunk(c, carry, h=h, q_abs=q_abs):
            c0 = pl.multiple_of(c * kc, kc)
            ckv = ckv_ref[pl.ds(c0, kc), :]
            s = _dot_nt(q_abs, ckv) * scale
            tiles = []
            for g in range(kc // LANES):
                d = jnp.clip(qb - (c * (kc // LANES) + g), 0, n_bias - 1)
                tiles.append(bias_ref[h, d])
            s = s + jnp.concatenate(tiles, axis=1) + mask_ref[:, pl.ds(c0, kc)]
            m_old = m_ref[...]
            m_new = jnp.maximum(m_old, jnp.max(s, axis=-1, keepdims=True))
            alpha = jnp.exp(m_old - m_new)
            p = jnp.exp(s - m_new)
            l_ref[...] = alpha * l_ref[...] + jnp.sum(p, axis=-1, keepdims=True)
            acc_ref[...] = alpha * acc_ref[...] + _dot(p.astype(BF16), ckv)
            m_ref[...] = m_new
            return carry

        lax.fori_loop(0, n_kc, att_chunk, 0)
        o_lat = (acc_ref[...] / l_ref[...]).astype(BF16)
        o_ref[:, h * DSA_DH:(h + 1) * DSA_DH] = _dot(o_lat, wuv_ref[h]).astype(o_ref.dtype)


def _dsa(proj, kv_gain, ln_w, ln_b, w_uk_t, w_uv_t, bias_tiles, batch, seq):
    tq = DSA_QB
    nq = seq // tq
    n_keep = min(DSA_TOPK_MAX, seq // 4)
    n_bias = bias_tiles.shape[1]
    return pl.pallas_call(
        functools.partial(_dsa_kernel, seq=seq, n_keep=float(n_keep), n_bias=n_bias),
        out_shape=jax.ShapeDtypeStruct((batch * seq, HEAD_W), BF16),
        grid=(batch, nq),
        in_specs=[
            pl.BlockSpec((tq, HEAD_W), lambda b, i: (b * nq + i, COL_DQ // HEAD_W)),
            pl.BlockSpec((tq, HEAD_W), lambda b, i: (b * nq + i, COL_IQ // HEAD_W)),
            pl.BlockSpec((tq, LANES), lambda b, i: (b * nq + i, COL_IKW // LANES)),
            pl.BlockSpec((seq, DSA_KV_RANK), lambda b, i: (b, COL_DKV // DSA_KV_RANK)),
            pl.BlockSpec((seq, LANES), lambda b, i: (b, COL_IKW // LANES)),
            pl.BlockSpec((1, DSA_KV_RANK), lambda b, i: (0, 0)),
            pl.BlockSpec((1, IDX_DIM), lambda b, i: (0, 0)),
            pl.BlockSpec((1, IDX_DIM), lambda b, i: (0, 0)),
            pl.BlockSpec(w_uk_t.shape, lambda b, i: (0, 0, 0)),
            pl.BlockSpec(w_uv_t.shape, lambda b, i: (0, 0, 0)),
            pl.BlockSpec(bias_tiles.shape, lambda b, i: (0, 0, 0, 0)),
        ],
        out_specs=pl.BlockSpec((tq, HEAD_W), lambda b, i: (b * nq + i, 0)),
        scratch_shapes=[
            pltpu.VMEM((seq, DSA_KV_RANK), BF16),
            pltpu.VMEM((seq, IDX_DIM), BF16),
            pltpu.VMEM((tq, seq), jnp.int32),
            pltpu.VMEM((tq, seq), F32),
            pltpu.VMEM((tq, 1), F32),
            pltpu.VMEM((tq, 1), F32),
            pltpu.VMEM((tq, DSA_KV_RANK), F32),
        ],
        compiler_params=_cparams(("parallel", "arbitrary")),
        name="dsa",
    )(proj, proj, proj, proj, proj, kv_gain, ln_w, ln_b, w_uk_t, w_uv_t, bias_tiles)


def _merge_kernel(x_ref, ya_ref, yb_ref, yc_ref, g0_ref, g1_ref, g2_ref, wa_ref, wb_ref, wc_ref, wo_ref, gain_ref,
                  o_ref):
    merged = (_sigmoid(g0_ref[...]) * _dot(ya_ref[...], wa_ref[...])
              + _sigmoid(g1_ref[...]) * _dot(yb_ref[...], wb_ref[...])
              + _sigmoid(g2_ref[...]) * _dot(yc_ref[...], wc_ref[...]))
    z = _dot(merged.astype(BF16), wo_ref[...])
    zn = z * lax.rsqrt(jnp.mean(z * z, axis=-1, keepdims=True) + RMS_EPS) * gain_ref[...]
    o_ref[...] = x_ref[...] + zn


def _merge(x2, ya, yb, yc, proj, wa, wb, wc, wo, gain, tm):
    n = x2.shape[0]
    gcol = COL_GATES // D_MODEL

    def rows(w):
        return pl.BlockSpec((tm, w), lambda i: (i, 0))

    def gate(k):
        return pl.BlockSpec((tm, D_MODEL), lambda i, k=k: (i, gcol + k))

    def full(a):
        return pl.BlockSpec(a.shape, lambda i: (0, 0))

    return pl.pallas_call(
        _merge_kernel,
        out_shape=jax.ShapeDtypeStruct((n, D_MODEL), F32),
        grid=(n // tm,),
        in_specs=[rows(D_MODEL), rows(HEAD_W), rows(HEAD_W), rows(HEAD_W), gate(0), gate(1), gate(2),
                  full(wa), full(wb), full(wc), full(wo), full(gain)],
        out_specs=rows(D_MODEL),
        compiler_params=_cparams(("parallel",)),
        name="merge",
    )(x2, ya, yb, yc, proj, proj, proj, wa, wb, wc, wo, gain)


def _ffn_kernel(x_ref, g1_ref, wg_ref, wu_ref, wd_ref, g2_ref, o_ref, hn_ref, acc_ref):
    j = pl.program_id(1)

    @pl.when(j == 0)
    def _():
        x = x_ref[...]
        hn_ref[...] = (x * lax.rsqrt(jnp.mean(x * x, axis=-1, keepdims=True) + RMS_EPS) * g1_ref[...]).astype(BF16)
        acc_ref[...] = jnp.zeros_like(acc_ref)

    hn = hn_ref[...]
    gate = _dot(hn, wg_ref[...])
    up = _dot(hn, wu_ref[...])
    act = (gate * _sigmoid(gate)) * up
    acc_ref[...] += _dot(act.astype(BF16), wd_ref[...])

    @pl.when(j == pl.num_programs(1) - 1)
    def _():
        y = acc_ref[...]
        yn = y * lax.rsqrt(jnp.mean(y * y, axis=-1, keepdims=True) + RMS_EPS) * g2_ref[...]
        o_ref[...] = x_ref[...] + yn


def _ffn(x2, g1, w_in, w_out, g2, tm, tf):
    n = x2.shape[0]
    nf = D_FF // tf
    return pl.pallas_call(
        _ffn_kernel,
        out_shape=jax.ShapeDtypeStruct((n, D_MODEL), F32),
        grid=(n // tm, nf),
        in_specs=[
            pl.BlockSpec((tm, D_MODEL), lambda i, j: (i, 0)),
            pl.BlockSpec((1, D_MODEL), lambda i, j: (0, 0)),
            pl.BlockSpec((D_MODEL, tf), lambda i, j: (0, j)),
            pl.BlockSpec((D_MODEL, tf), lambda i, j: (0, nf + j)),
            pl.BlockSpec((tf, D_MODEL), lambda i, j: (j, 0)),
            pl.BlockSpec((1, D_MODEL), lambda i, j: (0, 0)),
        ],
        out_specs=pl.BlockSpec((tm, D_MODEL), lambda i, j: (i, 0)),
        scratch_shapes=[pltpu.VMEM((tm, D_MODEL), BF16), pltpu.VMEM((tm, D_MODEL), F32)],
        compiler_params=_cparams(("parallel", "arbitrary")),
        name="ffn",
    )(x2, g1, w_in, w_in, w_out, g2)


def _pack_w_in(w):
    o_dkv, o_iq, o_ik, o_gates = 4096, 4352, 4864, 4936
    z = lambda k: jnp.zeros((w.shape[0], k), w.dtype)
    packed = jnp.concatenate([
        w[:, :o_dkv],
        w[:, o_iq:o_ik],
        w[:, o_dkv:o_iq],
        w[:, o_ik:o_gates],
        z(COL_GATES - COL_IKW - (o_gates - o_ik)),
        w[:, o_gates:],
    ], axis=1)
    assert packed.shape[1] == PROJ_W
    return packed.astype(BF16)


def _tiles(batch, seq):
    n = batch * seq
    return dict(
        proj_tm=min(1024, n), proj_tn=1024,
        hgrn_rows=min(512, seq),
        merge_tm=min(512, n),
        ffn_tm=min(512, n), ffn_tf=D_FF // 2,
    )


def kernel(x, w_in, pre_mix_norm, post_mix_norm, pre_ffn_norm, post_ffn_norm, hg_lb_logits, hg_norm_w, dsa_kv_norm,
           dsa_w_uk, dsa_w_uv, idx_k_norm_w, idx_k_norm_b, rel_bias, w_branch_a, w_branch_b, w_branch_c, w_out,
           w_ffn_in, w_ffn_out):
    batch, seq, _ = x.shape
    depth = w_in.shape[0]
    t = _tiles(batch, seq)
    lbp = jax.nn.softmax(hg_lb_logits.astype(F32), axis=0)
    lower_bounds = jnp.cumsum(lbp, axis=0) - lbp[0]
    bias_b = _bias_tiles(rel_bias[:, :MOBA_HEADS], seq, MOBA_BLOCK)
    bias_c = _bias_tiles(rel_bias[:, MOBA_HEADS:], seq, DSA_QB)
    row = lambda a: a.reshape(1, -1).astype(F32)

    x2 = x.reshape(batch * seq, D_MODEL)
    for l in range(depth):
        proj = _proj(x2, row(pre_mix_norm[l]), _pack_w_in(w_in[l]), t["proj_tm"], t["proj_tn"])
        y_a = _hgrn(proj, row(lower_bounds[l]), row(hg_norm_w[l]), batch, seq, t["hgrn_rows"])
        y_b = _moba(proj, bias_b, batch, seq)
        y_c = _dsa(proj, row(dsa_kv_norm[l]), row(idx_k_norm_w[l]), row(idx_k_norm_b[l]),
                   jnp.transpose(dsa_w_uk[l], (1, 2, 0)).astype(BF16),
                   jnp.transpose(dsa_w_uv[l], (1, 0, 2)).astype(BF16),
                   bias_c, batch, seq)
        x2 = _merge(x2, y_a, y_b, y_c, proj, w_branch_a[l].astype(BF16), w_branch_b[l].astype(BF16),
                    w_branch_c[l].astype(BF16), w_out[l].astype(BF16), row(post_mix_norm[l]), t["merge_tm"])
        x2 = _ffn(x2, row(pre_ffn_norm[l]), w_ffn_in[l].astype(BF16), w_ffn_out[l].astype(BF16),
                  row(post_ffn_norm[l]), t["ffn_tm"], t["ffn_tf"])
    return x2.reshape(batch, seq, D_MODEL)
```

```python
import functools
import math

import numpy as np
import jax
import jax.numpy as jnp
from jax import lax
from jax.experimental import pallas as pl
from jax.experimental.pallas import tpu as pltpu

F32 = jnp.float32
BF16 = jnp.bfloat16

D_MODEL = 1024
RMS_EPS = 1e-6
NEG_BIG = -1e30
TINY = 1e-20
HG_HEADS = 4
HG_D = 128
HG_CHUNK = 64
MOBA_HEADS = 4
MOBA_DH = 128
MOBA_BLOCK = 256
MOBA_TOPK = 3
DSA_HEADS = 4
DSA_DH = 128
DSA_KV_RANK = 256
IDX_HEADS = 8
IDX_DIM = 64
DSA_TOPK_MAX = 256
DSA_KC = 512
REL_BUCKETS = 32
REL_MAX_DIST = 2048
D_FF = 2816
HEAD_W = 512

LANES = 128
SUBLANES = 8
DSA_QB = LANES

FCOL_HQ, FCOL_HF, FCOL_HI, FCOL_HG = 0, 512, 1024, 1536
FCOL_GATES, FCOL_DKV, FCOL_IKW = 2048, 5120, 5376
PROJ_F_W = 5632
BCOL_MQ, BCOL_MK, BCOL_MV, BCOL_DQ, BCOL_IQ = 0, 512, 1024, 1536, 2048
PROJ_B_W = 2560

VMEM_LIMIT = 56 * 1024 * 1024

_REL_EXACT = REL_BUCKETS // 2
REL_FAR_DIST = int(math.ceil(_REL_EXACT * (REL_MAX_DIST / _REL_EXACT) ** ((REL_BUCKETS - _REL_EXACT - 1) / (REL_BUCKETS - _REL_EXACT)))) + 1

_NEG_BITS = int(np.array(NEG_BIG, np.float32).view(np.int32))
KEY_NEG = int(np.int32(_NEG_BITS ^ 0x7FFFFFFF))


def _cparams(sem):
    return pltpu.CompilerParams(dimension_semantics=sem, vmem_limit_bytes=VMEM_LIMIT)


def _sigmoid(x):
    return 1.0 / (1.0 + jnp.exp(-x))


def _dot(a, b):
    return jnp.dot(a, b, preferred_element_type=F32)


def _dot_nt(a, b):
    return lax.dot_general(a, b, (((1,), (1,)), ((), ())), preferred_element_type=F32)


def _t_bf16(a):
    return a.astype(F32).T.astype(BF16)


ROW_SLAB = 64


def _slab(x, fn):
    return fn(x.reshape(x.shape[0] // ROW_SLAB, ROW_SLAB, x.shape[1]), axis=0)


def _reduce_rows(x, fn):
    y = _slab(x, fn)
    z = fn(y.reshape(ROW_SLAB // SUBLANES, SUBLANES, y.shape[1]), axis=0)
    return fn(z, axis=0, keepdims=True)


def _proj_kernel(x_ref, g_ref, w_ref, o_ref, xn_ref):
    @pl.when(pl.program_id(1) == 0)
    def _():
        x = x_ref[...]
        ms = jnp.mean(x * x, axis=-1, keepdims=True)
        xn_ref[...] = (x * lax.rsqrt(ms + RMS_EPS) * g_ref[...]).astype(BF16)

    o_ref[...] = _dot(xn_ref[...], w_ref[...]).astype(o_ref.dtype)


def _proj(x2, gain, w_packed, out_dtype, tm, tn, name):
    n = x2.shape[0]
    width = w_packed.shape[1]
    return pl.pallas_call(
        _proj_kernel,
        out_shape=jax.ShapeDtypeStruct((n, width), out_dtype),
        grid=(n // tm, width // tn),
        in_specs=[
            pl.BlockSpec((tm, D_MODEL), lambda i, j: (i, 0)),
            pl.BlockSpec((1, D_MODEL), lambda i, j: (0, 0)),
            pl.BlockSpec((D_MODEL, tn), lambda i, j: (0, j)),
        ],
        out_specs=pl.BlockSpec((tm, tn), lambda i, j: (i, j)),
        scratch_shapes=[pltpu.VMEM((tm, D_MODEL), BF16)],
        compiler_params=_cparams(("parallel", "arbitrary")),
        name=name,
    )(x2, gain, w_packed)


HG_LEVELS = (32, 16, 8, 4, 2, 1)


def _hgrn_constants():
    c = HG_CHUNK
    t = np.arange(c)
    i = t[:, None]
    tt = t[None, :]
    rows = [tt <= i]
    mq, mk, masks = [], [], []
    for m in HG_LEVELS:
        mid = ((t // (2 * m)) * (2 * m) + m)[:, None]
        mq.append((i >= mid) & (tt >= mid) & (tt <= i))
        mk.append((i < mid) & (tt > i) & (tt <= mid - 1))
        same = (t[:, None] // (2 * m)) == (t[None, :] // (2 * m))
        masks.append(same & (i >= mid) & (tt < mid))
    masks.append(np.eye(c, dtype=bool))
    m_all = np.concatenate(rows + mq + mk, axis=0).astype(np.float32)
    return m_all, np.stack(masks).astype(np.float32)


def _hgrn_kernel(q_ref, f_ref, i_ref, g_ref, lb_ref, nw_ref, mall_ref, mask_ref, o_ref, st_ref, e_ref, *, n_chunks):
    c = HG_CHUNK
    nl = len(HG_LEVELS)

    @pl.when(pl.program_id(1) == 0)
    def _():
        st_ref[...] = jnp.zeros_like(st_ref)

    lb = lb_ref[...]
    nw = nw_ref[...]

    def chunk(ci, carry):
        r0 = pl.multiple_of(ci * c, c)
        fp = f_ref[pl.ds(r0, c), :]
        f = lb + (1.0 - lb) * _sigmoid(fp)
        logf = jnp.log(jnp.maximum(f, TINY))
        kk = (1.0 - lb) * _sigmoid(-fp)
        g_hi = logf.astype(BF16)
        g_lo = (logf - g_hi.astype(F32)).astype(BF16)
        e_ref[...] = _dot(mall_ref[...], jnp.concatenate([g_hi, g_lo], axis=1))
        q = q_ref[pl.ds(r0, c), :]
        v = i_ref[pl.ds(r0, c), :]
        og = g_ref[pl.ds(r0, c), :]
        for h in range(HG_HEADS):
            lo, hi = h * HG_D, (h + 1) * HG_D

            def expo(row):
                return e_ref[row * c:(row + 1) * c, lo:hi] + e_ref[row * c:(row + 1) * c, HEAD_W + lo:HEAD_W + hi]

            qh, kh, vh = q[:, lo:hi], kk[:, lo:hi], v[:, lo:hi]
            b = expo(0)
            s = jnp.zeros((c, c), F32)
            for l in range(nl + 1):
                if l < nl:
                    ql = qh * jnp.exp(expo(1 + l))
                    kl = kh * jnp.exp(expo(1 + nl + l))
                else:
                    ql, kl = qh, kh
                sl = _dot_nt(ql.astype(BF16), kl.astype(BF16))
                s = s + jnp.where(mask_ref[l] > 0.0, sl, 0.0)
            st = st_ref[h]
            o = _dot_nt((qh * jnp.exp(b)).astype(BF16), st.astype(BF16)) + _dot(s.astype(BF16), vh.astype(BF16))
            b_last = b[c - 1:c, :]
            kdec = kh * jnp.exp(b_last - b)
            st_ref[h] = st * jnp.exp(b_last) + _dot(vh.T.astype(BF16), kdec.astype(BF16))
            y = o * lax.rsqrt(jnp.mean(o * o, axis=-1, keepdims=True) + RMS_EPS) * nw
            gh = og[:, lo:hi]
            o_ref[pl.ds(r0, c), lo:hi] = (y * (gh * _sigmoid(gh))).astype(o_ref.dtype)
        return carry

    lax.fori_loop(0, n_chunks, chunk, 0)


def _hgrn(proj_f, lb, norm_w, batch, seq, rows):
    m_all, masks = _hgrn_constants()
    n_rb = seq // rows
    n_chunks = rows // HG_CHUNK

    def col(c0):
        return pl.BlockSpec((rows, HEAD_W), lambda b, r, cidx=c0 // HEAD_W: (b * n_rb + r, cidx))

    return pl.pallas_call(
        functools.partial(_hgrn_kernel, n_chunks=n_chunks),
        out_shape=jax.ShapeDtypeStruct((batch * seq, HEAD_W), BF16),
        grid=(batch, n_rb),
        in_specs=[
            col(FCOL_HQ), col(FCOL_HF), col(FCOL_HI), col(FCOL_HG),
            pl.BlockSpec((1, HEAD_W), lambda b, r: (0, 0)),
            pl.BlockSpec((1, HG_D), lambda b, r: (0, 0)),
            pl.BlockSpec(m_all.shape, lambda b, r: (0, 0)),
            pl.BlockSpec(masks.shape, lambda b, r: (0, 0, 0)),
        ],
        out_specs=pl.BlockSpec((rows, HEAD_W), lambda b, r: (b * n_rb + r, 0)),
        scratch_shapes=[
            pltpu.VMEM((HG_HEADS, HG_D, HG_D), F32),
            pltpu.VMEM((m_all.shape[0], 2 * HEAD_W), F32),
        ],
        compiler_params=_cparams(("parallel", "arbitrary")),
        name="hgrn",
    )(proj_f, proj_f, proj_f, proj_f, lb, norm_w, jnp.asarray(m_all, BF16), jnp.asarray(masks, F32))


def _rel_bucket(dist):
    n = jnp.maximum(dist, 0)
    exact = REL_BUCKETS // 2
    nf = jnp.maximum(n, exact).astype(F32)
    large = exact + (jnp.log(nf / exact) / math.log(REL_MAX_DIST / exact) * (REL_BUCKETS - exact)).astype(jnp.int32)
    large = jnp.minimum(large, REL_BUCKETS - 1)
    return jnp.where(n < exact, n, large)


def _n_bias_tiles(tile):
    d = 0
    while d * tile - (tile - 1) < REL_FAR_DIST:
        d += 1
    return d + 1


def _bias_tiles_t(table, tile):
    n = _n_bias_tiles(tile)
    length = n * tile
    by_dist = table[_rel_bucket(jnp.arange(length, dtype=jnp.int32))].astype(F32).T
    heads = by_dist.shape[0]
    w = jnp.concatenate([by_dist, jnp.zeros((heads, 1), F32)], axis=1)
    skew = jnp.broadcast_to(w[:, None, :], (heads, tile, length + 1)).reshape(heads, tile * (length + 1))
    skew = skew[:, :tile * length].reshape(heads, tile, length)
    return skew.reshape(heads, tile, n, tile).transpose(0, 2, 1, 3)


def _moba_kernel(q_ref, k_ref, v_ref, bias_ref, o_ref, vt_ref, km_ref, sel_ref, m_ref, l_ref, acc_ref,
                 *, n_blocks, n_sel, n_bias):
    blk, dh, nh = MOBA_BLOCK, MOBA_DH, MOBA_HEADS
    qb = pl.program_id(1)
    scale = dh ** -0.5

    @pl.when(qb == 0)
    def _():
        for h in range(nh):
            for n in range(n_blocks):
                kblk = k_ref[n * blk:(n + 1) * blk, h * dh:(h + 1) * dh].astype(F32)
                km_ref[h, n:n + 1, :] = jnp.mean(kblk, axis=0, keepdims=True)
                vt_ref[h * dh:(h + 1) * dh, n * blk:(n + 1) * blk] = _t_bf16(v_ref[n * blk:(n + 1) * blk, h * dh:(h + 1) * dh])

    q_t = []
    n_iota = lax.broadcasted_iota(jnp.int32, (n_blocks, blk), 0)
    for h in range(nh):
        qt = _t_bf16(q_ref[:, h * dh:(h + 1) * dh])
        q_t.append(qt)
        gate = _dot(km_ref[h].astype(BF16), qt)
        gate = jnp.where(n_iota < qb, gate, NEG_BIG)
        rank = jnp.zeros(gate.shape, F32)
        for n in range(n_blocks):
            row = gate[n:n + 1, :]
            beats = (row > gate) | ((row == gate) & (n < n_iota))
            rank = rank + jnp.where(beats, 1.0, 0.0)
        sel_ref[h] = jnp.where((rank < n_sel) & (n_iota < qb), 0.0, NEG_BIG)
        m_ref[h] = jnp.full((1, blk), -jnp.inf, F32)
        l_ref[h] = jnp.zeros((1, blk), F32)
        acc_ref[h] = jnp.zeros((dh, blk), F32)

    def update(h, s, r0):
        m_old = m_ref[h]
        m_new = jnp.maximum(m_old, _reduce_rows(s, jnp.max))
        alpha = jnp.exp(m_old - m_new)
        p = jnp.exp(s - m_new)
        l_ref[h] = alpha * l_ref[h] + _reduce_rows(p, jnp.sum)
        acc_ref[h] = alpha * acc_ref[h] + _dot(vt_ref[h * dh:(h + 1) * dh, pl.ds(r0, blk)], p.astype(BF16))
        m_ref[h] = m_new

    def past(n, carry):
        r0 = pl.multiple_of(n * blk, blk)
        d = jnp.minimum(qb - n, n_bias - 1)
        for h in range(nh):
            s = _dot(k_ref[pl.ds(r0, blk), h * dh:(h + 1) * dh], q_t[h]) * scale
            s = s + bias_ref[h, d] + sel_ref[h, pl.ds(n, 1), :]
            update(h, s, r0)
        return carry

    lax.fori_loop(0, qb, past, 0)

    r0 = pl.multiple_of(qb * blk, blk)
    key_i = lax.broadcasted_iota(jnp.int32, (blk, blk), 0)
    qry_i = lax.broadcasted_iota(jnp.int32, (blk, blk), 1)
    for h in range(nh):
        s = _dot(k_ref[pl.ds(r0, blk), h * dh:(h + 1) * dh], q_t[h]) * scale + bias_ref[h, 0]
        s = jnp.where(key_i <= qry_i, s, NEG_BIG)
        update(h, s, r0)
        o_ref[:, h * dh:(h + 1) * dh] = (acc_ref[h] / l_ref[h]).T.astype(o_ref.dtype)


def _moba(proj_b, bias_tiles, batch, seq):
    blk = MOBA_BLOCK
    nb = seq // blk
    n_sel = min(MOBA_TOPK, nb - 1)
    n_bias = bias_tiles.shape[1]
    return pl.pallas_call(
        functools.partial(_moba_kernel, n_blocks=nb, n_sel=n_sel, n_bias=n_bias),
        out_shape=jax.ShapeDtypeStruct((batch * seq, HEAD_W), BF16),
        grid=(batch, nb),
        in_specs=[
            pl.BlockSpec((blk, HEAD_W), lambda b, i: (b * nb + i, BCOL_MQ // HEAD_W)),
            pl.BlockSpec((seq, HEAD_W), lambda b, i: (b, BCOL_MK // HEAD_W)),
            pl.BlockSpec((seq, HEAD_W), lambda b, i: (b, BCOL_MV // HEAD_W)),
            pl.BlockSpec(bias_tiles.shape, lambda b, i: (0, 0, 0, 0)),
        ],
        out_specs=pl.BlockSpec((blk, HEAD_W), lambda b, i: (b * nb + i, 0)),
        scratch_shapes=[
            pltpu.VMEM((HEAD_W, seq), BF16),
            pltpu.VMEM((MOBA_HEADS, nb, MOBA_DH), F32),
            pltpu.VMEM((MOBA_HEADS, nb, blk), F32),
            pltpu.VMEM((MOBA_HEADS, 1, blk), F32),
            pltpu.VMEM((MOBA_HEADS, 1, blk), F32),
            pltpu.VMEM((MOBA_HEADS, MOBA_DH, blk), F32),
        ],
        compiler_params=_cparams(("parallel", "arbitrary")),
        name="moba",
    )(proj_b, proj_b, proj_b, bias_tiles)


def _sort_key(x):
    bits = pltpu.bitcast(x, jnp.int32)
    return bits ^ ((bits >> 31) & 0x7FFFFFFF)


def _dsa_kernel(dq_ref, iq_ref, iwq_ref, dkv_ref, ikw_ref, kvg_ref, lnw_ref, lnb_ref, wuk_ref, wuv_ref, bias_ref,
                o_ref, ckv_ref, ckvt_ref, ikn_ref, key_ref, mask_ref, m_ref, l_ref, acc_ref, *, seq, n_keep, n_bias):
    tq, kc, nh, dh = DSA_QB, DSA_KC, DSA_HEADS, DSA_DH
    sub = kc // LANES
    qb = pl.program_id(1)
    t0 = qb * tq
    n_kc = (t0 + tq - 1) // kc + 1
    n_rest = (seq - n_kc * kc).astype(F32)

    @pl.when(qb == 0)
    def _():
        for c in range(seq // kc):
            rows = slice(c * kc, (c + 1) * kc)
            x = dkv_ref[rows, :]
            cn = x * lax.rsqrt(jnp.mean(x * x, axis=-1, keepdims=True) + RMS_EPS) * kvg_ref[...]
            ckv_ref[rows, :] = cn.astype(BF16)
            ckvt_ref[:, rows] = cn.T.astype(BF16)
            ik = ikw_ref[rows, 0:IDX_DIM]
            xc = ik - jnp.mean(ik, axis=-1, keepdims=True)
            ikn = xc * lax.rsqrt(jnp.mean(xc * xc, axis=-1, keepdims=True) + RMS_EPS) * lnw_ref[...] + lnb_ref[...]
            ikn_ref[rows, :] = ikn.astype(BF16)

    t_row = t0 + lax.broadcasted_iota(jnp.int32, (1, tq), 1)
    key_off = lax.broadcasted_iota(jnp.int32, (kc, tq), 0)

    def causal_of(c0):
        return (c0 + key_off) <= t_row

    iq_t = iq_ref[...].astype(F32).T
    iq_cat = jnp.concatenate([iq_t[h * IDX_DIM:(h + 1) * IDX_DIM, :] for h in range(IDX_HEADS)], axis=1).astype(BF16)
    iw_t = iwq_ref[...].T[IDX_DIM:IDX_DIM + IDX_HEADS, :] * (IDX_HEADS ** -0.5)
    iw_t = iw_t * (IDX_DIM ** -0.5)

    def score_chunk(c, carry):
        c0 = pl.multiple_of(c * kc, kc)
        s_all = _dot(ikn_ref[pl.ds(c0, kc), :], iq_cat)
        sc = jnp.zeros((kc, tq), F32)
        for h in range(IDX_HEADS):
            sc = sc + iw_t[h:h + 1, :] * jnp.maximum(s_all[:, h * tq:(h + 1) * tq], 0.0)
        sc = jnp.where(causal_of(c0), sc, NEG_BIG)
        sc = jnp.where(sc == 0.0, 0.0, sc)
        key_ref[pl.ds(c0, kc), :] = _sort_key(sc)
        return carry

    lax.fori_loop(0, n_kc, score_chunk, 0)

    def count(pred_fn):
        def body(c, tot):
            c0 = pl.multiple_of(c * kc, kc)
            hit = pred_fn(key_ref[pl.ds(c0, kc), :], c0)
            return tot + _slab(jnp.where(hit, 1.0, 0.0), jnp.sum)
        part = lax.fori_loop(0, n_kc, body, jnp.zeros((ROW_SLAB, tq), F32))
        return _reduce_rows(part, jnp.sum)

    def bit_step(k, base):
        cand = base + jnp.left_shift(jnp.int32(1), 31 - k)
        cnt = count(lambda keys, c0: keys >= cand) + jnp.where(cand <= KEY_NEG, n_rest, 0.0)
        return jnp.where(cnt >= n_keep, cand, base)

    tau = lax.fori_loop(0, 32, bit_step, jnp.full((1, tq), jnp.iinfo(jnp.int32).min, jnp.int32))

    c_gt = count(lambda keys, c0: keys > tau) + jnp.where(tau < KEY_NEG, n_rest, 0.0)
    need = n_keep - c_gt
    n_eq = count(lambda keys, c0: (keys == tau) & causal_of(c0))
    tie_break = jnp.max(n_eq - need) > 0.0

    @pl.when(jnp.logical_not(tie_break))
    def _():
        def body(c, carry):
            c0 = pl.multiple_of(c * kc, kc)
            keep = (key_ref[pl.ds(c0, kc), :] >= tau) & causal_of(c0)
            mask_ref[pl.ds(c0, kc), :] = jnp.where(keep, 0.0, NEG_BIG)
            return carry
        lax.fori_loop(0, n_kc, body, 0)

    @pl.when(tie_break)
    def _():
        lower = jnp.where(lax.broadcasted_iota(jnp.int32, (kc, kc), 1)
                          < lax.broadcasted_iota(jnp.int32, (kc, kc), 0), 1.0, 0.0).astype(BF16)

        def body(c, seen):
            c0 = pl.multiple_of(c * kc, kc)
            keys = key_ref[pl.ds(c0, kc), :]
            eq = keys == tau
            eqf = jnp.where(eq, 1.0, 0.0)
            rank = seen + _dot(lower, eqf.astype(BF16))
            keep = ((keys > tau) | (eq & (rank < need))) & causal_of(c0)
            mask_ref[pl.ds(c0, kc), :] = jnp.where(keep, 0.0, NEG_BIG)
            return seen + _reduce_rows(eqf, jnp.sum)
        lax.fori_loop(0, n_kc, body, jnp.zeros((1, tq), F32))

    scale = dh ** -0.5
    qa_t = []
    for h in range(nh):
        q_t = _t_bf16(dq_ref[:, h * dh:(h + 1) * dh])
        qa_t.append(_dot(wuk_ref[h], q_t).astype(BF16))
        m_ref[h] = jnp.full((1, tq), -jnp.inf, F32)
        l_ref[h] = jnp.zeros((1, tq), F32)
        acc_ref[h] = jnp.zeros((DSA_KV_RANK, tq), F32)

    def att_chunk(c, carry):
        c0 = pl.multiple_of(c * kc, kc)
        ckv = ckv_ref[pl.ds(c0, kc), :]
        ckv_t = ckvt_ref[:, pl.ds(c0, kc)]
        msk = mask_ref[pl.ds(c0, kc), :]
        for h in range(nh):
            s = _dot(ckv, qa_t[h]) * scale
            tiles = [bias_ref[h, jnp.clip(qb - (c * sub + g), 0, n_bias - 1)] for g in range(sub)]
            s = s + jnp.concatenate(tiles, axis=0) + msk
            m_old = m_ref[h]
            m_new = jnp.maximum(m_old, _reduce_rows(s, jnp.max))
            alpha = jnp.exp(m_old - m_new)
            p = jnp.exp(s - m_new)
            l_ref[h] = alpha * l_ref[h] + _reduce_rows(p, jnp.sum)
            acc_ref[h] = alpha * acc_ref[h] + _dot(ckv_t, p.astype(BF16))
            m_ref[h] = m_new
        return carry

    lax.fori_loop(0, n_kc, att_chunk, 0)

    for h in range(nh):
        o_lat_t = (acc_ref[h] / l_ref[h]).astype(BF16)
        o_ref[:, h * dh:(h + 1) * dh] = _dot(wuv_ref[h], o_lat_t).T.astype(o_ref.dtype)


def _dsa(proj_f, proj_b, kv_gain, ln_w, ln_b, w_uk_h, w_uv_h, bias_tiles, batch, seq):
    tq = DSA_QB
    nq = seq // tq
    n_keep = min(DSA_TOPK_MAX, seq // 4)
    n_bias = bias_tiles.shape[1]
    return pl.pallas_call(
        functools.partial(_dsa_kernel, seq=seq, n_keep=float(n_keep), n_bias=n_bias),
        out_shape=jax.ShapeDtypeStruct((batch * seq, HEAD_W), BF16),
        grid=(batch, nq),
        in_specs=[
            pl.BlockSpec((tq, HEAD_W), lambda b, i: (b * nq + i, BCOL_DQ // HEAD_W)),
            pl.BlockSpec((tq, HEAD_W), lambda b, i: (b * nq + i, BCOL_IQ // HEAD_W)),
            pl.BlockSpec((tq, LANES), lambda b, i: (b * nq + i, FCOL_IKW // LANES)),
            pl.BlockSpec((seq, DSA_KV_RANK), lambda b, i: (b, FCOL_DKV // DSA_KV_RANK)),
            pl.BlockSpec((seq, LANES), lambda b, i: (b, FCOL_IKW // LANES)),
            pl.BlockSpec((1, DSA_KV_RANK), lambda b, i: (0, 0)),
            pl.BlockSpec((1, IDX_DIM), lambda b, i: (0, 0)),
            pl.BlockSpec((1, IDX_DIM), lambda b, i: (0, 0)),
            pl.BlockSpec(w_uk_h.shape, lambda b, i: (0, 0, 0)),
            pl.BlockSpec(w_uv_h.shape, lambda b, i: (0, 0, 0)),
            pl.BlockSpec(bias_tiles.shape, lambda b, i: (0, 0, 0, 0)),
        ],
        out_specs=pl.BlockSpec((tq, HEAD_W), lambda b, i: (b * nq + i, 0)),
        scratch_shapes=[
            pltpu.VMEM((seq, DSA_KV_RANK), BF16),
            pltpu.VMEM((DSA_KV_RANK, seq), BF16),
            pltpu.VMEM((seq, IDX_DIM), BF16),
            pltpu.VMEM((seq, tq), jnp.int32),
            pltpu.VMEM((seq, tq), F32),
            pltpu.VMEM((DSA_HEADS, 1, tq), F32),
            pltpu.VMEM((DSA_HEADS, 1, tq), F32),
            pltpu.VMEM((DSA_HEADS, DSA_KV_RANK, tq), F32),
        ],
        compiler_params=_cparams(("parallel", "arbitrary")),
        name="dsa",
    )(proj_b, proj_b, proj_f, proj_f, proj_f, kv_gain, ln_w, ln_b, w_uk_h, w_uv_h, bias_tiles)


def _merge_kernel(x_ref, ya_ref, yb_ref, yc_ref, g0_ref, g1_ref, g2_ref, wa_ref, wb_ref, wc_ref, wo_ref, gain_ref,
                  o_ref):
    merged = (_sigmoid(g0_ref[...]) * _dot(ya_ref[...], wa_ref[...])
              + _sigmoid(g1_ref[...]) * _dot(yb_ref[...], wb_ref[...])
              + _sigmoid(g2_ref[...]) * _dot(yc_ref[...], wc_ref[...]))
    z = _dot(merged.astype(BF16), wo_ref[...])
    zn = z * lax.rsqrt(jnp.mean(z * z, axis=-1, keepdims=True) + RMS_EPS) * gain_ref[...]
    o_ref[...] = x_ref[...] + zn


def _merge(x2, ya, yb, yc, proj_f, wa, wb, wc, wo, gain, tm):
    n = x2.shape[0]
    gcol = FCOL_GATES // D_MODEL

    def rows(w):
        return pl.BlockSpec((tm, w), lambda i: (i, 0))

    def gate(k):
        return pl.BlockSpec((tm, D_MODEL), lambda i, k=k: (i, gcol + k))

    def full(a):
        return pl.BlockSpec(a.shape, lambda i: (0, 0))

    return pl.pallas_call(
        _merge_kernel,
        out_shape=jax.ShapeDtypeStruct((n, D_MODEL), F32),
        grid=(n // tm,),
        in_specs=[rows(D_MODEL), rows(HEAD_W), rows(HEAD_W), rows(HEAD_W), gate(0), gate(1), gate(2),
                  full(wa), full(wb), full(wc), full(wo), full(gain)],
        out_specs=rows(D_MODEL),
        compiler_params=_cparams(("parallel",)),
        name="merge",
    )(x2, ya, yb, yc, proj_f, proj_f, proj_f, wa, wb, wc, wo, gain)


def _ffn_kernel(x_ref, g1_ref, wg_ref, wu_ref, wd_ref, g2_ref, o_ref, hn_ref, acc_ref):
    j = pl.program_id(1)

    @pl.when(j == 0)
    def _():
        x = x_ref[...]
        hn_ref[...] = (x * lax.rsqrt(jnp.mean(x * x, axis=-1, keepdims=True) + RMS_EPS) * g1_ref[...]).astype(BF16)
        acc_ref[...] = jnp.zeros_like(acc_ref)

    hn = hn_ref[...]
    gate = _dot(hn, wg_ref[...])
    up = _dot(hn, wu_ref[...])
    act = (gate * _sigmoid(gate)) * up
    acc_ref[...] += _dot(act.astype(BF16), wd_ref[...])

    @pl.when(j == pl.num_programs(1) - 1)
    def _():
        y = acc_ref[...]
        yn = y * lax.rsqrt(jnp.mean(y * y, axis=-1, keepdims=True) + RMS_EPS) * g2_ref[...]
        o_ref[...] = x_ref[...] + yn


def _ffn(x2, g1, w_in, w_out, g2, tm, tf):
    n = x2.shape[0]
    nf = D_FF // tf
    return pl.pallas_call(
        _ffn_kernel,
        out_shape=jax.ShapeDtypeStruct((n, D_MODEL), F32),
        grid=(n // tm, nf),
        in_specs=[
            pl.BlockSpec((tm, D_MODEL), lambda i, j: (i, 0)),
            pl.BlockSpec((1, D_MODEL), lambda i, j: (0, 0)),
            pl.BlockSpec((D_MODEL, tf), lambda i, j: (0, j)),
            pl.BlockSpec((D_MODEL, tf), lambda i, j: (0, nf + j)),
            pl.BlockSpec((tf, D_MODEL), lambda i, j: (j, 0)),
            pl.BlockSpec((1, D_MODEL), lambda i, j: (0, 0)),
        ],
        out_specs=pl.BlockSpec((tm, D_MODEL), lambda i, j: (i, 0)),
        scratch_shapes=[pltpu.VMEM((tm, D_MODEL), BF16), pltpu.VMEM((tm, D_MODEL), F32)],
        compiler_params=_cparams(("parallel", "arbitrary")),
        name="ffn",
    )(x2, g1, w_in, w_in, w_out, g2)


def _pack_w_in(w):
    o_mq, o_dq, o_dkv, o_iq, o_ik, o_gates = 2048, 3584, 4096, 4352, 4864, 4936
    z = lambda k: jnp.zeros((w.shape[0], k), w.dtype)
    w_f = jnp.concatenate([
        w[:, :o_mq],
        w[:, o_gates:],
        w[:, o_dkv:o_iq],
        w[:, o_ik:o_gates],
        z(PROJ_F_W - FCOL_IKW - (o_gates - o_ik)),
    ], axis=1)
    w_b = jnp.concatenate([
        w[:, o_mq:o_dq],
        w[:, o_dq:o_dkv],
        w[:, o_iq:o_ik],
    ], axis=1)
    assert w_f.shape[1] == PROJ_F_W and w_b.shape[1] == PROJ_B_W
    return w_f.astype(BF16), w_b.astype(BF16)


def _tiles(batch, seq):
    n = batch * seq
    return dict(
        proj_tm=min(1024, n), proj_f_tn=512, proj_b_tn=1280,
        hgrn_rows=min(512, seq),
        merge_tm=min(512, n),
        ffn_tm=min(512, n), ffn_tf=D_FF // 2,
    )


def kernel(x, w_in, pre_mix_norm, post_mix_norm, pre_ffn_norm, post_ffn_norm, hg_lb_logits, hg_norm_w, dsa_kv_norm,
           dsa_w_uk, dsa_w_uv, idx_k_norm_w, idx_k_norm_b, rel_bias, w_branch_a, w_branch_b, w_branch_c, w_out,
           w_ffn_in, w_ffn_out):
    batch, seq, _ = x.shape
    depth = w_in.shape[0]
    t = _tiles(batch, seq)
    lbp = jax.nn.softmax(hg_lb_logits.astype(F32), axis=0)
    lower_bounds = jnp.cumsum(lbp, axis=0) - lbp[0]
    bias_b = _bias_tiles_t(rel_bias[:, :MOBA_HEADS], MOBA_BLOCK)
    bias_c = _bias_tiles_t(rel_bias[:, MOBA_HEADS:], LANES)
    row = lambda a: a.reshape(1, -1).astype(F32)

    x2 = x.reshape(batch * seq, D_MODEL)
    for l in range(depth):
        w_f, w_b = _pack_w_in(w_in[l])
        gain = row(pre_mix_norm[l])
        proj_f = _proj(x2, gain, w_f, F32, t["proj_tm"], t["proj_f_tn"], "proj_f")
        proj_b = _proj(x2, gain, w_b, BF16, t["proj_tm"], t["proj_b_tn"], "proj_b")
        y_a = _hgrn(proj_f, row(lower_bounds[l]), row(hg_norm_w[l]), batch, seq, t["hgrn_rows"])
        y_b = _moba(proj_b, bias_b, batch, seq)
        y_c = _dsa(proj_f, proj_b, row(dsa_kv_norm[l]), row(idx_k_norm_w[l]), row(idx_k_norm_b[l]),
                   jnp.transpose(dsa_w_uk[l], (1, 0, 2)).astype(BF16),
                   jnp.transpose(dsa_w_uv[l], (1, 2, 0)).astype(BF16),
                   bias_c, batch, seq)
        x2 = _merge(x2, y_a, y_b, y_c, proj_f, w_branch_a[l].astype(BF16), w_branch_b[l].astype(BF16),
                    w_branch_c[l].astype(BF16), w_out[l].astype(BF16), row(post_mix_norm[l]), t["merge_tm"])
        x2 = _ffn(x2, row(pre_ffn_norm[l]), w_ffn_in[l].astype(BF16), w_ffn_out[l].astype(BF16),
                  row(post_ffn_norm[l]), t["ffn_tm"], t["ffn_tf"])
    return x2.reshape(batch, seq, D_MODEL)
```

```python
import functools
import math

import numpy as np
import jax
import jax.numpy as jnp
from jax import lax
from jax.experimental import pallas as pl
from jax.experimental.pallas import tpu as pltpu

F32 = jnp.float32
BF16 = jnp.bfloat16

D_MODEL = 1024
RMS_EPS = 1e-6
NEG_BIG = -1e30
TINY = 1e-20
HG_HEADS = 4
HG_D = 128
HG_CHUNK = 64
MOBA_HEADS = 4
MOBA_DH = 128
MOBA_BLOCK = 256
MOBA_TOPK = 3
DSA_HEADS = 4
DSA_DH = 128
DSA_KV_RANK = 256
IDX_HEADS = 8
IDX_DIM = 64
DSA_TOPK_MAX = 256
DSA_KC = 512
REL_BUCKETS = 32
REL_MAX_DIST = 2048
D_FF = 2816
HEAD_W = 512

LANES = 128
SUBLANES = 8
DSA_QB = LANES

FCOL_HQ, FCOL_HF, FCOL_HI, FCOL_HG = 0, 512, 1024, 1536
FCOL_GATES, FCOL_DKV, FCOL_IKW = 2048, 5120, 5376
PROJ_F_W = 5632
BCOL_MQ, BCOL_MK, BCOL_MV, BCOL_DQ, BCOL_IQ = 0, 512, 1024, 1536, 2048
PROJ_B_W = 2560

VMEM_LIMIT = 56 * 1024 * 1024

_REL_EXACT = REL_BUCKETS // 2
REL_FAR_DIST = int(math.ceil(_REL_EXACT * (REL_MAX_DIST / _REL_EXACT) ** ((REL_BUCKETS - _REL_EXACT - 1) / (REL_BUCKETS - _REL_EXACT)))) + 1

_NEG_BITS = int(np.array(NEG_BIG, np.float32).view(np.int32))
KEY_NEG = int(np.int32(_NEG_BITS ^ 0x7FFFFFFF))


def _cparams(sem):
    return pltpu.CompilerParams(dimension_semantics=sem, vmem_limit_bytes=VMEM_LIMIT)


def _sigmoid(x):
    return 1.0 / (1.0 + jnp.exp(-x))


def _dot(a, b):
    return jnp.dot(a, b, preferred_element_type=F32)


def _dot_nt(a, b):
    return lax.dot_general(a, b, (((1,), (1,)), ((), ())), preferred_element_type=F32)


def _t_bf16(a):
    return a.astype(F32).T.astype(BF16)


ROW_SLAB = 64


def _slab(x, fn):
    return fn(x.reshape(x.shape[0] // ROW_SLAB, ROW_SLAB, x.shape[1]), axis=0)


def _reduce_rows(x, fn):
    y = _slab(x, fn)
    z = fn(y.reshape(ROW_SLAB // SUBLANES, SUBLANES, y.shape[1]), axis=0)
    return fn(z, axis=0, keepdims=True)


def _proj_kernel(x_ref, g_ref, w_ref, o_ref, xn_ref):
    @pl.when(pl.program_id(1) == 0)
    def _():
        x = x_ref[...]
        ms = jnp.mean(x * x, axis=-1, keepdims=True)
        xn_ref[...] = (x * lax.rsqrt(ms + RMS_EPS) * g_ref[...]).astype(BF16)

    o_ref[...] = _dot(xn_ref[...], w_ref[...]).astype(o_ref.dtype)


def _proj(x2, gain, w_packed, out_dtype, tm, tn, name):
    n = x2.shape[0]
    width = w_packed.shape[1]
    return pl.pallas_call(
        _proj_kernel,
        out_shape=jax.ShapeDtypeStruct((n, width), out_dtype),
        grid=(n // tm, width // tn),
        in_specs=[
            pl.BlockSpec((tm, D_MODEL), lambda i, j: (i, 0)),
            pl.BlockSpec((1, D_MODEL), lambda i, j: (0, 0)),
            pl.BlockSpec((D_MODEL, tn), lambda i, j: (0, j)),
        ],
        out_specs=pl.BlockSpec((tm, tn), lambda i, j: (i, j)),
        scratch_shapes=[pltpu.VMEM((tm, D_MODEL), BF16)],
        compiler_params=_cparams(("parallel", "arbitrary")),
        name=name,
    )(x2, gain, w_packed)


HG_LEVELS = (32, 16, 8, 4, 2, 1)


def _hgrn_constants():
    c = HG_CHUNK
    t = np.arange(c)
    i = t[:, None]
    tt = t[None, :]
    rows = [tt <= i]
    mq, mk, masks = [], [], []
    for m in HG_LEVELS:
        mid = ((t // (2 * m)) * (2 * m) + m)[:, None]
        mq.append((i >= mid) & (tt >= mid) & (tt <= i))
        mk.append((i < mid) & (tt > i) & (tt <= mid - 1))
        same = (t[:, None] // (2 * m)) == (t[None, :] // (2 * m))
        masks.append(same & (i >= mid) & (tt < mid))
    masks.append(np.eye(c, dtype=bool))
    m_all = np.concatenate(rows + mq + mk, axis=0).astype(np.float32)
    return m_all, np.stack(masks).astype(np.float32)


def _hgrn_kernel(q_ref, f_ref, i_ref, g_ref, lb_ref, nw_ref, mall_ref, mask_ref, o_ref, st_ref, e_ref, *, n_chunks):
    c = HG_CHUNK
    nl = len(HG_LEVELS)

    @pl.when(pl.program_id(1) == 0)
    def _():
        st_ref[...] = jnp.zeros_like(st_ref)

    lb = lb_ref[...]
    nw = nw_ref[...]

    def chunk(ci, carry):
        r0 = pl.multiple_of(ci * c, c)
        fp = f_ref[pl.ds(r0, c), :]
        f = lb + (1.0 - lb) * _sigmoid(fp)
        logf = jnp.log(jnp.maximum(f, TINY))
        kk = (1.0 - lb) * _sigmoid(-fp)
        g_hi = logf.astype(BF16)
        g_lo = (logf - g_hi.astype(F32)).astype(BF16)
        e_ref[...] = _dot(mall_ref[...], jnp.concatenate([g_hi, g_lo], axis=1))
        q = q_ref[pl.ds(r0, c), :]
        v = i_ref[pl.ds(r0, c), :]
        og = g_ref[pl.ds(r0, c), :]
        for h in range(HG_HEADS):
            lo, hi = h * HG_D, (h + 1) * HG_D

            def expo(row):
                return e_ref[row * c:(row + 1) * c, lo:hi] + e_ref[row * c:(row + 1) * c, HEAD_W + lo:HEAD_W + hi]

            qh, kh, vh = q[:, lo:hi], kk[:, lo:hi], v[:, lo:hi]
            b = expo(0)
            s = jnp.zeros((c, c), F32)
            for l in range(nl + 1):
                if l < nl:
                    ql = qh * jnp.exp(expo(1 + l))
                    kl = kh * jnp.exp(expo(1 + nl + l))
                else:
                    ql, kl = qh, kh
                sl = _dot_nt(ql.astype(BF16), kl.astype(BF16))
                s = s + jnp.where(mask_ref[l] > 0.0, sl, 0.0)
            st = st_ref[h]
            o = _dot_nt((qh * jnp.exp(b)).astype(BF16), st.astype(BF16)) + _dot(s.astype(BF16), vh.astype(BF16))
            b_last = b[c - 1:c, :]
            kdec = kh * jnp.exp(b_last - b)
            st_ref[h] = st * jnp.exp(b_last) + _dot(vh.T.astype(BF16), kdec.astype(BF16))
            y = o * lax.rsqrt(jnp.mean(o * o, axis=-1, keepdims=True) + RMS_EPS) * nw
            gh = og[:, lo:hi]
            o_ref[pl.ds(r0, c), lo:hi] = (y * (gh * _sigmoid(gh))).astype(o_ref.dtype)
        return carry

    lax.fori_loop(0, n_chunks, chunk, 0)


def _hgrn(proj_f, lb, norm_w, batch, seq, rows):
    m_all, masks = _hgrn_constants()
    n_rb = seq // rows
    n_chunks = rows // HG_CHUNK

    def col(c0):
        return pl.BlockSpec((rows, HEAD_W), lambda b, r, cidx=c0 // HEAD_W: (b * n_rb + r, cidx))

    return pl.pallas_call(
        functools.partial(_hgrn_kernel, n_chunks=n_chunks),
        out_shape=jax.ShapeDtypeStruct((batch * seq, HEAD_W), BF16),
        grid=(batch, n_rb),
        in_specs=[
            col(FCOL_HQ), col(FCOL_HF), col(FCOL_HI), col(FCOL_HG),
            pl.BlockSpec((1, HEAD_W), lambda b, r: (0, 0)),
            pl.BlockSpec((1, HG_D), lambda b, r: (0, 0)),
            pl.BlockSpec(m_all.shape, lambda b, r: (0, 0)),
            pl.BlockSpec(masks.shape, lambda b, r: (0, 0, 0)),
        ],
        out_specs=pl.BlockSpec((rows, HEAD_W), lambda b, r: (b * n_rb + r, 0)),
        scratch_shapes=[
            pltpu.VMEM((HG_HEADS, HG_D, HG_D), F32),
            pltpu.VMEM((m_all.shape[0], 2 * HEAD_W), F32),
        ],
        compiler_params=_cparams(("parallel", "arbitrary")),
        name="hgrn",
    )(proj_f, proj_f, proj_f, proj_f, lb, norm_w, jnp.asarray(m_all, BF16), jnp.asarray(masks, F32))


def _rel_bucket(dist):
    n = jnp.maximum(dist, 0)
    exact = REL_BUCKETS // 2
    nf = jnp.maximum(n, exact).astype(F32)
    large = exact + (jnp.log(nf / exact) / math.log(REL_MAX_DIST / exact) * (REL_BUCKETS - exact)).astype(jnp.int32)
    large = jnp.minimum(large, REL_BUCKETS - 1)
    return jnp.where(n < exact, n, large)


def _n_bias_tiles(tile):
    d = 0
    while d * tile - (tile - 1) < REL_FAR_DIST:
        d += 1
    return d + 1


def _bias_tiles_t(table, tile):
    n = _n_bias_tiles(tile)
    length = n * tile
    by_dist = table[_rel_bucket(jnp.arange(length, dtype=jnp.int32))].astype(F32).T
    heads = by_dist.shape[0]
    w = jnp.concatenate([by_dist, jnp.zeros((heads, 1), F32)], axis=1)
    skew = jnp.broadcast_to(w[:, None, :], (heads, tile, length + 1)).reshape(heads, tile * (length + 1))
    skew = skew[:, :tile * length].reshape(heads, tile, length)
    return skew.reshape(heads, tile, n, tile).transpose(0, 2, 1, 3)


def _moba_kernel(q_ref, k_ref, v_ref, bias_ref, o_ref, vt_ref, km_ref, sel_ref, m_ref, l_ref, acc_ref,
                 *, n_blocks, n_sel, n_bias):
    blk, dh, nh = MOBA_BLOCK, MOBA_DH, MOBA_HEADS
    qb = pl.program_id(1)
    scale = dh ** -0.5

    @pl.when(qb == 0)
    def _():
        for h in range(nh):
            for n in range(n_blocks):
                kblk = k_ref[n * blk:(n + 1) * blk, h * dh:(h + 1) * dh].astype(F32)
                km_ref[h, n:n + 1, :] = jnp.mean(kblk, axis=0, keepdims=True)
                vt_ref[h * dh:(h + 1) * dh, n * blk:(n + 1) * blk] = _t_bf16(v_ref[n * blk:(n + 1) * blk, h * dh:(h + 1) * dh])

    q_t = []
    n_iota = lax.broadcasted_iota(jnp.int32, (n_blocks, blk), 0)
    for h in range(nh):
        qt = _t_bf16(q_ref[:, h * dh:(h + 1) * dh])
        q_t.append(qt)
        gate = _dot(km_ref[h].astype(BF16), qt)
        gate = jnp.where(n_iota < qb, gate, NEG_BIG)
        rank = jnp.zeros(gate.shape, F32)
        for n in range(n_blocks):
            row = gate[n:n + 1, :]
            beats = (row > gate) | ((row == gate) & (n < n_iota))
            rank = rank + jnp.where(beats, 1.0, 0.0)
        sel_ref[h] = jnp.where((rank < n_sel) & (n_iota < qb), 0.0, NEG_BIG)
        m_ref[h] = jnp.full((1, blk), -jnp.inf, F32)
        l_ref[h] = jnp.zeros((1, blk), F32)
        acc_ref[h] = jnp.zeros((dh, blk), F32)

    def attend(r0, nk, mask_fn):
        ss = [_dot(k_ref[pl.ds(r0, nk), h * dh:(h + 1) * dh], q_t[h]) for h in range(nh)]
        ps, alphas = [], []
        for h in range(nh):
            s = ss[h] * scale + mask_fn(h)
            m_old = m_ref[h]
            m_new = jnp.maximum(m_old, _reduce_rows(s, jnp.max))
            alphas.append(jnp.exp(m_old - m_new))
            p = jnp.exp(s - m_new)
            l_ref[h] = alphas[h] * l_ref[h] + _reduce_rows(p, jnp.sum)
            m_ref[h] = m_new
            ps.append(p.astype(BF16))
        for h in range(nh):
            acc_ref[h] = alphas[h] * acc_ref[h] + _dot(vt_ref[h * dh:(h + 1) * dh, pl.ds(r0, nk)], ps[h])

    def past_mask(h, n):
        return bias_ref[h, jnp.minimum(qb - n, n_bias - 1)] + sel_ref[h, pl.ds(n, 1), :]

    causal = jnp.where(lax.broadcasted_iota(jnp.int32, (blk, blk), 0) <= lax.broadcasted_iota(jnp.int32, (blk, blk), 1),
                       0.0, NEG_BIG)

    def own_mask(h):
        return bias_ref[h, 0] + causal

    def past_pair(i, carry):
        n = 2 * i
        attend(pl.multiple_of(n * blk, 2 * blk), 2 * blk,
               lambda h: jnp.concatenate([past_mask(h, n), past_mask(h, n + 1)], axis=0))
        return carry

    n_pairs = qb // 2
    lax.fori_loop(0, n_pairs, past_pair, 0)

    @pl.when(qb % 2 == 1)
    def _():
        attend(pl.multiple_of((qb - 1) * blk, blk), 2 * blk,
               lambda h: jnp.concatenate([past_mask(h, qb - 1), own_mask(h)], axis=0))

    @pl.when(qb % 2 == 0)
    def _():
        attend(pl.multiple_of(qb * blk, blk), blk, own_mask)

    for h in range(nh):
        o_ref[:, h * dh:(h + 1) * dh] = (acc_ref[h] / l_ref[h]).T.astype(o_ref.dtype)


def _moba(proj_b, bias_tiles, batch, seq):
    blk = MOBA_BLOCK
    nb = seq // blk
    n_sel = min(MOBA_TOPK, nb - 1)
    n_bias = bias_tiles.shape[1]
    return pl.pallas_call(
        functools.partial(_moba_kernel, n_blocks=nb, n_sel=n_sel, n_bias=n_bias),
        out_shape=jax.ShapeDtypeStruct((batch * seq, HEAD_W), BF16),
        grid=(batch, nb),
        in_specs=[
            pl.BlockSpec((blk, HEAD_W), lambda b, i: (b * nb + i, BCOL_MQ // HEAD_W)),
            pl.BlockSpec((seq, HEAD_W), lambda b, i: (b, BCOL_MK // HEAD_W)),
            pl.BlockSpec((seq, HEAD_W), lambda b, i: (b, BCOL_MV // HEAD_W)),
            pl.BlockSpec(bias_tiles.shape, lambda b, i: (0, 0, 0, 0)),
        ],
        out_specs=pl.BlockSpec((blk, HEAD_W), lambda b, i: (b * nb + i, 0)),
        scratch_shapes=[
            pltpu.VMEM((HEAD_W, seq), BF16),
            pltpu.VMEM((MOBA_HEADS, nb, MOBA_DH), F32),
            pltpu.VMEM((MOBA_HEADS, nb, blk), F32),
            pltpu.VMEM((MOBA_HEADS, 1, blk), F32),
            pltpu.VMEM((MOBA_HEADS, 1, blk), F32),
            pltpu.VMEM((MOBA_HEADS, MOBA_DH, blk), F32),
        ],
        compiler_params=_cparams(("parallel", "arbitrary")),
        name="moba",
    )(proj_b, proj_b, proj_b, bias_tiles)


def _sort_key(x):
    bits = pltpu.bitcast(x, jnp.int32)
    return bits ^ ((bits >> 31) & 0x7FFFFFFF)


def _dsa_kernel(dq_ref, iq_ref, iwq_ref, dkv_ref, ikw_ref, kvg_ref, lnw_ref, lnb_ref, wuk_ref, wuv_ref, bias_ref,
                o_ref, ckv_ref, ckvt_ref, ikn_ref, key_ref, mask_ref, m_ref, l_ref, acc_ref, *, seq, n_keep, n_bias):
    tq, kc, nh, dh = DSA_QB, DSA_KC, DSA_HEADS, DSA_DH
    sub = kc // LANES
    qb = pl.program_id(1)
    t0 = qb * tq
    n_kc = (t0 + tq - 1) // kc + 1
    n_rest = (seq - n_kc * kc).astype(F32)

    @pl.when(qb == 0)
    def _():
        for c in range(seq // kc):
            rows = slice(c * kc, (c + 1) * kc)
            x = dkv_ref[rows, :]
            cn = x * lax.rsqrt(jnp.mean(x * x, axis=-1, keepdims=True) + RMS_EPS) * kvg_ref[...]
            ckv_ref[rows, :] = cn.astype(BF16)
            ckvt_ref[:, rows] = cn.T.astype(BF16)
            ik = ikw_ref[rows, 0:IDX_DIM]
            xc = ik - jnp.mean(ik, axis=-1, keepdims=True)
            ikn = xc * lax.rsqrt(jnp.mean(xc * xc, axis=-1, keepdims=True) + RMS_EPS) * lnw_ref[...] + lnb_ref[...]
            ikn_ref[rows, :] = ikn.astype(BF16)

    t_row = t0 + lax.broadcasted_iota(jnp.int32, (1, tq), 1)
    key_off = lax.broadcasted_iota(jnp.int32, (kc, tq), 0)

    def causal_of(c0):
        return (c0 + key_off) <= t_row

    iq_t = iq_ref[...].astype(F32).T
    iq_cat = jnp.concatenate([iq_t[h * IDX_DIM:(h + 1) * IDX_DIM, :] for h in range(IDX_HEADS)], axis=1).astype(BF16)
    iw_t = iwq_ref[...].T[IDX_DIM:IDX_DIM + IDX_HEADS, :] * (IDX_HEADS ** -0.5)
    iw_t = iw_t * (IDX_DIM ** -0.5)

    def score_chunk(c, carry):
        c0 = pl.multiple_of(c * kc, kc)
        s_all = _dot(ikn_ref[pl.ds(c0, kc), :], iq_cat)
        sc = jnp.zeros((kc, tq), F32)
        for h in range(IDX_HEADS):
            sc = sc + iw_t[h:h + 1, :] * jnp.maximum(s_all[:, h * tq:(h + 1) * tq], 0.0)
        sc = jnp.where(causal_of(c0), sc, NEG_BIG)
        sc = jnp.where(sc == 0.0, 0.0, sc)
        key_ref[pl.ds(c0, kc), :] = _sort_key(sc)
        return carry

    lax.fori_loop(0, n_kc, score_chunk, 0)

    def count(pred_fn):
        def body(c, tot):
            c0 = pl.multiple_of(c * kc, kc)
            hit = pred_fn(key_ref[pl.ds(c0, kc), :], c0)
            return tot + _slab(jnp.where(hit, 1.0, 0.0), jnp.sum)
        part = lax.fori_loop(0, n_kc, body, jnp.zeros((ROW_SLAB, tq), F32))
        return _reduce_rows(part, jnp.sum)

    def bit_step(k, base):
        cand = base + jnp.left_shift(jnp.int32(1), 31 - k)
        cnt = count(lambda keys, c0: keys >= cand) + jnp.where(cand <= KEY_NEG, n_rest, 0.0)
        return jnp.where(cnt >= n_keep, cand, base)

    tau = lax.fori_loop(0, 32, bit_step, jnp.full((1, tq), jnp.iinfo(jnp.int32).min, jnp.int32))

    c_gt = count(lambda keys, c0: keys > tau) + jnp.where(tau < KEY_NEG, n_rest, 0.0)
    need = n_keep - c_gt
    n_eq = count(lambda keys, c0: (keys == tau) & causal_of(c0))
    tie_break = jnp.max(n_eq - need) > 0.0

    @pl.when(jnp.logical_not(tie_break))
    def _():
        def body(c, carry):
            c0 = pl.multiple_of(c * kc, kc)
            keep = (key_ref[pl.ds(c0, kc), :] >= tau) & causal_of(c0)
            mask_ref[pl.ds(c0, kc), :] = jnp.where(keep, 0.0, NEG_BIG)
            return carry
        lax.fori_loop(0, n_kc, body, 0)

    @pl.when(tie_break)
    def _():
        lower = jnp.where(lax.broadcasted_iota(jnp.int32, (kc, kc), 1)
                          < lax.broadcasted_iota(jnp.int32, (kc, kc), 0), 1.0, 0.0).astype(BF16)

        def body(c, seen):
            c0 = pl.multiple_of(c * kc, kc)
            keys = key_ref[pl.ds(c0, kc), :]
            eq = keys == tau
            eqf = jnp.where(eq, 1.0, 0.0)
            rank = seen + _dot(lower, eqf.astype(BF16))
            keep = ((keys > tau) | (eq & (rank < need))) & causal_of(c0)
            mask_ref[pl.ds(c0, kc), :] = jnp.where(keep, 0.0, NEG_BIG)
            return seen + _reduce_rows(eqf, jnp.sum)
        lax.fori_loop(0, n_kc, body, jnp.zeros((1, tq), F32))

    scale = dh ** -0.5
    qa_cat = jnp.concatenate(
        [_dot(wuk_ref[h], _t_bf16(dq_ref[:, h * dh:(h + 1) * dh])) for h in range(nh)], axis=1).astype(BF16)
    m_ref[...] = jnp.full(m_ref.shape, -jnp.inf, F32)
    l_ref[...] = jnp.zeros(l_ref.shape, F32)
    acc_ref[...] = jnp.zeros(acc_ref.shape, F32)

    def att_chunk(c, carry):
        c0 = pl.multiple_of(c * kc, kc)
        ckv = ckv_ref[pl.ds(c0, kc), :]
        msk = mask_ref[pl.ds(c0, kc), :]
        bias = jnp.concatenate([bias_ref[jnp.clip(qb - (c * sub + g), 0, n_bias - 1)] for g in range(sub)], axis=0)
        s = _dot(ckv, qa_cat) * scale + bias + jnp.concatenate([msk] * nh, axis=1)
        m_old = m_ref[...]
        m_new = jnp.maximum(m_old, _reduce_rows(s, jnp.max))
        alpha = jnp.exp(m_old - m_new)
        p = jnp.exp(s - m_new)
        l_ref[...] = alpha * l_ref[...] + _reduce_rows(p, jnp.sum)
        acc_ref[...] = alpha * acc_ref[...] + _dot(ckvt_ref[:, pl.ds(c0, kc)], p.astype(BF16))
        m_ref[...] = m_new
        return carry

    lax.fori_loop(0, n_kc, att_chunk, 0)

    o_lat_t = (acc_ref[...] / l_ref[...]).astype(BF16)
    for h in range(nh):
        o_ref[:, h * dh:(h + 1) * dh] = _dot(wuv_ref[h], o_lat_t[:, h * tq:(h + 1) * tq]).T.astype(o_ref.dtype)


def _dsa(proj_f, proj_b, kv_gain, ln_w, ln_b, w_uk_h, w_uv_h, bias_tiles, batch, seq):
    tq = DSA_QB
    nq = seq // tq
    n_keep = min(DSA_TOPK_MAX, seq // 4)
    n_bias = bias_tiles.shape[0]
    return pl.pallas_call(
        functools.partial(_dsa_kernel, seq=seq, n_keep=float(n_keep), n_bias=n_bias),
        out_shape=jax.ShapeDtypeStruct((batch * seq, HEAD_W), BF16),
        grid=(batch, nq),
        in_specs=[
            pl.BlockSpec((tq, HEAD_W), lambda b, i: (b * nq + i, BCOL_DQ // HEAD_W)),
            pl.BlockSpec((tq, HEAD_W), lambda b, i: (b * nq + i, BCOL_IQ // HEAD_W)),
            pl.BlockSpec((tq, LANES), lambda b, i: (b * nq + i, FCOL_IKW // LANES)),
            pl.BlockSpec((seq, DSA_KV_RANK), lambda b, i: (b, FCOL_DKV // DSA_KV_RANK)),
            pl.BlockSpec((seq, LANES), lambda b, i: (b, FCOL_IKW // LANES)),
            pl.BlockSpec((1, DSA_KV_RANK), lambda b, i: (0, 0)),
            pl.BlockSpec((1, IDX_DIM), lambda b, i: (0, 0)),
            pl.BlockSpec((1, IDX_DIM), lambda b, i: (0, 0)),
            pl.BlockSpec(w_uk_h.shape, lambda b, i: (0, 0, 0)),
            pl.BlockSpec(w_uv_h.shape, lambda b, i: (0, 0, 0)),
            pl.BlockSpec(bias_tiles.shape, lambda b, i: (0, 0, 0)),
        ],
        out_specs=pl.BlockSpec((tq, HEAD_W), lambda b, i: (b * nq + i, 0)),
        scratch_shapes=[
            pltpu.VMEM((seq, DSA_KV_RANK), BF16),
            pltpu.VMEM((DSA_KV_RANK, seq), BF16),
            pltpu.VMEM((seq, IDX_DIM), BF16),
            pltpu.VMEM((seq, tq), jnp.int32),
            pltpu.VMEM((seq, tq), F32),
            pltpu.VMEM((1, DSA_HEADS * tq), F32),
            pltpu.VMEM((1, DSA_HEADS * tq), F32),
            pltpu.VMEM((DSA_KV_RANK, DSA_HEADS * tq), F32),
        ],
        compiler_params=_cparams(("parallel", "arbitrary")),
        name="dsa",
    )(proj_b, proj_b, proj_f, proj_f, proj_f, kv_gain, ln_w, ln_b, w_uk_h, w_uv_h, bias_tiles)


def _merge_kernel(x_ref, ya_ref, yb_ref, yc_ref, g0_ref, g1_ref, g2_ref, wa_ref, wb_ref, wc_ref, wo_ref, gain_ref,
                  o_ref):
    merged = (_sigmoid(g0_ref[...]) * _dot(ya_ref[...], wa_ref[...])
              + _sigmoid(g1_ref[...]) * _dot(yb_ref[...], wb_ref[...])
              + _sigmoid(g2_ref[...]) * _dot(yc_ref[...], wc_ref[...]))
    z = _dot(merged.astype(BF16), wo_ref[...])
    zn = z * lax.rsqrt(jnp.mean(z * z, axis=-1, keepdims=True) + RMS_EPS) * gain_ref[...]
    o_ref[...] = x_ref[...] + zn


def _merge(x2, ya, yb, yc, proj_f, wa, wb, wc, wo, gain, tm):
    n = x2.shape[0]
    gcol = FCOL_GATES // D_MODEL

    def rows(w):
        return pl.BlockSpec((tm, w), lambda i: (i, 0))

    def gate(k):
        return pl.BlockSpec((tm, D_MODEL), lambda i, k=k: (i, gcol + k))

    def full(a):
        return pl.BlockSpec(a.shape, lambda i: (0, 0))

    return pl.pallas_call(
        _merge_kernel,
        out_shape=jax.ShapeDtypeStruct((n, D_MODEL), F32),
        grid=(n // tm,),
        in_specs=[rows(D_MODEL), rows(HEAD_W), rows(HEAD_W), rows(HEAD_W), gate(0), gate(1), gate(2),
                  full(wa), full(wb), full(wc), full(wo), full(gain)],
        out_specs=rows(D_MODEL),
        compiler_params=_cparams(("parallel",)),
        name="merge",
    )(x2, ya, yb, yc, proj_f, proj_f, proj_f, wa, wb, wc, wo, gain)


def _ffn_kernel(x_ref, g1_ref, wg_ref, wu_ref, wd_ref, g2_ref, o_ref, hn_ref, acc_ref):
    j = pl.program_id(1)

    @pl.when(j == 0)
    def _():
        x = x_ref[...]
        hn_ref[...] = (x * lax.rsqrt(jnp.mean(x * x, axis=-1, keepdims=True) + RMS_EPS) * g1_ref[...]).astype(BF16)
        acc_ref[...] = jnp.zeros_like(acc_ref)

    hn = hn_ref[...]
    gate = _dot(hn, wg_ref[...])
    up = _dot(hn, wu_ref[...])
    act = (gate * _sigmoid(gate)) * up
    acc_ref[...] += _dot(act.astype(BF16), wd_ref[...])

    @pl.when(j == pl.num_programs(1) - 1)
    def _():
        y = acc_ref[...]
        yn = y * lax.rsqrt(jnp.mean(y * y, axis=-1, keepdims=True) + RMS_EPS) * g2_ref[...]
        o_ref[...] = x_ref[...] + yn


def _ffn(x2, g1, w_in, w_out, g2, tm, tf):
    n = x2.shape[0]
    nf = D_FF // tf
    return pl.pallas_call(
        _ffn_kernel,
        out_shape=jax.ShapeDtypeStruct((n, D_MODEL), F32),
        grid=(n // tm, nf),
        in_specs=[
            pl.BlockSpec((tm, D_MODEL), lambda i, j: (i, 0)),
            pl.BlockSpec((1, D_MODEL), lambda i, j: (0, 0)),
            pl.BlockSpec((D_MODEL, tf), lambda i, j: (0, j)),
            pl.BlockSpec((D_MODEL, tf), lambda i, j: (0, nf + j)),
            pl.BlockSpec((tf, D_MODEL), lambda i, j: (j, 0)),
            pl.BlockSpec((1, D_MODEL), lambda i, j: (0, 0)),
        ],
        out_specs=pl.BlockSpec((tm, D_MODEL), lambda i, j: (i, 0)),
        scratch_shapes=[pltpu.VMEM((tm, D_MODEL), BF16), pltpu.VMEM((tm, D_MODEL), F32)],
        compiler_params=_cparams(("parallel", "arbitrary")),
        name="ffn",
    )(x2, g1, w_in, w_in, w_out, g2)


def _pack_w_in(w):
    o_mq, o_dq, o_dkv, o_iq, o_ik, o_gates = 2048, 3584, 4096, 4352, 4864, 4936
    z = lambda k: jnp.zeros((w.shape[0], k), w.dtype)
    w_f = jnp.concatenate([
        w[:, :o_mq],
        w[:, o_gates:],
        w[:, o_dkv:o_iq],
        w[:, o_ik:o_gates],
        z(PROJ_F_W - FCOL_IKW - (o_gates - o_ik)),
    ], axis=1)
    w_b = jnp.concatenate([
        w[:, o_mq:o_dq],
        w[:, o_dq:o_dkv],
        w[:, o_iq:o_ik],
    ], axis=1)
    assert w_f.shape[1] == PROJ_F_W and w_b.shape[1] == PROJ_B_W
    return w_f.astype(BF16), w_b.astype(BF16)


def _tiles(batch, seq):
    n = batch * seq
    return dict(
        proj_tm=min(1024, n), proj_f_tn=512, proj_b_tn=1280,
        hgrn_rows=min(512, seq),
        merge_tm=min(512, n),
        ffn_tm=min(512, n), ffn_tf=D_FF // 2,
    )


def kernel(x, w_in, pre_mix_norm, post_mix_norm, pre_ffn_norm, post_ffn_norm, hg_lb_logits, hg_norm_w, dsa_kv_norm,
           dsa_w_uk, dsa_w_uv, idx_k_norm_w, idx_k_norm_b, rel_bias, w_branch_a, w_branch_b, w_branch_c, w_out,
           w_ffn_in, w_ffn_out):
    batch, seq, _ = x.shape
    depth = w_in.shape[0]
    t = _tiles(batch, seq)
    lbp = jax.nn.softmax(hg_lb_logits.astype(F32), axis=0)
    lower_bounds = jnp.cumsum(lbp, axis=0) - lbp[0]
    bias_b = _bias_tiles_t(rel_bias[:, :MOBA_HEADS], MOBA_BLOCK)
    bias_c = _bias_tiles_t(rel_bias[:, MOBA_HEADS:], LANES)
    bias_c = bias_c.transpose(1, 2, 0, 3).reshape(bias_c.shape[1], LANES, DSA_HEADS * LANES)
    row = lambda a: a.reshape(1, -1).astype(F32)

    x2 = x.reshape(batch * seq, D_MODEL)
    for l in range(depth):
        w_f, w_b = _pack_w_in(w_in[l])
        gain = row(pre_mix_norm[l])
        proj_f = _proj(x2, gain, w_f, F32, t["proj_tm"], t["proj_f_tn"], "proj_f")
        proj_b = _proj(x2, gain, w_b, BF16, t["proj_tm"], t["proj_b_tn"], "proj_b")
        y_a = _hgrn(proj_f, row(lower_bounds[l]), row(hg_norm_w[l]), batch, seq, t["hgrn_rows"])
        y_b = _moba(proj_b, bias_b, batch, seq)
        y_c = _dsa(proj_f, proj_b, row(dsa_kv_norm[l]), row(idx_k_norm_w[l]), row(idx_k_norm_b[l]),
                   jnp.transpose(dsa_w_uk[l], (1, 0, 2)).astype(BF16),
                   jnp.transpose(dsa_w_uv[l], (1, 2, 0)).astype(BF16),
                   bias_c, batch, seq)
        x2 = _merge(x2, y_a, y_b, y_c, proj_f, w_branch_a[l].astype(BF16), w_branch_b[l].astype(BF16),
                    w_branch_c[l].astype(BF16), w_out[l].astype(BF16), row(post_mix_norm[l]), t["merge_tm"])
        x2 = _ffn(x2, row(pre_ffn_norm[l]), w_ffn_in[l].astype(BF16), w_ffn_out[l].astype(BF16),
                  row(post_ffn_norm[l]), t["ffn_tm"], t["ffn_tf"])
    return x2.reshape(batch, seq, D_MODEL)
```

```python
import functools
import math

import numpy as np
import jax
import jax.numpy as jnp
from jax import lax
from jax.experimental import pallas as pl
from jax.experimental.pallas import tpu as pltpu

F32 = jnp.float32
BF16 = jnp.bfloat16

D_MODEL = 1024
RMS_EPS = 1e-6
NEG_BIG = -1e30
TINY = 1e-20
HG_HEADS = 4
HG_D = 128
HG_CHUNK = 64
MOBA_HEADS = 4
MOBA_DH = 128
MOBA_BLOCK = 256
MOBA_TOPK = 3
DSA_HEADS = 4
DSA_DH = 128
DSA_KV_RANK = 256
IDX_HEADS = 8
IDX_DIM = 64
DSA_TOPK_MAX = 256
DSA_KC = 512
REL_BUCKETS = 32
REL_MAX_DIST = 2048
D_FF = 2816
HEAD_W = 512

LANES = 128
SUBLANES = 8
DSA_QB = LANES

FCOL_HQ, FCOL_HF, FCOL_HI, FCOL_HG = 0, 512, 1024, 1536
FCOL_GATES, FCOL_DKV, FCOL_IKW = 2048, 5120, 5376
PROJ_F_W = 5632
BCOL_MQ, BCOL_MK, BCOL_MV, BCOL_DQ, BCOL_IQ = 0, 512, 1024, 1536, 2048
PROJ_B_W = 2560

VMEM_LIMIT = 56 * 1024 * 1024

_REL_EXACT = REL_BUCKETS // 2
REL_FAR_DIST = int(math.ceil(_REL_EXACT * (REL_MAX_DIST / _REL_EXACT) ** ((REL_BUCKETS - _REL_EXACT - 1) / (REL_BUCKETS - _REL_EXACT)))) + 1

_NEG_BITS = int(np.array(NEG_BIG, np.float32).view(np.int32))
KEY_NEG = int(np.int32(_NEG_BITS ^ 0x7FFFFFFF))
I16_MIN, I16_MAX = -32768, 32767
KEY_NEG_HI = KEY_NEG >> 16
KEY_NEG_LO = (KEY_NEG & 0xFFFF) + I16_MIN


def _cparams(sem):
    return pltpu.CompilerParams(dimension_semantics=sem, vmem_limit_bytes=VMEM_LIMIT)


def _sigmoid(x):
    return 1.0 / (1.0 + jnp.exp(-x))


def _dot(a, b):
    return jnp.dot(a, b, preferred_element_type=F32)


def _dot_nt(a, b):
    return lax.dot_general(a, b, (((1,), (1,)), ((), ())), preferred_element_type=F32)


def _t_bf16(a):
    return a.astype(F32).T.astype(BF16)


ROW_SLAB = 64
HALF_SLAB = 128


def _slab(x, fn):
    return fn(x.reshape(x.shape[0] // ROW_SLAB, ROW_SLAB, x.shape[1]), axis=0)


def _reduce_rows(x, fn):
    y = _slab(x, fn)
    z = fn(y.reshape(ROW_SLAB // SUBLANES, SUBLANES, y.shape[1]), axis=0)
    return fn(z, axis=0, keepdims=True)


def _proj_kernel(x_ref, g_ref, w_ref, o_ref, xn_ref):
    @pl.when(pl.program_id(1) == 0)
    def _():
        x = x_ref[...]
        ms = jnp.mean(x * x, axis=-1, keepdims=True)
        xn_ref[...] = (x * lax.rsqrt(ms + RMS_EPS) * g_ref[...]).astype(BF16)

    o_ref[...] = _dot(xn_ref[...], w_ref[...]).astype(o_ref.dtype)


def _proj(x2, gain, w_packed, out_dtype, tm, tn, name):
    n = x2.shape[0]
    width = w_packed.shape[1]
    return pl.pallas_call(
        _proj_kernel,
        out_shape=jax.ShapeDtypeStruct((n, width), out_dtype),
        grid=(n // tm, width // tn),
        in_specs=[
            pl.BlockSpec((tm, D_MODEL), lambda i, j: (i, 0)),
            pl.BlockSpec((1, D_MODEL), lambda i, j: (0, 0)),
            pl.BlockSpec((D_MODEL, tn), lambda i, j: (0, j)),
        ],
        out_specs=pl.BlockSpec((tm, tn), lambda i, j: (i, j)),
        scratch_shapes=[pltpu.VMEM((tm, D_MODEL), BF16)],
        compiler_params=_cparams(("parallel", "arbitrary")),
        name=name,
    )(x2, gain, w_packed)


HG_LEVELS = (32, 16, 8, 4, 2, 1)


def _hgrn_constants():
    c = HG_CHUNK
    t = np.arange(c)
    i = t[:, None]
    tt = t[None, :]
    rows = [tt <= i]
    masks = []
    for m in HG_LEVELS:
        mid = ((t // (2 * m)) * (2 * m) + m)[:, None]
        upper = (i >= mid) & (tt >= mid) & (tt <= i)
        lower = (i < mid) & (tt > i) & (tt <= mid - 1)
        rows.append(upper | lower)
        same = (t[:, None] // (2 * m)) == (t[None, :] // (2 * m))
        masks.append(same & (i >= mid) & (tt < mid))
    masks.append(np.eye(c, dtype=bool))
    m_all = np.concatenate(rows, axis=0).astype(np.float32)
    return m_all, np.stack(masks).astype(np.float32)


def _hgrn_kernel(q_ref, f_ref, i_ref, g_ref, lb_ref, nw_ref, mall_ref, mask_ref, o_ref, st_ref, e_ref, *, n_chunks):
    c = HG_CHUNK
    nl = len(HG_LEVELS)

    @pl.when(pl.program_id(1) == 0)
    def _():
        st_ref[...] = jnp.zeros_like(st_ref)

    lb = lb_ref[...]
    nw = nw_ref[...]

    def chunk(ci, carry):
        r0 = pl.multiple_of(ci * c, c)
        fp = f_ref[pl.ds(r0, c), :]
        f = lb + (1.0 - lb) * _sigmoid(fp)
        logf = jnp.log(jnp.maximum(f, TINY))
        kk = (1.0 - lb) * _sigmoid(-fp)
        g_hi = logf.astype(BF16)
        g_lo = (logf - g_hi.astype(F32)).astype(BF16)
        e_ref[...] = _dot(mall_ref[...], jnp.concatenate([g_hi, g_lo], axis=1))
        q = q_ref[pl.ds(r0, c), :]
        v = i_ref[pl.ds(r0, c), :]
        og = g_ref[pl.ds(r0, c), :]
        for h in range(HG_HEADS):
            lo, hi = h * HG_D, (h + 1) * HG_D

            def expo(row):
                return e_ref[row * c:(row + 1) * c, lo:hi] + e_ref[row * c:(row + 1) * c, HEAD_W + lo:HEAD_W + hi]

            qh, kh, vh = q[:, lo:hi], kk[:, lo:hi], v[:, lo:hi]
            b = expo(0)
            s = jnp.zeros((c, c), F32)
            for l in range(nl + 1):
                if l < nl:
                    decay = jnp.exp(expo(1 + l))
                    ql, kl = qh * decay, kh * decay
                else:
                    ql, kl = qh, kh
                sl = _dot_nt(ql.astype(BF16), kl.astype(BF16))
                s = s + jnp.where(mask_ref[l] > 0.0, sl, 0.0)
            st = st_ref[h]
            o = _dot_nt((qh * jnp.exp(b)).astype(BF16), st.astype(BF16)) + _dot(s.astype(BF16), vh.astype(BF16))
            b_last = b[c - 1:c, :]
            kdec = kh * jnp.exp(b_last - b)
            st_ref[h] = st * jnp.exp(b_last) + _dot(vh.T.astype(BF16), kdec.astype(BF16))
            y = o * lax.rsqrt(jnp.mean(o * o, axis=-1, keepdims=True) + RMS_EPS) * nw
            gh = og[:, lo:hi]
            o_ref[pl.ds(r0, c), lo:hi] = (y * (gh * _sigmoid(gh))).astype(o_ref.dtype)
        return carry

    lax.fori_loop(0, n_chunks, chunk, 0)


def _hgrn(proj_f, lb, norm_w, batch, seq, rows):
    m_all, masks = _hgrn_constants()
    n_rb = seq // rows
    n_chunks = rows // HG_CHUNK

    def col(c0):
        return pl.BlockSpec((rows, HEAD_W), lambda b, r, cidx=c0 // HEAD_W: (b * n_rb + r, cidx))

    return pl.pallas_call(
        functools.partial(_hgrn_kernel, n_chunks=n_chunks),
        out_shape=jax.ShapeDtypeStruct((batch * seq, HEAD_W), BF16),
        grid=(batch, n_rb),
        in_specs=[
            col(FCOL_HQ), col(FCOL_HF), col(FCOL_HI), col(FCOL_HG),
            pl.BlockSpec((1, HEAD_W), lambda b, r: (0, 0)),
            pl.BlockSpec((1, HG_D), lambda b, r: (0, 0)),
            pl.BlockSpec(m_all.shape, lambda b, r: (0, 0)),
            pl.BlockSpec(masks.shape, lambda b, r: (0, 0, 0)),
        ],
        out_specs=pl.BlockSpec((rows, HEAD_W), lambda b, r: (b * n_rb + r, 0)),
        scratch_shapes=[
            pltpu.VMEM((HG_HEADS, HG_D, HG_D), F32),
            pltpu.VMEM((m_all.shape[0], 2 * HEAD_W), F32),
        ],
        compiler_params=_cparams(("parallel", "arbitrary")),
        name="hgrn",
    )(proj_f, proj_f, proj_f, proj_f, lb, norm_w, jnp.asarray(m_all, BF16), jnp.asarray(masks, F32))


def _rel_bucket(dist):
    n = jnp.maximum(dist, 0)
    exact = REL_BUCKETS // 2
    nf = jnp.maximum(n, exact).astype(F32)
    large = exact + (jnp.log(nf / exact) / math.log(REL_MAX_DIST / exact) * (REL_BUCKETS - exact)).astype(jnp.int32)
    large = jnp.minimum(large, REL_BUCKETS - 1)
    return jnp.where(n < exact, n, large)


def _n_bias_tiles(tile):
    d = 0
    while d * tile - (tile - 1) < REL_FAR_DIST:
        d += 1
    return d + 1


def _bias_tiles_t(table, tile):
    n = _n_bias_tiles(tile)
    length = n * tile
    by_dist = table[_rel_bucket(jnp.arange(length, dtype=jnp.int32))].astype(F32).T
    heads = by_dist.shape[0]
    w = jnp.concatenate([by_dist, jnp.zeros((heads, 1), F32)], axis=1)
    skew = jnp.broadcast_to(w[:, None, :], (heads, tile, length + 1)).reshape(heads, tile * (length + 1))
    skew = skew[:, :tile * length].reshape(heads, tile, length)
    return skew.reshape(heads, tile, n, tile).transpose(0, 2, 1, 3)


def _moba_kernel(q_ref, k_ref, v_ref, bias_ref, o_ref, vt_ref, km_ref, sel_ref, m_ref, l_ref, acc_ref,
                 *, n_blocks, n_sel, n_bias):
    blk, dh, nh = MOBA_BLOCK, MOBA_DH, MOBA_HEADS
    qb = pl.program_id(1)
    scale = dh ** -0.5

    @pl.when(qb == 0)
    def _():
        for h in range(nh):
            for n in range(n_blocks):
                kblk = k_ref[n * blk:(n + 1) * blk, h * dh:(h + 1) * dh].astype(F32)
                km_ref[h, n:n + 1, :] = jnp.mean(kblk, axis=0, keepdims=True)
                vt_ref[h * dh:(h + 1) * dh, n * blk:(n + 1) * blk] = _t_bf16(v_ref[n * blk:(n + 1) * blk, h * dh:(h + 1) * dh])

    q_t = []
    n_iota = lax.broadcasted_iota(jnp.int32, (n_blocks, blk), 0)
    for h in range(nh):
        qt = _t_bf16(q_ref[:, h * dh:(h + 1) * dh])
        q_t.append(qt)
        gate = _dot(km_ref[h].astype(BF16), qt)
        gate = jnp.where(n_iota < qb, gate, NEG_BIG)
        rank = jnp.zeros(gate.shape, F32)
        for n in range(n_blocks):
            row = gate[n:n + 1, :]
            beats = (row > gate) | ((row == gate) & (n < n_iota))
            rank = rank + jnp.where(beats, 1.0, 0.0)
        sel_ref[h] = jnp.where((rank < n_sel) & (n_iota < qb), 0.0, NEG_BIG)
        m_ref[h] = jnp.full((1, blk), -jnp.inf, F32)
        l_ref[h] = jnp.zeros((1, blk), F32)
        acc_ref[h] = jnp.zeros((dh, blk), F32)

    def attend(r0, nk, mask_fn):
        ss = [_dot(k_ref[pl.ds(r0, nk), h * dh:(h + 1) * dh], q_t[h]) for h in range(nh)]
        ps, alphas = [], []
        for h in range(nh):
            s = ss[h] * scale + mask_fn(h)
            m_old = m_ref[h]
            m_new = jnp.maximum(m_old, _reduce_rows(s, jnp.max))
            alphas.append(jnp.exp(m_old - m_new))
            p = jnp.exp(s - m_new)
            l_ref[h] = alphas[h] * l_ref[h] + _reduce_rows(p, jnp.sum)
            m_ref[h] = m_new
            ps.append(p.astype(BF16))
        for h in range(nh):
            acc_ref[h] = alphas[h] * acc_ref[h] + _dot(vt_ref[h * dh:(h + 1) * dh, pl.ds(r0, nk)], ps[h])

    def past_mask(h, n):
        return bias_ref[h, jnp.minimum(qb - n, n_bias - 1)] + sel_ref[h, pl.ds(n, 1), :]

    causal = jnp.where(lax.broadcasted_iota(jnp.int32, (blk, blk), 0) <= lax.broadcasted_iota(jnp.int32, (blk, blk), 1),
                       0.0, NEG_BIG)

    def own_mask(h):
        return bias_ref[h, 0] + causal

    def past_pair(i, carry):
        n = 2 * i
        attend(pl.multiple_of(n * blk, 2 * blk), 2 * blk,
               lambda h: jnp.concatenate([past_mask(h, n), past_mask(h, n + 1)], axis=0))
        return carry

    n_pairs = qb // 2
    lax.fori_loop(0, n_pairs, past_pair, 0)

    @pl.when(qb % 2 == 1)
    def _():
        attend(pl.multiple_of((qb - 1) * blk, blk), 2 * blk,
               lambda h: jnp.concatenate([past_mask(h, qb - 1), own_mask(h)], axis=0))

    @pl.when(qb % 2 == 0)
    def _():
        attend(pl.multiple_of(qb * blk, blk), blk, own_mask)

    for h in range(nh):
        o_ref[:, h * dh:(h + 1) * dh] = (acc_ref[h] / l_ref[h]).T.astype(o_ref.dtype)


def _moba(proj_b, bias_tiles, batch, seq):
    blk = MOBA_BLOCK
    nb = seq // blk
    n_sel = min(MOBA_TOPK, nb - 1)
    n_bias = bias_tiles.shape[1]
    return pl.pallas_call(
        functools.partial(_moba_kernel, n_blocks=nb, n_sel=n_sel, n_bias=n_bias),
        out_shape=jax.ShapeDtypeStruct((batch * seq, HEAD_W), BF16),
        grid=(batch, nb),
        in_specs=[
            pl.BlockSpec((blk, HEAD_W), lambda b, i: (b * nb + i, BCOL_MQ // HEAD_W)),
            pl.BlockSpec((seq, HEAD_W), lambda b, i: (b, BCOL_MK // HEAD_W)),
            pl.BlockSpec((seq, HEAD_W), lambda b, i: (b, BCOL_MV // HEAD_W)),
            pl.BlockSpec(bias_tiles.shape, lambda b, i: (0, 0, 0, 0)),
        ],
        out_specs=pl.BlockSpec((blk, HEAD_W), lambda b, i: (b * nb + i, 0)),
        scratch_shapes=[
            pltpu.VMEM((HEAD_W, seq), BF16),
            pltpu.VMEM((MOBA_HEADS, nb, MOBA_DH), F32),
            pltpu.VMEM((MOBA_HEADS, nb, blk), F32),
            pltpu.VMEM((MOBA_HEADS, 1, blk), F32),
            pltpu.VMEM((MOBA_HEADS, 1, blk), F32),
            pltpu.VMEM((MOBA_HEADS, MOBA_DH, blk), F32),
        ],
        compiler_params=_cparams(("parallel", "arbitrary")),
        name="moba",
    )(proj_b, proj_b, proj_b, bias_tiles)


def _sort_key(x):
    bits = pltpu.bitcast(x, jnp.int32)
    return bits ^ ((bits >> 31) & 0x7FFFFFFF)


def _split16(key):
    return (key >> 16).astype(jnp.int16), ((key & 0xFFFF) + I16_MIN).astype(jnp.int16)


def _dsa_kernel(dq_ref, iq_ref, iwq_ref, dkv_ref, ikw_ref, kvg_ref, lnw_ref, lnb_ref, wuk_ref, wuv_ref, bias_ref,
                o_ref, ckv_ref, ckvt_ref, ikn_ref, key_ref, hi_ref, lo_ref, lo2_ref, mask_ref, m_ref, l_ref, acc_ref,
                *, seq, n_keep, n_bias):
    tq, kc, nh, dh = DSA_QB, DSA_KC, DSA_HEADS, DSA_DH
    sub = kc // LANES
    qb = pl.program_id(1)
    t0 = qb * tq
    n_kc = (t0 + tq - 1) // kc + 1
    n_rest = (seq - n_kc * kc).astype(F32)

    @pl.when(qb == 0)
    def _():
        for c in range(seq // kc):
            rows = slice(c * kc, (c + 1) * kc)
            x = dkv_ref[rows, :]
            cn = x * lax.rsqrt(jnp.mean(x * x, axis=-1, keepdims=True) + RMS_EPS) * kvg_ref[...]
            ckv_ref[rows, :] = cn.astype(BF16)
            ckvt_ref[:, rows] = cn.T.astype(BF16)
            ik = ikw_ref[rows, 0:IDX_DIM]
            xc = ik - jnp.mean(ik, axis=-1, keepdims=True)
            ikn = xc * lax.rsqrt(jnp.mean(xc * xc, axis=-1, keepdims=True) + RMS_EPS) * lnw_ref[...] + lnb_ref[...]
            ikn_ref[rows, :] = ikn.astype(BF16)

    t_row = t0 + lax.broadcasted_iota(jnp.int32, (1, tq), 1)
    key_off = lax.broadcasted_iota(jnp.int32, (kc, tq), 0)

    def causal_of(c0):
        return (c0 + key_off) <= t_row

    iq_t = iq_ref[...].astype(F32).T
    iq_cat = jnp.concatenate([iq_t[h * IDX_DIM:(h + 1) * IDX_DIM, :] for h in range(IDX_HEADS)], axis=1).astype(BF16)
    iw_t = iwq_ref[...].T[IDX_DIM:IDX_DIM + IDX_HEADS, :] * (IDX_HEADS ** -0.5)
    iw_t = iw_t * (IDX_DIM ** -0.5)

    def score_chunk(c, carry):
        c0 = pl.multiple_of(c * kc, kc)
        s_all = _dot(ikn_ref[pl.ds(c0, kc), :], iq_cat)
        sc = jnp.zeros((kc, tq), F32)
        for h in range(IDX_HEADS):
            sc = sc + iw_t[h:h + 1, :] * jnp.maximum(s_all[:, h * tq:(h + 1) * tq], 0.0)
        sc = jnp.where(causal_of(c0), sc, NEG_BIG)
        sc = jnp.where(sc == 0.0, 0.0, sc)
        key = _sort_key(sc)
        key_ref[pl.ds(c0, kc), :] = key
        hi_ref[pl.ds(c0, kc), :], lo_ref[pl.ds(c0, kc), :] = _split16(key)
        return carry

    lax.fori_loop(0, n_kc, score_chunk, 0)

    def count(pred_fn):
        def body(c, tot):
            c0 = pl.multiple_of(c * kc, kc)
            hit = pred_fn(key_ref[pl.ds(c0, kc), :], c0)
            return tot + _slab(jnp.where(hit, 1.0, 0.0), jnp.sum)
        part = lax.fori_loop(0, n_kc, body, jnp.zeros((ROW_SLAB, tq), F32))
        return _reduce_rows(part, jnp.sum)

    one16, zero16 = jnp.ones((), BF16), jnp.zeros((), BF16)

    def count16(ref, cand):
        cand16 = cand.astype(jnp.int16)

        def body(c, tot):
            c0 = pl.multiple_of(c * kc, kc)
            hit = jnp.where(ref[pl.ds(c0, kc), :] >= cand16, one16, zero16)
            parts = [hit[r * HALF_SLAB:(r + 1) * HALF_SLAB] for r in range(kc // HALF_SLAB)]
            while len(parts) > 1:
                parts = [parts[i] + parts[i + 1] for i in range(0, len(parts), 2)]
            return tot + parts[0]
        part = lax.fori_loop(0, n_kc, body, jnp.zeros((HALF_SLAB, tq), BF16))
        return _reduce_rows(part.astype(F32), jnp.sum)

    def greedy16(count_fn):
        def step(k, base):
            cand = base + jnp.left_shift(jnp.int32(1), 15 - k)
            return jnp.where(count_fn(cand) >= n_keep, cand, base)
        return lax.fori_loop(0, 16, step, jnp.full((1, tq), I16_MIN, jnp.int32))

    tau_hi = greedy16(lambda cand: count16(hi_ref, cand) + jnp.where(cand <= KEY_NEG_HI, n_rest, 0.0))
    above = jnp.where(tau_hi < I16_MAX, count16(hi_ref, jnp.minimum(tau_hi + 1, I16_MAX)), 0.0)
    above = above + jnp.where(KEY_NEG_HI > tau_hi, n_rest, 0.0)
    tau_hi16 = tau_hi.astype(jnp.int16)

    def low_of_equal(c, carry):
        c0 = pl.multiple_of(c * kc, kc)
        lo2_ref[pl.ds(c0, kc), :] = jnp.where(hi_ref[pl.ds(c0, kc), :] == tau_hi16, lo_ref[pl.ds(c0, kc), :],
                                              jnp.int16(I16_MIN))
        return carry

    lax.fori_loop(0, n_kc, low_of_equal, 0)
    rest_same_hi = jnp.where(tau_hi == KEY_NEG_HI, n_rest, 0.0)
    tau_lo = greedy16(lambda cand: above + count16(lo2_ref, cand) + jnp.where(cand <= KEY_NEG_LO, rest_same_hi, 0.0))
    tau = tau_hi * 65536 + (tau_lo - I16_MIN)

    c_gt = count(lambda keys, c0: keys > tau) + jnp.where(tau < KEY_NEG, n_rest, 0.0)
    need = n_keep - c_gt
    n_eq = count(lambda keys, c0: (keys == tau) & causal_of(c0))
    tie_break = jnp.max(n_eq - need) > 0.0

    @pl.when(jnp.logical_not(tie_break))
    def _():
        def body(c, carry):
            c0 = pl.multiple_of(c * kc, kc)
            keep = (key_ref[pl.ds(c0, kc), :] >= tau) & causal_of(c0)
            mask_ref[pl.ds(c0, kc), :] = jnp.where(keep, 0.0, NEG_BIG)
            return carry
        lax.fori_loop(0, n_kc, body, 0)

    @pl.when(tie_break)
    def _():
        lower = jnp.where(lax.broadcasted_iota(jnp.int32, (kc, kc), 1)
                          < lax.broadcasted_iota(jnp.int32, (kc, kc), 0), 1.0, 0.0).astype(BF16)

        def body(c, seen):
            c0 = pl.multiple_of(c * kc, kc)
            keys = key_ref[pl.ds(c0, kc), :]
            eq = keys == tau
            eqf = jnp.where(eq, 1.0, 0.0)
            rank = seen + _dot(lower, eqf.astype(BF16))
            keep = ((keys > tau) | (eq & (rank < need))) & causal_of(c0)
            mask_ref[pl.ds(c0, kc), :] = jnp.where(keep, 0.0, NEG_BIG)
            return seen + _reduce_rows(eqf, jnp.sum)
        lax.fori_loop(0, n_kc, body, jnp.zeros((1, tq), F32))

    scale = dh ** -0.5
    qa_cat = jnp.concatenate(
        [_dot(wuk_ref[h], _t_bf16(dq_ref[:, h * dh:(h + 1) * dh])) for h in range(nh)], axis=1).astype(BF16)
    m_ref[...] = jnp.full(m_ref.shape, -jnp.inf, F32)
    l_ref[...] = jnp.zeros(l_ref.shape, F32)
    acc_ref[...] = jnp.zeros(acc_ref.shape, F32)

    def attend(c, nc):
        c0 = pl.multiple_of(c * kc, kc)
        nk = nc * kc
        msk = mask_ref[pl.ds(c0, nk), :]
        bias = jnp.concatenate(
            [bias_ref[jnp.clip(qb - (c * sub + g), 0, n_bias - 1)] for g in range(nc * sub)], axis=0)
        s = _dot(ckv_ref[pl.ds(c0, nk), :], qa_cat) * scale + bias + jnp.concatenate([msk] * nh, axis=1)
        m_old = m_ref[...]
        m_new = jnp.maximum(m_old, _reduce_rows(s, jnp.max))
        alpha = jnp.exp(m_old - m_new)
        p = jnp.exp(s - m_new)
        l_ref[...] = alpha * l_ref[...] + _reduce_rows(p, jnp.sum)
        acc_ref[...] = alpha * acc_ref[...] + _dot(ckvt_ref[:, pl.ds(c0, nk)], p.astype(BF16))
        m_ref[...] = m_new

    def att_pair(i, carry):
        attend(2 * i, 2)
        return carry

    lax.fori_loop(0, n_kc // 2, att_pair, 0)

    @pl.when(n_kc % 2 == 1)
    def _():
        attend(n_kc - 1, 1)

    o_lat_t = (acc_ref[...] / l_ref[...]).astype(BF16)
    for h in range(nh):
        o_ref[:, h * dh:(h + 1) * dh] = _dot(wuv_ref[h], o_lat_t[:, h * tq:(h + 1) * tq]).T.astype(o_ref.dtype)


def _dsa(proj_f, proj_b, kv_gain, ln_w, ln_b, w_uk_h, w_uv_h, bias_tiles, batch, seq):
    tq = DSA_QB
    nq = seq // tq
    n_keep = min(DSA_TOPK_MAX, seq // 4)
    n_bias = bias_tiles.shape[0]
    assert seq // HALF_SLAB <= 256, "bf16 hit counts must stay exact"
    return pl.pallas_call(
        functools.partial(_dsa_kernel, seq=seq, n_keep=float(n_keep), n_bias=n_bias),
        out_shape=jax.ShapeDtypeStruct((batch * seq, HEAD_W), BF16),
        grid=(batch, nq),
        in_specs=[
            pl.BlockSpec((tq, HEAD_W), lambda b, i: (b * nq + i, BCOL_DQ // HEAD_W)),
            pl.BlockSpec((tq, HEAD_W), lambda b, i: (b * nq + i, BCOL_IQ // HEAD_W)),
            pl.BlockSpec((tq, LANES), lambda b, i: (b * nq + i, FCOL_IKW // LANES)),
            pl.BlockSpec((seq, DSA_KV_RANK), lambda b, i: (b, FCOL_DKV // DSA_KV_RANK)),
            pl.BlockSpec((seq, LANES), lambda b, i: (b, FCOL_IKW // LANES)),
            pl.BlockSpec((1, DSA_KV_RANK), lambda b, i: (0, 0)),
            pl.BlockSpec((1, IDX_DIM), lambda b, i: (0, 0)),
            pl.BlockSpec((1, IDX_DIM), lambda b, i: (0, 0)),
            pl.BlockSpec(w_uk_h.shape, lambda b, i: (0, 0, 0)),
            pl.BlockSpec(w_uv_h.shape, lambda b, i: (0, 0, 0)),
            pl.BlockSpec(bias_tiles.shape, lambda b, i: (0, 0, 0)),
        ],
        out_specs=pl.BlockSpec((tq, HEAD_W), lambda b, i: (b * nq + i, 0)),
        scratch_shapes=[
            pltpu.VMEM((seq, DSA_KV_RANK), BF16),
            pltpu.VMEM((DSA_KV_RANK, seq), BF16),
            pltpu.VMEM((seq, IDX_DIM), BF16),
            pltpu.VMEM((seq, tq), jnp.int32),
            pltpu.VMEM((seq, tq), jnp.int16),
            pltpu.VMEM((seq, tq), jnp.int16),
            pltpu.VMEM((seq, tq), jnp.int16),
            pltpu.VMEM((seq, tq), F32),
            pltpu.VMEM((1, DSA_HEADS * tq), F32),
            pltpu.VMEM((1, DSA_HEADS * tq), F32),
            pltpu.VMEM((DSA_KV_RANK, DSA_HEADS * tq), F32),
        ],
        compiler_params=_cparams(("parallel", "arbitrary")),
        name="dsa",
    )(proj_b, proj_b, proj_f, proj_f, proj_f, kv_gain, ln_w, ln_b, w_uk_h, w_uv_h, bias_tiles)


def _merge_kernel(x_ref, ya_ref, yb_ref, yc_ref, g0_ref, g1_ref, g2_ref, wa_ref, wb_ref, wc_ref, wo_ref, gain_ref,
                  o_ref):
    merged = (_sigmoid(g0_ref[...]) * _dot(ya_ref[...], wa_ref[...])
              + _sigmoid(g1_ref[...]) * _dot(yb_ref[...], wb_ref[...])
              + _sigmoid(g2_ref[...]) * _dot(yc_ref[...], wc_ref[...]))
    z = _dot(merged.astype(BF16), wo_ref[...])
    zn = z * lax.rsqrt(jnp.mean(z * z, axis=-1, keepdims=True) + RMS_EPS) * gain_ref[...]
    o_ref[...] = x_ref[...] + zn


def _merge(x2, ya, yb, yc, proj_f, wa, wb, wc, wo, gain, tm):
    n = x2.shape[0]
    gcol = FCOL_GATES // D_MODEL

    def rows(w):
        return pl.BlockSpec((tm, w), lambda i: (i, 0))

    def gate(k):
        return pl.BlockSpec((tm, D_MODEL), lambda i, k=k: (i, gcol + k))

    def full(a):
        return pl.BlockSpec(a.shape, lambda i: (0, 0))

    return pl.pallas_call(
        _merge_kernel,
        out_shape=jax.ShapeDtypeStruct((n, D_MODEL), F32),
        grid=(n // tm,),
        in_specs=[rows(D_MODEL), rows(HEAD_W), rows(HEAD_W), rows(HEAD_W), gate(0), gate(1), gate(2),
                  full(wa), full(wb), full(wc), full(wo), full(gain)],
        out_specs=rows(D_MODEL),
        compiler_params=_cparams(("parallel",)),
        name="merge",
    )(x2, ya, yb, yc, proj_f, proj_f, proj_f, wa, wb, wc, wo, gain)


def _ffn_kernel(x_ref, g1_ref, wg_ref, wu_ref, wd_ref, g2_ref, o_ref, hn_ref, acc_ref):
    j = pl.program_id(1)

    @pl.when(j == 0)
    def _():
        x = x_ref[...]
        hn_ref[...] = (x * lax.rsqrt(jnp.mean(x * x, axis=-1, keepdims=True) + RMS_EPS) * g1_ref[...]).astype(BF16)
        acc_ref[...] = jnp.zeros_like(acc_ref)

    hn = hn_ref[...]
    gate = _dot(hn, wg_ref[...])
    up = _dot(hn, wu_ref[...])
    act = (gate * _sigmoid(gate)) * up
    acc_ref[...] += _dot(act.astype(BF16), wd_ref[...])

    @pl.when(j == pl.num_programs(1) - 1)
    def _():
        y = acc_ref[...]
        yn = y * lax.rsqrt(jnp.mean(y * y, axis=-1, keepdims=True) + RMS_EPS) * g2_ref[...]
        o_ref[...] = x_ref[...] + yn


def _ffn(x2, g1, w_in, w_out, g2, tm, tf):
    n = x2.shape[0]
    nf = D_FF // tf
    return pl.pallas_call(
        _ffn_kernel,
        out_shape=jax.ShapeDtypeStruct((n, D_MODEL), F32),
        grid=(n // tm, nf),
        in_specs=[
            pl.BlockSpec((tm, D_MODEL), lambda i, j: (i, 0)),
            pl.BlockSpec((1, D_MODEL), lambda i, j: (0, 0)),
            pl.BlockSpec((D_MODEL, tf), lambda i, j: (0, j)),
            pl.BlockSpec((D_MODEL, tf), lambda i, j: (0, nf + j)),
            pl.BlockSpec((tf, D_MODEL), lambda i, j: (j, 0)),
            pl.BlockSpec((1, D_MODEL), lambda i, j: (0, 0)),
        ],
        out_specs=pl.BlockSpec((tm, D_MODEL), lambda i, j: (i, 0)),
        scratch_shapes=[pltpu.VMEM((tm, D_MODEL), BF16), pltpu.VMEM((tm, D_MODEL), F32)],
        compiler_params=_cparams(("parallel", "arbitrary")),
        name="ffn",
    )(x2, g1, w_in, w_in, w_out, g2)


def _pack_w_in(w):
    o_mq, o_dq, o_dkv, o_iq, o_ik, o_gates = 2048, 3584, 4096, 4352, 4864, 4936
    z = lambda k: jnp.zeros((w.shape[0], k), w.dtype)
    w_f = jnp.concatenate([
        w[:, :o_mq],
        w[:, o_gates:],
        w[:, o_dkv:o_iq],
        w[:, o_ik:o_gates],
        z(PROJ_F_W - FCOL_IKW - (o_gates - o_ik)),
    ], axis=1)
    w_b = jnp.concatenate([
        w[:, o_mq:o_dq],
        w[:, o_dq:o_dkv],
        w[:, o_iq:o_ik],
    ], axis=1)
    assert w_f.shape[1] == PROJ_F_W and w_b.shape[1] == PROJ_B_W
    return w_f.astype(BF16), w_b.astype(BF16)


def _tiles(batch, seq):
    n = batch * seq
    return dict(
        proj_tm=min(1024, n), proj_f_tn=512, proj_b_tn=1280,
        hgrn_rows=min(512, seq),
        merge_tm=min(512, n),
        ffn_tm=min(512, n), ffn_tf=D_FF // 2,
    )


def kernel(x, w_in, pre_mix_norm, post_mix_norm, pre_ffn_norm, post_ffn_norm, hg_lb_logits, hg_norm_w, dsa_kv_norm,
           dsa_w_uk, dsa_w_uv, idx_k_norm_w, idx_k_norm_b, rel_bias, w_branch_a, w_branch_b, w_branch_c, w_out,
           w_ffn_in, w_ffn_out):
    batch, seq, _ = x.shape
    depth = w_in.shape[0]
    t = _tiles(batch, seq)
    lbp = jax.nn.softmax(hg_lb_logits.astype(F32), axis=0)
    lower_bounds = jnp.cumsum(lbp, axis=0) - lbp[0]
    bias_b = _bias_tiles_t(rel_bias[:, :MOBA_HEADS], MOBA_BLOCK)
    bias_c = _bias_tiles_t(rel_bias[:, MOBA_HEADS:], LANES)
    bias_c = bias_c.transpose(1, 2, 0, 3).reshape(bias_c.shape[1], LANES, DSA_HEADS * LANES)
    row = lambda a: a.reshape(1, -1).astype(F32)

    x2 = x.reshape(batch * seq, D_MODEL)
    for l in range(depth):
        w_f, w_b = _pack_w_in(w_in[l])
        gain = row(pre_mix_norm[l])
        proj_f = _proj(x2, gain, w_f, F32, t["proj_tm"], t["proj_f_tn"], "proj_f")
        proj_b = _proj(x2, gain, w_b, BF16, t["proj_tm"], t["proj_b_tn"], "proj_b")
        y_a = _hgrn(proj_f, row(lower_bounds[l]), row(hg_norm_w[l]), batch, seq, t["hgrn_rows"])
        y_b = _moba(proj_b, bias_b, batch, seq)
        y_c = _dsa(proj_f, proj_b, row(dsa_kv_norm[l]), row(idx_k_norm_w[l]), row(idx_k_norm_b[l]),
                   jnp.transpose(dsa_w_uk[l], (1, 0, 2)).astype(BF16),
                   jnp.transpose(dsa_w_uv[l], (1, 2, 0)).astype(BF16),
                   bias_c, batch, seq)
        x2 = _merge(x2, y_a, y_b, y_c, proj_f, w_branch_a[l].astype(BF16), w_branch_b[l].astype(BF16),
                    w_branch_c[l].astype(BF16), w_out[l].astype(BF16), row(post_mix_norm[l]), t["merge_tm"])
        x2 = _ffn(x2, row(pre_ffn_norm[l]), w_ffn_in[l].astype(BF16), w_ffn_out[l].astype(BF16),
                  row(post_ffn_norm[l]), t["ffn_tm"], t["ffn_tf"])
    return x2.reshape(batch, seq, D_MODEL)
```

```python
import functools
import math

import numpy as np
import jax
import jax.numpy as jnp
from jax import lax
from jax.experimental import pallas as pl
from jax.experimental.pallas import tpu as pltpu

F32 = jnp.float32
BF16 = jnp.bfloat16

D_MODEL = 1024
RMS_EPS = 1e-6
NEG_BIG = -1e30
TINY = 1e-20
HG_HEADS = 4
HG_D = 128
HG_CHUNK = 64
MOBA_HEADS = 4
MOBA_DH = 128
MOBA_BLOCK = 256
MOBA_TOPK = 3
DSA_HEADS = 4
DSA_DH = 128
DSA_KV_RANK = 256
IDX_HEADS = 8
IDX_DIM = 64
DSA_TOPK_MAX = 256
DSA_KC = 512
REL_BUCKETS = 32
REL_MAX_DIST = 2048
D_FF = 2816
HEAD_W = 512

LANES = 128
SUBLANES = 8
DSA_QB = LANES

FCOL_HQ, FCOL_HF, FCOL_HI, FCOL_HG = 0, 512, 1024, 1536
FCOL_GATES, FCOL_DKV, FCOL_IKW = 2048, 5120, 5376
PROJ_F_W = 5632
BCOL_MQ, BCOL_MK, BCOL_MV, BCOL_DQ, BCOL_IQ = 0, 512, 1024, 1536, 2048
PROJ_B_W = 2560

VMEM_LIMIT = 56 * 1024 * 1024

_REL_EXACT = REL_BUCKETS // 2
REL_FAR_DIST = int(math.ceil(_REL_EXACT * (REL_MAX_DIST / _REL_EXACT) ** ((REL_BUCKETS - _REL_EXACT - 1) / (REL_BUCKETS - _REL_EXACT)))) + 1

_NEG_BITS = int(np.array(NEG_BIG, np.float32).view(np.int32))
KEY_NEG = int(np.int32(_NEG_BITS ^ 0x7FFFFFFF))
I16_MIN, I16_MAX = -32768, 32767
KEY_NEG_HI = KEY_NEG >> 16
KEY_NEG_LO = (KEY_NEG & 0xFFFF) + I16_MIN


def _cparams(sem):
    return pltpu.CompilerParams(dimension_semantics=sem, vmem_limit_bytes=VMEM_LIMIT)


def _sigmoid(x):
    return 1.0 / (1.0 + jnp.exp(-x))


def _dot(a, b):
    return jnp.dot(a, b, preferred_element_type=F32)


def _dot_nt(a, b):
    return lax.dot_general(a, b, (((1,), (1,)), ((), ())), preferred_element_type=F32)


def _t_bf16(a):
    return a.astype(F32).T.astype(BF16)


ROW_SLAB = 64
HALF_SLAB = 128


def _slab(x, fn):
    return fn(x.reshape(x.shape[0] // ROW_SLAB, ROW_SLAB, x.shape[1]), axis=0)


def _reduce_rows(x, fn):
    y = _slab(x, fn)
    z = fn(y.reshape(ROW_SLAB // SUBLANES, SUBLANES, y.shape[1]), axis=0)
    return fn(z, axis=0, keepdims=True)


def _proj_kernel(x_ref, g_ref, w_ref, o_ref, xn_ref):
    @pl.when(pl.program_id(1) == 0)
    def _():
        x = x_ref[...]
        ms = jnp.mean(x * x, axis=-1, keepdims=True)
        xn_ref[...] = (x * lax.rsqrt(ms + RMS_EPS) * g_ref[...]).astype(BF16)

    o_ref[...] = _dot(xn_ref[...], w_ref[...]).astype(o_ref.dtype)


def _proj(x2, gain, w_packed, out_dtype, tm, tn, name):
    n = x2.shape[0]
    width = w_packed.shape[1]
    return pl.pallas_call(
        _proj_kernel,
        out_shape=jax.ShapeDtypeStruct((n, width), out_dtype),
        grid=(n // tm, width // tn),
        in_specs=[
            pl.BlockSpec((tm, D_MODEL), lambda i, j: (i, 0)),
            pl.BlockSpec((1, D_MODEL), lambda i, j: (0, 0)),
            pl.BlockSpec((D_MODEL, tn), lambda i, j: (0, j)),
        ],
        out_specs=pl.BlockSpec((tm, tn), lambda i, j: (i, j)),
        scratch_shapes=[pltpu.VMEM((tm, D_MODEL), BF16)],
        compiler_params=_cparams(("parallel", "arbitrary")),
        name=name,
    )(x2, gain, w_packed)


HG_LEVELS = (32, 16, 8, 4, 2, 1)


def _hgrn_constants():
    c = HG_CHUNK
    t = np.arange(c)
    i = t[:, None]
    tt = t[None, :]
    rows = [tt <= i]
    masks = []
    for m in HG_LEVELS:
        mid = ((t // (2 * m)) * (2 * m) + m)[:, None]
        upper = (i >= mid) & (tt >= mid) & (tt <= i)
        lower = (i < mid) & (tt > i) & (tt <= mid - 1)
        rows.append(upper | lower)
        same = (t[:, None] // (2 * m)) == (t[None, :] // (2 * m))
        masks.append(same & (i >= mid) & (tt < mid))
    masks.append(np.eye(c, dtype=bool))
    m_all = np.concatenate(rows, axis=0).astype(np.float32)
    return m_all, np.stack(masks).astype(np.float32)


def _hgrn_kernel(q_ref, f_ref, i_ref, g_ref, lb_ref, nw_ref, mall_ref, mask_ref, o_ref, st_ref, e_ref, *, n_chunks):
    c = HG_CHUNK
    nl = len(HG_LEVELS)

    @pl.when(pl.program_id(1) == 0)
    def _():
        st_ref[...] = jnp.zeros_like(st_ref)

    lb = lb_ref[...]
    nw = nw_ref[...]

    def chunk(ci, carry):
        r0 = pl.multiple_of(ci * c, c)
        fp = f_ref[pl.ds(r0, c), :]
        f = lb + (1.0 - lb) * _sigmoid(fp)
        logf = jnp.log(jnp.maximum(f, TINY))
        kk = (1.0 - lb) * _sigmoid(-fp)
        g_hi = logf.astype(BF16)
        g_lo = (logf - g_hi.astype(F32)).astype(BF16)
        e_ref[...] = _dot(mall_ref[...], jnp.concatenate([g_hi, g_lo], axis=1))
        q = q_ref[pl.ds(r0, c), :]
        v = i_ref[pl.ds(r0, c), :]
        og = g_ref[pl.ds(r0, c), :]
        for h in range(HG_HEADS):
            lo, hi = h * HG_D, (h + 1) * HG_D

            def expo(row):
                return e_ref[row * c:(row + 1) * c, lo:hi] + e_ref[row * c:(row + 1) * c, HEAD_W + lo:HEAD_W + hi]

            qh, kh, vh = q[:, lo:hi], kk[:, lo:hi], v[:, lo:hi]
            b = expo(0)
            s = jnp.zeros((c, c), F32)
            for l in range(nl + 1):
                if l < nl:
                    decay = jnp.exp(expo(1 + l))
                    ql, kl = qh * decay, kh * decay
                else:
                    ql, kl = qh, kh
                sl = _dot_nt(ql.astype(BF16), kl.astype(BF16))
                s = s + jnp.where(mask_ref[l] > 0.0, sl, 0.0)
            st = st_ref[h]
            o = _dot_nt((qh * jnp.exp(b)).astype(BF16), st.astype(BF16)) + _dot(s.astype(BF16), vh.astype(BF16))
            b_last = b[c - 1:c, :]
            kdec = kh * jnp.exp(b_last - b)
            st_ref[h] = st * jnp.exp(b_last) + _dot(vh.T.astype(BF16), kdec.astype(BF16))
            y = o * lax.rsqrt(jnp.mean(o * o, axis=-1, keepdims=True) + RMS_EPS) * nw
            gh = og[:, lo:hi]
            o_ref[pl.ds(r0, c), lo:hi] = (y * (gh * _sigmoid(gh))).astype(o_ref.dtype)
        return carry

    lax.fori_loop(0, n_chunks, chunk, 0)


def _hgrn(proj_f, lb, norm_w, batch, seq, rows):
    m_all, masks = _hgrn_constants()
    n_rb = seq // rows
    n_chunks = rows // HG_CHUNK

    def col(c0):
        return pl.BlockSpec((rows, HEAD_W), lambda b, r, cidx=c0 // HEAD_W: (b * n_rb + r, cidx))

    return pl.pallas_call(
        functools.partial(_hgrn_kernel, n_chunks=n_chunks),
        out_shape=jax.ShapeDtypeStruct((batch * seq, HEAD_W), BF16),
        grid=(batch, n_rb),
        in_specs=[
            col(FCOL_HQ), col(FCOL_HF), col(FCOL_HI), col(FCOL_HG),
            pl.BlockSpec((1, HEAD_W), lambda b, r: (0, 0)),
            pl.BlockSpec((1, HG_D), lambda b, r: (0, 0)),
            pl.BlockSpec(m_all.shape, lambda b, r: (0, 0)),
            pl.BlockSpec(masks.shape, lambda b, r: (0, 0, 0)),
        ],
        out_specs=pl.BlockSpec((rows, HEAD_W), lambda b, r: (b * n_rb + r, 0)),
        scratch_shapes=[
            pltpu.VMEM((HG_HEADS, HG_D, HG_D), F32),
            pltpu.VMEM((m_all.shape[0], 2 * HEAD_W), F32),
        ],
        compiler_params=_cparams(("parallel", "arbitrary")),
        name="hgrn",
    )(proj_f, proj_f, proj_f, proj_f, lb, norm_w, jnp.asarray(m_all, BF16), jnp.asarray(masks, F32))


def _rel_bucket(dist):
    n = jnp.maximum(dist, 0)
    exact = REL_BUCKETS // 2
    nf = jnp.maximum(n, exact).astype(F32)
    large = exact + (jnp.log(nf / exact) / math.log(REL_MAX_DIST / exact) * (REL_BUCKETS - exact)).astype(jnp.int32)
    large = jnp.minimum(large, REL_BUCKETS - 1)
    return jnp.where(n < exact, n, large)


def _n_bias_tiles(tile):
    d = 0
    while d * tile - (tile - 1) < REL_FAR_DIST:
        d += 1
    return d + 1


def _bias_tiles_t(table, tile):
    n = _n_bias_tiles(tile)
    length = n * tile
    by_dist = table[_rel_bucket(jnp.arange(length, dtype=jnp.int32))].astype(F32).T
    heads = by_dist.shape[0]
    w = jnp.concatenate([by_dist, jnp.zeros((heads, 1), F32)], axis=1)
    skew = jnp.broadcast_to(w[:, None, :], (heads, tile, length + 1)).reshape(heads, tile * (length + 1))
    skew = skew[:, :tile * length].reshape(heads, tile, length)
    return skew.reshape(heads, tile, n, tile).transpose(0, 2, 1, 3)


def _moba_kernel(q_ref, k_ref, v_ref, bias_ref, o_ref, vt_ref, km_ref, sel_ref, m_ref, l_ref, acc_ref,
                 *, n_blocks, n_sel, n_bias):
    blk, dh, nh = MOBA_BLOCK, MOBA_DH, MOBA_HEADS
    qb = pl.program_id(1)
    scale = dh ** -0.5

    @pl.when(qb == 0)
    def _():
        for h in range(nh):
            for n in range(n_blocks):
                kblk = k_ref[n * blk:(n + 1) * blk, h * dh:(h + 1) * dh].astype(F32)
                km_ref[h, n:n + 1, :] = jnp.mean(kblk, axis=0, keepdims=True)
                vt_ref[h * dh:(h + 1) * dh, n * blk:(n + 1) * blk] = _t_bf16(v_ref[n * blk:(n + 1) * blk, h * dh:(h + 1) * dh])

    q_t = []
    n_iota = lax.broadcasted_iota(jnp.int32, (n_blocks, blk), 0)
    for h in range(nh):
        qt = _t_bf16(q_ref[:, h * dh:(h + 1) * dh])
        q_t.append(qt)
        gate = _dot(km_ref[h].astype(BF16), qt)
        gate = jnp.where(n_iota < qb, gate, NEG_BIG)
        rank = jnp.zeros(gate.shape, F32)
        for n in range(n_blocks):
            row = gate[n:n + 1, :]
            beats = (row > gate) | ((row == gate) & (n < n_iota))
            rank = rank + jnp.where(beats, 1.0, 0.0)
        sel_ref[h] = jnp.where((rank < n_sel) & (n_iota < qb), 0.0, NEG_BIG)
        m_ref[h] = jnp.full((1, blk), -jnp.inf, F32)
        l_ref[h] = jnp.zeros((1, blk), F32)
        acc_ref[h] = jnp.zeros((dh, blk), F32)

    def attend(r0, nk, mask_fn):
        ss = [_dot(k_ref[pl.ds(r0, nk), h * dh:(h + 1) * dh], q_t[h]) for h in range(nh)]
        ps, alphas = [], []
        for h in range(nh):
            s = ss[h] * scale + mask_fn(h)
            m_old = m_ref[h]
            m_new = jnp.maximum(m_old, _reduce_rows(s, jnp.max))
            alphas.append(jnp.exp(m_old - m_new))
            p = jnp.exp(s - m_new)
            l_ref[h] = alphas[h] * l_ref[h] + _reduce_rows(p, jnp.sum)
            m_ref[h] = m_new
            ps.append(p.astype(BF16))
        for h in range(nh):
            acc_ref[h] = alphas[h] * acc_ref[h] + _dot(vt_ref[h * dh:(h + 1) * dh, pl.ds(r0, nk)], ps[h])

    def past_mask(h, n):
        return bias_ref[h, jnp.minimum(qb - n, n_bias - 1)] + sel_ref[h, pl.ds(n, 1), :]

    causal = jnp.where(lax.broadcasted_iota(jnp.int32, (blk, blk), 0) <= lax.broadcasted_iota(jnp.int32, (blk, blk), 1),
                       0.0, NEG_BIG)

    def own_mask(h):
        return bias_ref[h, 0] + causal

    def past_pair(i, carry):
        n = 2 * i
        attend(pl.multiple_of(n * blk, 2 * blk), 2 * blk,
               lambda h: jnp.concatenate([past_mask(h, n), past_mask(h, n + 1)], axis=0))
        return carry

    n_pairs = qb // 2
    lax.fori_loop(0, n_pairs, past_pair, 0)

    @pl.when(qb % 2 == 1)
    def _():
        attend(pl.multiple_of((qb - 1) * blk, blk), 2 * blk,
               lambda h: jnp.concatenate([past_mask(h, qb - 1), own_mask(h)], axis=0))

    @pl.when(qb % 2 == 0)
    def _():
        attend(pl.multiple_of(qb * blk, blk), blk, own_mask)

    for h in range(nh):
        o_ref[:, h * dh:(h + 1) * dh] = (acc_ref[h] / l_ref[h]).T.astype(o_ref.dtype)


def _moba(proj_b, bias_tiles, batch, seq):
    blk = MOBA_BLOCK
    nb = seq // blk
    n_sel = min(MOBA_TOPK, nb - 1)
    n_bias = bias_tiles.shape[1]
    return pl.pallas_call(
        functools.partial(_moba_kernel, n_blocks=nb, n_sel=n_sel, n_bias=n_bias),
        out_shape=jax.ShapeDtypeStruct((batch * seq, HEAD_W), BF16),
        grid=(batch, nb),
        in_specs=[
            pl.BlockSpec((blk, HEAD_W), lambda b, i: (b * nb + i, BCOL_MQ // HEAD_W)),
            pl.BlockSpec((seq, HEAD_W), lambda b, i: (b, BCOL_MK // HEAD_W)),
            pl.BlockSpec((seq, HEAD_W), lambda b, i: (b, BCOL_MV // HEAD_W)),
            pl.BlockSpec(bias_tiles.shape, lambda b, i: (0, 0, 0, 0)),
        ],
        out_specs=pl.BlockSpec((blk, HEAD_W), lambda b, i: (b * nb + i, 0)),
        scratch_shapes=[
            pltpu.VMEM((HEAD_W, seq), BF16),
            pltpu.VMEM((MOBA_HEADS, nb, MOBA_DH), F32),
            pltpu.VMEM((MOBA_HEADS, nb, blk), F32),
            pltpu.VMEM((MOBA_HEADS, 1, blk), F32),
            pltpu.VMEM((MOBA_HEADS, 1, blk), F32),
            pltpu.VMEM((MOBA_HEADS, MOBA_DH, blk), F32),
        ],
        compiler_params=_cparams(("parallel", "arbitrary")),
        name="moba",
    )(proj_b, proj_b, proj_b, bias_tiles)


def _sort_key(x):
    bits = pltpu.bitcast(x, jnp.int32)
    return bits ^ ((bits >> 31) & 0x7FFFFFFF)


def _split16(key):
    return (key >> 16).astype(jnp.int16), ((key & 0xFFFF) + I16_MIN).astype(jnp.int16)


def _dsa_kernel(dq_ref, iq_ref, iwq_ref, dkv_ref, ikw_ref, kvg_ref, lnw_ref, lnb_ref, wuk_ref, wuv_ref, bias_ref,
                o_ref, ckv_ref, ckvt_ref, ikn_ref, key_ref, hi_ref, lo_ref, lo2_ref, mask_ref, m_ref, l_ref, acc_ref,
                *, seq, n_keep, n_bias):
    tq, kc, nh, dh = DSA_QB, DSA_KC, DSA_HEADS, DSA_DH
    sub = kc // LANES
    qb = pl.program_id(1)
    t0 = qb * tq
    n_kc = (t0 + tq - 1) // kc + 1
    n_rest = (seq - n_kc * kc).astype(F32)

    @pl.when(qb == 0)
    def _():
        for c in range(seq // kc):
            rows = slice(c * kc, (c + 1) * kc)
            x = dkv_ref[rows, :]
            cn = x * lax.rsqrt(jnp.mean(x * x, axis=-1, keepdims=True) + RMS_EPS) * kvg_ref[...]
            ckv_ref[rows, :] = cn.astype(BF16)
            ckvt_ref[:, rows] = cn.T.astype(BF16)
            ik = ikw_ref[rows, 0:IDX_DIM]
            xc = ik - jnp.mean(ik, axis=-1, keepdims=True)
            ikn = xc * lax.rsqrt(jnp.mean(xc * xc, axis=-1, keepdims=True) + RMS_EPS) * lnw_ref[...] + lnb_ref[...]
            ikn_ref[rows, :] = ikn.astype(BF16)

    t_row = t0 + lax.broadcasted_iota(jnp.int32, (1, tq), 1)
    key_off = lax.broadcasted_iota(jnp.int32, (kc, tq), 0)

    def causal_of(c0):
        return (c0 + key_off) <= t_row

    iq_t = iq_ref[...].astype(F32).T
    iq_cat = jnp.concatenate([iq_t[h * IDX_DIM:(h + 1) * IDX_DIM, :] for h in range(IDX_HEADS)], axis=1).astype(BF16)
    iw_t = iwq_ref[...].T[IDX_DIM:IDX_DIM + IDX_HEADS, :] * (IDX_HEADS ** -0.5)
    iw_t = iw_t * (IDX_DIM ** -0.5)

    def score_chunk(c, carry):
        c0 = pl.multiple_of(c * kc, kc)
        s_all = _dot(ikn_ref[pl.ds(c0, kc), :], iq_cat)
        sc = jnp.zeros((kc, tq), F32)
        for h in range(IDX_HEADS):
            sc = sc + iw_t[h:h + 1, :] * jnp.maximum(s_all[:, h * tq:(h + 1) * tq], 0.0)
        sc = jnp.where(causal_of(c0), sc, NEG_BIG)
        sc = jnp.where(sc == 0.0, 0.0, sc)
        key = _sort_key(sc)
        key_ref[pl.ds(c0, kc), :] = key
        hi_ref[pl.ds(c0, kc), :], lo_ref[pl.ds(c0, kc), :] = _split16(key)
        return carry

    lax.fori_loop(0, n_kc, score_chunk, 0)

    def count(pred_fn):
        def body(c, tot):
            c0 = pl.multiple_of(c * kc, kc)
            hit = pred_fn(key_ref[pl.ds(c0, kc), :], c0)
            return tot + _slab(jnp.where(hit, 1.0, 0.0), jnp.sum)
        part = lax.fori_loop(0, n_kc, body, jnp.zeros((ROW_SLAB, tq), F32))
        return _reduce_rows(part, jnp.sum)

    one16, zero16 = jnp.ones((), BF16), jnp.zeros((), BF16)

    def count16(ref, cand):
        cand16 = cand.astype(jnp.int16)

        def body(c, tot):
            c0 = pl.multiple_of(c * kc, kc)
            hit = jnp.where(ref[pl.ds(c0, kc), :] >= cand16, one16, zero16)
            parts = [hit[r * HALF_SLAB:(r + 1) * HALF_SLAB] for r in range(kc // HALF_SLAB)]
            while len(parts) > 1:
                parts = [parts[i] + parts[i + 1] for i in range(0, len(parts), 2)]
            return tot + parts[0]
        part = lax.fori_loop(0, n_kc, body, jnp.zeros((HALF_SLAB, tq), BF16))
        return _reduce_rows(part.astype(F32), jnp.sum)

    def greedy16(count_fn):
        def step(k, base):
            cand = base + jnp.left_shift(jnp.int32(1), 15 - k)
            return jnp.where(count_fn(cand) >= n_keep, cand, base)
        return lax.fori_loop(0, 16, step, jnp.full((1, tq), I16_MIN, jnp.int32))

    tau_hi = greedy16(lambda cand: count16(hi_ref, cand) + jnp.where(cand <= KEY_NEG_HI, n_rest, 0.0))
    above = jnp.where(tau_hi < I16_MAX, count16(hi_ref, jnp.minimum(tau_hi + 1, I16_MAX)), 0.0)
    above = above + jnp.where(KEY_NEG_HI > tau_hi, n_rest, 0.0)
    tau_hi16 = tau_hi.astype(jnp.int16)

    def low_of_equal(c, carry):
        c0 = pl.multiple_of(c * kc, kc)
        lo2_ref[pl.ds(c0, kc), :] = jnp.where(hi_ref[pl.ds(c0, kc), :] == tau_hi16, lo_ref[pl.ds(c0, kc), :],
                                              jnp.int16(I16_MIN))
        return carry

    lax.fori_loop(0, n_kc, low_of_equal, 0)
    rest_same_hi = jnp.where(tau_hi == KEY_NEG_HI, n_rest, 0.0)
    tau_lo = greedy16(lambda cand: above + count16(lo2_ref, cand) + jnp.where(cand <= KEY_NEG_LO, rest_same_hi, 0.0))
    tau = tau_hi * 65536 + (tau_lo - I16_MIN)

    c_gt = count(lambda keys, c0: keys > tau) + jnp.where(tau < KEY_NEG, n_rest, 0.0)
    need = n_keep - c_gt
    n_eq = count(lambda keys, c0: (keys == tau) & causal_of(c0))
    tie_break = jnp.max(n_eq - need) > 0.0

    @pl.when(jnp.logical_not(tie_break))
    def _():
        def body(c, carry):
            c0 = pl.multiple_of(c * kc, kc)
            keep = (key_ref[pl.ds(c0, kc), :] >= tau) & causal_of(c0)
            mask_ref[pl.ds(c0, kc), :] = jnp.where(keep, 0.0, NEG_BIG)
            return carry
        lax.fori_loop(0, n_kc, body, 0)

    @pl.when(tie_break)
    def _():
        lower = jnp.where(lax.broadcasted_iota(jnp.int32, (kc, kc), 1)
                          < lax.broadcasted_iota(jnp.int32, (kc, kc), 0), 1.0, 0.0).astype(BF16)

        def body(c, seen):
            c0 = pl.multiple_of(c * kc, kc)
            keys = key_ref[pl.ds(c0, kc), :]
            eq = keys == tau
            eqf = jnp.where(eq, 1.0, 0.0)
            rank = seen + _dot(lower, eqf.astype(BF16))
            keep = ((keys > tau) | (eq & (rank < need))) & causal_of(c0)
            mask_ref[pl.ds(c0, kc), :] = jnp.where(keep, 0.0, NEG_BIG)
            return seen + _reduce_rows(eqf, jnp.sum)
        lax.fori_loop(0, n_kc, body, jnp.zeros((1, tq), F32))

    scale = dh ** -0.5
    qa_cat = jnp.concatenate(
        [_dot(wuk_ref[h], _t_bf16(dq_ref[:, h * dh:(h + 1) * dh])) for h in range(nh)], axis=1).astype(BF16)
    m_ref[...] = jnp.full(m_ref.shape, -jnp.inf, F32)
    l_ref[...] = jnp.zeros(l_ref.shape, F32)
    acc_ref[...] = jnp.zeros(acc_ref.shape, F32)

    def attend(c, nc):
        c0 = pl.multiple_of(c * kc, kc)
        nk = nc * kc
        msk = mask_ref[pl.ds(c0, nk), :]
        bias = jnp.concatenate(
            [bias_ref[jnp.clip(qb - (c * sub + g), 0, n_bias - 1)] for g in range(nc * sub)], axis=0)
        s = _dot(ckv_ref[pl.ds(c0, nk), :], qa_cat) * scale + bias + jnp.concatenate([msk] * nh, axis=1)
        m_old = m_ref[...]
        m_new = jnp.maximum(m_old, _reduce_rows(s, jnp.max))
        alpha = jnp.exp(m_old - m_new)
        p = jnp.exp(s - m_new)
        l_ref[...] = alpha * l_ref[...] + _reduce_rows(p, jnp.sum)
        acc_ref[...] = alpha * acc_ref[...] + _dot(ckvt_ref[:, pl.ds(c0, nk)], p.astype(BF16))
        m_ref[...] = m_new

    def att_chunk(c, carry):
        attend(c, 1)
        return carry

    lax.fori_loop(0, n_kc, att_chunk, 0)

    o_lat_t = (acc_ref[...] / l_ref[...]).astype(BF16)
    for h in range(nh):
        o_ref[:, h * dh:(h + 1) * dh] = _dot(wuv_ref[h], o_lat_t[:, h * tq:(h + 1) * tq]).T.astype(o_ref.dtype)


def _dsa(proj_f, proj_b, kv_gain, ln_w, ln_b, w_uk_h, w_uv_h, bias_tiles, batch, seq):
    tq = DSA_QB
    nq = seq // tq
    n_keep = min(DSA_TOPK_MAX, seq // 4)
    n_bias = bias_tiles.shape[0]
    assert seq // HALF_SLAB <= 256, "bf16 hit counts must stay exact"
    return pl.pallas_call(
        functools.partial(_dsa_kernel, seq=seq, n_keep=float(n_keep), n_bias=n_bias),
        out_shape=jax.ShapeDtypeStruct((batch * seq, HEAD_W), BF16),
        grid=(batch, nq),
        in_specs=[
            pl.BlockSpec((tq, HEAD_W), lambda b, i: (b * nq + i, BCOL_DQ // HEAD_W)),
            pl.BlockSpec((tq, HEAD_W), lambda b, i: (b * nq + i, BCOL_IQ // HEAD_W)),
            pl.BlockSpec((tq, LANES), lambda b, i: (b * nq + i, FCOL_IKW // LANES)),
            pl.BlockSpec((seq, DSA_KV_RANK), lambda b, i: (b, FCOL_DKV // DSA_KV_RANK)),
            pl.BlockSpec((seq, LANES), lambda b, i: (b, FCOL_IKW // LANES)),
            pl.BlockSpec((1, DSA_KV_RANK), lambda b, i: (0, 0)),
            pl.BlockSpec((1, IDX_DIM), lambda b, i: (0, 0)),
            pl.BlockSpec((1, IDX_DIM), lambda b, i: (0, 0)),
            pl.BlockSpec(w_uk_h.shape, lambda b, i: (0, 0, 0)),
            pl.BlockSpec(w_uv_h.shape, lambda b, i: (0, 0, 0)),
            pl.BlockSpec(bias_tiles.shape, lambda b, i: (0, 0, 0)),
        ],
        out_specs=pl.BlockSpec((tq, HEAD_W), lambda b, i: (b * nq + i, 0)),
        scratch_shapes=[
            pltpu.VMEM((seq, DSA_KV_RANK), BF16),
            pltpu.VMEM((DSA_KV_RANK, seq), BF16),
            pltpu.VMEM((seq, IDX_DIM), BF16),
            pltpu.VMEM((seq, tq), jnp.int32),
            pltpu.VMEM((seq, tq), jnp.int16),
            pltpu.VMEM((seq, tq), jnp.int16),
            pltpu.VMEM((seq, tq), jnp.int16),
            pltpu.VMEM((seq, tq), F32),
            pltpu.VMEM((1, DSA_HEADS * tq), F32),
            pltpu.VMEM((1, DSA_HEADS * tq), F32),
            pltpu.VMEM((DSA_KV_RANK, DSA_HEADS * tq), F32),
        ],
        compiler_params=_cparams(("parallel", "arbitrary")),
        name="dsa",
    )(proj_b, proj_b, proj_f, proj_f, proj_f, kv_gain, ln_w, ln_b, w_uk_h, w_uv_h, bias_tiles)


def _merge_kernel(x_ref, ya_ref, yb_ref, yc_ref, g0_ref, g1_ref, g2_ref, wa_ref, wb_ref, wc_ref, wo_ref, gain_ref,
                  o_ref):
    merged = (_sigmoid(g0_ref[...]) * _dot(ya_ref[...], wa_ref[...])
              + _sigmoid(g1_ref[...]) * _dot(yb_ref[...], wb_ref[...])
              + _sigmoid(g2_ref[...]) * _dot(yc_ref[...], wc_ref[...]))
    z = _dot(merged.astype(BF16), wo_ref[...])
    zn = z * lax.rsqrt(jnp.mean(z * z, axis=-1, keepdims=True) + RMS_EPS) * gain_ref[...]
    o_ref[...] = x_ref[...] + zn


def _merge(x2, ya, yb, yc, proj_f, wa, wb, wc, wo, gain, tm):
    n = x2.shape[0]
    gcol = FCOL_GATES // D_MODEL

    def rows(w):
        return pl.BlockSpec((tm, w), lambda i: (i, 0))

    def gate(k):
        return pl.BlockSpec((tm, D_MODEL), lambda i, k=k: (i, gcol + k))

    def full(a):
        return pl.BlockSpec(a.shape, lambda i: (0, 0))

    return pl.pallas_call(
        _merge_kernel,
        out_shape=jax.ShapeDtypeStruct((n, D_MODEL), F32),
        grid=(n // tm,),
        in_specs=[rows(D_MODEL), rows(HEAD_W), rows(HEAD_W), rows(HEAD_W), gate(0), gate(1), gate(2),
                  full(wa), full(wb), full(wc), full(wo), full(gain)],
        out_specs=rows(D_MODEL),
        compiler_params=_cparams(("parallel",)),
        name="merge",
    )(x2, ya, yb, yc, proj_f, proj_f, proj_f, wa, wb, wc, wo, gain)


def _ffn_kernel(x_ref, g1_ref, wg_ref, wu_ref, wd_ref, g2_ref, o_ref, hn_ref, acc_ref):
    j = pl.program_id(1)

    @pl.when(j == 0)
    def _():
        x = x_ref[...]
        hn_ref[...] = (x * lax.rsqrt(jnp.mean(x * x, axis=-1, keepdims=True) + RMS_EPS) * g1_ref[...]).astype(BF16)
        acc_ref[...] = jnp.zeros_like(acc_ref)

    hn = hn_ref[...]
    gate = _dot(hn, wg_ref[...])
    up = _dot(hn, wu_ref[...])
    act = (gate * _sigmoid(gate)) * up
    acc_ref[...] += _dot(act.astype(BF16), wd_ref[...])

    @pl.when(j == pl.num_programs(1) - 1)
    def _():
        y = acc_ref[...]
        yn = y * lax.rsqrt(jnp.mean(y * y, axis=-1, keepdims=True) + RMS_EPS) * g2_ref[...]
        o_ref[...] = x_ref[...] + yn


def _ffn(x2, g1, w_in, w_out, g2, tm, tf):
    n = x2.shape[0]
    nf = D_FF // tf
    return pl.pallas_call(
        _ffn_kernel,
        out_shape=jax.ShapeDtypeStruct((n, D_MODEL), F32),
        grid=(n // tm, nf),
        in_specs=[
            pl.BlockSpec((tm, D_MODEL), lambda i, j: (i, 0)),
            pl.BlockSpec((1, D_MODEL), lambda i, j: (0, 0)),
            pl.BlockSpec((D_MODEL, tf), lambda i, j: (0, j)),
            pl.BlockSpec((D_MODEL, tf), lambda i, j: (0, nf + j)),
            pl.BlockSpec((tf, D_MODEL), lambda i, j: (j, 0)),
            pl.BlockSpec((1, D_MODEL), lambda i, j: (0, 0)),
        ],
        out_specs=pl.BlockSpec((tm, D_MODEL), lambda i, j: (i, 0)),
        scratch_shapes=[pltpu.VMEM((tm, D_MODEL), BF16), pltpu.VMEM((tm, D_MODEL), F32)],
        compiler_params=_cparams(("parallel", "arbitrary")),
        name="ffn",
    )(x2, g1, w_in, w_in, w_out, g2)


def _pack_w_in(w):
    o_mq, o_dq, o_dkv, o_iq, o_ik, o_gates = 2048, 3584, 4096, 4352, 4864, 4936
    z = lambda k: jnp.zeros((w.shape[0], k), w.dtype)
    w_f = jnp.concatenate([
        w[:, :o_mq],
        w[:, o_gates:],
        w[:, o_dkv:o_iq],
        w[:, o_ik:o_gates],
        z(PROJ_F_W - FCOL_IKW - (o_gates - o_ik)),
    ], axis=1)
    w_b = jnp.concatenate([
        w[:, o_mq:o_dq],
        w[:, o_dq:o_dkv],
        w[:, o_iq:o_ik],
    ], axis=1)
    assert w_f.shape[1] == PROJ_F_W and w_b.shape[1] == PROJ_B_W
    return w_f.astype(BF16), w_b.astype(BF16)


def _tiles(batch, seq):
    n = batch * seq
    return dict(
        proj_tm=min(1024, n), proj_f_tn=PROJ_F_W // 4, proj_b_tn=PROJ_B_W // 2,
        hgrn_rows=min(512, seq),
        merge_tm=min(512, n),
        ffn_tm=min(1024, n), ffn_tf=D_FF // 2,
    )


def kernel(x, w_in, pre_mix_norm, post_mix_norm, pre_ffn_norm, post_ffn_norm, hg_lb_logits, hg_norm_w, dsa_kv_norm,
           dsa_w_uk, dsa_w_uv, idx_k_norm_w, idx_k_norm_b, rel_bias, w_branch_a, w_branch_b, w_branch_c, w_out,
           w_ffn_in, w_ffn_out):
    batch, seq, _ = x.shape
    depth = w_in.shape[0]
    t = _tiles(batch, seq)
    lbp = jax.nn.softmax(hg_lb_logits.astype(F32), axis=0)
    lower_bounds = jnp.cumsum(lbp, axis=0) - lbp[0]
    bias_b = _bias_tiles_t(rel_bias[:, :MOBA_HEADS], MOBA_BLOCK)
    bias_c = _bias_tiles_t(rel_bias[:, MOBA_HEADS:], LANES)
    bias_c = bias_c.transpose(1, 2, 0, 3).reshape(bias_c.shape[1], LANES, DSA_HEADS * LANES)
    row = lambda a: a.reshape(1, -1).astype(F32)

    x2 = x.reshape(batch * seq, D_MODEL)
    for l in range(depth):
        w_f, w_b = _pack_w_in(w_in[l])
        gain = row(pre_mix_norm[l])
        proj_f = _proj(x2, gain, w_f, F32, t["proj_tm"], t["proj_f_tn"], "proj_f")
        proj_b = _proj(x2, gain, w_b, BF16, t["proj_tm"], t["proj_b_tn"], "proj_b")
        y_a = _hgrn(proj_f, row(lower_bounds[l]), row(hg_norm_w[l]), batch, seq, t["hgrn_rows"])
        y_b = _moba(proj_b, bias_b, batch, seq)
        y_c = _dsa(proj_f, proj_b, row(dsa_kv_norm[l]), row(idx_k_norm_w[l]), row(idx_k_norm_b[l]),
                   jnp.transpose(dsa_w_uk[l], (1, 0, 2)).astype(BF16),
                   jnp.transpose(dsa_w_uv[l], (1, 2, 0)).astype(BF16),
                   bias_c, batch, seq)
        x2 = _merge(x2, y_a, y_b, y_c, proj_f, w_branch_a[l].astype(BF16), w_branch_b[l].astype(BF16),
                    w_branch_c[l].astype(BF16), w_out[l].astype(BF16), row(post_mix_norm[l]), t["merge_tm"])
        x2 = _ffn(x2, row(pre_ffn_norm[l]), w_ffn_in[l].astype(BF16), w_ffn_out[l].astype(BF16),
                  row(post_ffn_norm[l]), t["ffn_tm"], t["ffn_tf"])
    return x2.reshape(batch, seq, D_MODEL)
```

```python
import functools
import math

import numpy as np
import jax
import jax.numpy as jnp
from jax import lax
from jax.experimental import pallas as pl
from jax.experimental.pallas import tpu as pltpu

F32 = jnp.float32
BF16 = jnp.bfloat16

D_MODEL = 1024
RMS_EPS = 1e-6
NEG_BIG = -1e30
TINY = 1e-20
HG_HEADS = 4
HG_D = 128
HG_CHUNK = 64
MOBA_HEADS = 4
MOBA_DH = 128
MOBA_BLOCK = 256
MOBA_TOPK = 3
DSA_HEADS = 4
DSA_DH = 128
DSA_KV_RANK = 256
IDX_HEADS = 8
IDX_DIM = 64
DSA_TOPK_MAX = 256
DSA_KC = 512
REL_BUCKETS = 32
REL_MAX_DIST = 2048
D_FF = 2816
HEAD_W = 512

LANES = 128
SUBLANES = 8
DSA_QB = LANES

FCOL_HQ, FCOL_HF, FCOL_HI, FCOL_HG = 0, 512, 1024, 1536
FCOL_GATES, FCOL_DKV, FCOL_IKW = 2048, 5120, 5376
PROJ_F_W = 5632
BCOL_MQ, BCOL_MK, BCOL_MV, BCOL_DQ, BCOL_IQ = 0, 512, 1024, 1536, 2048
PROJ_B_W = 2560

VMEM_LIMIT = 56 * 1024 * 1024

_REL_EXACT = REL_BUCKETS // 2
REL_FAR_DIST = int(math.ceil(_REL_EXACT * (REL_MAX_DIST / _REL_EXACT) ** ((REL_BUCKETS - _REL_EXACT - 1) / (REL_BUCKETS - _REL_EXACT)))) + 1

_NEG_BITS = int(np.array(NEG_BIG, np.float32).view(np.int32))
KEY_NEG = int(np.int32(_NEG_BITS ^ 0x7FFFFFFF))


def _cparams(sem):
    return pltpu.CompilerParams(dimension_semantics=sem, vmem_limit_bytes=VMEM_LIMIT)


def _sigmoid(x):
    return 1.0 / (1.0 + jnp.exp(-x))


def _dot(a, b):
    return jnp.dot(a, b, preferred_element_type=F32)


def _dot_nt(a, b):
    return lax.dot_general(a, b, (((1,), (1,)), ((), ())), preferred_element_type=F32)


def _t_bf16(a):
    return a.astype(F32).T.astype(BF16)


ROW_SLAB = 64


def _slab(x, fn):
    return fn(x.reshape(x.shape[0] // ROW_SLAB, ROW_SLAB, x.shape[1]), axis=0)


def _reduce_rows(x, fn):
    y = _slab(x, fn)
    z = fn(y.reshape(ROW_SLAB // SUBLANES, SUBLANES, y.shape[1]), axis=0)
    return fn(z, axis=0, keepdims=True)


def _proj_kernel(x_ref, g_ref, w_ref, o_ref, xn_ref):
    @pl.when(pl.program_id(1) == 0)
    def _():
        x = x_ref[...]
        ms = jnp.mean(x * x, axis=-1, keepdims=True)
        xn_ref[...] = (x * lax.rsqrt(ms + RMS_EPS) * g_ref[...]).astype(BF16)

    o_ref[...] = _dot(xn_ref[...], w_ref[...]).astype(o_ref.dtype)


def _proj(x2, gain, w_packed, out_dtype, tm, tn, name):
    n = x2.shape[0]
    width = w_packed.shape[1]
    return pl.pallas_call(
        _proj_kernel,
        out_shape=jax.ShapeDtypeStruct((n, width), out_dtype),
        grid=(n // tm, width // tn),
        in_specs=[
            pl.BlockSpec((tm, D_MODEL), lambda i, j: (i, 0)),
            pl.BlockSpec((1, D_MODEL), lambda i, j: (0, 0)),
            pl.BlockSpec((D_MODEL, tn), lambda i, j: (0, j)),
        ],
        out_specs=pl.BlockSpec((tm, tn), lambda i, j: (i, j)),
        scratch_shapes=[pltpu.VMEM((tm, D_MODEL), BF16)],
        compiler_params=_cparams(("parallel", "arbitrary")),
        name=name,
    )(x2, gain, w_packed)


HG_LEVELS = (32, 16, 8, 4, 2, 1)
HG_UNROLL = 2


def _hgrn_constants():
    c = HG_CHUNK
    t = np.arange(c)
    i = t[:, None]
    tt = t[None, :]
    rows = [tt <= i]
    masks = []
    for m in HG_LEVELS:
        mid = ((t // (2 * m)) * (2 * m) + m)[:, None]
        upper = (i >= mid) & (tt >= mid) & (tt <= i)
        lower = (i < mid) & (tt > i) & (tt <= mid - 1)
        rows.append(upper | lower)
        same = (t[:, None] // (2 * m)) == (t[None, :] // (2 * m))
        masks.append(same & (i >= mid) & (tt < mid))
    masks.append(np.eye(c, dtype=bool))
    m_all = np.concatenate(rows, axis=0).astype(np.float32)
    return m_all, np.stack(masks).astype(np.float32)


def _hgrn_kernel(q_ref, f_ref, i_ref, g_ref, lb_ref, nw_ref, mall_ref, mask_ref, o_ref, st_ref, e_ref, *, n_chunks):
    c = HG_CHUNK
    nl = len(HG_LEVELS)

    @pl.when(pl.program_id(1) == 0)
    def _():
        st_ref[...] = jnp.zeros_like(st_ref)

    lb = lb_ref[...]
    nw = nw_ref[...]

    heads = [slice(h * HG_D, (h + 1) * HG_D) for h in range(HG_HEADS)]

    def intra(r0, u):
        fp = f_ref[pl.ds(r0, c), :]
        f = lb + (1.0 - lb) * _sigmoid(fp)
        logf = jnp.log(jnp.maximum(f, TINY))
        kk = (1.0 - lb) * _sigmoid(-fp)
        g_hi = logf.astype(BF16)
        g_lo = (logf - g_hi.astype(F32)).astype(BF16)
        e_ref[u] = _dot(mall_ref[...], jnp.concatenate([g_hi, g_lo], axis=1))
        q = q_ref[pl.ds(r0, c), :]

        def expo(row):
            return e_ref[u, row * c:(row + 1) * c, 0:HEAD_W] + e_ref[u, row * c:(row + 1) * c, HEAD_W:2 * HEAD_W]

        b = expo(0)
        b_last = b[c - 1:c, :]
        q_in = (q * jnp.exp(b)).astype(BF16)
        k_out = (kk * jnp.exp(b_last - b)).astype(BF16)
        qls, kls = [], []
        for l in range(nl):
            decay = jnp.exp(expo(1 + l))
            qls.append((q * decay).astype(BF16))
            kls.append((kk * decay).astype(BF16))
        qls.append(q.astype(BF16))
        kls.append(kk.astype(BF16))
        sls = [[_dot_nt(qls[l][:, hs], kls[l][:, hs]) for l in range(nl + 1)] for hs in heads]
        ss = []
        for h in range(HG_HEADS):
            s = jnp.zeros((c, c), F32)
            for l in range(nl + 1):
                s = s + jnp.where(mask_ref[l] > 0.0, sls[h][l], 0.0)
            ss.append(s.astype(BF16))
        v = i_ref[pl.ds(r0, c), :].astype(BF16)
        o_intra = [_dot(ss[h], v[:, hs]) for h, hs in enumerate(heads)]
        kv = [_dot(v[:, hs].astype(F32).T.astype(BF16), k_out[:, hs]) for hs in heads]
        return q_in, jnp.exp(b_last), o_intra, kv

    def step(ci, carry):
        parts = [intra(pl.multiple_of((ci * HG_UNROLL + u) * c, c), u) for u in range(HG_UNROLL)]
        for u, (q_in, st_decay, o_intra, kv) in enumerate(parts):
            r0 = pl.multiple_of((ci * HG_UNROLL + u) * c, c)
            og = g_ref[pl.ds(r0, c), :]
            outs = [_dot_nt(q_in[:, hs], st_ref[h].astype(BF16)) + o_intra[h] for h, hs in enumerate(heads)]
            for h, hs in enumerate(heads):
                st_ref[h] = st_ref[h] * st_decay[:, hs] + kv[h]
            for h, hs in enumerate(heads):
                o = outs[h]
                y = o * lax.rsqrt(jnp.mean(o * o, axis=-1, keepdims=True) + RMS_EPS) * nw
                gh = og[:, hs]
                o_ref[pl.ds(r0, c), hs] = (y * (gh * _sigmoid(gh))).astype(o_ref.dtype)
        return carry

    lax.fori_loop(0, n_chunks // HG_UNROLL, step, 0)


def _hgrn(proj_f, lb, norm_w, batch, seq, rows):
    m_all, masks = _hgrn_constants()
    n_rb = seq // rows
    n_chunks = rows // HG_CHUNK

    def col(c0):
        return pl.BlockSpec((rows, HEAD_W), lambda b, r, cidx=c0 // HEAD_W: (b * n_rb + r, cidx))

    return pl.pallas_call(
        functools.partial(_hgrn_kernel, n_chunks=n_chunks),
        out_shape=jax.ShapeDtypeStruct((batch * seq, HEAD_W), BF16),
        grid=(batch, n_rb),
        in_specs=[
            col(FCOL_HQ), col(FCOL_HF), col(FCOL_HI), col(FCOL_HG),
            pl.BlockSpec((1, HEAD_W), lambda b, r: (0, 0)),
            pl.BlockSpec((1, HG_D), lambda b, r: (0, 0)),
            pl.BlockSpec(m_all.shape, lambda b, r: (0, 0)),
            pl.BlockSpec(masks.shape, lambda b, r: (0, 0, 0)),
        ],
        out_specs=pl.BlockSpec((rows, HEAD_W), lambda b, r: (b * n_rb + r, 0)),
        scratch_shapes=[
            pltpu.VMEM((HG_HEADS, HG_D, HG_D), F32),
            pltpu.VMEM((HG_UNROLL, m_all.shape[0], 2 * HEAD_W), F32),
        ],
        compiler_params=_cparams(("parallel", "arbitrary")),
        name="hgrn",
    )(proj_f, proj_f, proj_f, proj_f, lb, norm_w, jnp.asarray(m_all, BF16), jnp.asarray(masks, F32))


def _rel_bucket(dist):
    n = jnp.maximum(dist, 0)
    exact = REL_BUCKETS // 2
    nf = jnp.maximum(n, exact).astype(F32)
    large = exact + (jnp.log(nf / exact) / math.log(REL_MAX_DIST / exact) * (REL_BUCKETS - exact)).astype(jnp.int32)
    large = jnp.minimum(large, REL_BUCKETS - 1)
    return jnp.where(n < exact, n, large)


def _n_bias_tiles(tile):
    d = 0
    while d * tile - (tile - 1) < REL_FAR_DIST:
        d += 1
    return d + 1


def _bias_tiles_t(table, tile):
    n = _n_bias_tiles(tile)
    length = n * tile
    by_dist = table[_rel_bucket(jnp.arange(length, dtype=jnp.int32))].astype(F32).T
    heads = by_dist.shape[0]
    w = jnp.concatenate([by_dist, jnp.zeros((heads, 1), F32)], axis=1)
    skew = jnp.broadcast_to(w[:, None, :], (heads, tile, length + 1)).reshape(heads, tile * (length + 1))
    skew = skew[:, :tile * length].reshape(heads, tile, length)
    return skew.reshape(heads, tile, n, tile).transpose(0, 2, 1, 3)


def _moba_kernel(q_ref, k_ref, v_ref, bias_ref, o_ref, vt_ref, km_ref, sel_ref, m_ref, l_ref, acc_ref,
                 *, n_blocks, n_sel, n_bias):
    blk, dh, nh = MOBA_BLOCK, MOBA_DH, MOBA_HEADS
    qb = pl.program_id(1)
    scale = dh ** -0.5

    @pl.when(qb == 0)
    def _():
        for h in range(nh):
            for n in range(n_blocks):
                kblk = k_ref[n * blk:(n + 1) * blk, h * dh:(h + 1) * dh].astype(F32)
                km_ref[h, n:n + 1, :] = jnp.mean(kblk, axis=0, keepdims=True)
                vt_ref[h * dh:(h + 1) * dh, n * blk:(n + 1) * blk] = _t_bf16(v_ref[n * blk:(n + 1) * blk, h * dh:(h + 1) * dh])

    q_t = []
    n_iota = lax.broadcasted_iota(jnp.int32, (n_blocks, blk), 0)
    for h in range(nh):
        qt = _t_bf16(q_ref[:, h * dh:(h + 1) * dh])
        q_t.append(qt)
        gate = _dot(km_ref[h].astype(BF16), qt)
        gate = jnp.where(n_iota < qb, gate, NEG_BIG)
        rank = jnp.zeros(gate.shape, F32)
        for n in range(n_blocks):
            row = gate[n:n + 1, :]
            beats = (row > gate) | ((row == gate) & (n < n_iota))
            rank = rank + jnp.where(beats, 1.0, 0.0)
        sel_ref[h] = jnp.where((rank < n_sel) & (n_iota < qb), 0.0, NEG_BIG)
        m_ref[h] = jnp.full((1, blk), -jnp.inf, F32)
        l_ref[h] = jnp.zeros((1, blk), F32)
        acc_ref[h] = jnp.zeros((dh, blk), F32)

    def attend(r0, nk, mask_fn):
        ss = [_dot(k_ref[pl.ds(r0, nk), h * dh:(h + 1) * dh], q_t[h]) for h in range(nh)]
        ps, alphas = [], []
        for h in range(nh):
            s = ss[h] * scale + mask_fn(h)
            m_old = m_ref[h]
            m_new = jnp.maximum(m_old, _reduce_rows(s, jnp.max))
            alphas.append(jnp.exp(m_old - m_new))
            p = jnp.exp(s - m_new)
            l_ref[h] = alphas[h] * l_ref[h] + _reduce_rows(p, jnp.sum)
            m_ref[h] = m_new
            ps.append(p.astype(BF16))
        for h in range(nh):
            acc_ref[h] = alphas[h] * acc_ref[h] + _dot(vt_ref[h * dh:(h + 1) * dh, pl.ds(r0, nk)], ps[h])

    def past_mask(h, n):
        return bias_ref[h, jnp.minimum(qb - n, n_bias - 1)] + sel_ref[h, pl.ds(n, 1), :]

    causal = jnp.where(lax.broadcasted_iota(jnp.int32, (blk, blk), 0) <= lax.broadcasted_iota(jnp.int32, (blk, blk), 1),
                       0.0, NEG_BIG)

    def own_mask(h):
        return bias_ref[h, 0] + causal

    def past_pair(i, carry):
        n = 2 * i
        attend(pl.multiple_of(n * blk, 2 * blk), 2 * blk,
               lambda h: jnp.concatenate([past_mask(h, n), past_mask(h, n + 1)], axis=0))
        return carry

    n_pairs = qb // 2
    lax.fori_loop(0, n_pairs, past_pair, 0)

    @pl.when(qb % 2 == 1)
    def _():
        attend(pl.multiple_of((qb - 1) * blk, blk), 2 * blk,
               lambda h: jnp.concatenate([past_mask(h, qb - 1), own_mask(h)], axis=0))

    @pl.when(qb % 2 == 0)
    def _():
        attend(pl.multiple_of(qb * blk, blk), blk, own_mask)

    for h in range(nh):
        o_ref[:, h * dh:(h + 1) * dh] = (acc_ref[h] / l_ref[h]).T.astype(o_ref.dtype)


def _moba(proj_b, bias_tiles, batch, seq):
    blk = MOBA_BLOCK
    nb = seq // blk
    n_sel = min(MOBA_TOPK, nb - 1)
    n_bias = bias_tiles.shape[1]
    return pl.pallas_call(
        functools.partial(_moba_kernel, n_blocks=nb, n_sel=n_sel, n_bias=n_bias),
        out_shape=jax.ShapeDtypeStruct((batch * seq, HEAD_W), BF16),
        grid=(batch, nb),
        in_specs=[
            pl.BlockSpec((blk, HEAD_W), lambda b, i: (b * nb + i, BCOL_MQ // HEAD_W)),
            pl.BlockSpec((seq, HEAD_W), lambda b, i: (b, BCOL_MK // HEAD_W)),
            pl.BlockSpec((seq, HEAD_W), lambda b, i: (b, BCOL_MV // HEAD_W)),
            pl.BlockSpec(bias_tiles.shape, lambda b, i: (0, 0, 0, 0)),
        ],
        out_specs=pl.BlockSpec((blk, HEAD_W), lambda b, i: (b * nb + i, 0)),
        scratch_shapes=[
            pltpu.VMEM((HEAD_W, seq), BF16),
            pltpu.VMEM((MOBA_HEADS, nb, MOBA_DH), F32),
            pltpu.VMEM((MOBA_HEADS, nb, blk), F32),
            pltpu.VMEM((MOBA_HEADS, 1, blk), F32),
            pltpu.VMEM((MOBA_HEADS, 1, blk), F32),
            pltpu.VMEM((MOBA_HEADS, MOBA_DH, blk), F32),
        ],
        compiler_params=_cparams(("parallel", "arbitrary")),
        name="moba",
    )(proj_b, proj_b, proj_b, bias_tiles)


def _sort_key(x):
    bits = pltpu.bitcast(x, jnp.int32)
    return bits ^ ((bits >> 31) & 0x7FFFFFFF)


def _dsa_kernel(dq_ref, iq_ref, iwq_ref, dkv_ref, ikw_ref, kvg_ref, lnw_ref, lnb_ref, wuk_ref, wuv_ref, bias_ref,
                o_ref, ckv_ref, ckvt_ref, ikn_ref, key_ref, mask_ref, m_ref, l_ref, acc_ref, *, seq, n_keep, n_bias):
    tq, kc, nh, dh = DSA_QB, DSA_KC, DSA_HEADS, DSA_DH
    sub = kc // LANES
    qb = pl.program_id(1)
    t0 = qb * tq
    n_kc = (t0 + tq - 1) // kc + 1
    n_rest = (seq - n_kc * kc).astype(F32)

    @pl.when(qb == 0)
    def _():
        for c in range(seq // kc):
            rows = slice(c * kc, (c + 1) * kc)
            x = dkv_ref[rows, :]
            cn = x * lax.rsqrt(jnp.mean(x * x, axis=-1, keepdims=True) + RMS_EPS) * kvg_ref[...]
            ckv_ref[rows, :] = cn.astype(BF16)
            ckvt_ref[:, rows] = cn.T.astype(BF16)
            ik = ikw_ref[rows, 0:IDX_DIM]
            xc = ik - jnp.mean(ik, axis=-1, keepdims=True)
            ikn = xc * lax.rsqrt(jnp.mean(xc * xc, axis=-1, keepdims=True) + RMS_EPS) * lnw_ref[...] + lnb_ref[...]
            ikn_ref[rows, :] = ikn.astype(BF16)

    t_row = t0 + lax.broadcasted_iota(jnp.int32, (1, tq), 1)
    key_off = lax.broadcasted_iota(jnp.int32, (kc, tq), 0)

    def causal_of(c0):
        return (c0 + key_off) <= t_row

    iq_t = iq_ref[...].astype(F32).T
    iq_cat = jnp.concatenate([iq_t[h * IDX_DIM:(h + 1) * IDX_DIM, :] for h in range(IDX_HEADS)], axis=1).astype(BF16)
    iw_t = iwq_ref[...].T[IDX_DIM:IDX_DIM + IDX_HEADS, :] * (IDX_HEADS ** -0.5)
    iw_t = iw_t * (IDX_DIM ** -0.5)

    def score_chunk(c, carry):
        c0 = pl.multiple_of(c * kc, kc)
        s_all = _dot(ikn_ref[pl.ds(c0, kc), :], iq_cat)
        sc = jnp.zeros((kc, tq), F32)
        for h in range(IDX_HEADS):
            sc = sc + iw_t[h:h + 1, :] * jnp.maximum(s_all[:, h * tq:(h + 1) * tq], 0.0)
        sc = jnp.where(causal_of(c0), sc, NEG_BIG)
        sc = jnp.where(sc == 0.0, 0.0, sc)
        key_ref[pl.ds(c0, kc), :] = _sort_key(sc)
        return carry

    lax.fori_loop(0, n_kc, score_chunk, 0)

    def count(pred_fn):
        def body(c, tot):
            c0 = pl.multiple_of(c * kc, kc)
            hit = pred_fn(key_ref[pl.ds(c0, kc), :], c0)
            return tot + _slab(jnp.where(hit, 1.0, 0.0), jnp.sum)
        part = lax.fori_loop(0, n_kc, body, jnp.zeros((ROW_SLAB, tq), F32))
        return _reduce_rows(part, jnp.sum)

    def bit_step(k, base):
        cand = base + jnp.left_shift(jnp.int32(1), 31 - k)
        cnt = count(lambda keys, c0: keys >= cand) + jnp.where(cand <= KEY_NEG, n_rest, 0.0)
        return jnp.where(cnt >= n_keep, cand, base)

    tau = lax.fori_loop(0, 32, bit_step, jnp.full((1, tq), jnp.iinfo(jnp.int32).min, jnp.int32))

    c_gt = count(lambda keys, c0: keys > tau) + jnp.where(tau < KEY_NEG, n_rest, 0.0)
    need = n_keep - c_gt
    n_eq = count(lambda keys, c0: (keys == tau) & causal_of(c0))
    tie_break = jnp.max(n_eq - need) > 0.0

    @pl.when(jnp.logical_not(tie_break))
    def _():
        def body(c, carry):
            c0 = pl.multiple_of(c * kc, kc)
            keep = (key_ref[pl.ds(c0, kc), :] >= tau) & causal_of(c0)
            mask_ref[pl.ds(c0, kc), :] = jnp.where(keep, 0.0, NEG_BIG)
            return carry
        lax.fori_loop(0, n_kc, body, 0)

    @pl.when(tie_break)
    def _():
        lower = jnp.where(lax.broadcasted_iota(jnp.int32, (kc, kc), 1)
                          < lax.broadcasted_iota(jnp.int32, (kc, kc), 0), 1.0, 0.0).astype(BF16)

        def body(c, seen):
            c0 = pl.multiple_of(c * kc, kc)
            keys = key_ref[pl.ds(c0, kc), :]
            eq = keys == tau
            eqf = jnp.where(eq, 1.0, 0.0)
            rank = seen + _dot(lower, eqf.astype(BF16))
            keep = ((keys > tau) | (eq & (rank < need))) & causal_of(c0)
            mask_ref[pl.ds(c0, kc), :] = jnp.where(keep, 0.0, NEG_BIG)
            return seen + _reduce_rows(eqf, jnp.sum)
        lax.fori_loop(0, n_kc, body, jnp.zeros((1, tq), F32))

    scale = dh ** -0.5
    qa_cat = jnp.concatenate(
        [_dot(wuk_ref[h], _t_bf16(dq_ref[:, h * dh:(h + 1) * dh])) for h in range(nh)], axis=1).astype(BF16)
    m_ref[...] = jnp.full(m_ref.shape, -jnp.inf, F32)
    l_ref[...] = jnp.zeros(l_ref.shape, F32)
    acc_ref[...] = jnp.zeros(acc_ref.shape, F32)

    def attend(c, nc):
        c0 = pl.multiple_of(c * kc, kc)
        nk = nc * kc
        msk = mask_ref[pl.ds(c0, nk), :]
        bias = jnp.concatenate(
            [bias_ref[jnp.clip(qb - (c * sub + g), 0, n_bias - 1)] for g in range(nc * sub)], axis=0)
        s = _dot(ckv_ref[pl.ds(c0, nk), :], qa_cat) * scale + bias + jnp.concatenate([msk] * nh, axis=1)
        m_old = m_ref[...]
        m_new = jnp.maximum(m_old, _reduce_rows(s, jnp.max))
        alpha = jnp.exp(m_old - m_new)
        p = jnp.exp(s - m_new)
        l_ref[...] = alpha * l_ref[...] + _reduce_rows(p, jnp.sum)
        acc_ref[...] = alpha * acc_ref[...] + _dot(ckvt_ref[:, pl.ds(c0, nk)], p.astype(BF16))
        m_ref[...] = m_new

    def att_pair(i, carry):
        attend(2 * i, 2)
        return carry

    lax.fori_loop(0, n_kc // 2, att_pair, 0)

    @pl.when(n_kc % 2 == 1)
    def _():
        attend(n_kc - 1, 1)

    o_lat_t = (acc_ref[...] / l_ref[...]).astype(BF16)
    for h in range(nh):
        o_ref[:, h * dh:(h + 1) * dh] = _dot(wuv_ref[h], o_lat_t[:, h * tq:(h + 1) * tq]).T.astype(o_ref.dtype)


def _dsa(proj_f, proj_b, kv_gain, ln_w, ln_b, w_uk_h, w_uv_h, bias_tiles, batch, seq):
    tq = DSA_QB
    nq = seq // tq
    n_keep = min(DSA_TOPK_MAX, seq // 4)
    n_bias = bias_tiles.shape[0]
    return pl.pallas_call(
        functools.partial(_dsa_kernel, seq=seq, n_keep=float(n_keep), n_bias=n_bias),
        out_shape=jax.ShapeDtypeStruct((batch * seq, HEAD_W), BF16),
        grid=(batch, nq),
        in_specs=[
            pl.BlockSpec((tq, HEAD_W), lambda b, i: (b * nq + i, BCOL_DQ // HEAD_W)),
            pl.BlockSpec((tq, HEAD_W), lambda b, i: (b * nq + i, BCOL_IQ // HEAD_W)),
            pl.BlockSpec((tq, LANES), lambda b, i: (b * nq + i, FCOL_IKW // LANES)),
            pl.BlockSpec((seq, DSA_KV_RANK), lambda b, i: (b, FCOL_DKV // DSA_KV_RANK)),
            pl.BlockSpec((seq, LANES), lambda b, i: (b, FCOL_IKW // LANES)),
            pl.BlockSpec((1, DSA_KV_RANK), lambda b, i: (0, 0)),
            pl.BlockSpec((1, IDX_DIM), lambda b, i: (0, 0)),
            pl.BlockSpec((1, IDX_DIM), lambda b, i: (0, 0)),
            pl.BlockSpec(w_uk_h.shape, lambda b, i: (0, 0, 0)),
            pl.BlockSpec(w_uv_h.shape, lambda b, i: (0, 0, 0)),
            pl.BlockSpec(bias_tiles.shape, lambda b, i: (0, 0, 0)),
        ],
        out_specs=pl.BlockSpec((tq, HEAD_W), lambda b, i: (b * nq + i, 0)),
        scratch_shapes=[
            pltpu.VMEM((seq, DSA_KV_RANK), BF16),
            pltpu.VMEM((DSA_KV_RANK, seq), BF16),
            pltpu.VMEM((seq, IDX_DIM), BF16),
            pltpu.VMEM((seq, tq), jnp.int32),
            pltpu.VMEM((seq, tq), F32),
            pltpu.VMEM((1, DSA_HEADS * tq), F32),
            pltpu.VMEM((1, DSA_HEADS * tq), F32),
            pltpu.VMEM((DSA_KV_RANK, DSA_HEADS * tq), F32),
        ],
        compiler_params=_cparams(("parallel", "arbitrary")),
        name="dsa",
    )(proj_b, proj_b, proj_f, proj_f, proj_f, kv_gain, ln_w, ln_b, w_uk_h, w_uv_h, bias_tiles)


def _merge_kernel(x_ref, ya_ref, yb_ref, yc_ref, g0_ref, g1_ref, g2_ref, wa_ref, wb_ref, wc_ref, wo_ref, gain_ref,
                  o_ref):
    merged = (_sigmoid(g0_ref[...]) * _dot(ya_ref[...], wa_ref[...])
              + _sigmoid(g1_ref[...]) * _dot(yb_ref[...], wb_ref[...])
              + _sigmoid(g2_ref[...]) * _dot(yc_ref[...], wc_ref[...]))
    z = _dot(merged.astype(BF16), wo_ref[...])
    zn = z * lax.rsqrt(jnp.mean(z * z, axis=-1, keepdims=True) + RMS_EPS) * gain_ref[...]
    o_ref[...] = x_ref[...] + zn


def _merge(x2, ya, yb, yc, proj_f, wa, wb, wc, wo, gain, tm):
    n = x2.shape[0]
    gcol = FCOL_GATES // D_MODEL

    def rows(w):
        return pl.BlockSpec((tm, w), lambda i: (i, 0))

    def gate(k):
        return pl.BlockSpec((tm, D_MODEL), lambda i, k=k: (i, gcol + k))

    def full(a):
        return pl.BlockSpec(a.shape, lambda i: (0, 0))

    return pl.pallas_call(
        _merge_kernel,
        out_shape=jax.ShapeDtypeStruct((n, D_MODEL), F32),
        grid=(n // tm,),
        in_specs=[rows(D_MODEL), rows(HEAD_W), rows(HEAD_W), rows(HEAD_W), gate(0), gate(1), gate(2),
                  full(wa), full(wb), full(wc), full(wo), full(gain)],
        out_specs=rows(D_MODEL),
        compiler_params=_cparams(("parallel",)),
        name="merge",
    )(x2, ya, yb, yc, proj_f, proj_f, proj_f, wa, wb, wc, wo, gain)


def _ffn_kernel(x_ref, g1_ref, wg_ref, wu_ref, wd_ref, g2_ref, o_ref, hn_ref, acc_ref):
    j = pl.program_id(1)

    @pl.when(j == 0)
    def _():
        x = x_ref[...]
        hn_ref[...] = (x * lax.rsqrt(jnp.mean(x * x, axis=-1, keepdims=True) + RMS_EPS) * g1_ref[...]).astype(BF16)
        acc_ref[...] = jnp.zeros_like(acc_ref)

    hn = hn_ref[...]
    gate = _dot(hn, wg_ref[...])
    up = _dot(hn, wu_ref[...])
    act = (gate * _sigmoid(gate)) * up
    acc_ref[...] += _dot(act.astype(BF16), wd_ref[...])

    @pl.when(j == pl.num_programs(1) - 1)
    def _():
        y = acc_ref[...]
        yn = y * lax.rsqrt(jnp.mean(y * y, axis=-1, keepdims=True) + RMS_EPS) * g2_ref[...]
        o_ref[...] = x_ref[...] + yn


def _ffn(x2, g1, w_in, w_out, g2, tm, tf):
    n = x2.shape[0]
    nf = D_FF // tf
    return pl.pallas_call(
        _ffn_kernel,
        out_shape=jax.ShapeDtypeStruct((n, D_MODEL), F32),
        grid=(n // tm, nf),
        in_specs=[
            pl.BlockSpec((tm, D_MODEL), lambda i, j: (i, 0)),
            pl.BlockSpec((1, D_MODEL), lambda i, j: (0, 0)),
            pl.BlockSpec((D_MODEL, tf), lambda i, j: (0, j)),
            pl.BlockSpec((D_MODEL, tf), lambda i, j: (0, nf + j)),
            pl.BlockSpec((tf, D_MODEL), lambda i, j: (j, 0)),
            pl.BlockSpec((1, D_MODEL), lambda i, j: (0, 0)),
        ],
        out_specs=pl.BlockSpec((tm, D_MODEL), lambda i, j: (i, 0)),
        scratch_shapes=[pltpu.VMEM((tm, D_MODEL), BF16), pltpu.VMEM((tm, D_MODEL), F32)],
        compiler_params=_cparams(("parallel", "arbitrary")),
        name="ffn",
    )(x2, g1, w_in, w_in, w_out, g2)


def _pack_w_in(w):
    o_mq, o_dq, o_dkv, o_iq, o_ik, o_gates = 2048, 3584, 4096, 4352, 4864, 4936
    z = lambda k: jnp.zeros((w.shape[0], k), w.dtype)
    w_f = jnp.concatenate([
        w[:, :o_mq],
        w[:, o_gates:],
        w[:, o_dkv:o_iq],
        w[:, o_ik:o_gates],
        z(PROJ_F_W - FCOL_IKW - (o_gates - o_ik)),
    ], axis=1)
    w_b = jnp.concatenate([
        w[:, o_mq:o_dq],
        w[:, o_dq:o_dkv],
        w[:, o_iq:o_ik],
    ], axis=1)
    assert w_f.shape[1] == PROJ_F_W and w_b.shape[1] == PROJ_B_W
    return w_f.astype(BF16), w_b.astype(BF16)


def _tiles(batch, seq):
    n = batch * seq
    return dict(
        proj_tm=min(1024, n), proj_f_tn=PROJ_F_W // 4, proj_b_tn=PROJ_B_W // 2,
        hgrn_rows=min(512, seq),
        merge_tm=min(512, n),
        ffn_tm=min(1024, n), ffn_tf=D_FF // 2,
    )


def kernel(x, w_in, pre_mix_norm, post_mix_norm, pre_ffn_norm, post_ffn_norm, hg_lb_logits, hg_norm_w, dsa_kv_norm,
           dsa_w_uk, dsa_w_uv, idx_k_norm_w, idx_k_norm_b, rel_bias, w_branch_a, w_branch_b, w_branch_c, w_out,
           w_ffn_in, w_ffn_out):
    batch, seq, _ = x.shape
    depth = w_in.shape[0]
    t = _tiles(batch, seq)
    lbp = jax.nn.softmax(hg_lb_logits.astype(F32), axis=0)
    lower_bounds = jnp.cumsum(lbp, axis=0) - lbp[0]
    bias_b = _bias_tiles_t(rel_bias[:, :MOBA_HEADS], MOBA_BLOCK)
    bias_c = _bias_tiles_t(rel_bias[:, MOBA_HEADS:], LANES)
    bias_c = bias_c.transpose(1, 2, 0, 3).reshape(bias_c.shape[1], LANES, DSA_HEADS * LANES)
    row = lambda a: a.reshape(1, -1).astype(F32)

    x2 = x.reshape(batch * seq, D_MODEL)
    for l in range(depth):
        w_f, w_b = _pack_w_in(w_in[l])
        gain = row(pre_mix_norm[l])
        proj_f = _proj(x2, gain, w_f, F32, t["proj_tm"], t["proj_f_tn"], "proj_f")
        proj_b = _proj(x2, gain, w_b, BF16, t["proj_tm"], t["proj_b_tn"], "proj_b")
        y_a = _hgrn(proj_f, row(lower_bounds[l]), row(hg_norm_w[l]), batch, seq, t["hgrn_rows"])
        y_b = _moba(proj_b, bias_b, batch, seq)
        y_c = _dsa(proj_f, proj_b, row(dsa_kv_norm[l]), row(idx_k_norm_w[l]), row(idx_k_norm_b[l]),
                   jnp.transpose(dsa_w_uk[l], (1, 0, 2)).astype(BF16),
                   jnp.transpose(dsa_w_uv[l], (1, 2, 0)).astype(BF16),
                   bias_c, batch, seq)
        x2 = _merge(x2, y_a, y_b, y_c, proj_f, w_branch_a[l].astype(BF16), w_branch_b[l].astype(BF16),
                    w_branch_c[l].astype(BF16), w_out[l].astype(BF16), row(post_mix_norm[l]), t["merge_tm"])
        x2 = _ffn(x2, row(pre_ffn_norm[l]), w_ffn_in[l].astype(BF16), w_ffn_out[l].astype(BF16),
                  row(post_ffn_norm[l]), t["ffn_tm"], t["ffn_tf"])
    return x2.reshape(batch, seq, D_MODEL)
```

```python
import functools
import math

import numpy as np
import jax
import jax.numpy as jnp
from jax import lax
from jax.experimental import pallas as pl
from jax.experimental.pallas import tpu as pltpu

F32 = jnp.float32
BF16 = jnp.bfloat16

D_MODEL = 1024
RMS_EPS = 1e-6
LOG2E = 1.0 / math.log(2.0)
NEG_BIG = -1e30
TINY = 1e-20
HG_HEADS = 4
HG_D = 128
HG_CHUNK = 64
MOBA_HEADS = 4
MOBA_DH = 128
MOBA_BLOCK = 256
MOBA_TOPK = 3
DSA_HEADS = 4
DSA_DH = 128
DSA_KV_RANK = 256
IDX_HEADS = 8
IDX_DIM = 64
DSA_TOPK_MAX = 256
DSA_KC = 512
REL_BUCKETS = 32
REL_MAX_DIST = 2048
D_FF = 2816
HEAD_W = 512

LANES = 128
SUBLANES = 8
DSA_QB = LANES

FCOL_HQ, FCOL_HF, FCOL_HI, FCOL_HG = 0, 512, 1024, 1536
FCOL_GATES, FCOL_DKV, FCOL_IKW = 2048, 5120, 5376
PROJ_F_W = 5632
BCOL_MQ, BCOL_MK, BCOL_MV, BCOL_DQ, BCOL_IQ = 0, 512, 1024, 1536, 2048
PROJ_B_W = 2560

VMEM_LIMIT = 56 * 1024 * 1024

_REL_EXACT = REL_BUCKETS // 2
REL_FAR_DIST = int(math.ceil(_REL_EXACT * (REL_MAX_DIST / _REL_EXACT) ** ((REL_BUCKETS - _REL_EXACT - 1) / (REL_BUCKETS - _REL_EXACT)))) + 1

_NEG_BITS = int(np.array(NEG_BIG, np.float32).view(np.int32))
KEY_NEG = int(np.int32(_NEG_BITS ^ 0x7FFFFFFF))


def _cparams(sem):
    return pltpu.CompilerParams(dimension_semantics=sem, vmem_limit_bytes=VMEM_LIMIT)


def _sigmoid(x):
    return 1.0 / (1.0 + jnp.exp(-x))


def _dot(a, b):
    return jnp.dot(a, b, preferred_element_type=F32)


def _dot_nt(a, b):
    return lax.dot_general(a, b, (((1,), (1,)), ((), ())), preferred_element_type=F32)


def _t_bf16(a):
    return a.astype(F32).T.astype(BF16)


ROW_SLAB = 64


def _slab(x, fn):
    return fn(x.reshape(x.shape[0] // ROW_SLAB, ROW_SLAB, x.shape[1]), axis=0)


def _reduce_rows(x, fn):
    y = _slab(x, fn)
    z = fn(y.reshape(ROW_SLAB // SUBLANES, SUBLANES, y.shape[1]), axis=0)
    return fn(z, axis=0, keepdims=True)


def _proj_kernel(x_ref, g_ref, w_ref, o_ref, xn_ref):
    @pl.when(pl.program_id(1) == 0)
    def _():
        x = x_ref[...]
        ms = jnp.mean(x * x, axis=-1, keepdims=True)
        xn_ref[...] = (x * lax.rsqrt(ms + RMS_EPS) * g_ref[...]).astype(BF16)

    o_ref[...] = _dot(xn_ref[...], w_ref[...]).astype(o_ref.dtype)


def _proj(x2, gain, w_packed, out_dtype, tm, tn, name):
    n = x2.shape[0]
    width = w_packed.shape[1]
    return pl.pallas_call(
        _proj_kernel,
        out_shape=jax.ShapeDtypeStruct((n, width), out_dtype),
        grid=(n // tm, width // tn),
        in_specs=[
            pl.BlockSpec((tm, D_MODEL), lambda i, j: (i, 0)),
            pl.BlockSpec((1, D_MODEL), lambda i, j: (0, 0)),
            pl.BlockSpec((D_MODEL, tn), lambda i, j: (0, j)),
        ],
        out_specs=pl.BlockSpec((tm, tn), lambda i, j: (i, j)),
        scratch_shapes=[pltpu.VMEM((tm, D_MODEL), BF16)],
        compiler_params=_cparams(("parallel", "arbitrary")),
        name=name,
    )(x2, gain, w_packed)


HG_LEVELS = (32, 16, 8, 4, 2, 1)
HG_UNROLL = 2


def _hgrn_constants():
    c = HG_CHUNK
    t = np.arange(c)
    i = t[:, None]
    tt = t[None, :]
    rows = [tt <= i]
    masks = []
    for m in HG_LEVELS:
        mid = ((t // (2 * m)) * (2 * m) + m)[:, None]
        upper = (i >= mid) & (tt >= mid) & (tt <= i)
        lower = (i < mid) & (tt > i) & (tt <= mid - 1)
        rows.append(upper | lower)
        same = (t[:, None] // (2 * m)) == (t[None, :] // (2 * m))
        masks.append(same & (i >= mid) & (tt < mid))
    masks.append(np.eye(c, dtype=bool))
    m_all = np.concatenate(rows, axis=0).astype(np.float32)
    return m_all, np.stack(masks).astype(np.float32)


def _hgrn_kernel(q_ref, f_ref, i_ref, g_ref, lb_ref, nw_ref, mall_ref, mask_ref, o_ref, st_ref, e_ref, *, n_chunks):
    c = HG_CHUNK
    nl = len(HG_LEVELS)

    @pl.when(pl.program_id(1) == 0)
    def _():
        st_ref[...] = jnp.zeros_like(st_ref)

    lb = lb_ref[...]
    nw = nw_ref[...]

    heads = [slice(h * HG_D, (h + 1) * HG_D) for h in range(HG_HEADS)]

    def intra(r0, u):
        fp = f_ref[pl.ds(r0, c), :]
        f = lb + (1.0 - lb) * _sigmoid(fp)
        logf = jnp.log(jnp.maximum(f, TINY))
        kk = (1.0 - lb) * _sigmoid(-fp)
        g_hi = logf.astype(BF16)
        g_lo = (logf - g_hi.astype(F32)).astype(BF16)
        e_ref[u] = _dot(mall_ref[...], jnp.concatenate([g_hi, g_lo], axis=1))
        q = q_ref[pl.ds(r0, c), :]

        def expo(row):
            return e_ref[u, row * c:(row + 1) * c, 0:HEAD_W] + e_ref[u, row * c:(row + 1) * c, HEAD_W:2 * HEAD_W]

        b = expo(0)
        b_last = b[c - 1:c, :]
        q_in = (q * jnp.exp(b)).astype(BF16)
        k_out = (kk * jnp.exp(b_last - b)).astype(BF16)
        qls, kls = [], []
        for l in range(nl):
            decay = jnp.exp(expo(1 + l))
            qls.append((q * decay).astype(BF16))
            kls.append((kk * decay).astype(BF16))
        qls.append(q.astype(BF16))
        kls.append(kk.astype(BF16))
        sls = [[_dot_nt(qls[l][:, hs], kls[l][:, hs]) for l in range(nl + 1)] for hs in heads]
        ss = []
        for h in range(HG_HEADS):
            s = jnp.zeros((c, c), F32)
            for l in range(nl + 1):
                s = s + jnp.where(mask_ref[l] > 0.0, sls[h][l], 0.0)
            ss.append(s.astype(BF16))
        v = i_ref[pl.ds(r0, c), :].astype(BF16)
        o_intra = [_dot(ss[h], v[:, hs]) for h, hs in enumerate(heads)]
        kv = [_dot(v[:, hs].astype(F32).T.astype(BF16), k_out[:, hs]) for hs in heads]
        return q_in, jnp.exp(b_last), o_intra, kv

    def step(ci, carry):
        parts = [intra(pl.multiple_of((ci * HG_UNROLL + u) * c, c), u) for u in range(HG_UNROLL)]
        for u, (q_in, st_decay, o_intra, kv) in enumerate(parts):
            r0 = pl.multiple_of((ci * HG_UNROLL + u) * c, c)
            og = g_ref[pl.ds(r0, c), :]
            outs = [_dot_nt(q_in[:, hs], st_ref[h].astype(BF16)) + o_intra[h] for h, hs in enumerate(heads)]
            for h, hs in enumerate(heads):
                st_ref[h] = st_ref[h] * st_decay[:, hs] + kv[h]
            for h, hs in enumerate(heads):
                o = outs[h]
                y = o * lax.rsqrt(jnp.mean(o * o, axis=-1, keepdims=True) + RMS_EPS) * nw
                gh = og[:, hs]
                o_ref[pl.ds(r0, c), hs] = (y * (gh * _sigmoid(gh))).astype(o_ref.dtype)
        return carry

    lax.fori_loop(0, n_chunks // HG_UNROLL, step, 0)


def _hgrn(proj_f, lb, norm_w, batch, seq, rows):
    m_all, masks = _hgrn_constants()
    n_rb = seq // rows
    n_chunks = rows // HG_CHUNK

    def col(c0):
        return pl.BlockSpec((rows, HEAD_W), lambda b, r, cidx=c0 // HEAD_W: (b * n_rb + r, cidx))

    return pl.pallas_call(
        functools.partial(_hgrn_kernel, n_chunks=n_chunks),
        out_shape=jax.ShapeDtypeStruct((batch * seq, HEAD_W), BF16),
        grid=(batch, n_rb),
        in_specs=[
            col(FCOL_HQ), col(FCOL_HF), col(FCOL_HI), col(FCOL_HG),
            pl.BlockSpec((1, HEAD_W), lambda b, r: (0, 0)),
            pl.BlockSpec((1, HG_D), lambda b, r: (0, 0)),
            pl.BlockSpec(m_all.shape, lambda b, r: (0, 0)),
            pl.BlockSpec(masks.shape, lambda b, r: (0, 0, 0)),
        ],
        out_specs=pl.BlockSpec((rows, HEAD_W), lambda b, r: (b * n_rb + r, 0)),
        scratch_shapes=[
            pltpu.VMEM((HG_HEADS, HG_D, HG_D), F32),
            pltpu.VMEM((HG_UNROLL, m_all.shape[0], 2 * HEAD_W), F32),
        ],
        compiler_params=_cparams(("parallel", "arbitrary")),
        name="hgrn",
    )(proj_f, proj_f, proj_f, proj_f, lb, norm_w, jnp.asarray(m_all, BF16), jnp.asarray(masks, F32))


def _rel_bucket(dist):
    n = jnp.maximum(dist, 0)
    exact = REL_BUCKETS // 2
    nf = jnp.maximum(n, exact).astype(F32)
    large = exact + (jnp.log(nf / exact) / math.log(REL_MAX_DIST / exact) * (REL_BUCKETS - exact)).astype(jnp.int32)
    large = jnp.minimum(large, REL_BUCKETS - 1)
    return jnp.where(n < exact, n, large)


def _n_bias_tiles(tile):
    d = 0
    while d * tile - (tile - 1) < REL_FAR_DIST:
        d += 1
    return d + 1


def _bias_tiles_t(table, tile):
    n = _n_bias_tiles(tile)
    length = n * tile
    by_dist = (table[_rel_bucket(jnp.arange(length, dtype=jnp.int32))].astype(F32) * LOG2E).T
    heads = by_dist.shape[0]
    w = jnp.concatenate([by_dist, jnp.zeros((heads, 1), F32)], axis=1)
    skew = jnp.broadcast_to(w[:, None, :], (heads, tile, length + 1)).reshape(heads, tile * (length + 1))
    skew = skew[:, :tile * length].reshape(heads, tile, length)
    return skew.reshape(heads, tile, n, tile).transpose(0, 2, 1, 3)


def _moba_kernel(q_ref, k_ref, v_ref, bias_ref, o_ref, vt_ref, km_ref, sel_ref, m_ref, l_ref, acc_ref,
                 *, n_blocks, n_sel, n_bias):
    blk, dh, nh = MOBA_BLOCK, MOBA_DH, MOBA_HEADS
    qb = pl.program_id(1)
    scale2 = dh ** -0.5 * LOG2E

    @pl.when(qb == 0)
    def _():
        for h in range(nh):
            for n in range(n_blocks):
                kblk = k_ref[n * blk:(n + 1) * blk, h * dh:(h + 1) * dh].astype(F32)
                km_ref[h, n:n + 1, :] = jnp.mean(kblk, axis=0, keepdims=True)
                vt_ref[h * dh:(h + 1) * dh, n * blk:(n + 1) * blk] = _t_bf16(v_ref[n * blk:(n + 1) * blk, h * dh:(h + 1) * dh])

    q_t = []
    n_iota = lax.broadcasted_iota(jnp.int32, (n_blocks, blk), 0)
    for h in range(nh):
        qt = _t_bf16(q_ref[:, h * dh:(h + 1) * dh])
        q_t.append(qt)
        gate = _dot(km_ref[h].astype(BF16), qt)
        gate = jnp.where(n_iota < qb, gate, NEG_BIG)
        rank = jnp.zeros(gate.shape, F32)
        for n in range(n_blocks):
            row = gate[n:n + 1, :]
            beats = (row > gate) | ((row == gate) & (n < n_iota))
            rank = rank + jnp.where(beats, 1.0, 0.0)
        sel_ref[h] = jnp.where((rank < n_sel) & (n_iota < qb), 0.0, NEG_BIG)
        m_ref[h] = jnp.full((1, blk), -jnp.inf, F32)
        l_ref[h] = jnp.zeros((1, blk), F32)
        acc_ref[h] = jnp.zeros((dh, blk), F32)

    def attend(r0, nk, mask_fn):
        ss = [_dot(k_ref[pl.ds(r0, nk), h * dh:(h + 1) * dh], q_t[h]) for h in range(nh)]
        ps, alphas = [], []
        for h in range(nh):
            s = ss[h] * scale2 + mask_fn(h)
            m_old = m_ref[h]
            m_new = jnp.maximum(m_old, _reduce_rows(s, jnp.max))
            alphas.append(jnp.exp2(m_old - m_new))
            p = jnp.exp2(s - m_new)
            l_ref[h] = alphas[h] * l_ref[h] + _reduce_rows(p, jnp.sum)
            m_ref[h] = m_new
            ps.append(p.astype(BF16))
        for h in range(nh):
            acc_ref[h] = alphas[h] * acc_ref[h] + _dot(vt_ref[h * dh:(h + 1) * dh, pl.ds(r0, nk)], ps[h])

    def past_mask(h, n):
        return bias_ref[h, jnp.minimum(qb - n, n_bias - 1)] + sel_ref[h, pl.ds(n, 1), :]

    causal = jnp.where(lax.broadcasted_iota(jnp.int32, (blk, blk), 0) <= lax.broadcasted_iota(jnp.int32, (blk, blk), 1),
                       0.0, NEG_BIG)

    def own_mask(h):
        return bias_ref[h, 0] + causal

    def past_pair(i, carry):
        n = 2 * i
        attend(pl.multiple_of(n * blk, 2 * blk), 2 * blk,
               lambda h: jnp.concatenate([past_mask(h, n), past_mask(h, n + 1)], axis=0))
        return carry

    n_pairs = qb // 2
    lax.fori_loop(0, n_pairs, past_pair, 0)

    @pl.when(qb % 2 == 1)
    def _():
        attend(pl.multiple_of((qb - 1) * blk, blk), 2 * blk,
               lambda h: jnp.concatenate([past_mask(h, qb - 1), own_mask(h)], axis=0))

    @pl.when(qb % 2 == 0)
    def _():
        attend(pl.multiple_of(qb * blk, blk), blk, own_mask)

    for h in range(nh):
        o_ref[:, h * dh:(h + 1) * dh] = (acc_ref[h] / l_ref[h]).T.astype(o_ref.dtype)


def _moba(proj_b, bias_tiles, batch, seq):
    blk = MOBA_BLOCK
    nb = seq // blk
    n_sel = min(MOBA_TOPK, nb - 1)
    n_bias = bias_tiles.shape[1]
    return pl.pallas_call(
        functools.partial(_moba_kernel, n_blocks=nb, n_sel=n_sel, n_bias=n_bias),
        out_shape=jax.ShapeDtypeStruct((batch * seq, HEAD_W), BF16),
        grid=(batch, nb),
        in_specs=[
            pl.BlockSpec((blk, HEAD_W), lambda b, i: (b * nb + i, BCOL_MQ // HEAD_W)),
            pl.BlockSpec((seq, HEAD_W), lambda b, i: (b, BCOL_MK // HEAD_W)),
            pl.BlockSpec((seq, HEAD_W), lambda b, i: (b, BCOL_MV // HEAD_W)),
            pl.BlockSpec(bias_tiles.shape, lambda b, i: (0, 0, 0, 0)),
        ],
        out_specs=pl.BlockSpec((blk, HEAD_W), lambda b, i: (b * nb + i, 0)),
        scratch_shapes=[
            pltpu.VMEM((HEAD_W, seq), BF16),
            pltpu.VMEM((MOBA_HEADS, nb, MOBA_DH), F32),
            pltpu.VMEM((MOBA_HEADS, nb, blk), F32),
            pltpu.VMEM((MOBA_HEADS, 1, blk), F32),
            pltpu.VMEM((MOBA_HEADS, 1, blk), F32),
            pltpu.VMEM((MOBA_HEADS, MOBA_DH, blk), F32),
        ],
        compiler_params=_cparams(("parallel", "arbitrary")),
        name="moba",
    )(proj_b, proj_b, proj_b, bias_tiles)


def _sort_key(x):
    bits = pltpu.bitcast(x, jnp.int32)
    return bits ^ ((bits >> 31) & 0x7FFFFFFF)


def _dsa_kernel(dq_ref, iq_ref, iwq_ref, dkv_ref, ikw_ref, kvg_ref, lnw_ref, lnb_ref, wuk_ref, wuv_ref, bias_ref,
                o_ref, ckv_ref, ckvt_ref, ikn_ref, key_ref, mask_ref, m_ref, l_ref, acc_ref, *, seq, n_keep, n_bias):
    tq, kc, nh, dh = DSA_QB, DSA_KC, DSA_HEADS, DSA_DH
    sub = kc // LANES
    qb = pl.program_id(1)
    t0 = qb * tq
    n_kc = (t0 + tq - 1) // kc + 1
    n_rest = (seq - n_kc * kc).astype(F32)

    @pl.when(qb == 0)
    def _():
        for c in range(seq // kc):
            rows = slice(c * kc, (c + 1) * kc)
            x = dkv_ref[rows, :]
            cn = x * lax.rsqrt(jnp.mean(x * x, axis=-1, keepdims=True) + RMS_EPS) * kvg_ref[...]
            ckv_ref[rows, :] = cn.astype(BF16)
            ckvt_ref[:, rows] = cn.T.astype(BF16)
            ik = ikw_ref[rows, 0:IDX_DIM]
            xc = ik - jnp.mean(ik, axis=-1, keepdims=True)
            ikn = xc * lax.rsqrt(jnp.mean(xc * xc, axis=-1, keepdims=True) + RMS_EPS) * lnw_ref[...] + lnb_ref[...]
            ikn_ref[rows, :] = ikn.astype(BF16)

    t_row = t0 + lax.broadcasted_iota(jnp.int32, (1, tq), 1)
    key_off = lax.broadcasted_iota(jnp.int32, (kc, tq), 0)

    def causal_of(c0):
        return (c0 + key_off) <= t_row

    iq_t = iq_ref[...].astype(F32).T
    iq_cat = jnp.concatenate([iq_t[h * IDX_DIM:(h + 1) * IDX_DIM, :] for h in range(IDX_HEADS)], axis=1).astype(BF16)
    iw_t = iwq_ref[...].T[IDX_DIM:IDX_DIM + IDX_HEADS, :] * (IDX_HEADS ** -0.5)
    iw_t = iw_t * (IDX_DIM ** -0.5)

    def score_chunk(c, carry):
        c0 = pl.multiple_of(c * kc, kc)
        ikc = ikn_ref[pl.ds(c0, kc), :]
        hpg = 2 * LANES // tq
        s_grp = [_dot(ikc, iq_cat[:, g * hpg * tq:(g + 1) * hpg * tq]) for g in range(IDX_HEADS // hpg)]
        sc = jnp.zeros((kc, tq), F32)
        for h in range(IDX_HEADS):
            s_h = s_grp[h // hpg][:, (h % hpg) * tq:(h % hpg + 1) * tq]
            sc = sc + iw_t[h:h + 1, :] * jnp.maximum(s_h, 0.0)
        sc = jnp.where(causal_of(c0), sc, NEG_BIG)
        sc = jnp.where(sc == 0.0, 0.0, sc)
        key_ref[pl.ds(c0, kc), :] = _sort_key(sc)
        return carry

    lax.fori_loop(0, n_kc, score_chunk, 0)

    def count(pred_fn):
        def body(c, tot):
            c0 = pl.multiple_of(c * kc, kc)
            hit = pred_fn(key_ref[pl.ds(c0, kc), :], c0)
            return tot + _slab(jnp.where(hit, 1.0, 0.0), jnp.sum)
        part = lax.fori_loop(0, n_kc, body, jnp.zeros((ROW_SLAB, tq), F32))
        return _reduce_rows(part, jnp.sum)

    def bit_step(k, carry):
        base, c_base = carry
        cand = base + jnp.left_shift(jnp.int32(1), 31 - k)
        cnt = count(lambda keys, c0: keys >= cand) + jnp.where(cand <= KEY_NEG, n_rest, 0.0)
        take = cnt >= n_keep
        return jnp.where(take, cand, base), jnp.where(take, cnt, c_base)

    tau, c_ge = lax.fori_loop(0, 32, bit_step, (jnp.full((1, tq), jnp.iinfo(jnp.int32).min, jnp.int32),
                                                jnp.full((1, tq), float(seq), F32)))

    c_gt = count(lambda keys, c0: keys > tau) + jnp.where(tau < KEY_NEG, n_rest, 0.0)
    need = n_keep - c_gt
    n_eq = (c_ge - c_gt) - jnp.where(tau == KEY_NEG, (seq - 1 - t_row).astype(F32), 0.0)
    tie_break = jnp.max(n_eq - need) > 0.0

    @pl.when(jnp.logical_not(tie_break))
    def _():
        def body(c, carry):
            c0 = pl.multiple_of(c * kc, kc)
            keep = (key_ref[pl.ds(c0, kc), :] >= tau) & causal_of(c0)
            mask_ref[pl.ds(c0, kc), :] = jnp.where(keep, 0.0, NEG_BIG)
            return carry
        lax.fori_loop(0, n_kc, body, 0)

    @pl.when(tie_break)
    def _():
        lower = jnp.where(lax.broadcasted_iota(jnp.int32, (kc, kc), 1)
                          < lax.broadcasted_iota(jnp.int32, (kc, kc), 0), 1.0, 0.0).astype(BF16)

        def body(c, seen):
            c0 = pl.multiple_of(c * kc, kc)
            keys = key_ref[pl.ds(c0, kc), :]
            eq = keys == tau
            eqf = jnp.where(eq, 1.0, 0.0)
            rank = seen + _dot(lower, eqf.astype(BF16))
            keep = ((keys > tau) | (eq & (rank < need))) & causal_of(c0)
            mask_ref[pl.ds(c0, kc), :] = jnp.where(keep, 0.0, NEG_BIG)
            return seen + _reduce_rows(eqf, jnp.sum)
        lax.fori_loop(0, n_kc, body, jnp.zeros((1, tq), F32))

    scale2 = dh ** -0.5 * LOG2E
    qa_cat = jnp.concatenate(
        [_dot(wuk_ref[h], _t_bf16(dq_ref[:, h * dh:(h + 1) * dh])) for h in range(nh)], axis=1).astype(BF16)
    m_ref[...] = jnp.full(m_ref.shape, -jnp.inf, F32)
    l_ref[...] = jnp.zeros(l_ref.shape, F32)
    acc_ref[...] = jnp.zeros(acc_ref.shape, F32)

    def attend(c, nc):
        c0 = pl.multiple_of(c * kc, kc)
        nk = nc * kc
        msk = mask_ref[pl.ds(c0, nk), :]
        bias = jnp.concatenate(
            [bias_ref[jnp.clip(qb - (c * sub + g), 0, n_bias - 1)] for g in range(nc * sub)], axis=0)
        ckv = ckv_ref[pl.ds(c0, nk), :]
        halves = [slice(g * 2 * LANES, (g + 1) * 2 * LANES) for g in range(nh * tq // (2 * LANES))]
        ss = [_dot(ckv, qa_cat[:, hv]) for hv in halves]
        ps, alphas = [], []
        for hv, s_raw in zip(halves, ss):
            s = s_raw * scale2 + bias[:, hv] + jnp.concatenate([msk] * (2 * LANES // tq), axis=1)
            m_old = m_ref[:, hv]
            m_new = jnp.maximum(m_old, _reduce_rows(s, jnp.max))
            alpha = jnp.exp2(m_old - m_new)
            p = jnp.exp2(s - m_new)
            l_ref[:, hv] = alpha * l_ref[:, hv] + _reduce_rows(p, jnp.sum)
            m_ref[:, hv] = m_new
            alphas.append(alpha)
            ps.append(p.astype(BF16))
        ckv_t = ckvt_ref[:, pl.ds(c0, nk)]
        for hv, alpha, p in zip(halves, alphas, ps):
            acc_ref[:, hv] = alpha * acc_ref[:, hv] + _dot(ckv_t, p)

    def att_pair(i, carry):
        attend(2 * i, 2)
        return carry

    lax.fori_loop(0, n_kc // 2, att_pair, 0)

    @pl.when(n_kc % 2 == 1)
    def _():
        attend(n_kc - 1, 1)

    o_lat_t = (acc_ref[...] / l_ref[...]).astype(BF16)
    for h in range(nh):
        o_ref[:, h * dh:(h + 1) * dh] = _dot(wuv_ref[h], o_lat_t[:, h * tq:(h + 1) * tq]).T.astype(o_ref.dtype)


def _dsa(proj_f, proj_b, kv_gain, ln_w, ln_b, w_uk_h, w_uv_h, bias_tiles, batch, seq):
    tq = DSA_QB
    nq = seq // tq
    n_keep = min(DSA_TOPK_MAX, seq // 4)
    n_bias = bias_tiles.shape[0]
    return pl.pallas_call(
        functools.partial(_dsa_kernel, seq=seq, n_keep=float(n_keep), n_bias=n_bias),
        out_shape=jax.ShapeDtypeStruct((batch * seq, HEAD_W), BF16),
        grid=(batch, nq),
        in_specs=[
            pl.BlockSpec((tq, HEAD_W), lambda b, i: (b * nq + i, BCOL_DQ // HEAD_W)),
            pl.BlockSpec((tq, HEAD_W), lambda b, i: (b * nq + i, BCOL_IQ // HEAD_W)),
            pl.BlockSpec((tq, LANES), lambda b, i: (b * nq + i, FCOL_IKW // LANES)),
            pl.BlockSpec((seq, DSA_KV_RANK), lambda b, i: (b, FCOL_DKV // DSA_KV_RANK)),
            pl.BlockSpec((seq, LANES), lambda b, i: (b, FCOL_IKW // LANES)),
            pl.BlockSpec((1, DSA_KV_RANK), lambda b, i: (0, 0)),
            pl.BlockSpec((1, IDX_DIM), lambda b, i: (0, 0)),
            pl.BlockSpec((1, IDX_DIM), lambda b, i: (0, 0)),
            pl.BlockSpec(w_uk_h.shape, lambda b, i: (0, 0, 0)),
            pl.BlockSpec(w_uv_h.shape, lambda b, i: (0, 0, 0)),
            pl.BlockSpec(bias_tiles.shape, lambda b, i: (0, 0, 0)),
        ],
        out_specs=pl.BlockSpec((tq, HEAD_W), lambda b, i: (b * nq + i, 0)),
        scratch_shapes=[
            pltpu.VMEM((seq, DSA_KV_RANK), BF16),
            pltpu.VMEM((DSA_KV_RANK, seq), BF16),
            pltpu.VMEM((seq, IDX_DIM), BF16),
            pltpu.VMEM((seq, tq), jnp.int32),
            pltpu.VMEM((seq, tq), F32),
            pltpu.VMEM((1, DSA_HEADS * tq), F32),
            pltpu.VMEM((1, DSA_HEADS * tq), F32),
            pltpu.VMEM((DSA_KV_RANK, DSA_HEADS * tq), F32),
        ],
        compiler_params=_cparams(("parallel", "arbitrary")),
        name="dsa",
    )(proj_b, proj_b, proj_f, proj_f, proj_f, kv_gain, ln_w, ln_b, w_uk_h, w_uv_h, bias_tiles)


def _merge_kernel(x_ref, ya_ref, yb_ref, yc_ref, g0_ref, g1_ref, g2_ref, wa_ref, wb_ref, wc_ref, wo_ref, gain_ref,
                  o_ref):
    merged = (_sigmoid(g0_ref[...]) * _dot(ya_ref[...], wa_ref[...])
              + _sigmoid(g1_ref[...]) * _dot(yb_ref[...], wb_ref[...])
              + _sigmoid(g2_ref[...]) * _dot(yc_ref[...], wc_ref[...]))
    z = _dot(merged.astype(BF16), wo_ref[...])
    zn = z * lax.rsqrt(jnp.mean(z * z, axis=-1, keepdims=True) + RMS_EPS) * gain_ref[...]
    o_ref[...] = x_ref[...] + zn


def _merge(x2, ya, yb, yc, proj_f, wa, wb, wc, wo, gain, tm):
    n = x2.shape[0]
    gcol = FCOL_GATES // D_MODEL

    def rows(w):
        return pl.BlockSpec((tm, w), lambda i: (i, 0))

    def gate(k):
        return pl.BlockSpec((tm, D_MODEL), lambda i, k=k: (i, gcol + k))

    def full(a):
        return pl.BlockSpec(a.shape, lambda i: (0, 0))

    return pl.pallas_call(
        _merge_kernel,
        out_shape=jax.ShapeDtypeStruct((n, D_MODEL), F32),
        grid=(n // tm,),
        in_specs=[rows(D_MODEL), rows(HEAD_W), rows(HEAD_W), rows(HEAD_W), gate(0), gate(1), gate(2),
                  full(wa), full(wb), full(wc), full(wo), full(gain)],
        out_specs=rows(D_MODEL),
        compiler_params=_cparams(("parallel",)),
        name="merge",
    )(x2, ya, yb, yc, proj_f, proj_f, proj_f, wa, wb, wc, wo, gain)


def _ffn_kernel(x_ref, g1_ref, wg_ref, wu_ref, wd_ref, g2_ref, o_ref, hn_ref, acc_ref):
    j = pl.program_id(1)

    @pl.when(j == 0)
    def _():
        x = x_ref[...]
        hn_ref[...] = (x * lax.rsqrt(jnp.mean(x * x, axis=-1, keepdims=True) + RMS_EPS) * g1_ref[...]).astype(BF16)
        acc_ref[...] = jnp.zeros_like(acc_ref)

    hn = hn_ref[...]
    gate = _dot(hn, wg_ref[...])
    up = _dot(hn, wu_ref[...])
    act = (gate * _sigmoid(gate)) * up
    acc_ref[...] += _dot(act.astype(BF16), wd_ref[...])

    @pl.when(j == pl.num_programs(1) - 1)
    def _():
        y = acc_ref[...]
        yn = y * lax.rsqrt(jnp.mean(y * y, axis=-1, keepdims=True) + RMS_EPS) * g2_ref[...]
        o_ref[...] = x_ref[...] + yn


def _ffn(x2, g1, w_in, w_out, g2, tm, tf):
    n = x2.shape[0]
    nf = D_FF // tf
    return pl.pallas_call(
        _ffn_kernel,
        out_shape=jax.ShapeDtypeStruct((n, D_MODEL), F32),
        grid=(n // tm, nf),
        in_specs=[
            pl.BlockSpec((tm, D_MODEL), lambda i, j: (i, 0)),
            pl.BlockSpec((1, D_MODEL), lambda i, j: (0, 0)),
            pl.BlockSpec((D_MODEL, tf), lambda i, j: (0, j)),
            pl.BlockSpec((D_MODEL, tf), lambda i, j: (0, nf + j)),
            pl.BlockSpec((tf, D_MODEL), lambda i, j: (j, 0)),
            pl.BlockSpec((1, D_MODEL), lambda i, j: (0, 0)),
        ],
        out_specs=pl.BlockSpec((tm, D_MODEL), lambda i, j: (i, 0)),
        scratch_shapes=[pltpu.VMEM((tm, D_MODEL), BF16), pltpu.VMEM((tm, D_MODEL), F32)],
        compiler_params=_cparams(("parallel", "arbitrary")),
        name="ffn",
    )(x2, g1, w_in, w_in, w_out, g2)


def _pack_w_in(w):
    o_mq, o_dq, o_dkv, o_iq, o_ik, o_gates = 2048, 3584, 4096, 4352, 4864, 4936
    z = lambda k: jnp.zeros((w.shape[0], k), w.dtype)
    w_f = jnp.concatenate([
        w[:, :o_mq],
        w[:, o_gates:],
        w[:, o_dkv:o_iq],
        w[:, o_ik:o_gates],
        z(PROJ_F_W - FCOL_IKW - (o_gates - o_ik)),
    ], axis=1)
    w_b = jnp.concatenate([
        w[:, o_mq:o_dq],
        w[:, o_dq:o_dkv],
        w[:, o_iq:o_ik],
    ], axis=1)
    assert w_f.shape[1] == PROJ_F_W and w_b.shape[1] == PROJ_B_W
    return w_f.astype(BF16), w_b.astype(BF16)


def _tiles(batch, seq):
    n = batch * seq
    return dict(
        proj_tm=min(1024, n), proj_f_tn=PROJ_F_W // 4, proj_b_tn=PROJ_B_W // 2,
        hgrn_rows=min(512, seq),
        merge_tm=min(512, n),
        ffn_tm=min(1024, n), ffn_tf=D_FF // 2,
    )


def kernel(x, w_in, pre_mix_norm, post_mix_norm, pre_ffn_norm, post_ffn_norm, hg_lb_logits, hg_norm_w, dsa_kv_norm,
           dsa_w_uk, dsa_w_uv, idx_k_norm_w, idx_k_norm_b, rel_bias, w_branch_a, w_branch_b, w_branch_c, w_out,
           w_ffn_in, w_ffn_out):
    batch, seq, _ = x.shape
    depth = w_in.shape[0]
    t = _tiles(batch, seq)
    lbp = jax.nn.softmax(hg_lb_logits.astype(F32), axis=0)
    lower_bounds = jnp.cumsum(lbp, axis=0) - lbp[0]
    bias_b = _bias_tiles_t(rel_bias[:, :MOBA_HEADS], MOBA_BLOCK)
    bias_c = _bias_tiles_t(rel_bias[:, MOBA_HEADS:], LANES)
    bias_c = bias_c.transpose(1, 2, 0, 3).reshape(bias_c.shape[1], LANES, DSA_HEADS * LANES)
    row = lambda a: a.reshape(1, -1).astype(F32)

    x2 = x.reshape(batch * seq, D_MODEL)
    for l in range(depth):
        w_f, w_b = _pack_w_in(w_in[l])
        gain = row(pre_mix_norm[l])
        proj_f = _proj(x2, gain, w_f, F32, t["proj_tm"], t["proj_f_tn"], "proj_f")
        proj_b = _proj(x2, gain, w_b, BF16, t["proj_tm"], t["proj_b_tn"], "proj_b")
        y_a = _hgrn(proj_f, row(lower_bounds[l]), row(hg_norm_w[l]), batch, seq, t["hgrn_rows"])
        y_b = _moba(proj_b, bias_b, batch, seq)
        y_c = _dsa(proj_f, proj_b, row(dsa_kv_norm[l]), row(idx_k_norm_w[l]), row(idx_k_norm_b[l]),
                   jnp.transpose(dsa_w_uk[l], (1, 0, 2)).astype(BF16),
                   jnp.transpose(dsa_w_uv[l], (1, 2, 0)).astype(BF16),
                   bias_c, batch, seq)
        x2 = _merge(x2, y_a, y_b, y_c, proj_f, w_branch_a[l].astype(BF16), w_branch_b[l].astype(BF16),
                    w_branch_c[l].astype(BF16), w_out[l].astype(BF16), row(post_mix_norm[l]), t["merge_tm"])
        x2 = _ffn(x2, row(pre_ffn_norm[l]), w_ffn_in[l].astype(BF16), w_ffn_out[l].astype(BF16),
                  row(post_ffn_norm[l]), t["ffn_tm"], t["ffn_tf"])
    return x2.reshape(batch, seq, D_MODEL)
```

```python
import functools
import math

import numpy as np
import jax
import jax.numpy as jnp
from jax import lax
from jax.experimental import pallas as pl
from jax.experimental.pallas import tpu as pltpu

F32 = jnp.float32
BF16 = jnp.bfloat16

D_MODEL = 1024
RMS_EPS = 1e-6
LOG2E = 1.0 / math.log(2.0)
NEG_BIG = -1e30
TINY = 1e-20
HG_HEADS = 4
HG_D = 128
HG_CHUNK = 64
MOBA_HEADS = 4
MOBA_DH = 128
MOBA_BLOCK = 256
MOBA_TOPK = 3
DSA_HEADS = 4
DSA_DH = 128
DSA_KV_RANK = 256
IDX_HEADS = 8
IDX_DIM = 64
DSA_TOPK_MAX = 256
DSA_KC = 512
DSA_SURE_BITS = 28
REL_BUCKETS = 32
REL_MAX_DIST = 2048
D_FF = 2816
HEAD_W = 512

LANES = 128
SUBLANES = 8
DSA_QB = LANES

FCOL_HQ, FCOL_HF, FCOL_HI, FCOL_HG = 0, 512, 1024, 1536
FCOL_GATES, FCOL_DKV, FCOL_IKW = 2048, 5120, 5376
PROJ_F_W = 5632
BCOL_MQ, BCOL_MK, BCOL_MV, BCOL_DQ, BCOL_IQ = 0, 512, 1024, 1536, 2048
PROJ_B_W = 2560

VMEM_LIMIT = 56 * 1024 * 1024

_REL_EXACT = REL_BUCKETS // 2
REL_FAR_DIST = int(math.ceil(_REL_EXACT * (REL_MAX_DIST / _REL_EXACT) ** ((REL_BUCKETS - _REL_EXACT - 1) / (REL_BUCKETS - _REL_EXACT)))) + 1

_NEG_BITS = int(np.array(NEG_BIG, np.float32).view(np.int32))
KEY_NEG = int(np.int32(_NEG_BITS ^ 0x7FFFFFFF))


def _cparams(sem):
    return pltpu.CompilerParams(dimension_semantics=sem, vmem_limit_bytes=VMEM_LIMIT)


def _sigmoid(x):
    return 1.0 / (1.0 + jnp.exp(-x))


def _dot(a, b):
    return jnp.dot(a, b, preferred_element_type=F32)


def _dot_nt(a, b):
    return lax.dot_general(a, b, (((1,), (1,)), ((), ())), preferred_element_type=F32)


def _t_bf16(a):
    return a.astype(F32).T.astype(BF16)


ROW_SLAB = 64


def _slab(x, fn):
    return fn(x.reshape(x.shape[0] // ROW_SLAB, ROW_SLAB, x.shape[1]), axis=0)


def _reduce_rows(x, fn):
    y = _slab(x, fn)
    z = fn(y.reshape(ROW_SLAB // SUBLANES, SUBLANES, y.shape[1]), axis=0)
    return fn(z, axis=0, keepdims=True)


def _proj_kernel(x_ref, g_ref, w_ref, o_ref, xn_ref):
    @pl.when(pl.program_id(1) == 0)
    def _():
        x = x_ref[...]
        ms = jnp.mean(x * x, axis=-1, keepdims=True)
        xn_ref[...] = (x * lax.rsqrt(ms + RMS_EPS) * g_ref[...]).astype(BF16)

    o_ref[...] = _dot(xn_ref[...], w_ref[...]).astype(o_ref.dtype)


def _proj(x2, gain, w_packed, out_dtype, tm, tn, name):
    n = x2.shape[0]
    width = w_packed.shape[1]
    return pl.pallas_call(
        _proj_kernel,
        out_shape=jax.ShapeDtypeStruct((n, width), out_dtype),
        grid=(n // tm, width // tn),
        in_specs=[
            pl.BlockSpec((tm, D_MODEL), lambda i, j: (i, 0)),
            pl.BlockSpec((1, D_MODEL), lambda i, j: (0, 0)),
            pl.BlockSpec((D_MODEL, tn), lambda i, j: (0, j)),
        ],
        out_specs=pl.BlockSpec((tm, tn), lambda i, j: (i, j)),
        scratch_shapes=[pltpu.VMEM((tm, D_MODEL), BF16)],
        compiler_params=_cparams(("parallel", "arbitrary")),
        name=name,
    )(x2, gain, w_packed)


HG_LEVELS = (32, 16, 8, 4, 2, 1)
HG_UNROLL = 2


def _hgrn_constants():
    c = HG_CHUNK
    t = np.arange(c)
    i = t[:, None]
    tt = t[None, :]
    rows = [tt <= i]
    masks = []
    for m in HG_LEVELS:
        mid = ((t // (2 * m)) * (2 * m) + m)[:, None]
        upper = (i >= mid) & (tt >= mid) & (tt <= i)
        lower = (i < mid) & (tt > i) & (tt <= mid - 1)
        rows.append(upper | lower)
        same = (t[:, None] // (2 * m)) == (t[None, :] // (2 * m))
        masks.append(same & (i >= mid) & (tt < mid))
    masks.append(np.eye(c, dtype=bool))
    m_all = np.concatenate(rows, axis=0).astype(np.float32)
    return m_all, np.stack(masks).astype(np.float32)


def _hgrn_kernel(q_ref, f_ref, i_ref, g_ref, lb_ref, nw_ref, mall_ref, mask_ref, o_ref, st_ref, e_ref, *, n_chunks):
    c = HG_CHUNK
    nl = len(HG_LEVELS)

    @pl.when(pl.program_id(1) == 0)
    def _():
        st_ref[...] = jnp.zeros_like(st_ref)

    lb = lb_ref[...]
    nw = nw_ref[...]

    heads = [slice(h * HG_D, (h + 1) * HG_D) for h in range(HG_HEADS)]

    def intra(r0, u):
        fp = f_ref[pl.ds(r0, c), :]
        f = lb + (1.0 - lb) * _sigmoid(fp)
        logf = jnp.log(jnp.maximum(f, TINY))
        kk = (1.0 - lb) * _sigmoid(-fp)
        g_hi = logf.astype(BF16)
        g_lo = (logf - g_hi.astype(F32)).astype(BF16)
        e_ref[u] = _dot(mall_ref[...], jnp.concatenate([g_hi, g_lo], axis=1))
        q = q_ref[pl.ds(r0, c), :]

        def expo(row):
            return e_ref[u, row * c:(row + 1) * c, 0:HEAD_W] + e_ref[u, row * c:(row + 1) * c, HEAD_W:2 * HEAD_W]

        b = expo(0)
        b_last = b[c - 1:c, :]
        q_in = (q * jnp.exp(b)).astype(BF16)
        k_out = (kk * jnp.exp(b_last - b)).astype(BF16)
        qls, kls = [], []
        for l in range(nl):
            decay = jnp.exp(expo(1 + l))
            qls.append((q * decay).astype(BF16))
            kls.append((kk * decay).astype(BF16))
        qls.append(q.astype(BF16))
        kls.append(kk.astype(BF16))
        sls = [[_dot_nt(qls[l][:, hs], kls[l][:, hs]) for l in range(nl + 1)] for hs in heads]
        ss = []
        for h in range(HG_HEADS):
            s = jnp.zeros((c, c), F32)
            for l in range(nl + 1):
                s = s + jnp.where(mask_ref[l] > 0.0, sls[h][l], 0.0)
            ss.append(s.astype(BF16))
        v = i_ref[pl.ds(r0, c), :].astype(BF16)
        o_intra = [_dot(ss[h], v[:, hs]) for h, hs in enumerate(heads)]
        kv = [_dot(v[:, hs].astype(F32).T.astype(BF16), k_out[:, hs]) for hs in heads]
        return q_in, jnp.exp(b_last), o_intra, kv

    def step(ci, carry):
        parts = [intra(pl.multiple_of((ci * HG_UNROLL + u) * c, c), u) for u in range(HG_UNROLL)]
        for u, (q_in, st_decay, o_intra, kv) in enumerate(parts):
            r0 = pl.multiple_of((ci * HG_UNROLL + u) * c, c)
            og = g_ref[pl.ds(r0, c), :]
            outs = [_dot_nt(q_in[:, hs], st_ref[h].astype(BF16)) + o_intra[h] for h, hs in enumerate(heads)]
            for h, hs in enumerate(heads):
                st_ref[h] = st_ref[h] * st_decay[:, hs] + kv[h]
            for h, hs in enumerate(heads):
                o = outs[h]
                y = o * lax.rsqrt(jnp.mean(o * o, axis=-1, keepdims=True) + RMS_EPS) * nw
                gh = og[:, hs]
                o_ref[pl.ds(r0, c), hs] = (y * (gh * _sigmoid(gh))).astype(o_ref.dtype)
        return carry

    lax.fori_loop(0, n_chunks // HG_UNROLL, step, 0)


def _hgrn(proj_f, lb, norm_w, batch, seq, rows):
    m_all, masks = _hgrn_constants()
    n_rb = seq // rows
    n_chunks = rows // HG_CHUNK

    def col(c0):
        return pl.BlockSpec((rows, HEAD_W), lambda b, r, cidx=c0 // HEAD_W: (b * n_rb + r, cidx))

    return pl.pallas_call(
        functools.partial(_hgrn_kernel, n_chunks=n_chunks),
        out_shape=jax.ShapeDtypeStruct((batch * seq, HEAD_W), BF16),
        grid=(batch, n_rb),
        in_specs=[
            col(FCOL_HQ), col(FCOL_HF), col(FCOL_HI), col(FCOL_HG),
            pl.BlockSpec((1, HEAD_W), lambda b, r: (0, 0)),
            pl.BlockSpec((1, HG_D), lambda b, r: (0, 0)),
            pl.BlockSpec(m_all.shape, lambda b, r: (0, 0)),
            pl.BlockSpec(masks.shape, lambda b, r: (0, 0, 0)),
        ],
        out_specs=pl.BlockSpec((rows, HEAD_W), lambda b, r: (b * n_rb + r, 0)),
        scratch_shapes=[
            pltpu.VMEM((HG_HEADS, HG_D, HG_D), F32),
            pltpu.VMEM((HG_UNROLL, m_all.shape[0], 2 * HEAD_W), F32),
        ],
        compiler_params=_cparams(("parallel", "arbitrary")),
        name="hgrn",
    )(proj_f, proj_f, proj_f, proj_f, lb, norm_w, jnp.asarray(m_all, BF16), jnp.asarray(masks, F32))


def _rel_bucket(dist):
    n = jnp.maximum(dist, 0)
    exact = REL_BUCKETS // 2
    nf = jnp.maximum(n, exact).astype(F32)
    large = exact + (jnp.log(nf / exact) / math.log(REL_MAX_DIST / exact) * (REL_BUCKETS - exact)).astype(jnp.int32)
    large = jnp.minimum(large, REL_BUCKETS - 1)
    return jnp.where(n < exact, n, large)


def _n_bias_tiles(tile):
    d = 0
    while d * tile - (tile - 1) < REL_FAR_DIST:
        d += 1
    return d + 1


def _bias_tiles_t(table, tile):
    n = _n_bias_tiles(tile)
    length = n * tile
    by_dist = (table[_rel_bucket(jnp.arange(length, dtype=jnp.int32))].astype(F32) * LOG2E).T
    heads = by_dist.shape[0]
    w = jnp.concatenate([by_dist, jnp.zeros((heads, 1), F32)], axis=1)
    skew = jnp.broadcast_to(w[:, None, :], (heads, tile, length + 1)).reshape(heads, tile * (length + 1))
    skew = skew[:, :tile * length].reshape(heads, tile, length)
    return skew.reshape(heads, tile, n, tile).transpose(0, 2, 1, 3)


def _moba_kernel(q_ref, k_ref, v_ref, bias_ref, o_ref, vt_ref, km_ref, sel_ref, m_ref, l_ref, acc_ref,
                 *, n_blocks, n_sel, n_bias):
    blk, dh, nh = MOBA_BLOCK, MOBA_DH, MOBA_HEADS
    qb = pl.program_id(1)
    scale2 = dh ** -0.5 * LOG2E

    @pl.when(qb == 0)
    def _():
        for h in range(nh):
            for n in range(n_blocks):
                kblk = k_ref[n * blk:(n + 1) * blk, h * dh:(h + 1) * dh].astype(F32)
                km_ref[h, n:n + 1, :] = jnp.mean(kblk, axis=0, keepdims=True)
                vt_ref[h * dh:(h + 1) * dh, n * blk:(n + 1) * blk] = _t_bf16(v_ref[n * blk:(n + 1) * blk, h * dh:(h + 1) * dh])

    q_t = []
    n_iota = lax.broadcasted_iota(jnp.int32, (n_blocks, blk), 0)
    for h in range(nh):
        qt = _t_bf16(q_ref[:, h * dh:(h + 1) * dh])
        q_t.append(qt)
        gate = _dot(km_ref[h].astype(BF16), qt)
        gate = jnp.where(n_iota < qb, gate, NEG_BIG)
        rank = jnp.zeros(gate.shape, F32)
        for n in range(n_blocks):
            row = gate[n:n + 1, :]
            beats = (row > gate) | ((row == gate) & (n < n_iota))
            rank = rank + jnp.where(beats, 1.0, 0.0)
        sel_ref[h] = jnp.where((rank < n_sel) & (n_iota < qb), 0.0, NEG_BIG)
        m_ref[h] = jnp.full((1, blk), -jnp.inf, F32)
        l_ref[h] = jnp.zeros((1, blk), F32)
        acc_ref[h] = jnp.zeros((dh, blk), F32)

    def attend(r0, nk, mask_fn):
        ss = [_dot(k_ref[pl.ds(r0, nk), h * dh:(h + 1) * dh], q_t[h]) for h in range(nh)]
        ps, alphas = [], []
        for h in range(nh):
            s = ss[h] * scale2 + mask_fn(h)
            m_old = m_ref[h]
            m_new = jnp.maximum(m_old, _reduce_rows(s, jnp.max))
            alphas.append(jnp.exp2(m_old - m_new))
            p = jnp.exp2(s - m_new)
            l_ref[h] = alphas[h] * l_ref[h] + _reduce_rows(p, jnp.sum)
            m_ref[h] = m_new
            ps.append(p.astype(BF16))
        for h in range(nh):
            acc_ref[h] = alphas[h] * acc_ref[h] + _dot(vt_ref[h * dh:(h + 1) * dh, pl.ds(r0, nk)], ps[h])

    def past_mask(h, n):
        return bias_ref[h, jnp.minimum(qb - n, n_bias - 1)] + sel_ref[h, pl.ds(n, 1), :]

    causal = jnp.where(lax.broadcasted_iota(jnp.int32, (blk, blk), 0) <= lax.broadcasted_iota(jnp.int32, (blk, blk), 1),
                       0.0, NEG_BIG)

    def own_mask(h):
        return bias_ref[h, 0] + causal

    def past_pair(i, carry):
        n = 2 * i
        attend(pl.multiple_of(n * blk, 2 * blk), 2 * blk,
               lambda h: jnp.concatenate([past_mask(h, n), past_mask(h, n + 1)], axis=0))
        return carry

    n_pairs = qb // 2
    lax.fori_loop(0, n_pairs, past_pair, 0)

    @pl.when(qb % 2 == 1)
    def _():
        attend(pl.multiple_of((qb - 1) * blk, blk), 2 * blk,
               lambda h: jnp.concatenate([past_mask(h, qb - 1), own_mask(h)], axis=0))

    @pl.when(qb % 2 == 0)
    def _():
        attend(pl.multiple_of(qb * blk, blk), blk, own_mask)

    for h in range(nh):
        o_ref[:, h * dh:(h + 1) * dh] = (acc_ref[h] / l_ref[h]).T.astype(o_ref.dtype)


def _moba(proj_b, bias_tiles, batch, seq):
    blk = MOBA_BLOCK
    nb = seq // blk
    n_sel = min(MOBA_TOPK, nb - 1)
    n_bias = bias_tiles.shape[1]
    return pl.pallas_call(
        functools.partial(_moba_kernel, n_blocks=nb, n_sel=n_sel, n_bias=n_bias),
        out_shape=jax.ShapeDtypeStruct((batch * seq, HEAD_W), BF16),
        grid=(batch, nb),
        in_specs=[
            pl.BlockSpec((blk, HEAD_W), lambda b, i: (b * nb + i, BCOL_MQ // HEAD_W)),
            pl.BlockSpec((seq, HEAD_W), lambda b, i: (b, BCOL_MK // HEAD_W)),
            pl.BlockSpec((seq, HEAD_W), lambda b, i: (b, BCOL_MV // HEAD_W)),
            pl.BlockSpec(bias_tiles.shape, lambda b, i: (0, 0, 0, 0)),
        ],
        out_specs=pl.BlockSpec((blk, HEAD_W), lambda b, i: (b * nb + i, 0)),
        scratch_shapes=[
            pltpu.VMEM((HEAD_W, seq), BF16),
            pltpu.VMEM((MOBA_HEADS, nb, MOBA_DH), F32),
            pltpu.VMEM((MOBA_HEADS, nb, blk), F32),
            pltpu.VMEM((MOBA_HEADS, 1, blk), F32),
            pltpu.VMEM((MOBA_HEADS, 1, blk), F32),
            pltpu.VMEM((MOBA_HEADS, MOBA_DH, blk), F32),
        ],
        compiler_params=_cparams(("parallel", "arbitrary")),
        name="moba",
    )(proj_b, proj_b, proj_b, bias_tiles)


def _sort_key(x):
    bits = pltpu.bitcast(x, jnp.int32)
    return bits ^ ((bits >> 31) & 0x7FFFFFFF)


def _dsa_kernel(dq_ref, iq_ref, iwq_ref, dkv_ref, ikw_ref, kvg_ref, lnw_ref, lnb_ref, wuk_ref, wuv_ref, bias_ref,
                o_ref, ckv_ref, ckvt_ref, ikn_ref, key_ref, mask_ref, tau_ref, cge_ref, m_ref, l_ref, acc_ref,
                *, seq, n_keep, n_bias):
    tq, kc, nh, dh = DSA_QB, DSA_KC, DSA_HEADS, DSA_DH
    sub = kc // LANES
    qb = pl.program_id(1)
    t0 = qb * tq
    n_kc = (t0 + tq - 1) // kc + 1
    n_rest = (seq - n_kc * kc).astype(F32)

    @pl.when(qb == 0)
    def _():
        for c in range(seq // kc):
            rows = slice(c * kc, (c + 1) * kc)
            x = dkv_ref[rows, :]
            cn = x * lax.rsqrt(jnp.mean(x * x, axis=-1, keepdims=True) + RMS_EPS) * kvg_ref[...]
            ckv_ref[rows, :] = cn.astype(BF16)
            ckvt_ref[:, rows] = cn.T.astype(BF16)
            ik = ikw_ref[rows, 0:IDX_DIM]
            xc = ik - jnp.mean(ik, axis=-1, keepdims=True)
            ikn = xc * lax.rsqrt(jnp.mean(xc * xc, axis=-1, keepdims=True) + RMS_EPS) * lnw_ref[...] + lnb_ref[...]
            ikn_ref[rows, :] = ikn.astype(BF16)

    t_row = t0 + lax.broadcasted_iota(jnp.int32, (1, tq), 1)
    key_off = lax.broadcasted_iota(jnp.int32, (kc, tq), 0)

    def causal_of(c0):
        return (c0 + key_off) <= t_row

    iq_t = iq_ref[...].astype(F32).T
    iq_cat = jnp.concatenate([iq_t[h * IDX_DIM:(h + 1) * IDX_DIM, :] for h in range(IDX_HEADS)], axis=1).astype(BF16)
    iw_t = iwq_ref[...].T[IDX_DIM:IDX_DIM + IDX_HEADS, :] * (IDX_HEADS ** -0.5)
    iw_t = iw_t * (IDX_DIM ** -0.5)

    def score_chunk(c, carry):
        c0 = pl.multiple_of(c * kc, kc)
        ikc = ikn_ref[pl.ds(c0, kc), :]
        hpg = 2 * LANES // tq
        s_grp = [_dot(ikc, iq_cat[:, g * hpg * tq:(g + 1) * hpg * tq]) for g in range(IDX_HEADS // hpg)]
        sc = jnp.zeros((kc, tq), F32)
        for h in range(IDX_HEADS):
            s_h = s_grp[h // hpg][:, (h % hpg) * tq:(h % hpg + 1) * tq]
            sc = sc + iw_t[h:h + 1, :] * jnp.maximum(s_h, 0.0)
        sc = jnp.where(causal_of(c0), sc, NEG_BIG)
        sc = jnp.where(sc == 0.0, 0.0, sc)
        key_ref[pl.ds(c0, kc), :] = _sort_key(sc)
        return carry

    lax.fori_loop(0, n_kc, score_chunk, 0)

    def count(pred_fn):
        def body(c, tot):
            c0 = pl.multiple_of(c * kc, kc)
            hit = pred_fn(key_ref[pl.ds(c0, kc), :], c0)
            return tot + _slab(jnp.where(hit, 1.0, 0.0), jnp.sum)
        part = lax.fori_loop(0, n_kc, body, jnp.zeros((ROW_SLAB, tq), F32))
        return _reduce_rows(part, jnp.sum)

    def bit_step(k, carry):
        base, c_base = carry
        cand = base + jnp.left_shift(jnp.int32(1), 31 - k)
        cnt = count(lambda keys, c0: keys >= cand) + jnp.where(cand <= KEY_NEG, n_rest, 0.0)
        take = cnt >= n_keep
        return jnp.where(take, cand, base), jnp.where(take, cnt, c_base)

    tau, c_ge = lax.fori_loop(0, DSA_SURE_BITS, bit_step, (jnp.full((1, tq), jnp.iinfo(jnp.int32).min, jnp.int32),
                                                           jnp.full((1, tq), float(seq), F32)))
    tau_ref[...] = tau
    cge_ref[...] = c_ge

    @pl.when(jnp.max(jnp.abs(c_ge - n_keep)) > 0.0)
    def _():
        tau_ref[...], cge_ref[...] = lax.fori_loop(DSA_SURE_BITS, 32, bit_step, (tau, c_ge))

    tau = tau_ref[...]
    c_ge = cge_ref[...]

    c_gt = count(lambda keys, c0: keys > tau) + jnp.where(tau < KEY_NEG, n_rest, 0.0)
    need = n_keep - c_gt
    n_eq = (c_ge - c_gt) - jnp.where(tau == KEY_NEG, (seq - 1 - t_row).astype(F32), 0.0)
    tie_break = jnp.max(n_eq - need) > 0.0

    @pl.when(jnp.logical_not(tie_break))
    def _():
        def body(c, carry):
            c0 = pl.multiple_of(c * kc, kc)
            keep = (key_ref[pl.ds(c0, kc), :] >= tau) & causal_of(c0)
            mask_ref[pl.ds(c0, kc), :] = jnp.where(keep, 0.0, NEG_BIG)
            return carry
        lax.fori_loop(0, n_kc, body, 0)

    @pl.when(tie_break)
    def _():
        lower = jnp.where(lax.broadcasted_iota(jnp.int32, (kc, kc), 1)
                          < lax.broadcasted_iota(jnp.int32, (kc, kc), 0), 1.0, 0.0).astype(BF16)

        def body(c, seen):
            c0 = pl.multiple_of(c * kc, kc)
            keys = key_ref[pl.ds(c0, kc), :]
            eq = keys == tau
            eqf = jnp.where(eq, 1.0, 0.0)
            rank = seen + _dot(lower, eqf.astype(BF16))
            keep = ((keys > tau) | (eq & (rank < need))) & causal_of(c0)
            mask_ref[pl.ds(c0, kc), :] = jnp.where(keep, 0.0, NEG_BIG)
            return seen + _reduce_rows(eqf, jnp.sum)
        lax.fori_loop(0, n_kc, body, jnp.zeros((1, tq), F32))

    scale2 = dh ** -0.5 * LOG2E
    qa_cat = jnp.concatenate(
        [_dot(wuk_ref[h], _t_bf16(dq_ref[:, h * dh:(h + 1) * dh])) for h in range(nh)], axis=1).astype(BF16)
    m_ref[...] = jnp.full(m_ref.shape, -jnp.inf, F32)
    l_ref[...] = jnp.zeros(l_ref.shape, F32)
    acc_ref[...] = jnp.zeros(acc_ref.shape, F32)

    def attend(c, nc):
        c0 = pl.multiple_of(c * kc, kc)
        nk = nc * kc
        msk = mask_ref[pl.ds(c0, nk), :]
        bias = jnp.concatenate(
            [bias_ref[jnp.clip(qb - (c * sub + g), 0, n_bias - 1)] for g in range(nc * sub)], axis=0)
        s = _dot(ckv_ref[pl.ds(c0, nk), :], qa_cat) * scale2 + bias + jnp.concatenate([msk] * nh, axis=1)
        m_old = m_ref[...]
        m_new = jnp.maximum(m_old, _reduce_rows(s, jnp.max))
        alpha = jnp.exp2(m_old - m_new)
        p = jnp.exp2(s - m_new)
        l_ref[...] = alpha * l_ref[...] + _reduce_rows(p, jnp.sum)
        acc_ref[...] = alpha * acc_ref[...] + _dot(ckvt_ref[:, pl.ds(c0, nk)], p.astype(BF16))
        m_ref[...] = m_new

    def att_pair(i, carry):
        attend(2 * i, 2)
        return carry

    lax.fori_loop(0, n_kc // 2, att_pair, 0)

    @pl.when(n_kc % 2 == 1)
    def _():
        attend(n_kc - 1, 1)

    o_lat_t = (acc_ref[...] / l_ref[...]).astype(BF16)
    for h in range(nh):
        o_ref[:, h * dh:(h + 1) * dh] = _dot(wuv_ref[h], o_lat_t[:, h * tq:(h + 1) * tq]).T.astype(o_ref.dtype)


def _dsa(proj_f, proj_b, kv_gain, ln_w, ln_b, w_uk_h, w_uv_h, bias_tiles, batch, seq):
    tq = DSA_QB
    nq = seq // tq
    n_keep = min(DSA_TOPK_MAX, seq // 4)
    n_bias = bias_tiles.shape[0]
    return pl.pallas_call(
        functools.partial(_dsa_kernel, seq=seq, n_keep=float(n_keep), n_bias=n_bias),
        out_shape=jax.ShapeDtypeStruct((batch * seq, HEAD_W), BF16),
        grid=(batch, nq),
        in_specs=[
            pl.BlockSpec((tq, HEAD_W), lambda b, i: (b * nq + i, BCOL_DQ // HEAD_W)),
            pl.BlockSpec((tq, HEAD_W), lambda b, i: (b * nq + i, BCOL_IQ // HEAD_W)),
            pl.BlockSpec((tq, LANES), lambda b, i: (b * nq + i, FCOL_IKW // LANES)),
            pl.BlockSpec((seq, DSA_KV_RANK), lambda b, i: (b, FCOL_DKV // DSA_KV_RANK)),
            pl.BlockSpec((seq, LANES), lambda b, i: (b, FCOL_IKW // LANES)),
            pl.BlockSpec((1, DSA_KV_RANK), lambda b, i: (0, 0)),
            pl.BlockSpec((1, IDX_DIM), lambda b, i: (0, 0)),
            pl.BlockSpec((1, IDX_DIM), lambda b, i: (0, 0)),
            pl.BlockSpec(w_uk_h.shape, lambda b, i: (0, 0, 0)),
            pl.BlockSpec(w_uv_h.shape, lambda b, i: (0, 0, 0)),
            pl.BlockSpec(bias_tiles.shape, lambda b, i: (0, 0, 0)),
        ],
        out_specs=pl.BlockSpec((tq, HEAD_W), lambda b, i: (b * nq + i, 0)),
        scratch_shapes=[
            pltpu.VMEM((seq, DSA_KV_RANK), BF16),
            pltpu.VMEM((DSA_KV_RANK, seq), BF16),
            pltpu.VMEM((seq, IDX_DIM), BF16),
            pltpu.VMEM((seq, tq), jnp.int32),
            pltpu.VMEM((seq, tq), F32),
            pltpu.VMEM((1, tq), jnp.int32),
            pltpu.VMEM((1, tq), F32),
            pltpu.VMEM((1, DSA_HEADS * tq), F32),
            pltpu.VMEM((1, DSA_HEADS * tq), F32),
            pltpu.VMEM((DSA_KV_RANK, DSA_HEADS * tq), F32),
        ],
        compiler_params=_cparams(("parallel", "arbitrary")),
        name="dsa",
    )(proj_b, proj_b, proj_f, proj_f, proj_f, kv_gain, ln_w, ln_b, w_uk_h, w_uv_h, bias_tiles)


def _merge_kernel(x_ref, ya_ref, yb_ref, yc_ref, g0_ref, g1_ref, g2_ref, wa_ref, wb_ref, wc_ref, wo_ref, gain_ref,
                  o_ref):
    merged = (_sigmoid(g0_ref[...]) * _dot(ya_ref[...], wa_ref[...])
              + _sigmoid(g1_ref[...]) * _dot(yb_ref[...], wb_ref[...])
              + _sigmoid(g2_ref[...]) * _dot(yc_ref[...], wc_ref[...]))
    z = _dot(merged.astype(BF16), wo_ref[...])
    zn = z * lax.rsqrt(jnp.mean(z * z, axis=-1, keepdims=True) + RMS_EPS) * gain_ref[...]
    o_ref[...] = x_ref[...] + zn


def _merge(x2, ya, yb, yc, proj_f, wa, wb, wc, wo, gain, tm):
    n = x2.shape[0]
    gcol = FCOL_GATES // D_MODEL

    def rows(w):
        return pl.BlockSpec((tm, w), lambda i: (i, 0))

    def gate(k):
        return pl.BlockSpec((tm, D_MODEL), lambda i, k=k: (i, gcol + k))

    def full(a):
        return pl.BlockSpec(a.shape, lambda i: (0, 0))

    return pl.pallas_call(
        _merge_kernel,
        out_shape=jax.ShapeDtypeStruct((n, D_MODEL), F32),
        grid=(n // tm,),
        in_specs=[rows(D_MODEL), rows(HEAD_W), rows(HEAD_W), rows(HEAD_W), gate(0), gate(1), gate(2),
                  full(wa), full(wb), full(wc), full(wo), full(gain)],
        out_specs=rows(D_MODEL),
        compiler_params=_cparams(("parallel",)),
        name="merge",
    )(x2, ya, yb, yc, proj_f, proj_f, proj_f, wa, wb, wc, wo, gain)


def _ffn_kernel(x_ref, g1_ref, wg_ref, wu_ref, wd_ref, g2_ref, o_ref, hn_ref, acc_ref):
    j = pl.program_id(1)

    @pl.when(j == 0)
    def _():
        x = x_ref[...]
        hn_ref[...] = (x * lax.rsqrt(jnp.mean(x * x, axis=-1, keepdims=True) + RMS_EPS) * g1_ref[...]).astype(BF16)
        acc_ref[...] = jnp.zeros_like(acc_ref)

    hn = hn_ref[...]
    gate = _dot(hn, wg_ref[...])
    up = _dot(hn, wu_ref[...])
    act = (gate * _sigmoid(gate)) * up
    acc_ref[...] += _dot(act.astype(BF16), wd_ref[...])

    @pl.when(j == pl.num_programs(1) - 1)
    def _():
        y = acc_ref[...]
        yn = y * lax.rsqrt(jnp.mean(y * y, axis=-1, keepdims=True) + RMS_EPS) * g2_ref[...]
        o_ref[...] = x_ref[...] + yn


def _ffn(x2, g1, w_in, w_out, g2, tm, tf):
    n = x2.shape[0]
    nf = D_FF // tf
    return pl.pallas_call(
        _ffn_kernel,
        out_shape=jax.ShapeDtypeStruct((n, D_MODEL), F32),
        grid=(n // tm, nf),
        in_specs=[
            pl.BlockSpec((tm, D_MODEL), lambda i, j: (i, 0)),
            pl.BlockSpec((1, D_MODEL), lambda i, j: (0, 0)),
            pl.BlockSpec((D_MODEL, tf), lambda i, j: (0, j)),
            pl.BlockSpec((D_MODEL, tf), lambda i, j: (0, nf + j)),
            pl.BlockSpec((tf, D_MODEL), lambda i, j: (j, 0)),
            pl.BlockSpec((1, D_MODEL), lambda i, j: (0, 0)),
        ],
        out_specs=pl.BlockSpec((tm, D_MODEL), lambda i, j: (i, 0)),
        scratch_shapes=[pltpu.VMEM((tm, D_MODEL), BF16), pltpu.VMEM((tm, D_MODEL), F32)],
        compiler_params=_cparams(("parallel", "arbitrary")),
        name="ffn",
    )(x2, g1, w_in, w_in, w_out, g2)


def _pack_w_in(w):
    o_mq, o_dq, o_dkv, o_iq, o_ik, o_gates = 2048, 3584, 4096, 4352, 4864, 4936
    w = w.astype(BF16)
    z = lambda k: jnp.zeros((w.shape[0], k), w.dtype)
    w_f = jnp.concatenate([
        w[:, :o_mq],
        w[:, o_gates:],
        w[:, o_dkv:o_iq],
        w[:, o_ik:o_gates],
        z(PROJ_F_W - FCOL_IKW - (o_gates - o_ik)),
    ], axis=1)
    w_b = jnp.concatenate([
        w[:, o_mq:o_dq],
        w[:, o_dq:o_dkv],
        w[:, o_iq:o_ik],
    ], axis=1)
    assert w_f.shape[1] == PROJ_F_W and w_b.shape[1] == PROJ_B_W
    return w_f, w_b


def _tiles(batch, seq):
    n = batch * seq
    return dict(
        proj_tm=min(1024, n), proj_f_tn=PROJ_F_W // 4, proj_b_tm=min(2048, n), proj_b_tn=PROJ_B_W // 2,
        hgrn_rows=min(512, seq),
        merge_tm=min(512, n),
        ffn_tm=min(1024, n), ffn_tf=D_FF // 2,
    )


def kernel(x, w_in, pre_mix_norm, post_mix_norm, pre_ffn_norm, post_ffn_norm, hg_lb_logits, hg_norm_w, dsa_kv_norm,
           dsa_w_uk, dsa_w_uv, idx_k_norm_w, idx_k_norm_b, rel_bias, w_branch_a, w_branch_b, w_branch_c, w_out,
           w_ffn_in, w_ffn_out):
    batch, seq, _ = x.shape
    depth = w_in.shape[0]
    t = _tiles(batch, seq)
    lbp = jax.nn.softmax(hg_lb_logits.astype(F32), axis=0)
    lower_bounds = jnp.cumsum(lbp, axis=0) - lbp[0]
    bias_b = _bias_tiles_t(rel_bias[:, :MOBA_HEADS], MOBA_BLOCK)
    bias_c = _bias_tiles_t(rel_bias[:, MOBA_HEADS:], LANES)
    bias_c = bias_c.transpose(1, 2, 0, 3).reshape(bias_c.shape[1], LANES, DSA_HEADS * LANES)
    row = lambda a: a.reshape(1, -1).astype(F32)

    x2 = x.reshape(batch * seq, D_MODEL)
    for l in range(depth):
        w_f, w_b = _pack_w_in(w_in[l])
        gain = row(pre_mix_norm[l])
        proj_f = _proj(x2, gain, w_f, F32, t["proj_tm"], t["proj_f_tn"], "proj_f")
        proj_b = _proj(x2, gain, w_b, BF16, t["proj_b_tm"], t["proj_b_tn"], "proj_b")
        y_a = _hgrn(proj_f, row(lower_bounds[l]), row(hg_norm_w[l]), batch, seq, t["hgrn_rows"])
        y_b = _moba(proj_b, bias_b, batch, seq)
        y_c = _dsa(proj_f, proj_b, row(dsa_kv_norm[l]), row(idx_k_norm_w[l]), row(idx_k_norm_b[l]),
                   jnp.transpose(dsa_w_uk[l], (1, 0, 2)).astype(BF16),
                   jnp.transpose(dsa_w_uv[l], (1, 2, 0)).astype(BF16),
                   bias_c, batch, seq)
        x2 = _merge(x2, y_a, y_b, y_c, proj_f, w_branch_a[l].astype(BF16), w_branch_b[l].astype(BF16),
                    w_branch_c[l].astype(BF16), w_out[l].astype(BF16), row(post_mix_norm[l]), t["merge_tm"])
        x2 = _ffn(x2, row(pre_ffn_norm[l]), w_ffn_in[l].astype(BF16), w_ffn_out[l].astype(BF16),
                  row(post_ffn_norm[l]), t["ffn_tm"], t["ffn_tf"])
    return x2.reshape(batch, seq, D_MODEL)
```

```python
import functools
import math

import numpy as np
import jax
import jax.numpy as jnp
from jax import lax
from jax.experimental import pallas as pl
from jax.experimental.pallas import tpu as pltpu

F32 = jnp.float32
BF16 = jnp.bfloat16

D_MODEL = 1024
RMS_EPS = 1e-6
LOG2E = 1.0 / math.log(2.0)
NEG_BIG = -1e30
TINY = 1e-20
HG_HEADS = 4
HG_D = 128
HG_CHUNK = 64
MOBA_HEADS = 4
MOBA_DH = 128
MOBA_BLOCK = 256
MOBA_TOPK = 3
DSA_HEADS = 4
DSA_DH = 128
DSA_KV_RANK = 256
IDX_HEADS = 8
IDX_DIM = 64
DSA_TOPK_MAX = 256
DSA_KC = 512
DSA_SURE_BITS = 28
REL_BUCKETS = 32
REL_MAX_DIST = 2048
D_FF = 2816
HEAD_W = 512

LANES = 128
SUBLANES = 8
DSA_QB = LANES

FCOL_HQ, FCOL_HF, FCOL_HI, FCOL_HG = 0, 512, 1024, 1536
FCOL_GATES, FCOL_DKV, FCOL_IKW = 2048, 5120, 5376
PROJ_F_W = 5632
BCOL_MQ, BCOL_MK, BCOL_MV, BCOL_DQ, BCOL_IQ = 0, 512, 1024, 1536, 2048
PROJ_B_W = 2560

VMEM_LIMIT = 56 * 1024 * 1024

_REL_EXACT = REL_BUCKETS // 2
REL_FAR_DIST = int(math.ceil(_REL_EXACT * (REL_MAX_DIST / _REL_EXACT) ** ((REL_BUCKETS - _REL_EXACT - 1) / (REL_BUCKETS - _REL_EXACT)))) + 1

_NEG_BITS = int(np.array(NEG_BIG, np.float32).view(np.int32))
KEY_NEG = int(np.int32(_NEG_BITS ^ 0x7FFFFFFF))


def _cparams(sem):
    return pltpu.CompilerParams(dimension_semantics=sem, vmem_limit_bytes=VMEM_LIMIT)


def _sigmoid(x):
    return 1.0 / (1.0 + jnp.exp(-x))


def _dot(a, b):
    return jnp.dot(a, b, preferred_element_type=F32)


def _dot_nt(a, b):
    return lax.dot_general(a, b, (((1,), (1,)), ((), ())), preferred_element_type=F32)


def _t_bf16(a):
    return a.astype(F32).T.astype(BF16)


ROW_SLAB = 64


def _slab(x, fn):
    return fn(x.reshape(x.shape[0] // ROW_SLAB, ROW_SLAB, x.shape[1]), axis=0)


def _reduce_rows(x, fn):
    y = _slab(x, fn)
    z = fn(y.reshape(ROW_SLAB // SUBLANES, SUBLANES, y.shape[1]), axis=0)
    return fn(z, axis=0, keepdims=True)


def _proj_kernel(x_ref, g_ref, w_ref, o_ref, xn_ref):
    @pl.when(pl.program_id(1) == 0)
    def _():
        x = x_ref[...]
        ms = jnp.mean(x * x, axis=-1, keepdims=True)
        xn_ref[...] = (x * lax.rsqrt(ms + RMS_EPS) * g_ref[...]).astype(BF16)

    o_ref[...] = _dot(xn_ref[...], w_ref[...]).astype(o_ref.dtype)


def _proj(x2, gain, w_packed, out_dtype, tm, tn, name):
    n = x2.shape[0]
    width = w_packed.shape[1]
    return pl.pallas_call(
        _proj_kernel,
        out_shape=jax.ShapeDtypeStruct((n, width), out_dtype),
        grid=(n // tm, width // tn),
        in_specs=[
            pl.BlockSpec((tm, D_MODEL), lambda i, j: (i, 0)),
            pl.BlockSpec((1, D_MODEL), lambda i, j: (0, 0)),
            pl.BlockSpec((D_MODEL, tn), lambda i, j: (0, j)),
        ],
        out_specs=pl.BlockSpec((tm, tn), lambda i, j: (i, j)),
        scratch_shapes=[pltpu.VMEM((tm, D_MODEL), BF16)],
        compiler_params=_cparams(("parallel", "arbitrary")),
        name=name,
    )(x2, gain, w_packed)


HG_LEVELS = (32, 16, 8, 4, 2, 1)
HG_UNROLL = 8


def _hgrn_constants():
    c = HG_CHUNK
    t = np.arange(c)
    i = t[:, None]
    tt = t[None, :]
    rows = [tt <= i]
    masks = []
    for m in HG_LEVELS:
        mid = ((t // (2 * m)) * (2 * m) + m)[:, None]
        upper = (i >= mid) & (tt >= mid) & (tt <= i)
        lower = (i < mid) & (tt > i) & (tt <= mid - 1)
        rows.append(upper | lower)
        same = (t[:, None] // (2 * m)) == (t[None, :] // (2 * m))
        masks.append(same & (i >= mid) & (tt < mid))
    masks.append(np.eye(c, dtype=bool))
    m_all = np.concatenate(rows, axis=0).astype(np.float32)
    return m_all, np.stack(masks).astype(np.float32)


def _hgrn_kernel(q_ref, f_ref, i_ref, g_ref, lb_ref, nw_ref, mall_ref, mask_ref, o_ref, st_ref, e_ref, *, n_chunks):
    c = HG_CHUNK
    nl = len(HG_LEVELS)

    @pl.when(pl.program_id(1) == 0)
    def _():
        st_ref[...] = jnp.zeros_like(st_ref)

    lb = lb_ref[...]
    nw = nw_ref[...]

    heads = [slice(h * HG_D, (h + 1) * HG_D) for h in range(HG_HEADS)]

    def intra(r0, u):
        fp = f_ref[pl.ds(r0, c), :]
        f = lb + (1.0 - lb) * _sigmoid(fp)
        logf = jnp.log(jnp.maximum(f, TINY))
        kk = (1.0 - lb) * _sigmoid(-fp)
        g_hi = logf.astype(BF16)
        g_lo = (logf - g_hi.astype(F32)).astype(BF16)
        e_ref[u] = _dot(mall_ref[...], jnp.concatenate([g_hi, g_lo], axis=1))
        q = q_ref[pl.ds(r0, c), :]

        def expo(row):
            return e_ref[u, row * c:(row + 1) * c, 0:HEAD_W] + e_ref[u, row * c:(row + 1) * c, HEAD_W:2 * HEAD_W]

        b = expo(0)
        b_last = b[c - 1:c, :]
        q_in = (q * jnp.exp(b)).astype(BF16)
        k_out = (kk * jnp.exp(b_last - b)).astype(BF16)
        qls, kls = [], []
        for l in range(nl):
            decay = jnp.exp(expo(1 + l))
            qls.append((q * decay).astype(BF16))
            kls.append((kk * decay).astype(BF16))
        qls.append(q.astype(BF16))
        kls.append(kk.astype(BF16))
        sls = [[_dot_nt(qls[l][:, hs], kls[l][:, hs]) for l in range(nl + 1)] for hs in heads]
        ss = []
        for h in range(HG_HEADS):
            s = jnp.zeros((c, c), F32)
            for l in range(nl + 1):
                s = s + jnp.where(mask_ref[l] > 0.0, sls[h][l], 0.0)
            ss.append(s.astype(BF16))
        v = i_ref[pl.ds(r0, c), :].astype(BF16)
        o_intra = [_dot(ss[h], v[:, hs]) for h, hs in enumerate(heads)]
        kv = [_dot(v[:, hs].astype(F32).T.astype(BF16), k_out[:, hs]) for hs in heads]
        return q_in, jnp.exp(b_last), o_intra, kv

    def step(ci, carry):
        parts = [intra(pl.multiple_of((ci * HG_UNROLL + u) * c, c), u) for u in range(HG_UNROLL)]
        for u, (q_in, st_decay, o_intra, kv) in enumerate(parts):
            r0 = pl.multiple_of((ci * HG_UNROLL + u) * c, c)
            og = g_ref[pl.ds(r0, c), :]
            outs = [_dot_nt(q_in[:, hs], st_ref[h].astype(BF16)) + o_intra[h] for h, hs in enumerate(heads)]
            for h, hs in enumerate(heads):
                st_ref[h] = st_ref[h] * st_decay[:, hs] + kv[h]
            for h, hs in enumerate(heads):
                o = outs[h]
                y = o * lax.rsqrt(jnp.mean(o * o, axis=-1, keepdims=True) + RMS_EPS) * nw
                gh = og[:, hs]
                o_ref[pl.ds(r0, c), hs] = (y * (gh * _sigmoid(gh))).astype(o_ref.dtype)
        return carry

    lax.fori_loop(0, n_chunks // HG_UNROLL, step, 0)


def _hgrn(proj_f, lb, norm_w, batch, seq, rows):
    m_all, masks = _hgrn_constants()
    n_rb = seq // rows
    n_chunks = rows // HG_CHUNK

    def col(c0):
        return pl.BlockSpec((rows, HEAD_W), lambda b, r, cidx=c0 // HEAD_W: (b * n_rb + r, cidx))

    return pl.pallas_call(
        functools.partial(_hgrn_kernel, n_chunks=n_chunks),
        out_shape=jax.ShapeDtypeStruct((batch * seq, HEAD_W), BF16),
        grid=(batch, n_rb),
        in_specs=[
            col(FCOL_HQ), col(FCOL_HF), col(FCOL_HI), col(FCOL_HG),
            pl.BlockSpec((1, HEAD_W), lambda b, r: (0, 0)),
            pl.BlockSpec((1, HG_D), lambda b, r: (0, 0)),
            pl.BlockSpec(m_all.shape, lambda b, r: (0, 0)),
            pl.BlockSpec(masks.shape, lambda b, r: (0, 0, 0)),
        ],
        out_specs=pl.BlockSpec((rows, HEAD_W), lambda b, r: (b * n_rb + r, 0)),
        scratch_shapes=[
            pltpu.VMEM((HG_HEADS, HG_D, HG_D), F32),
            pltpu.VMEM((HG_UNROLL, m_all.shape[0], 2 * HEAD_W), F32),
        ],
        compiler_params=_cparams(("parallel", "arbitrary")),
        name="hgrn",
    )(proj_f, proj_f, proj_f, proj_f, lb, norm_w, jnp.asarray(m_all, BF16), jnp.asarray(masks, F32))


def _rel_bucket(dist):
    n = jnp.maximum(dist, 0)
    exact = REL_BUCKETS // 2
    nf = jnp.maximum(n, exact).astype(F32)
    large = exact + (jnp.log(nf / exact) / math.log(REL_MAX_DIST / exact) * (REL_BUCKETS - exact)).astype(jnp.int32)
    large = jnp.minimum(large, REL_BUCKETS - 1)
    return jnp.where(n < exact, n, large)


def _n_bias_tiles(tile):
    d = 0
    while d * tile - (tile - 1) < REL_FAR_DIST:
        d += 1
    return d + 1


def _bias_tiles_t(table, tile):
    n = _n_bias_tiles(tile)
    length = n * tile
    by_dist = (table[_rel_bucket(jnp.arange(length, dtype=jnp.int32))].astype(F32) * LOG2E).T
    heads = by_dist.shape[0]
    w = jnp.concatenate([by_dist, jnp.zeros((heads, 1), F32)], axis=1)
    skew = jnp.broadcast_to(w[:, None, :], (heads, tile, length + 1)).reshape(heads, tile * (length + 1))
    skew = skew[:, :tile * length].reshape(heads, tile, length)
    return skew.reshape(heads, tile, n, tile).transpose(0, 2, 1, 3)


def _moba_kernel(q_ref, k_ref, v_ref, bias_ref, o_ref, vt_ref, km_ref, sel_ref, m_ref, l_ref, acc_ref,
                 *, n_blocks, n_sel, n_bias):
    blk, dh, nh = MOBA_BLOCK, MOBA_DH, MOBA_HEADS
    qb = pl.program_id(1)
    scale2 = dh ** -0.5 * LOG2E

    @pl.when(qb == 0)
    def _():
        for h in range(nh):
            for n in range(n_blocks):
                kblk = k_ref[n * blk:(n + 1) * blk, h * dh:(h + 1) * dh].astype(F32)
                km_ref[h, n:n + 1, :] = jnp.mean(kblk, axis=0, keepdims=True)
                vt_ref[h * dh:(h + 1) * dh, n * blk:(n + 1) * blk] = _t_bf16(v_ref[n * blk:(n + 1) * blk, h * dh:(h + 1) * dh])

    q_t = []
    n_iota = lax.broadcasted_iota(jnp.int32, (n_blocks, blk), 0)
    for h in range(nh):
        qt = _t_bf16(q_ref[:, h * dh:(h + 1) * dh])
        q_t.append(qt)
        gate = _dot(km_ref[h].astype(BF16), qt)
        gate = jnp.where(n_iota < qb, gate, NEG_BIG)
        rank = jnp.zeros(gate.shape, F32)
        for n in range(n_blocks):
            row = gate[n:n + 1, :]
            beats = (row > gate) | ((row == gate) & (n < n_iota))
            rank = rank + jnp.where(beats, 1.0, 0.0)
        sel_ref[h] = jnp.where((rank < n_sel) & (n_iota < qb), 0.0, NEG_BIG)
        m_ref[h] = jnp.full((1, blk), -jnp.inf, F32)
        l_ref[h] = jnp.zeros((1, blk), F32)
        acc_ref[h] = jnp.zeros((dh, blk), F32)

    def attend(r0, nk, mask_fn):
        ss = [_dot(k_ref[pl.ds(r0, nk), h * dh:(h + 1) * dh], q_t[h]) for h in range(nh)]
        ps, alphas = [], []
        for h in range(nh):
            s = ss[h] * scale2 + mask_fn(h)
            m_old = m_ref[h]
            m_new = jnp.maximum(m_old, _reduce_rows(s, jnp.max))
            alphas.append(jnp.exp2(m_old - m_new))
            p = jnp.exp2(s - m_new)
            l_ref[h] = alphas[h] * l_ref[h] + _reduce_rows(p, jnp.sum)
            m_ref[h] = m_new
            ps.append(p.astype(BF16))
        for h in range(nh):
            acc_ref[h] = alphas[h] * acc_ref[h] + _dot(vt_ref[h * dh:(h + 1) * dh, pl.ds(r0, nk)], ps[h])

    def past_mask(h, n):
        return bias_ref[h, jnp.minimum(qb - n, n_bias - 1)] + sel_ref[h, pl.ds(n, 1), :]

    causal = jnp.where(lax.broadcasted_iota(jnp.int32, (blk, blk), 0) <= lax.broadcasted_iota(jnp.int32, (blk, blk), 1),
                       0.0, NEG_BIG)

    def own_mask(h):
        return bias_ref[h, 0] + causal

    def past_pair(i, carry):
        n = 2 * i
        attend(pl.multiple_of(n * blk, 2 * blk), 2 * blk,
               lambda h: jnp.concatenate([past_mask(h, n), past_mask(h, n + 1)], axis=0))
        return carry

    n_pairs = qb // 2
    lax.fori_loop(0, n_pairs, past_pair, 0)

    @pl.when(qb % 2 == 1)
    def _():
        attend(pl.multiple_of((qb - 1) * blk, blk), 2 * blk,
               lambda h: jnp.concatenate([past_mask(h, qb - 1), own_mask(h)], axis=0))

    @pl.when(qb % 2 == 0)
    def _():
        attend(pl.multiple_of(qb * blk, blk), blk, own_mask)

    for h in range(nh):
        o_ref[:, h * dh:(h + 1) * dh] = (acc_ref[h] / l_ref[h]).T.astype(o_ref.dtype)


def _moba(proj_b, bias_tiles, batch, seq):
    blk = MOBA_BLOCK
    nb = seq // blk
    n_sel = min(MOBA_TOPK, nb - 1)
    n_bias = bias_tiles.shape[1]
    return pl.pallas_call(
        functools.partial(_moba_kernel, n_blocks=nb, n_sel=n_sel, n_bias=n_bias),
        out_shape=jax.ShapeDtypeStruct((batch * seq, HEAD_W), BF16),
        grid=(batch, nb),
        in_specs=[
            pl.BlockSpec((blk, HEAD_W), lambda b, i: (b * nb + i, BCOL_MQ // HEAD_W)),
            pl.BlockSpec((seq, HEAD_W), lambda b, i: (b, BCOL_MK // HEAD_W)),
            pl.BlockSpec((seq, HEAD_W), lambda b, i: (b, BCOL_MV // HEAD_W)),
            pl.BlockSpec(bias_tiles.shape, lambda b, i: (0, 0, 0, 0)),
        ],
        out_specs=pl.BlockSpec((blk, HEAD_W), lambda b, i: (b * nb + i, 0)),
        scratch_shapes=[
            pltpu.VMEM((HEAD_W, seq), BF16),
            pltpu.VMEM((MOBA_HEADS, nb, MOBA_DH), F32),
            pltpu.VMEM((MOBA_HEADS, nb, blk), F32),
            pltpu.VMEM((MOBA_HEADS, 1, blk), F32),
            pltpu.VMEM((MOBA_HEADS, 1, blk), F32),
            pltpu.VMEM((MOBA_HEADS, MOBA_DH, blk), F32),
        ],
        compiler_params=_cparams(("parallel", "arbitrary")),
        name="moba",
    )(proj_b, proj_b, proj_b, bias_tiles)


def _sort_key(x):
    bits = pltpu.bitcast(x, jnp.int32)
    return bits ^ ((bits >> 31) & 0x7FFFFFFF)


def _dsa_kernel(dq_ref, iq_ref, iwq_ref, dkv_ref, ikw_ref, kvg_ref, lnw_ref, lnb_ref, wuk_ref, wuv_ref, bias_ref,
                o_ref, ckv_ref, ckvt_ref, ikn_ref, key_ref, mask_ref, tau_ref, cge_ref, m_ref, l_ref, acc_ref,
                *, seq, n_keep, n_bias):
    tq, kc, nh, dh = DSA_QB, DSA_KC, DSA_HEADS, DSA_DH
    sub = kc // LANES
    qb = pl.program_id(1)
    t0 = qb * tq
    n_kc = (t0 + tq - 1) // kc + 1
    n_rest = (seq - n_kc * kc).astype(F32)

    @pl.when(qb == 0)
    def _():
        for c in range(seq // kc):
            rows = slice(c * kc, (c + 1) * kc)
            x = dkv_ref[rows, :]
            cn = x * lax.rsqrt(jnp.mean(x * x, axis=-1, keepdims=True) + RMS_EPS) * kvg_ref[...]
            ckv_ref[rows, :] = cn.astype(BF16)
            ckvt_ref[:, rows] = cn.T.astype(BF16)
            ik = ikw_ref[rows, 0:IDX_DIM]
            xc = ik - jnp.mean(ik, axis=-1, keepdims=True)
            ikn = xc * lax.rsqrt(jnp.mean(xc * xc, axis=-1, keepdims=True) + RMS_EPS) * lnw_ref[...] + lnb_ref[...]
            ikn_ref[rows, :] = ikn.astype(BF16)

    t_row = t0 + lax.broadcasted_iota(jnp.int32, (1, tq), 1)
    key_off = lax.broadcasted_iota(jnp.int32, (kc, tq), 0)

    def causal_of(c0):
        return (c0 + key_off) <= t_row

    iq_t = iq_ref[...].astype(F32).T
    iq_cat = jnp.concatenate([iq_t[h * IDX_DIM:(h + 1) * IDX_DIM, :] for h in range(IDX_HEADS)], axis=1).astype(BF16)
    iw_t = iwq_ref[...].T[IDX_DIM:IDX_DIM + IDX_HEADS, :] * (IDX_HEADS ** -0.5)
    iw_t = iw_t * (IDX_DIM ** -0.5)

    hpg = 2 * LANES // tq

    def score_chunks(c, nc):
        c0s = [pl.multiple_of((c + u) * kc, kc) for u in range(nc)]
        s_grp = [[_dot(ikn_ref[pl.ds(c0, kc), :], iq_cat[:, g * hpg * tq:(g + 1) * hpg * tq])
                  for g in range(IDX_HEADS // hpg)] for c0 in c0s]
        for u, c0 in enumerate(c0s):
            sc = jnp.zeros((kc, tq), F32)
            for h in range(IDX_HEADS):
                s_h = s_grp[u][h // hpg][:, (h % hpg) * tq:(h % hpg + 1) * tq]
                sc = sc + iw_t[h:h + 1, :] * jnp.maximum(s_h, 0.0)
            sc = jnp.where(causal_of(c0), sc, NEG_BIG)
            sc = jnp.where(sc == 0.0, 0.0, sc)
            key_ref[pl.ds(c0, kc), :] = _sort_key(sc)

    def score_pair(i, carry):
        score_chunks(2 * i, 2)
        return carry

    lax.fori_loop(0, n_kc // 2, score_pair, 0)

    @pl.when(n_kc % 2 == 1)
    def _():
        score_chunks(n_kc - 1, 1)

    def count(pred_fn):
        def body(c, tot):
            c0 = pl.multiple_of(c * kc, kc)
            hit = pred_fn(key_ref[pl.ds(c0, kc), :], c0)
            return tot + _slab(jnp.where(hit, 1.0, 0.0), jnp.sum)
        part = lax.fori_loop(0, n_kc, body, jnp.zeros((ROW_SLAB, tq), F32))
        return _reduce_rows(part, jnp.sum)

    def bit_step(k, carry):
        base, c_base = carry
        cand = base + jnp.left_shift(jnp.int32(1), 31 - k)
        cnt = count(lambda keys, c0: keys >= cand) + jnp.where(cand <= KEY_NEG, n_rest, 0.0)
        take = cnt >= n_keep
        return jnp.where(take, cand, base), jnp.where(take, cnt, c_base)

    tau, c_ge = lax.fori_loop(0, DSA_SURE_BITS, bit_step, (jnp.full((1, tq), jnp.iinfo(jnp.int32).min, jnp.int32),
                                                           jnp.full((1, tq), float(seq), F32)))
    tau_ref[...] = tau
    cge_ref[...] = c_ge

    @pl.when(jnp.max(jnp.abs(c_ge - n_keep)) > 0.0)
    def _():
        tau_ref[...], cge_ref[...] = lax.fori_loop(DSA_SURE_BITS, 32, bit_step, (tau, c_ge))

    tau = tau_ref[...]
    c_ge = cge_ref[...]

    c_gt = count(lambda keys, c0: keys > tau) + jnp.where(tau < KEY_NEG, n_rest, 0.0)
    need = n_keep - c_gt
    n_eq = (c_ge - c_gt) - jnp.where(tau == KEY_NEG, (seq - 1 - t_row).astype(F32), 0.0)
    tie_break = jnp.max(n_eq - need) > 0.0

    @pl.when(jnp.logical_not(tie_break))
    def _():
        def body(c, carry):
            c0 = pl.multiple_of(c * kc, kc)
            keep = (key_ref[pl.ds(c0, kc), :] >= tau) & causal_of(c0)
            mask_ref[pl.ds(c0, kc), :] = jnp.where(keep, 0.0, NEG_BIG)
            return carry
        lax.fori_loop(0, n_kc, body, 0)

    @pl.when(tie_break)
    def _():
        lower = jnp.where(lax.broadcasted_iota(jnp.int32, (kc, kc), 1)
                          < lax.broadcasted_iota(jnp.int32, (kc, kc), 0), 1.0, 0.0).astype(BF16)

        def body(c, seen):
            c0 = pl.multiple_of(c * kc, kc)
            keys = key_ref[pl.ds(c0, kc), :]
            eq = keys == tau
            eqf = jnp.where(eq, 1.0, 0.0)
            rank = seen + _dot(lower, eqf.astype(BF16))
            keep = ((keys > tau) | (eq & (rank < need))) & causal_of(c0)
            mask_ref[pl.ds(c0, kc), :] = jnp.where(keep, 0.0, NEG_BIG)
            return seen + _reduce_rows(eqf, jnp.sum)
        lax.fori_loop(0, n_kc, body, jnp.zeros((1, tq), F32))

    scale2 = dh ** -0.5 * LOG2E
    qa_cat = jnp.concatenate(
        [_dot(wuk_ref[h], _t_bf16(dq_ref[:, h * dh:(h + 1) * dh])) for h in range(nh)], axis=1).astype(BF16)
    m_ref[...] = jnp.full(m_ref.shape, -jnp.inf, F32)
    l_ref[...] = jnp.zeros(l_ref.shape, F32)
    acc_ref[...] = jnp.zeros(acc_ref.shape, F32)

    def attend(c, nc):
        c0 = pl.multiple_of(c * kc, kc)
        nk = nc * kc
        msk = mask_ref[pl.ds(c0, nk), :]
        bias = jnp.concatenate(
            [bias_ref[jnp.clip(qb - (c * sub + g), 0, n_bias - 1)] for g in range(nc * sub)], axis=0)
        s = _dot(ckv_ref[pl.ds(c0, nk), :], qa_cat) * scale2 + bias + jnp.concatenate([msk] * nh, axis=1)
        m_old = m_ref[...]
        m_new = jnp.maximum(m_old, _reduce_rows(s, jnp.max))
        alpha = jnp.exp2(m_old - m_new)
        p = jnp.exp2(s - m_new)
        l_ref[...] = alpha * l_ref[...] + _reduce_rows(p, jnp.sum)
        acc_ref[...] = alpha * acc_ref[...] + _dot(ckvt_ref[:, pl.ds(c0, nk)], p.astype(BF16))
        m_ref[...] = m_new

    def att_pair(i, carry):
        attend(2 * i, 2)
        return carry

    lax.fori_loop(0, n_kc // 2, att_pair, 0)

    @pl.when(n_kc % 2 == 1)
    def _():
        attend(n_kc - 1, 1)

    o_lat_t = (acc_ref[...] / l_ref[...]).astype(BF16)
    for h in range(nh):
        o_ref[:, h * dh:(h + 1) * dh] = _dot(wuv_ref[h], o_lat_t[:, h * tq:(h + 1) * tq]).T.astype(o_ref.dtype)


def _dsa(proj_f, proj_b, kv_gain, ln_w, ln_b, w_uk_h, w_uv_h, bias_tiles, batch, seq):
    tq = DSA_QB
    nq = seq // tq
    n_keep = min(DSA_TOPK_MAX, seq // 4)
    n_bias = bias_tiles.shape[0]
    return pl.pallas_call(
        functools.partial(_dsa_kernel, seq=seq, n_keep=float(n_keep), n_bias=n_bias),
        out_shape=jax.ShapeDtypeStruct((batch * seq, HEAD_W), BF16),
        grid=(batch, nq),
        in_specs=[
            pl.BlockSpec((tq, HEAD_W), lambda b, i: (b * nq + i, BCOL_DQ // HEAD_W)),
            pl.BlockSpec((tq, HEAD_W), lambda b, i: (b * nq + i, BCOL_IQ // HEAD_W)),
            pl.BlockSpec((tq, LANES), lambda b, i: (b * nq + i, FCOL_IKW // LANES)),
            pl.BlockSpec((seq, DSA_KV_RANK), lambda b, i: (b, FCOL_DKV // DSA_KV_RANK)),
            pl.BlockSpec((seq, LANES), lambda b, i: (b, FCOL_IKW // LANES)),
            pl.BlockSpec((1, DSA_KV_RANK), lambda b, i: (0, 0)),
            pl.BlockSpec((1, IDX_DIM), lambda b, i: (0, 0)),
            pl.BlockSpec((1, IDX_DIM), lambda b, i: (0, 0)),
            pl.BlockSpec(w_uk_h.shape, lambda b, i: (0, 0, 0)),
            pl.BlockSpec(w_uv_h.shape, lambda b, i: (0, 0, 0)),
            pl.BlockSpec(bias_tiles.shape, lambda b, i: (0, 0, 0)),
        ],
        out_specs=pl.BlockSpec((tq, HEAD_W), lambda b, i: (b * nq + i, 0)),
        scratch_shapes=[
            pltpu.VMEM((seq, DSA_KV_RANK), BF16),
            pltpu.VMEM((DSA_KV_RANK, seq), BF16),
            pltpu.VMEM((seq, IDX_DIM), BF16),
            pltpu.VMEM((seq, tq), jnp.int32),
            pltpu.VMEM((seq, tq), F32),
            pltpu.VMEM((1, tq), jnp.int32),
            pltpu.VMEM((1, tq), F32),
            pltpu.VMEM((1, DSA_HEADS * tq), F32),
            pltpu.VMEM((1, DSA_HEADS * tq), F32),
            pltpu.VMEM((DSA_KV_RANK, DSA_HEADS * tq), F32),
        ],
        compiler_params=_cparams(("parallel", "arbitrary")),
        name="dsa",
    )(proj_b, proj_b, proj_f, proj_f, proj_f, kv_gain, ln_w, ln_b, w_uk_h, w_uv_h, bias_tiles)


def _merge_kernel(x_ref, ya_ref, yb_ref, yc_ref, g0_ref, g1_ref, g2_ref, wa_ref, wb_ref, wc_ref, wo_ref, gain_ref,
                  o_ref):
    merged = (_sigmoid(g0_ref[...]) * _dot(ya_ref[...], wa_ref[...])
              + _sigmoid(g1_ref[...]) * _dot(yb_ref[...], wb_ref[...])
              + _sigmoid(g2_ref[...]) * _dot(yc_ref[...], wc_ref[...]))
    z = _dot(merged.astype(BF16), wo_ref[...])
    zn = z * lax.rsqrt(jnp.mean(z * z, axis=-1, keepdims=True) + RMS_EPS) * gain_ref[...]
    o_ref[...] = x_ref[...] + zn


def _merge(x2, ya, yb, yc, proj_f, wa, wb, wc, wo, gain, tm):
    n = x2.shape[0]
    gcol = FCOL_GATES // D_MODEL

    def rows(w):
        return pl.BlockSpec((tm, w), lambda i: (i, 0))

    def gate(k):
        return pl.BlockSpec((tm, D_MODEL), lambda i, k=k: (i, gcol + k))

    def full(a):
        return pl.BlockSpec(a.shape, lambda i: (0, 0))

    return pl.pallas_call(
        _merge_kernel,
        out_shape=jax.ShapeDtypeStruct((n, D_MODEL), F32),
        grid=(n // tm,),
        in_specs=[rows(D_MODEL), rows(HEAD_W), rows(HEAD_W), rows(HEAD_W), gate(0), gate(1), gate(2),
                  full(wa), full(wb), full(wc), full(wo), full(gain)],
        out_specs=rows(D_MODEL),
        compiler_params=_cparams(("parallel",)),
        name="merge",
    )(x2, ya, yb, yc, proj_f, proj_f, proj_f, wa, wb, wc, wo, gain)


def _ffn_kernel(x_ref, g1_ref, wg_ref, wu_ref, wd_ref, g2_ref, o_ref, hn_ref, acc_ref):
    j = pl.program_id(1)

    @pl.when(j == 0)
    def _():
        x = x_ref[...]
        hn_ref[...] = (x * lax.rsqrt(jnp.mean(x * x, axis=-1, keepdims=True) + RMS_EPS) * g1_ref[...]).astype(BF16)
        acc_ref[...] = jnp.zeros_like(acc_ref)

    hn = hn_ref[...]
    gate = _dot(hn, wg_ref[...])
    up = _dot(hn, wu_ref[...])
    act = (gate * _sigmoid(gate)) * up
    acc_ref[...] += _dot(act.astype(BF16), wd_ref[...])

    @pl.when(j == pl.num_programs(1) - 1)
    def _():
        y = acc_ref[...]
        yn = y * lax.rsqrt(jnp.mean(y * y, axis=-1, keepdims=True) + RMS_EPS) * g2_ref[...]
        o_ref[...] = x_ref[...] + yn


def _ffn(x2, g1, w_in, w_out, g2, tm, tf):
    n = x2.shape[0]
    nf = D_FF // tf
    return pl.pallas_call(
        _ffn_kernel,
        out_shape=jax.ShapeDtypeStruct((n, D_MODEL), F32),
        grid=(n // tm, nf),
        in_specs=[
            pl.BlockSpec((tm, D_MODEL), lambda i, j: (i, 0)),
            pl.BlockSpec((1, D_MODEL), lambda i, j: (0, 0)),
            pl.BlockSpec((D_MODEL, tf), lambda i, j: (0, j)),
            pl.BlockSpec((D_MODEL, tf), lambda i, j: (0, nf + j)),
            pl.BlockSpec((tf, D_MODEL), lambda i, j: (j, 0)),
            pl.BlockSpec((1, D_MODEL), lambda i, j: (0, 0)),
        ],
        out_specs=pl.BlockSpec((tm, D_MODEL), lambda i, j: (i, 0)),
        scratch_shapes=[pltpu.VMEM((tm, D_MODEL), BF16), pltpu.VMEM((tm, D_MODEL), F32)],
        compiler_params=_cparams(("parallel", "arbitrary")),
        name="ffn",
    )(x2, g1, w_in, w_in, w_out, g2)


def _pack_w_in(w):
    o_mq, o_dq, o_dkv, o_iq, o_ik, o_gates = 2048, 3584, 4096, 4352, 4864, 4936
    w = w.astype(BF16)
    z = lambda k: jnp.zeros((w.shape[0], k), w.dtype)
    w_f = jnp.concatenate([
        w[:, :o_mq],
        w[:, o_gates:],
        w[:, o_dkv:o_iq],
        w[:, o_ik:o_gates],
        z(PROJ_F_W - FCOL_IKW - (o_gates - o_ik)),
    ], axis=1)
    w_b = jnp.concatenate([
        w[:, o_mq:o_dq],
        w[:, o_dq:o_dkv],
        w[:, o_iq:o_ik],
    ], axis=1)
    assert w_f.shape[1] == PROJ_F_W and w_b.shape[1] == PROJ_B_W
    return w_f, w_b


def _tiles(batch, seq):
    n = batch * seq
    return dict(
        proj_tm=min(1024, n), proj_f_tn=PROJ_F_W // 4, proj_b_tm=min(2048, n), proj_b_tn=PROJ_B_W // 2,
        hgrn_rows=min(512, seq),
        merge_tm=min(512, n),
        ffn_tm=min(1024, n), ffn_tf=D_FF // 2,
    )


def kernel(x, w_in, pre_mix_norm, post_mix_norm, pre_ffn_norm, post_ffn_norm, hg_lb_logits, hg_norm_w, dsa_kv_norm,
           dsa_w_uk, dsa_w_uv, idx_k_norm_w, idx_k_norm_b, rel_bias, w_branch_a, w_branch_b, w_branch_c, w_out,
           w_ffn_in, w_ffn_out):
    batch, seq, _ = x.shape
    depth = w_in.shape[0]
    t = _tiles(batch, seq)
    lbp = jax.nn.softmax(hg_lb_logits.astype(F32), axis=0)
    lower_bounds = jnp.cumsum(lbp, axis=0) - lbp[0]
    bias_b = _bias_tiles_t(rel_bias[:, :MOBA_HEADS], MOBA_BLOCK)
    bias_c = _bias_tiles_t(rel_bias[:, MOBA_HEADS:], LANES)
    bias_c = bias_c.transpose(1, 2, 0, 3).reshape(bias_c.shape[1], LANES, DSA_HEADS * LANES)
    row = lambda a: a.reshape(1, -1).astype(F32)

    x2 = x.reshape(batch * seq, D_MODEL)
    for l in range(depth):
        w_f, w_b = _pack_w_in(w_in[l])
        gain = row(pre_mix_norm[l])
        proj_f = _proj(x2, gain, w_f, F32, t["proj_tm"], t["proj_f_tn"], "proj_f")
        proj_b = _proj(x2, gain, w_b, BF16, t["proj_b_tm"], t["proj_b_tn"], "proj_b")
        y_a = _hgrn(proj_f, row(lower_bounds[l]), row(hg_norm_w[l]), batch, seq, t["hgrn_rows"])
        y_b = _moba(proj_b, bias_b, batch, seq)
        y_c = _dsa(proj_f, proj_b, row(dsa_kv_norm[l]), row(idx_k_norm_w[l]), row(idx_k_norm_b[l]),
                   jnp.transpose(dsa_w_uk[l], (1, 0, 2)).astype(BF16),
                   jnp.transpose(dsa_w_uv[l], (1, 2, 0)).astype(BF16),
                   bias_c, batch, seq)
        x2 = _merge(x2, y_a, y_b, y_c, proj_f, w_branch_a[l].astype(BF16), w_branch_b[l].astype(BF16),
                    w_branch_c[l].astype(BF16), w_out[l].astype(BF16), row(post_mix_norm[l]), t["merge_tm"])
        x2 = _ffn(x2, row(pre_ffn_norm[l]), w_ffn_in[l].astype(BF16), w_ffn_out[l].astype(BF16),
                  row(post_ffn_norm[l]), t["ffn_tm"], t["ffn_tf"])
    return x2.reshape(batch, seq, D_MODEL)
```

```python
import functools
import math

import numpy as np
import jax
import jax.numpy as jnp
from jax import lax
from jax.experimental import pallas as pl
from jax.experimental.pallas import tpu as pltpu

F32 = jnp.float32
BF16 = jnp.bfloat16

D_MODEL = 1024
RMS_EPS = 1e-6
LOG2E = 1.0 / math.log(2.0)
NEG_BIG = -1e30
TINY = 1e-20
HG_HEADS = 4
HG_D = 128
HG_CHUNK = 64
MOBA_HEADS = 4
MOBA_DH = 128
MOBA_BLOCK = 256
MOBA_TOPK = 3
MOBA_STEP_BLOCKS = 4
DSA_HEADS = 4
DSA_DH = 128
DSA_KV_RANK = 256
IDX_HEADS = 8
IDX_DIM = 64
DSA_TOPK_MAX = 256
DSA_KC = 512
DSA_SURE_BITS = 28
REL_BUCKETS = 32
REL_MAX_DIST = 2048
D_FF = 2816
HEAD_W = 512

LANES = 128
SUBLANES = 8
DSA_QB = LANES

FCOL_HQ, FCOL_HF, FCOL_HI, FCOL_HG = 0, 512, 1024, 1536
FCOL_GATES, FCOL_DKV, FCOL_IKW = 2048, 5120, 5376
PROJ_F_W = 5632
BCOL_MQ, BCOL_MK, BCOL_MV, BCOL_DQ, BCOL_IQ = 0, 512, 1024, 1536, 2048
PROJ_B_W = 2560

VMEM_LIMIT = 56 * 1024 * 1024

_REL_EXACT = REL_BUCKETS // 2
REL_FAR_DIST = int(math.ceil(_REL_EXACT * (REL_MAX_DIST / _REL_EXACT) ** ((REL_BUCKETS - _REL_EXACT - 1) / (REL_BUCKETS - _REL_EXACT)))) + 1

_NEG_BITS = int(np.array(NEG_BIG, np.float32).view(np.int32))
KEY_NEG = int(np.int32(_NEG_BITS ^ 0x7FFFFFFF))


def _cparams(sem):
    return pltpu.CompilerParams(dimension_semantics=sem, vmem_limit_bytes=VMEM_LIMIT)


def _sigmoid(x):
    return 1.0 / (1.0 + jnp.exp(-x))


def _dot(a, b):
    return jnp.dot(a, b, preferred_element_type=F32)


def _dot_nt(a, b):
    return lax.dot_general(a, b, (((1,), (1,)), ((), ())), preferred_element_type=F32)


def _t_bf16(a):
    return a.astype(F32).T.astype(BF16)


ROW_SLAB = 64


def _slab(x, fn):
    return fn(x.reshape(x.shape[0] // ROW_SLAB, ROW_SLAB, x.shape[1]), axis=0)


def _reduce_rows(x, fn):
    y = _slab(x, fn)
    z = fn(y.reshape(ROW_SLAB // SUBLANES, SUBLANES, y.shape[1]), axis=0)
    return fn(z, axis=0, keepdims=True)


def _proj_kernel(x_ref, g_ref, w_ref, o_ref, xn_ref):
    @pl.when(pl.program_id(1) == 0)
    def _():
        x = x_ref[...]
        ms = jnp.mean(x * x, axis=-1, keepdims=True)
        xn_ref[...] = (x * lax.rsqrt(ms + RMS_EPS) * g_ref[...]).astype(BF16)

    o_ref[...] = _dot(xn_ref[...], w_ref[...]).astype(o_ref.dtype)


def _proj(x2, gain, w_packed, out_dtype, tm, tn, name):
    n = x2.shape[0]
    width = w_packed.shape[1]
    return pl.pallas_call(
        _proj_kernel,
        out_shape=jax.ShapeDtypeStruct((n, width), out_dtype),
        grid=(n // tm, width // tn),
        in_specs=[
            pl.BlockSpec((tm, D_MODEL), lambda i, j: (i, 0)),
            pl.BlockSpec((1, D_MODEL), lambda i, j: (0, 0)),
            pl.BlockSpec((D_MODEL, tn), lambda i, j: (0, j)),
        ],
        out_specs=pl.BlockSpec((tm, tn), lambda i, j: (i, j)),
        scratch_shapes=[pltpu.VMEM((tm, D_MODEL), BF16)],
        compiler_params=_cparams(("parallel", "arbitrary")),
        name=name,
    )(x2, gain, w_packed)


HG_LEVELS = (32, 16, 8, 4, 2, 1)
HG_UNROLL = 8


def _hgrn_constants():
    c = HG_CHUNK
    t = np.arange(c)
    i = t[:, None]
    tt = t[None, :]
    rows = [tt <= i]
    masks = []
    for m in HG_LEVELS:
        mid = ((t // (2 * m)) * (2 * m) + m)[:, None]
        upper = (i >= mid) & (tt >= mid) & (tt <= i)
        lower = (i < mid) & (tt > i) & (tt <= mid - 1)
        rows.append(upper | lower)
        same = (t[:, None] // (2 * m)) == (t[None, :] // (2 * m))
        masks.append(same & (i >= mid) & (tt < mid))
    masks.append(np.eye(c, dtype=bool))
    m_all = np.concatenate(rows, axis=0).astype(np.float32)
    return m_all, np.stack(masks).astype(np.float32)


def _hgrn_kernel(q_ref, f_ref, i_ref, g_ref, lb_ref, nw_ref, mall_ref, mask_ref, o_ref, st_ref, e_ref, *, n_chunks):
    c = HG_CHUNK
    nl = len(HG_LEVELS)

    @pl.when(pl.program_id(1) == 0)
    def _():
        st_ref[...] = jnp.zeros_like(st_ref)

    lb = lb_ref[...]
    nw = nw_ref[...]

    heads = [slice(h * HG_D, (h + 1) * HG_D) for h in range(HG_HEADS)]

    def intra(r0, u):
        fp = f_ref[pl.ds(r0, c), :]
        f = lb + (1.0 - lb) * _sigmoid(fp)
        logf = jnp.log(jnp.maximum(f, TINY))
        kk = (1.0 - lb) * _sigmoid(-fp)
        g_hi = logf.astype(BF16)
        g_lo = (logf - g_hi.astype(F32)).astype(BF16)
        e_ref[u] = _dot(mall_ref[...], jnp.concatenate([g_hi, g_lo], axis=1))
        q = q_ref[pl.ds(r0, c), :]

        def expo(row):
            return e_ref[u, row * c:(row + 1) * c, 0:HEAD_W] + e_ref[u, row * c:(row + 1) * c, HEAD_W:2 * HEAD_W]

        b = expo(0)
        b_last = b[c - 1:c, :]
        q_in = (q * jnp.exp(b)).astype(BF16)
        k_out = (kk * jnp.exp(b_last - b)).astype(BF16)
        qls, kls = [], []
        for l in range(nl):
            decay = jnp.exp(expo(1 + l))
            qls.append((q * decay).astype(BF16))
            kls.append((kk * decay).astype(BF16))
        qls.append(q.astype(BF16))
        kls.append(kk.astype(BF16))
        sls = [[_dot_nt(qls[l][:, hs], kls[l][:, hs]) for l in range(nl + 1)] for hs in heads]
        ss = []
        for h in range(HG_HEADS):
            s = jnp.zeros((c, c), F32)
            for l in range(nl + 1):
                s = s + jnp.where(mask_ref[l] > 0.0, sls[h][l], 0.0)
            ss.append(s.astype(BF16))
        v = i_ref[pl.ds(r0, c), :].astype(BF16)
        o_intra = [_dot(ss[h], v[:, hs]) for h, hs in enumerate(heads)]
        kv = [_dot(v[:, hs].astype(F32).T.astype(BF16), k_out[:, hs]) for hs in heads]
        return q_in, jnp.exp(b_last), o_intra, kv

    def step(ci, carry):
        parts = [intra(pl.multiple_of((ci * HG_UNROLL + u) * c, c), u) for u in range(HG_UNROLL)]
        for u, (q_in, st_decay, o_intra, kv) in enumerate(parts):
            r0 = pl.multiple_of((ci * HG_UNROLL + u) * c, c)
            og = g_ref[pl.ds(r0, c), :]
            outs = [_dot_nt(q_in[:, hs], st_ref[h].astype(BF16)) + o_intra[h] for h, hs in enumerate(heads)]
            for h, hs in enumerate(heads):
                st_ref[h] = st_ref[h] * st_decay[:, hs] + kv[h]
            for h, hs in enumerate(heads):
                o = outs[h]
                y = o * lax.rsqrt(jnp.mean(o * o, axis=-1, keepdims=True) + RMS_EPS) * nw
                gh = og[:, hs]
                o_ref[pl.ds(r0, c), hs] = (y * (gh * _sigmoid(gh))).astype(o_ref.dtype)
        return carry

    lax.fori_loop(0, n_chunks // HG_UNROLL, step, 0)


def _hgrn(proj_f, lb, norm_w, batch, seq, rows):
    m_all, masks = _hgrn_constants()
    n_rb = seq // rows
    n_chunks = rows // HG_CHUNK
    assert n_chunks % HG_UNROLL == 0

    def col(c0):
        return pl.BlockSpec((rows, HEAD_W), lambda b, r, cidx=c0 // HEAD_W: (b * n_rb + r, cidx))

    return pl.pallas_call(
        functools.partial(_hgrn_kernel, n_chunks=n_chunks),
        out_shape=jax.ShapeDtypeStruct((batch * seq, HEAD_W), BF16),
        grid=(batch, n_rb),
        in_specs=[
            col(FCOL_HQ), col(FCOL_HF), col(FCOL_HI), col(FCOL_HG),
            pl.BlockSpec((1, HEAD_W), lambda b, r: (0, 0)),
            pl.BlockSpec((1, HG_D), lambda b, r: (0, 0)),
            pl.BlockSpec(m_all.shape, lambda b, r: (0, 0)),
            pl.BlockSpec(masks.shape, lambda b, r: (0, 0, 0)),
        ],
        out_specs=pl.BlockSpec((rows, HEAD_W), lambda b, r: (b * n_rb + r, 0)),
        scratch_shapes=[
            pltpu.VMEM((HG_HEADS, HG_D, HG_D), F32),
            pltpu.VMEM((HG_UNROLL, m_all.shape[0], 2 * HEAD_W), F32),
        ],
        compiler_params=_cparams(("parallel", "arbitrary")),
        name="hgrn",
    )(proj_f, proj_f, proj_f, proj_f, lb, norm_w, jnp.asarray(m_all, BF16), jnp.asarray(masks, F32))


def _rel_bucket(dist):
    n = jnp.maximum(dist, 0)
    exact = REL_BUCKETS // 2
    nf = jnp.maximum(n, exact).astype(F32)
    large = exact + (jnp.log(nf / exact) / math.log(REL_MAX_DIST / exact) * (REL_BUCKETS - exact)).astype(jnp.int32)
    large = jnp.minimum(large, REL_BUCKETS - 1)
    return jnp.where(n < exact, n, large)


def _n_bias_tiles(tile):
    d = 0
    while d * tile - (tile - 1) < REL_FAR_DIST:
        d += 1
    return d + 1


def _bias_tiles_t(table, tile):
    n = _n_bias_tiles(tile)
    length = n * tile
    by_dist = (table[_rel_bucket(jnp.arange(length, dtype=jnp.int32))].astype(F32) * LOG2E).T
    heads = by_dist.shape[0]
    w = jnp.concatenate([by_dist, jnp.zeros((heads, 1), F32)], axis=1)
    skew = jnp.broadcast_to(w[:, None, :], (heads, tile, length + 1)).reshape(heads, tile * (length + 1))
    skew = skew[:, :tile * length].reshape(heads, tile, length)
    return skew.reshape(heads, tile, n, tile).transpose(0, 2, 1, 3)


def _moba_kernel(q_ref, k_ref, v_ref, bias_ref, o_ref, vt_ref, km_ref, sel_ref, m_ref, l_ref, acc_ref,
                 *, n_blocks, n_sel, n_bias):
    blk, dh, nh = MOBA_BLOCK, MOBA_DH, MOBA_HEADS
    qb = pl.program_id(1)
    scale2 = dh ** -0.5 * LOG2E

    @pl.when(qb == 0)
    def _():
        for h in range(nh):
            for n in range(n_blocks):
                kblk = k_ref[n * blk:(n + 1) * blk, h * dh:(h + 1) * dh].astype(F32)
                km_ref[h, n:n + 1, :] = jnp.mean(kblk, axis=0, keepdims=True)
                vt_ref[h * dh:(h + 1) * dh, n * blk:(n + 1) * blk] = _t_bf16(v_ref[n * blk:(n + 1) * blk, h * dh:(h + 1) * dh])

    q_t = []
    n_iota = lax.broadcasted_iota(jnp.int32, (n_blocks, blk), 0)
    for h in range(nh):
        qt = _t_bf16(q_ref[:, h * dh:(h + 1) * dh])
        q_t.append(qt)
        gate = _dot(km_ref[h].astype(BF16), qt)
        gate = jnp.where(n_iota < qb, gate, NEG_BIG)
        rank = jnp.zeros(gate.shape, F32)
        for n in range(n_blocks):
            row = gate[n:n + 1, :]
            beats = (row > gate) | ((row == gate) & (n < n_iota))
            rank = rank + jnp.where(beats, 1.0, 0.0)
        sel_ref[h] = jnp.where((rank < n_sel) & (n_iota < qb), 0.0, NEG_BIG)
        m_ref[h] = jnp.full((1, blk), -jnp.inf, F32)
        l_ref[h] = jnp.zeros((1, blk), F32)
        acc_ref[h] = jnp.zeros((dh, blk), F32)

    def attend(r0, nk, mask_fn):
        ss = [_dot(k_ref[pl.ds(r0, nk), h * dh:(h + 1) * dh], q_t[h]) for h in range(nh)]
        ps, alphas = [], []
        for h in range(nh):
            s = ss[h] * scale2 + mask_fn(h)
            m_old = m_ref[h]
            m_new = jnp.maximum(m_old, _reduce_rows(s, jnp.max))
            alphas.append(jnp.exp2(m_old - m_new))
            p = jnp.exp2(s - m_new)
            l_ref[h] = alphas[h] * l_ref[h] + _reduce_rows(p, jnp.sum)
            m_ref[h] = m_new
            ps.append(p.astype(BF16))
        for h in range(nh):
            acc_ref[h] = alphas[h] * acc_ref[h] + _dot(vt_ref[h * dh:(h + 1) * dh, pl.ds(r0, nk)], ps[h])

    def past_mask(h, n):
        return bias_ref[h, jnp.minimum(qb - n, n_bias - 1)] + sel_ref[h, pl.ds(n, 1), :]

    causal = jnp.where(lax.broadcasted_iota(jnp.int32, (blk, blk), 0) <= lax.broadcasted_iota(jnp.int32, (blk, blk), 1),
                       0.0, NEG_BIG)

    def own_mask(h):
        return bias_ref[h, 0] + causal

    group = MOBA_STEP_BLOCKS

    def past_group(i, carry):
        n = group * i
        attend(pl.multiple_of(n * blk, group * blk), group * blk,
               lambda h: jnp.concatenate([past_mask(h, n + u) for u in range(group)], axis=0))
        return carry

    lax.fori_loop(0, qb // group, past_group, 0)

    for k in range(group):
        @pl.when(qb % group == k)
        def _(k=k):
            attend(pl.multiple_of((qb - k) * blk, blk), (k + 1) * blk,
                   lambda h: jnp.concatenate([past_mask(h, qb - k + u) for u in range(k)] + [own_mask(h)], axis=0))

    for h in range(nh):
        o_ref[:, h * dh:(h + 1) * dh] = (acc_ref[h] / l_ref[h]).T.astype(o_ref.dtype)


def _moba(proj_b, bias_tiles, batch, seq):
    blk = MOBA_BLOCK
    nb = seq // blk
    n_sel = min(MOBA_TOPK, nb - 1)
    n_bias = bias_tiles.shape[1]
    return pl.pallas_call(
        functools.partial(_moba_kernel, n_blocks=nb, n_sel=n_sel, n_bias=n_bias),
        out_shape=jax.ShapeDtypeStruct((batch * seq, HEAD_W), BF16),
        grid=(batch, nb),
        in_specs=[
            pl.BlockSpec((blk, HEAD_W), lambda b, i: (b * nb + i, BCOL_MQ // HEAD_W)),
            pl.BlockSpec((seq, HEAD_W), lambda b, i: (b, BCOL_MK // HEAD_W)),
            pl.BlockSpec((seq, HEAD_W), lambda b, i: (b, BCOL_MV // HEAD_W)),
            pl.BlockSpec(bias_tiles.shape, lambda b, i: (0, 0, 0, 0)),
        ],
        out_specs=pl.BlockSpec((blk, HEAD_W), lambda b, i: (b * nb + i, 0)),
        scratch_shapes=[
            pltpu.VMEM((HEAD_W, seq), BF16),
            pltpu.VMEM((MOBA_HEADS, nb, MOBA_DH), F32),
            pltpu.VMEM((MOBA_HEADS, nb, blk), F32),
            pltpu.VMEM((MOBA_HEADS, 1, blk), F32),
            pltpu.VMEM((MOBA_HEADS, 1, blk), F32),
            pltpu.VMEM((MOBA_HEADS, MOBA_DH, blk), F32),
        ],
        compiler_params=_cparams(("parallel", "arbitrary")),
        name="moba",
    )(proj_b, proj_b, proj_b, bias_tiles)


def _sort_key(x):
    bits = pltpu.bitcast(x, jnp.int32)
    return bits ^ ((bits >> 31) & 0x7FFFFFFF)


def _dsa_kernel(dq_ref, iq_ref, iwq_ref, dkv_ref, ikw_ref, kvg_ref, lnw_ref, lnb_ref, wuk_ref, wuv_ref, bias_ref,
                o_ref, ckv_ref, ckvt_ref, ikn_ref, key_ref, mask_ref, tau_ref, cge_ref, m_ref, l_ref, acc_ref,
                *, seq, n_keep, n_bias):
    tq, kc, nh, dh = DSA_QB, DSA_KC, DSA_HEADS, DSA_DH
    sub = kc // LANES
    qb = pl.program_id(1)
    t0 = qb * tq
    n_kc = (t0 + tq - 1) // kc + 1
    n_rest = (seq - n_kc * kc).astype(F32)

    @pl.when(qb == 0)
    def _():
        for c in range(seq // kc):
            rows = slice(c * kc, (c + 1) * kc)
            x = dkv_ref[rows, :]
            cn = x * lax.rsqrt(jnp.mean(x * x, axis=-1, keepdims=True) + RMS_EPS) * kvg_ref[...]
            ckv_ref[rows, :] = cn.astype(BF16)
            ckvt_ref[:, rows] = cn.T.astype(BF16)
            ik = ikw_ref[rows, 0:IDX_DIM]
            xc = ik - jnp.mean(ik, axis=-1, keepdims=True)
            ikn = xc * lax.rsqrt(jnp.mean(xc * xc, axis=-1, keepdims=True) + RMS_EPS) * lnw_ref[...] + lnb_ref[...]
            ikn_ref[rows, :] = ikn.astype(BF16)

    t_row = t0 + lax.broadcasted_iota(jnp.int32, (1, tq), 1)
    key_off = lax.broadcasted_iota(jnp.int32, (kc, tq), 0)

    def causal_of(c0):
        return (c0 + key_off) <= t_row

    iq_t = iq_ref[...].astype(F32).T
    iq_cat = jnp.concatenate([iq_t[h * IDX_DIM:(h + 1) * IDX_DIM, :] for h in range(IDX_HEADS)], axis=1).astype(BF16)
    iw_t = iwq_ref[...].T[IDX_DIM:IDX_DIM + IDX_HEADS, :] * (IDX_HEADS ** -0.5)
    iw_t = iw_t * (IDX_DIM ** -0.5)

    hpg = 2 * LANES // tq

    def score_chunks(c, nc):
        c0s = [pl.multiple_of((c + u) * kc, kc) for u in range(nc)]
        s_grp = [[_dot(ikn_ref[pl.ds(c0, kc), :], iq_cat[:, g * hpg * tq:(g + 1) * hpg * tq])
                  for g in range(IDX_HEADS // hpg)] for c0 in c0s]
        for u, c0 in enumerate(c0s):
            sc = jnp.zeros((kc, tq), F32)
            for h in range(IDX_HEADS):
                s_h = s_grp[u][h // hpg][:, (h % hpg) * tq:(h % hpg + 1) * tq]
                sc = sc + iw_t[h:h + 1, :] * jnp.maximum(s_h, 0.0)
            sc = jnp.where(causal_of(c0), sc, NEG_BIG)
            sc = jnp.where(sc == 0.0, 0.0, sc)
            key_ref[pl.ds(c0, kc), :] = _sort_key(sc)

    def score_pair(i, carry):
        score_chunks(2 * i, 2)
        return carry

    lax.fori_loop(0, n_kc // 2, score_pair, 0)

    @pl.when(n_kc % 2 == 1)
    def _():
        score_chunks(n_kc - 1, 1)

    def count(pred_fn):
        def body(c, tot):
            c0 = pl.multiple_of(c * kc, kc)
            hit = pred_fn(key_ref[pl.ds(c0, kc), :], c0)
            return tot + _slab(jnp.where(hit, 1.0, 0.0), jnp.sum)
        part = lax.fori_loop(0, n_kc, body, jnp.zeros((ROW_SLAB, tq), F32))
        return _reduce_rows(part, jnp.sum)

    def bit_step(k, carry):
        base, c_base = carry
        cand = base + jnp.left_shift(jnp.int32(1), 31 - k)
        cnt = count(lambda keys, c0: keys >= cand) + jnp.where(cand <= KEY_NEG, n_rest, 0.0)
        take = cnt >= n_keep
        return jnp.where(take, cand, base), jnp.where(take, cnt, c_base)

    tau, c_ge = lax.fori_loop(0, DSA_SURE_BITS, bit_step, (jnp.full((1, tq), jnp.iinfo(jnp.int32).min, jnp.int32),
                                                           jnp.full((1, tq), float(seq), F32)))
    tau_ref[...] = tau
    cge_ref[...] = c_ge

    @pl.when(jnp.max(jnp.abs(c_ge - n_keep)) > 0.0)
    def _():
        tau_ref[...], cge_ref[...] = lax.fori_loop(DSA_SURE_BITS, 32, bit_step, (tau, c_ge))

    tau = tau_ref[...]
    c_ge = cge_ref[...]

    c_gt = count(lambda keys, c0: keys > tau) + jnp.where(tau < KEY_NEG, n_rest, 0.0)
    need = n_keep - c_gt
    n_eq = (c_ge - c_gt) - jnp.where(tau == KEY_NEG, (seq - 1 - t_row).astype(F32), 0.0)
    tie_break = jnp.max(n_eq - need) > 0.0

    @pl.when(jnp.logical_not(tie_break))
    def _():
        def body(c, carry):
            c0 = pl.multiple_of(c * kc, kc)
            keep = (key_ref[pl.ds(c0, kc), :] >= tau) & causal_of(c0)
            mask_ref[pl.ds(c0, kc), :] = jnp.where(keep, 0.0, NEG_BIG)
            return carry
        lax.fori_loop(0, n_kc, body, 0)

    @pl.when(tie_break)
    def _():
        lower = jnp.where(lax.broadcasted_iota(jnp.int32, (kc, kc), 1)
                          < lax.broadcasted_iota(jnp.int32, (kc, kc), 0), 1.0, 0.0).astype(BF16)

        def body(c, seen):
            c0 = pl.multiple_of(c * kc, kc)
            keys = key_ref[pl.ds(c0, kc), :]
            eq = keys == tau
            eqf = jnp.where(eq, 1.0, 0.0)
            rank = seen + _dot(lower, eqf.astype(BF16))
            keep = ((keys > tau) | (eq & (rank < need))) & causal_of(c0)
            mask_ref[pl.ds(c0, kc), :] = jnp.where(keep, 0.0, NEG_BIG)
            return seen + _reduce_rows(eqf, jnp.sum)
        lax.fori_loop(0, n_kc, body, jnp.zeros((1, tq), F32))

    scale2 = dh ** -0.5 * LOG2E
    qa_cat = jnp.concatenate(
        [_dot(wuk_ref[h], _t_bf16(dq_ref[:, h * dh:(h + 1) * dh])) for h in range(nh)], axis=1).astype(BF16)
    m_ref[...] = jnp.full(m_ref.shape, -jnp.inf, F32)
    l_ref[...] = jnp.zeros(l_ref.shape, F32)
    acc_ref[...] = jnp.zeros(acc_ref.shape, F32)

    def attend(c, nc):
        c0 = pl.multiple_of(c * kc, kc)
        nk = nc * kc
        msk = mask_ref[pl.ds(c0, nk), :]
        bias = jnp.concatenate(
            [bias_ref[jnp.clip(qb - (c * sub + g), 0, n_bias - 1)] for g in range(nc * sub)], axis=0)
        s = _dot(ckv_ref[pl.ds(c0, nk), :], qa_cat) * scale2 + bias + jnp.concatenate([msk] * nh, axis=1)
        m_old = m_ref[...]
        m_new = jnp.maximum(m_old, _reduce_rows(s, jnp.max))
        alpha = jnp.exp2(m_old - m_new)
        p = jnp.exp2(s - m_new)
        l_ref[...] = alpha * l_ref[...] + _reduce_rows(p, jnp.sum)
        acc_ref[...] = alpha * acc_ref[...] + _dot(ckvt_ref[:, pl.ds(c0, nk)], p.astype(BF16))
        m_ref[...] = m_new

    def att_pair(i, carry):
        attend(2 * i, 2)
        return carry

    lax.fori_loop(0, n_kc // 2, att_pair, 0)

    @pl.when(n_kc % 2 == 1)
    def _():
        attend(n_kc - 1, 1)

    o_lat_t = (acc_ref[...] / l_ref[...]).astype(BF16)
    for h in range(nh):
        o_ref[:, h * dh:(h + 1) * dh] = _dot(wuv_ref[h], o_lat_t[:, h * tq:(h + 1) * tq]).T.astype(o_ref.dtype)


def _dsa(proj_f, proj_b, kv_gain, ln_w, ln_b, w_uk_h, w_uv_h, bias_tiles, batch, seq):
    tq = DSA_QB
    nq = seq // tq
    n_keep = min(DSA_TOPK_MAX, seq // 4)
    n_bias = bias_tiles.shape[0]
    return pl.pallas_call(
        functools.partial(_dsa_kernel, seq=seq, n_keep=float(n_keep), n_bias=n_bias),
        out_shape=jax.ShapeDtypeStruct((batch * seq, HEAD_W), BF16),
        grid=(batch, nq),
        in_specs=[
            pl.BlockSpec((tq, HEAD_W), lambda b, i: (b * nq + i, BCOL_DQ // HEAD_W)),
            pl.BlockSpec((tq, HEAD_W), lambda b, i: (b * nq + i, BCOL_IQ // HEAD_W)),
            pl.BlockSpec((tq, LANES), lambda b, i: (b * nq + i, FCOL_IKW // LANES)),
            pl.BlockSpec((seq, DSA_KV_RANK), lambda b, i: (b, FCOL_DKV // DSA_KV_RANK)),
            pl.BlockSpec((seq, LANES), lambda b, i: (b, FCOL_IKW // LANES)),
            pl.BlockSpec((1, DSA_KV_RANK), lambda b, i: (0, 0)),
            pl.BlockSpec((1, IDX_DIM), lambda b, i: (0, 0)),
            pl.BlockSpec((1, IDX_DIM), lambda b, i: (0, 0)),
            pl.BlockSpec(w_uk_h.shape, lambda b, i: (0, 0, 0)),
            pl.BlockSpec(w_uv_h.shape, lambda b, i: (0, 0, 0)),
            pl.BlockSpec(bias_tiles.shape, lambda b, i: (0, 0, 0)),
        ],
        out_specs=pl.BlockSpec((tq, HEAD_W), lambda b, i: (b * nq + i, 0)),
        scratch_shapes=[
            pltpu.VMEM((seq, DSA_KV_RANK), BF16),
            pltpu.VMEM((DSA_KV_RANK, seq), BF16),
            pltpu.VMEM((seq, IDX_DIM), BF16),
            pltpu.VMEM((seq, tq), jnp.int32),
            pltpu.VMEM((seq, tq), F32),
            pltpu.VMEM((1, tq), jnp.int32),
            pltpu.VMEM((1, tq), F32),
            pltpu.VMEM((1, DSA_HEADS * tq), F32),
            pltpu.VMEM((1, DSA_HEADS * tq), F32),
            pltpu.VMEM((DSA_KV_RANK, DSA_HEADS * tq), F32),
        ],
        compiler_params=_cparams(("parallel", "arbitrary")),
        name="dsa",
    )(proj_b, proj_b, proj_f, proj_f, proj_f, kv_gain, ln_w, ln_b, w_uk_h, w_uv_h, bias_tiles)


def _merge_kernel(x_ref, ya_ref, yb_ref, yc_ref, g0_ref, g1_ref, g2_ref, wa_ref, wb_ref, wc_ref, wo_ref, gain_ref,
                  o_ref):
    merged = (_sigmoid(g0_ref[...]) * _dot(ya_ref[...], wa_ref[...])
              + _sigmoid(g1_ref[...]) * _dot(yb_ref[...], wb_ref[...])
              + _sigmoid(g2_ref[...]) * _dot(yc_ref[...], wc_ref[...]))
    z = _dot(merged.astype(BF16), wo_ref[...])
    zn = z * lax.rsqrt(jnp.mean(z * z, axis=-1, keepdims=True) + RMS_EPS) * gain_ref[...]
    o_ref[...] = x_ref[...] + zn


def _merge(x2, ya, yb, yc, proj_f, wa, wb, wc, wo, gain, tm):
    n = x2.shape[0]
    gcol = FCOL_GATES // D_MODEL

    def rows(w):
        return pl.BlockSpec((tm, w), lambda i: (i, 0))

    def gate(k):
        return pl.BlockSpec((tm, D_MODEL), lambda i, k=k: (i, gcol + k))

    def full(a):
        return pl.BlockSpec(a.shape, lambda i: (0, 0))

    return pl.pallas_call(
        _merge_kernel,
        out_shape=jax.ShapeDtypeStruct((n, D_MODEL), F32),
        grid=(n // tm,),
        in_specs=[rows(D_MODEL), rows(HEAD_W), rows(HEAD_W), rows(HEAD_W), gate(0), gate(1), gate(2),
                  full(wa), full(wb), full(wc), full(wo), full(gain)],
        out_specs=rows(D_MODEL),
        compiler_params=_cparams(("parallel",)),
        name="merge",
    )(x2, ya, yb, yc, proj_f, proj_f, proj_f, wa, wb, wc, wo, gain)


def _ffn_kernel(x_ref, g1_ref, wg_ref, wu_ref, wd_ref, g2_ref, o_ref, hn_ref, acc_ref):
    j = pl.program_id(1)

    @pl.when(j == 0)
    def _():
        x = x_ref[...]
        hn_ref[...] = (x * lax.rsqrt(jnp.mean(x * x, axis=-1, keepdims=True) + RMS_EPS) * g1_ref[...]).astype(BF16)
        acc_ref[...] = jnp.zeros_like(acc_ref)

    hn = hn_ref[...]
    gate = _dot(hn, wg_ref[...])
    up = _dot(hn, wu_ref[...])
    act = (gate * _sigmoid(gate)) * up
    acc_ref[...] += _dot(act.astype(BF16), wd_ref[...])

    @pl.when(j == pl.num_programs(1) - 1)
    def _():
        y = acc_ref[...]
        yn = y * lax.rsqrt(jnp.mean(y * y, axis=-1, keepdims=True) + RMS_EPS) * g2_ref[...]
        o_ref[...] = x_ref[...] + yn


def _ffn(x2, g1, w_in, w_out, g2, tm, tf):
    n = x2.shape[0]
    nf = D_FF // tf
    return pl.pallas_call(
        _ffn_kernel,
        out_shape=jax.ShapeDtypeStruct((n, D_MODEL), F32),
        grid=(n // tm, nf),
        in_specs=[
            pl.BlockSpec((tm, D_MODEL), lambda i, j: (i, 0)),
            pl.BlockSpec((1, D_MODEL), lambda i, j: (0, 0)),
            pl.BlockSpec((D_MODEL, tf), lambda i, j: (0, j)),
            pl.BlockSpec((D_MODEL, tf), lambda i, j: (0, nf + j)),
            pl.BlockSpec((tf, D_MODEL), lambda i, j: (j, 0)),
            pl.BlockSpec((1, D_MODEL), lambda i, j: (0, 0)),
        ],
        out_specs=pl.BlockSpec((tm, D_MODEL), lambda i, j: (i, 0)),
        scratch_shapes=[pltpu.VMEM((tm, D_MODEL), BF16), pltpu.VMEM((tm, D_MODEL), F32)],
        compiler_params=_cparams(("parallel", "arbitrary")),
        name="ffn",
    )(x2, g1, w_in, w_in, w_out, g2)


def _pack_w_in(w):
    o_mq, o_dq, o_dkv, o_iq, o_ik, o_gates = 2048, 3584, 4096, 4352, 4864, 4936
    w = w.astype(BF16)
    z = lambda k: jnp.zeros(w.shape[:-1] + (k,), w.dtype)
    w_f = jnp.concatenate([
        w[..., :o_mq],
        w[..., o_gates:],
        w[..., o_dkv:o_iq],
        w[..., o_ik:o_gates],
        z(PROJ_F_W - FCOL_IKW - (o_gates - o_ik)),
    ], axis=-1)
    w_b = jnp.concatenate([
        w[..., o_mq:o_dq],
        w[..., o_dq:o_dkv],
        w[..., o_iq:o_ik],
    ], axis=-1)
    assert w_f.shape[-1] == PROJ_F_W and w_b.shape[-1] == PROJ_B_W
    return w_f, w_b


def _tiles(batch, seq):
    n = batch * seq
    return dict(
        proj_tm=min(1024, n), proj_f_tn=PROJ_F_W // 4, proj_b_tm=min(2048, n), proj_b_tn=PROJ_B_W // 2,
        hgrn_rows=min(512, seq),
        merge_tm=min(512, n),
        ffn_tm=min(1024, n), ffn_tf=D_FF // 2,
    )


def kernel(x, w_in, pre_mix_norm, post_mix_norm, pre_ffn_norm, post_ffn_norm, hg_lb_logits, hg_norm_w, dsa_kv_norm,
           dsa_w_uk, dsa_w_uv, idx_k_norm_w, idx_k_norm_b, rel_bias, w_branch_a, w_branch_b, w_branch_c, w_out,
           w_ffn_in, w_ffn_out):
    batch, seq, _ = x.shape
    depth = w_in.shape[0]
    t = _tiles(batch, seq)
    lbp = jax.nn.softmax(hg_lb_logits.astype(F32), axis=0)
    lower_bounds = jnp.cumsum(lbp, axis=0) - lbp[0]
    bias_b = _bias_tiles_t(rel_bias[:, :MOBA_HEADS], MOBA_BLOCK)
    bias_c = _bias_tiles_t(rel_bias[:, MOBA_HEADS:], LANES)
    bias_c = bias_c.transpose(1, 2, 0, 3).reshape(bias_c.shape[1], LANES, DSA_HEADS * LANES)
    row = lambda a: a.reshape(1, -1).astype(F32)

    w_f, w_b = _pack_w_in(w_in)
    w_uk = jnp.transpose(dsa_w_uk, (0, 2, 1, 3)).astype(BF16)
    w_uv = jnp.transpose(dsa_w_uv, (0, 2, 3, 1)).astype(BF16)
    w_a, w_bb, w_c, w_o = (a.astype(BF16) for a in (w_branch_a, w_branch_b, w_branch_c, w_out))
    w_fi, w_fo = w_ffn_in.astype(BF16), w_ffn_out.astype(BF16)

    x2 = x.reshape(batch * seq, D_MODEL)
    for l in range(depth):
        gain = row(pre_mix_norm[l])
        proj_f = _proj(x2, gain, w_f[l], F32, t["proj_tm"], t["proj_f_tn"], "proj_f")
        proj_b = _proj(x2, gain, w_b[l], BF16, t["proj_b_tm"], t["proj_b_tn"], "proj_b")
        y_a = _hgrn(proj_f, row(lower_bounds[l]), row(hg_norm_w[l]), batch, seq, t["hgrn_rows"])
        y_b = _moba(proj_b, bias_b, batch, seq)
        y_c = _dsa(proj_f, proj_b, row(dsa_kv_norm[l]), row(idx_k_norm_w[l]), row(idx_k_norm_b[l]),
                   w_uk[l], w_uv[l], bias_c, batch, seq)
        x2 = _merge(x2, y_a, y_b, y_c, proj_f, w_a[l], w_bb[l], w_c[l], w_o[l], row(post_mix_norm[l]), t["merge_tm"])
        x2 = _ffn(x2, row(pre_ffn_norm[l]), w_fi[l], w_fo[l], row(post_ffn_norm[l]), t["ffn_tm"], t["ffn_tf"])
    return x2.reshape(batch, seq, D_MODEL)
```

```python
import functools
import math

import numpy as np
import jax
import jax.numpy as jnp
from jax import lax
from jax.experimental import pallas as pl
from jax.experimental.pallas import tpu as pltpu

F32 = jnp.float32
BF16 = jnp.bfloat16

D_MODEL = 1024
RMS_EPS = 1e-6
LOG2E = 1.0 / math.log(2.0)
NEG_BIG = -1e30
TINY = 1e-20
HG_HEADS = 4
HG_D = 128
HG_CHUNK = 64
MOBA_HEADS = 4
MOBA_DH = 128
MOBA_BLOCK = 256
MOBA_TOPK = 3
MOBA_STEP_BLOCKS = 4
DSA_HEADS = 4
DSA_DH = 128
DSA_KV_RANK = 256
IDX_HEADS = 8
IDX_DIM = 64
DSA_TOPK_MAX = 256
DSA_KC = 512
DSA_SURE_BITS = 28
REL_BUCKETS = 32
REL_MAX_DIST = 2048
D_FF = 2816
HEAD_W = 512

LANES = 128
SUBLANES = 8
DSA_QB = LANES

FCOL_HQ, FCOL_HF, FCOL_HI, FCOL_HG = 0, 512, 1024, 1536
FCOL_GATES, FCOL_DKV, FCOL_IKW = 2048, 5120, 5376
PROJ_F_W = 5632
BCOL_MQ, BCOL_MK, BCOL_MV, BCOL_DQ, BCOL_IQ = 0, 512, 1024, 1536, 2048
PROJ_B_W = 2560

VMEM_LIMIT = 56 * 1024 * 1024

_REL_EXACT = REL_BUCKETS // 2
REL_FAR_DIST = int(math.ceil(_REL_EXACT * (REL_MAX_DIST / _REL_EXACT) ** ((REL_BUCKETS - _REL_EXACT - 1) / (REL_BUCKETS - _REL_EXACT)))) + 1

_NEG_BITS = int(np.array(NEG_BIG, np.float32).view(np.int32))
KEY_NEG = int(np.int32(_NEG_BITS ^ 0x7FFFFFFF))


def _cparams(sem):
    return pltpu.CompilerParams(dimension_semantics=sem, vmem_limit_bytes=VMEM_LIMIT)


def _sigmoid(x):
    return 1.0 / (1.0 + jnp.exp(-x))


def _dot(a, b):
    return jnp.dot(a, b, preferred_element_type=F32)


def _dot_nt(a, b):
    return lax.dot_general(a, b, (((1,), (1,)), ((), ())), preferred_element_type=F32)


def _t_bf16(a):
    return a.astype(F32).T.astype(BF16)


ROW_SLAB = 64


def _slab(x, fn):
    return fn(x.reshape(x.shape[0] // ROW_SLAB, ROW_SLAB, x.shape[1]), axis=0)


def _reduce_rows(x, fn):
    y = _slab(x, fn)
    z = fn(y.reshape(ROW_SLAB // SUBLANES, SUBLANES, y.shape[1]), axis=0)
    return fn(z, axis=0, keepdims=True)


def _proj_kernel(x_ref, g_ref, w_ref, o_ref, xn_ref):
    @pl.when(pl.program_id(1) == 0)
    def _():
        x = x_ref[...]
        ms = jnp.mean(x * x, axis=-1, keepdims=True)
        xn_ref[...] = (x * lax.rsqrt(ms + RMS_EPS) * g_ref[...]).astype(BF16)

    o_ref[...] = _dot(xn_ref[...], w_ref[...]).astype(o_ref.dtype)


def _proj(x2, gain, w_packed, layer, out_dtype, tm, tn, name):
    n = x2.shape[0]
    width = w_packed.shape[-1]
    return pl.pallas_call(
        _proj_kernel,
        out_shape=jax.ShapeDtypeStruct((n, width), out_dtype),
        grid=(n // tm, width // tn),
        in_specs=[
            pl.BlockSpec((tm, D_MODEL), lambda i, j: (i, 0)),
            pl.BlockSpec((1, D_MODEL), lambda i, j: (0, 0)),
            pl.BlockSpec((None, D_MODEL, tn), lambda i, j: (layer, 0, j)),
        ],
        out_specs=pl.BlockSpec((tm, tn), lambda i, j: (i, j)),
        scratch_shapes=[pltpu.VMEM((tm, D_MODEL), BF16)],
        compiler_params=_cparams(("parallel", "arbitrary")),
        name=name,
    )(x2, gain, w_packed)


HG_LEVELS = (32, 16, 8, 4, 2, 1)
HG_UNROLL = 8


def _hgrn_constants():
    c = HG_CHUNK
    t = np.arange(c)
    i = t[:, None]
    tt = t[None, :]
    rows = [tt <= i]
    masks = []
    for m in HG_LEVELS:
        mid = ((t // (2 * m)) * (2 * m) + m)[:, None]
        upper = (i >= mid) & (tt >= mid) & (tt <= i)
        lower = (i < mid) & (tt > i) & (tt <= mid - 1)
        rows.append(upper | lower)
        same = (t[:, None] // (2 * m)) == (t[None, :] // (2 * m))
        masks.append(same & (i >= mid) & (tt < mid))
    masks.append(np.eye(c, dtype=bool))
    m_all = np.concatenate(rows, axis=0).astype(np.float32)
    return m_all, np.stack(masks).astype(np.float32)


def _hgrn_kernel(q_ref, f_ref, i_ref, g_ref, lb_ref, nw_ref, mall_ref, mask_ref, o_ref, st_ref, e_ref, *, n_chunks):
    c = HG_CHUNK
    nl = len(HG_LEVELS)

    @pl.when(pl.program_id(1) == 0)
    def _():
        st_ref[...] = jnp.zeros_like(st_ref)

    lb = lb_ref[...]
    nw = nw_ref[...]

    heads = [slice(h * HG_D, (h + 1) * HG_D) for h in range(HG_HEADS)]

    def intra(r0, u):
        fp = f_ref[pl.ds(r0, c), :]
        f = lb + (1.0 - lb) * _sigmoid(fp)
        logf = jnp.log(jnp.maximum(f, TINY))
        kk = (1.0 - lb) * _sigmoid(-fp)
        g_hi = logf.astype(BF16)
        g_lo = (logf - g_hi.astype(F32)).astype(BF16)
        e_ref[u] = _dot(mall_ref[...], jnp.concatenate([g_hi, g_lo], axis=1))
        q = q_ref[pl.ds(r0, c), :]

        def expo(row):
            return e_ref[u, row * c:(row + 1) * c, 0:HEAD_W] + e_ref[u, row * c:(row + 1) * c, HEAD_W:2 * HEAD_W]

        b = expo(0)
        b_last = b[c - 1:c, :]
        q_in = (q * jnp.exp(b)).astype(BF16)
        k_out = (kk * jnp.exp(b_last - b)).astype(BF16)
        qls, kls = [], []
        for l in range(nl):
            decay = jnp.exp(expo(1 + l))
            qls.append((q * decay).astype(BF16))
            kls.append((kk * decay).astype(BF16))
        qls.append(q.astype(BF16))
        kls.append(kk.astype(BF16))
        sls = [[_dot_nt(qls[l][:, hs], kls[l][:, hs]) for l in range(nl + 1)] for hs in heads]
        ss = []
        for h in range(HG_HEADS):
            s = jnp.zeros((c, c), F32)
            for l in range(nl + 1):
                s = s + jnp.where(mask_ref[l] > 0.0, sls[h][l], 0.0)
            ss.append(s.astype(BF16))
        v = i_ref[pl.ds(r0, c), :].astype(BF16)
        o_intra = [_dot(ss[h], v[:, hs]) for h, hs in enumerate(heads)]
        kv = [_dot(v[:, hs].astype(F32).T.astype(BF16), k_out[:, hs]) for hs in heads]
        return q_in, jnp.exp(b_last), o_intra, kv

    def step(ci, carry):
        parts = [intra(pl.multiple_of((ci * HG_UNROLL + u) * c, c), u) for u in range(HG_UNROLL)]
        for u, (q_in, st_decay, o_intra, kv) in enumerate(parts):
            r0 = pl.multiple_of((ci * HG_UNROLL + u) * c, c)
            og = g_ref[pl.ds(r0, c), :]
            outs = [_dot_nt(q_in[:, hs], st_ref[h].astype(BF16)) + o_intra[h] for h, hs in enumerate(heads)]
            for h, hs in enumerate(heads):
                st_ref[h] = st_ref[h] * st_decay[:, hs] + kv[h]
            for h, hs in enumerate(heads):
                o = outs[h]
                y = o * lax.rsqrt(jnp.mean(o * o, axis=-1, keepdims=True) + RMS_EPS) * nw
                gh = og[:, hs]
                o_ref[pl.ds(r0, c), hs] = (y * (gh * _sigmoid(gh))).astype(o_ref.dtype)
        return carry

    lax.fori_loop(0, n_chunks // HG_UNROLL, step, 0)


def _hgrn(proj_f, lb, norm_w, batch, seq, rows):
    m_all, masks = _hgrn_constants()
    n_rb = seq // rows
    n_chunks = rows // HG_CHUNK
    assert n_chunks % HG_UNROLL == 0

    def col(c0):
        return pl.BlockSpec((rows, HEAD_W), lambda b, r, cidx=c0 // HEAD_W: (b * n_rb + r, cidx))

    return pl.pallas_call(
        functools.partial(_hgrn_kernel, n_chunks=n_chunks),
        out_shape=jax.ShapeDtypeStruct((batch * seq, HEAD_W), BF16),
        grid=(batch, n_rb),
        in_specs=[
            col(FCOL_HQ), col(FCOL_HF), col(FCOL_HI), col(FCOL_HG),
            pl.BlockSpec((1, HEAD_W), lambda b, r: (0, 0)),
            pl.BlockSpec((1, HG_D), lambda b, r: (0, 0)),
            pl.BlockSpec(m_all.shape, lambda b, r: (0, 0)),
            pl.BlockSpec(masks.shape, lambda b, r: (0, 0, 0)),
        ],
        out_specs=pl.BlockSpec((rows, HEAD_W), lambda b, r: (b * n_rb + r, 0)),
        scratch_shapes=[
            pltpu.VMEM((HG_HEADS, HG_D, HG_D), F32),
            pltpu.VMEM((HG_UNROLL, m_all.shape[0], 2 * HEAD_W), F32),
        ],
        compiler_params=_cparams(("parallel", "arbitrary")),
        name="hgrn",
    )(proj_f, proj_f, proj_f, proj_f, lb, norm_w, jnp.asarray(m_all, BF16), jnp.asarray(masks, F32))


def _rel_bucket(dist):
    n = jnp.maximum(dist, 0)
    exact = REL_BUCKETS // 2
    nf = jnp.maximum(n, exact).astype(F32)
    large = exact + (jnp.log(nf / exact) / math.log(REL_MAX_DIST / exact) * (REL_BUCKETS - exact)).astype(jnp.int32)
    large = jnp.minimum(large, REL_BUCKETS - 1)
    return jnp.where(n < exact, n, large)


def _n_bias_tiles(tile):
    d = 0
    while d * tile - (tile - 1) < REL_FAR_DIST:
        d += 1
    return d + 1


def _bias_tiles_t(table, tile):
    n = _n_bias_tiles(tile)
    length = n * tile
    by_dist = (table[_rel_bucket(jnp.arange(length, dtype=jnp.int32))].astype(F32) * LOG2E).T
    heads = by_dist.shape[0]
    w = jnp.concatenate([by_dist, jnp.zeros((heads, 1), F32)], axis=1)
    skew = jnp.broadcast_to(w[:, None, :], (heads, tile, length + 1)).reshape(heads, tile * (length + 1))
    skew = skew[:, :tile * length].reshape(heads, tile, length)
    return skew.reshape(heads, tile, n, tile).transpose(0, 2, 1, 3)


def _moba_kernel(q_ref, k_ref, v_ref, bias_ref, o_ref, vt_ref, km_ref, sel_ref, m_ref, l_ref, acc_ref,
                 *, n_blocks, n_sel, n_bias):
    blk, dh, nh = MOBA_BLOCK, MOBA_DH, MOBA_HEADS
    qb = pl.program_id(1)
    scale2 = dh ** -0.5 * LOG2E

    @pl.when(qb == 0)
    def _():
        for h in range(nh):
            for n in range(n_blocks):
                kblk = k_ref[n * blk:(n + 1) * blk, h * dh:(h + 1) * dh].astype(F32)
                km_ref[h, n:n + 1, :] = jnp.mean(kblk, axis=0, keepdims=True)
                vt_ref[h * dh:(h + 1) * dh, n * blk:(n + 1) * blk] = _t_bf16(v_ref[n * blk:(n + 1) * blk, h * dh:(h + 1) * dh])

    q_t = []
    n_iota = lax.broadcasted_iota(jnp.int32, (n_blocks, blk), 0)
    for h in range(nh):
        qt = _t_bf16(q_ref[:, h * dh:(h + 1) * dh])
        q_t.append(qt)
        gate = _dot(km_ref[h].astype(BF16), qt)
        gate = jnp.where(n_iota < qb, gate, NEG_BIG)
        rank = jnp.zeros(gate.shape, F32)
        for n in range(n_blocks):
            row = gate[n:n + 1, :]
            beats = (row > gate) | ((row == gate) & (n < n_iota))
            rank = rank + jnp.where(beats, 1.0, 0.0)
        sel_ref[h] = jnp.where((rank < n_sel) & (n_iota < qb), 0.0, NEG_BIG)
        m_ref[h] = jnp.full((1, blk), -jnp.inf, F32)
        l_ref[h] = jnp.zeros((1, blk), F32)
        acc_ref[h] = jnp.zeros((dh, blk), F32)

    def attend(r0, nk, mask_fn):
        ss = [_dot(k_ref[pl.ds(r0, nk), h * dh:(h + 1) * dh], q_t[h]) for h in range(nh)]
        ps, alphas = [], []
        for h in range(nh):
            s = ss[h] * scale2 + mask_fn(h)
            m_old = m_ref[h]
            m_new = jnp.maximum(m_old, _reduce_rows(s, jnp.max))
            alphas.append(jnp.exp2(m_old - m_new))
            p = jnp.exp2(s - m_new)
            l_ref[h] = alphas[h] * l_ref[h] + _reduce_rows(p, jnp.sum)
            m_ref[h] = m_new
            ps.append(p.astype(BF16))
        for h in range(nh):
            acc_ref[h] = alphas[h] * acc_ref[h] + _dot(vt_ref[h * dh:(h + 1) * dh, pl.ds(r0, nk)], ps[h])

    def past_mask(h, n):
        return bias_ref[h, jnp.minimum(qb - n, n_bias - 1)] + sel_ref[h, pl.ds(n, 1), :]

    causal = jnp.where(lax.broadcasted_iota(jnp.int32, (blk, blk), 0) <= lax.broadcasted_iota(jnp.int32, (blk, blk), 1),
                       0.0, NEG_BIG)

    def own_mask(h):
        return bias_ref[h, 0] + causal

    group = MOBA_STEP_BLOCKS

    def past_group(i, carry):
        n = group * i
        attend(pl.multiple_of(n * blk, group * blk), group * blk,
               lambda h: jnp.concatenate([past_mask(h, n + u) for u in range(group)], axis=0))
        return carry

    lax.fori_loop(0, qb // group, past_group, 0)

    for k in range(group):
        @pl.when(qb % group == k)
        def _(k=k):
            attend(pl.multiple_of((qb - k) * blk, blk), (k + 1) * blk,
                   lambda h: jnp.concatenate([past_mask(h, qb - k + u) for u in range(k)] + [own_mask(h)], axis=0))

    for h in range(nh):
        o_ref[:, h * dh:(h + 1) * dh] = (acc_ref[h] / l_ref[h]).T.astype(o_ref.dtype)


def _moba(proj_b, bias_tiles, batch, seq):
    blk = MOBA_BLOCK
    nb = seq // blk
    n_sel = min(MOBA_TOPK, nb - 1)
    n_bias = bias_tiles.shape[1]
    return pl.pallas_call(
        functools.partial(_moba_kernel, n_blocks=nb, n_sel=n_sel, n_bias=n_bias),
        out_shape=jax.ShapeDtypeStruct((batch * seq, HEAD_W), BF16),
        grid=(batch, nb),
        in_specs=[
            pl.BlockSpec((blk, HEAD_W), lambda b, i: (b * nb + i, BCOL_MQ // HEAD_W)),
            pl.BlockSpec((seq, HEAD_W), lambda b, i: (b, BCOL_MK // HEAD_W)),
            pl.BlockSpec((seq, HEAD_W), lambda b, i: (b, BCOL_MV // HEAD_W)),
            pl.BlockSpec(bias_tiles.shape, lambda b, i: (0, 0, 0, 0)),
        ],
        out_specs=pl.BlockSpec((blk, HEAD_W), lambda b, i: (b * nb + i, 0)),
        scratch_shapes=[
            pltpu.VMEM((HEAD_W, seq), BF16),
            pltpu.VMEM((MOBA_HEADS, nb, MOBA_DH), F32),
            pltpu.VMEM((MOBA_HEADS, nb, blk), F32),
            pltpu.VMEM((MOBA_HEADS, 1, blk), F32),
            pltpu.VMEM((MOBA_HEADS, 1, blk), F32),
            pltpu.VMEM((MOBA_HEADS, MOBA_DH, blk), F32),
        ],
        compiler_params=_cparams(("parallel", "arbitrary")),
        name="moba",
    )(proj_b, proj_b, proj_b, bias_tiles)


def _sort_key(x):
    bits = pltpu.bitcast(x, jnp.int32)
    return bits ^ ((bits >> 31) & 0x7FFFFFFF)


def _dsa_kernel(dq_ref, iq_ref, iwq_ref, dkv_ref, ikw_ref, kvg_ref, lnw_ref, lnb_ref, wuk_ref, wuv_ref, bias_ref,
                o_ref, ckv_ref, ckvt_ref, ikn_ref, key_ref, mask_ref, tau_ref, cge_ref, m_ref, l_ref, acc_ref,
                *, seq, n_keep, n_bias):
    tq, kc, nh, dh = DSA_QB, DSA_KC, DSA_HEADS, DSA_DH
    sub = kc // LANES
    qb = pl.program_id(1)
    t0 = qb * tq
    n_kc = (t0 + tq - 1) // kc + 1
    n_rest = (seq - n_kc * kc).astype(F32)

    @pl.when(qb == 0)
    def _():
        for c in range(seq // kc):
            rows = slice(c * kc, (c + 1) * kc)
            x = dkv_ref[rows, :]
            cn = x * lax.rsqrt(jnp.mean(x * x, axis=-1, keepdims=True) + RMS_EPS) * kvg_ref[...]
            ckv_ref[rows, :] = cn.astype(BF16)
            ckvt_ref[:, rows] = cn.T.astype(BF16)
            ik = ikw_ref[rows, 0:IDX_DIM]
            xc = ik - jnp.mean(ik, axis=-1, keepdims=True)
            ikn = xc * lax.rsqrt(jnp.mean(xc * xc, axis=-1, keepdims=True) + RMS_EPS) * lnw_ref[...] + lnb_ref[...]
            ikn_ref[rows, :] = ikn.astype(BF16)

    t_row = t0 + lax.broadcasted_iota(jnp.int32, (1, tq), 1)
    key_off = lax.broadcasted_iota(jnp.int32, (kc, tq), 0)

    def causal_of(c0):
        return (c0 + key_off) <= t_row

    iq_t = iq_ref[...].astype(F32).T
    iq_cat = jnp.concatenate([iq_t[h * IDX_DIM:(h + 1) * IDX_DIM, :] for h in range(IDX_HEADS)], axis=1).astype(BF16)
    iw_t = iwq_ref[...].T[IDX_DIM:IDX_DIM + IDX_HEADS, :] * (IDX_HEADS ** -0.5)
    iw_t = iw_t * (IDX_DIM ** -0.5)

    hpg = 2 * LANES // tq

    def score_chunks(c, nc):
        c0s = [pl.multiple_of((c + u) * kc, kc) for u in range(nc)]
        s_grp = [[_dot(ikn_ref[pl.ds(c0, kc), :], iq_cat[:, g * hpg * tq:(g + 1) * hpg * tq])
                  for g in range(IDX_HEADS // hpg)] for c0 in c0s]
        for u, c0 in enumerate(c0s):
            sc = jnp.zeros((kc, tq), F32)
            for h in range(IDX_HEADS):
                s_h = s_grp[u][h // hpg][:, (h % hpg) * tq:(h % hpg + 1) * tq]
                sc = sc + iw_t[h:h + 1, :] * jnp.maximum(s_h, 0.0)
            sc = jnp.where(causal_of(c0), sc, NEG_BIG)
            sc = jnp.where(sc == 0.0, 0.0, sc)
            key_ref[pl.ds(c0, kc), :] = _sort_key(sc)

    def score_pair(i, carry):
        score_chunks(2 * i, 2)
        return carry

    lax.fori_loop(0, n_kc // 2, score_pair, 0)

    @pl.when(n_kc % 2 == 1)
    def _():
        score_chunks(n_kc - 1, 1)

    def count(pred_fn):
        def body(c, tot):
            c0 = pl.multiple_of(c * kc, kc)
            hit = pred_fn(key_ref[pl.ds(c0, kc), :], c0)
            return tot + _slab(jnp.where(hit, 1.0, 0.0), jnp.sum)
        part = lax.fori_loop(0, n_kc, body, jnp.zeros((ROW_SLAB, tq), F32))
        return _reduce_rows(part, jnp.sum)

    def bit_step(k, carry):
        base, c_base = carry
        cand = base + jnp.left_shift(jnp.int32(1), 31 - k)
        cnt = count(lambda keys, c0: keys >= cand) + jnp.where(cand <= KEY_NEG, n_rest, 0.0)
        take = cnt >= n_keep
        return jnp.where(take, cand, base), jnp.where(take, cnt, c_base)

    tau, c_ge = lax.fori_loop(0, DSA_SURE_BITS, bit_step, (jnp.full((1, tq), jnp.iinfo(jnp.int32).min, jnp.int32),
                                                           jnp.full((1, tq), float(seq), F32)))
    tau_ref[...] = tau
    cge_ref[...] = c_ge

    @pl.when(jnp.max(jnp.abs(c_ge - n_keep)) > 0.0)
    def _():
        tau_ref[...], cge_ref[...] = lax.fori_loop(DSA_SURE_BITS, 32, bit_step, (tau, c_ge))

    tau = tau_ref[...]
    c_ge = cge_ref[...]

    c_gt = count(lambda keys, c0: keys > tau) + jnp.where(tau < KEY_NEG, n_rest, 0.0)
    need = n_keep - c_gt
    n_eq = (c_ge - c_gt) - jnp.where(tau == KEY_NEG, (seq - 1 - t_row).astype(F32), 0.0)
    tie_break = jnp.max(n_eq - need) > 0.0

    @pl.when(jnp.logical_not(tie_break))
    def _():
        def body(c, carry):
            c0 = pl.multiple_of(c * kc, kc)
            keep = (key_ref[pl.ds(c0, kc), :] >= tau) & causal_of(c0)
            mask_ref[pl.ds(c0, kc), :] = jnp.where(keep, 0.0, NEG_BIG)
            return carry
        lax.fori_loop(0, n_kc, body, 0)

    @pl.when(tie_break)
    def _():
        lower = jnp.where(lax.broadcasted_iota(jnp.int32, (kc, kc), 1)
                          < lax.broadcasted_iota(jnp.int32, (kc, kc), 0), 1.0, 0.0).astype(BF16)

        def body(c, seen):
            c0 = pl.multiple_of(c * kc, kc)
            keys = key_ref[pl.ds(c0, kc), :]
            eq = keys == tau
            eqf = jnp.where(eq, 1.0, 0.0)
            rank = seen + _dot(lower, eqf.astype(BF16))
            keep = ((keys > tau) | (eq & (rank < need))) & causal_of(c0)
            mask_ref[pl.ds(c0, kc), :] = jnp.where(keep, 0.0, NEG_BIG)
            return seen + _reduce_rows(eqf, jnp.sum)
        lax.fori_loop(0, n_kc, body, jnp.zeros((1, tq), F32))

    scale2 = dh ** -0.5 * LOG2E
    qa_cat = jnp.concatenate(
        [_dot(wuk_ref[h], _t_bf16(dq_ref[:, h * dh:(h + 1) * dh])) for h in range(nh)], axis=1).astype(BF16)
    m_ref[...] = jnp.full(m_ref.shape, -jnp.inf, F32)
    l_ref[...] = jnp.zeros(l_ref.shape, F32)
    acc_ref[...] = jnp.zeros(acc_ref.shape, F32)

    def attend(c, nc):
        c0 = pl.multiple_of(c * kc, kc)
        nk = nc * kc
        msk = mask_ref[pl.ds(c0, nk), :]
        bias = jnp.concatenate(
            [bias_ref[jnp.clip(qb - (c * sub + g), 0, n_bias - 1)] for g in range(nc * sub)], axis=0)
        s = _dot(ckv_ref[pl.ds(c0, nk), :], qa_cat) * scale2 + bias + jnp.concatenate([msk] * nh, axis=1)
        m_old = m_ref[...]
        m_new = jnp.maximum(m_old, _reduce_rows(s, jnp.max))
        alpha = jnp.exp2(m_old - m_new)
        p = jnp.exp2(s - m_new)
        l_ref[...] = alpha * l_ref[...] + _reduce_rows(p, jnp.sum)
        acc_ref[...] = alpha * acc_ref[...] + _dot(ckvt_ref[:, pl.ds(c0, nk)], p.astype(BF16))
        m_ref[...] = m_new

    def att_pair(i, carry):
        attend(2 * i, 2)
        return carry

    lax.fori_loop(0, n_kc // 2, att_pair, 0)

    @pl.when(n_kc % 2 == 1)
    def _():
        attend(n_kc - 1, 1)

    o_lat_t = (acc_ref[...] / l_ref[...]).astype(BF16)
    for h in range(nh):
        o_ref[:, h * dh:(h + 1) * dh] = _dot(wuv_ref[h], o_lat_t[:, h * tq:(h + 1) * tq]).T.astype(o_ref.dtype)


def _dsa(proj_f, proj_b, kv_gain, ln_w, ln_b, w_uk_h, w_uv_h, bias_tiles, batch, seq):
    tq = DSA_QB
    nq = seq // tq
    n_keep = min(DSA_TOPK_MAX, seq // 4)
    n_bias = bias_tiles.shape[0]
    return pl.pallas_call(
        functools.partial(_dsa_kernel, seq=seq, n_keep=float(n_keep), n_bias=n_bias),
        out_shape=jax.ShapeDtypeStruct((batch * seq, HEAD_W), BF16),
        grid=(batch, nq),
        in_specs=[
            pl.BlockSpec((tq, HEAD_W), lambda b, i: (b * nq + i, BCOL_DQ // HEAD_W)),
            pl.BlockSpec((tq, HEAD_W), lambda b, i: (b * nq + i, BCOL_IQ // HEAD_W)),
            pl.BlockSpec((tq, LANES), lambda b, i: (b * nq + i, FCOL_IKW // LANES)),
            pl.BlockSpec((seq, DSA_KV_RANK), lambda b, i: (b, FCOL_DKV // DSA_KV_RANK)),
            pl.BlockSpec((seq, LANES), lambda b, i: (b, FCOL_IKW // LANES)),
            pl.BlockSpec((1, DSA_KV_RANK), lambda b, i: (0, 0)),
            pl.BlockSpec((1, IDX_DIM), lambda b, i: (0, 0)),
            pl.BlockSpec((1, IDX_DIM), lambda b, i: (0, 0)),
            pl.BlockSpec(w_uk_h.shape, lambda b, i: (0, 0, 0)),
            pl.BlockSpec(w_uv_h.shape, lambda b, i: (0, 0, 0)),
            pl.BlockSpec(bias_tiles.shape, lambda b, i: (0, 0, 0)),
        ],
        out_specs=pl.BlockSpec((tq, HEAD_W), lambda b, i: (b * nq + i, 0)),
        scratch_shapes=[
            pltpu.VMEM((seq, DSA_KV_RANK), BF16),
            pltpu.VMEM((DSA_KV_RANK, seq), BF16),
            pltpu.VMEM((seq, IDX_DIM), BF16),
            pltpu.VMEM((seq, tq), jnp.int32),
            pltpu.VMEM((seq, tq), F32),
            pltpu.VMEM((1, tq), jnp.int32),
            pltpu.VMEM((1, tq), F32),
            pltpu.VMEM((1, DSA_HEADS * tq), F32),
            pltpu.VMEM((1, DSA_HEADS * tq), F32),
            pltpu.VMEM((DSA_KV_RANK, DSA_HEADS * tq), F32),
        ],
        compiler_params=_cparams(("parallel", "arbitrary")),
        name="dsa",
    )(proj_b, proj_b, proj_f, proj_f, proj_f, kv_gain, ln_w, ln_b, w_uk_h, w_uv_h, bias_tiles)


def _merge_kernel(x_ref, ya_ref, yb_ref, yc_ref, g0_ref, g1_ref, g2_ref, wa_ref, wb_ref, wc_ref, wo_ref, gain_ref,
                  o_ref):
    merged = (_sigmoid(g0_ref[...]) * _dot(ya_ref[...], wa_ref[...])
              + _sigmoid(g1_ref[...]) * _dot(yb_ref[...], wb_ref[...])
              + _sigmoid(g2_ref[...]) * _dot(yc_ref[...], wc_ref[...]))
    z = _dot(merged.astype(BF16), wo_ref[...])
    zn = z * lax.rsqrt(jnp.mean(z * z, axis=-1, keepdims=True) + RMS_EPS) * gain_ref[...]
    o_ref[...] = x_ref[...] + zn


def _merge(x2, ya, yb, yc, proj_f, wa, wb, wc, wo, layer, gain, tm):
    n = x2.shape[0]
    gcol = FCOL_GATES // D_MODEL

    def rows(w):
        return pl.BlockSpec((tm, w), lambda i: (i, 0))

    def gate(k):
        return pl.BlockSpec((tm, D_MODEL), lambda i, k=k: (i, gcol + k))

    def weight(a):
        return pl.BlockSpec((None,) + a.shape[1:], lambda i: (layer, 0, 0))

    return pl.pallas_call(
        _merge_kernel,
        out_shape=jax.ShapeDtypeStruct((n, D_MODEL), F32),
        grid=(n // tm,),
        in_specs=[rows(D_MODEL), rows(HEAD_W), rows(HEAD_W), rows(HEAD_W), gate(0), gate(1), gate(2),
                  weight(wa), weight(wb), weight(wc), weight(wo), pl.BlockSpec(gain.shape, lambda i: (0, 0))],
        out_specs=rows(D_MODEL),
        compiler_params=_cparams(("parallel",)),
        name="merge",
    )(x2, ya, yb, yc, proj_f, proj_f, proj_f, wa, wb, wc, wo, gain)


def _ffn_kernel(x_ref, g1_ref, wg_ref, wu_ref, wd_ref, g2_ref, o_ref, hn_ref, acc_ref):
    j = pl.program_id(1)

    @pl.when(j == 0)
    def _():
        x = x_ref[...]
        hn_ref[...] = (x * lax.rsqrt(jnp.mean(x * x, axis=-1, keepdims=True) + RMS_EPS) * g1_ref[...]).astype(BF16)
        acc_ref[...] = jnp.zeros_like(acc_ref)

    hn = hn_ref[...]
    gate = _dot(hn, wg_ref[...])
    up = _dot(hn, wu_ref[...])
    act = (gate * _sigmoid(gate)) * up
    acc_ref[...] += _dot(act.astype(BF16), wd_ref[...])

    @pl.when(j == pl.num_programs(1) - 1)
    def _():
        y = acc_ref[...]
        yn = y * lax.rsqrt(jnp.mean(y * y, axis=-1, keepdims=True) + RMS_EPS) * g2_ref[...]
        o_ref[...] = x_ref[...] + yn


def _ffn(x2, g1, w_in, w_out, layer, g2, tm, tf):
    n = x2.shape[0]
    nf = D_FF // tf
    return pl.pallas_call(
        _ffn_kernel,
        out_shape=jax.ShapeDtypeStruct((n, D_MODEL), F32),
        grid=(n // tm, nf),
        in_specs=[
            pl.BlockSpec((tm, D_MODEL), lambda i, j: (i, 0)),
            pl.BlockSpec((1, D_MODEL), lambda i, j: (0, 0)),
            pl.BlockSpec((None, D_MODEL, tf), lambda i, j: (layer, 0, j)),
            pl.BlockSpec((None, D_MODEL, tf), lambda i, j: (layer, 0, nf + j)),
            pl.BlockSpec((None, tf, D_MODEL), lambda i, j: (layer, j, 0)),
            pl.BlockSpec((1, D_MODEL), lambda i, j: (0, 0)),
        ],
        out_specs=pl.BlockSpec((tm, D_MODEL), lambda i, j: (i, 0)),
        scratch_shapes=[pltpu.VMEM((tm, D_MODEL), BF16), pltpu.VMEM((tm, D_MODEL), F32)],
        compiler_params=_cparams(("parallel", "arbitrary")),
        name="ffn",
    )(x2, g1, w_in, w_in, w_out, g2)


def _pack_w_in(w):
    o_mq, o_dq, o_dkv, o_iq, o_ik, o_gates = 2048, 3584, 4096, 4352, 4864, 4936
    w = w.astype(BF16)
    z = lambda k: jnp.zeros(w.shape[:-1] + (k,), w.dtype)
    w_f = jnp.concatenate([
        w[..., :o_mq],
        w[..., o_gates:],
        w[..., o_dkv:o_iq],
        w[..., o_ik:o_gates],
        z(PROJ_F_W - FCOL_IKW - (o_gates - o_ik)),
    ], axis=-1)
    w_b = jnp.concatenate([
        w[..., o_mq:o_dq],
        w[..., o_dq:o_dkv],
        w[..., o_iq:o_ik],
    ], axis=-1)
    assert w_f.shape[-1] == PROJ_F_W and w_b.shape[-1] == PROJ_B_W
    return w_f, w_b


def _tiles(batch, seq):
    n = batch * seq
    return dict(
        proj_tm=min(1024, n), proj_f_tn=PROJ_F_W // 4, proj_b_tm=min(2048, n), proj_b_tn=PROJ_B_W // 2,
        hgrn_rows=min(512, seq),
        merge_tm=min(512, n),
        ffn_tm=min(1024, n), ffn_tf=D_FF // 2,
    )


def kernel(x, w_in, pre_mix_norm, post_mix_norm, pre_ffn_norm, post_ffn_norm, hg_lb_logits, hg_norm_w, dsa_kv_norm,
           dsa_w_uk, dsa_w_uv, idx_k_norm_w, idx_k_norm_b, rel_bias, w_branch_a, w_branch_b, w_branch_c, w_out,
           w_ffn_in, w_ffn_out):
    batch, seq, _ = x.shape
    depth = w_in.shape[0]
    t = _tiles(batch, seq)
    lbp = jax.nn.softmax(hg_lb_logits.astype(F32), axis=0)
    lower_bounds = jnp.cumsum(lbp, axis=0) - lbp[0]
    bias_b = _bias_tiles_t(rel_bias[:, :MOBA_HEADS], MOBA_BLOCK)
    bias_c = _bias_tiles_t(rel_bias[:, MOBA_HEADS:], LANES)
    bias_c = bias_c.transpose(1, 2, 0, 3).reshape(bias_c.shape[1], LANES, DSA_HEADS * LANES)
    row = lambda a: a.reshape(1, -1).astype(F32)

    w_f, w_b = _pack_w_in(w_in)
    w_uk = jnp.transpose(dsa_w_uk, (0, 2, 1, 3)).astype(BF16)
    w_uv = jnp.transpose(dsa_w_uv, (0, 2, 3, 1)).astype(BF16)
    w_a, w_bb, w_c, w_o = (a.astype(BF16) for a in (w_branch_a, w_branch_b, w_branch_c, w_out))
    w_fi, w_fo = w_ffn_in.astype(BF16), w_ffn_out.astype(BF16)

    x2 = x.reshape(batch * seq, D_MODEL)
    for l in range(depth):
        gain = row(pre_mix_norm[l])
        proj_f = _proj(x2, gain, w_f, l, F32, t["proj_tm"], t["proj_f_tn"], "proj_f")
        proj_b = _proj(x2, gain, w_b, l, BF16, t["proj_b_tm"], t["proj_b_tn"], "proj_b")
        y_a = _hgrn(proj_f, row(lower_bounds[l]), row(hg_norm_w[l]), batch, seq, t["hgrn_rows"])
        y_b = _moba(proj_b, bias_b, batch, seq)
        y_c = _dsa(proj_f, proj_b, row(dsa_kv_norm[l]), row(idx_k_norm_w[l]), row(idx_k_norm_b[l]),
                   w_uk[l], w_uv[l], bias_c, batch, seq)
        x2 = _merge(x2, y_a, y_b, y_c, proj_f, w_a, w_bb, w_c, w_o, l, row(post_mix_norm[l]), t["merge_tm"])
        x2 = _ffn(x2, row(pre_ffn_norm[l]), w_fi, w_fo, l, row(post_ffn_norm[l]), t["ffn_tm"], t["ffn_tf"])
    return x2.reshape(batch, seq, D_MODEL)
```

```python
import functools
import math

import numpy as np
import jax
import jax.numpy as jnp
from jax import lax
from jax.experimental import pallas as pl
from jax.experimental.pallas import tpu as pltpu

F32 = jnp.float32
BF16 = jnp.bfloat16

D_MODEL = 1024
RMS_EPS = 1e-6
LOG2E = 1.0 / math.log(2.0)
NEG_BIG = -1e30
TINY = 1e-20
HG_HEADS = 4
HG_D = 128
HG_CHUNK = 64
MOBA_HEADS = 4
MOBA_DH = 128
MOBA_BLOCK = 256
MOBA_TOPK = 3
MOBA_STEP_BLOCKS = 4
DSA_HEADS = 4
DSA_DH = 128
DSA_KV_RANK = 256
IDX_HEADS = 8
IDX_DIM = 64
DSA_TOPK_MAX = 256
DSA_KC = 512
DSA_SURE_BITS = 28
REL_BUCKETS = 32
REL_MAX_DIST = 2048
D_FF = 2816
HEAD_W = 512

LANES = 128
SUBLANES = 8
DSA_QB = LANES

FCOL_HQ, FCOL_HF, FCOL_HI, FCOL_HG = 0, 512, 1024, 1536
FCOL_GATES, FCOL_DKV, FCOL_IKW = 2048, 5120, 5376
PROJ_F_W = 5632
BCOL_MQ, BCOL_MK, BCOL_MV, BCOL_DQ, BCOL_IQ = 0, 512, 1024, 1536, 2048
PROJ_B_W = 2560

VMEM_LIMIT = 56 * 1024 * 1024

_REL_EXACT = REL_BUCKETS // 2
REL_FAR_DIST = int(math.ceil(_REL_EXACT * (REL_MAX_DIST / _REL_EXACT) ** ((REL_BUCKETS - _REL_EXACT - 1) / (REL_BUCKETS - _REL_EXACT)))) + 1

_NEG_BITS = int(np.array(NEG_BIG, np.float32).view(np.int32))
KEY_NEG = int(np.int32(_NEG_BITS ^ 0x7FFFFFFF))


def _cparams(sem):
    return pltpu.CompilerParams(dimension_semantics=sem, vmem_limit_bytes=VMEM_LIMIT)


def _sigmoid(x):
    return 1.0 / (1.0 + jnp.exp(-x))


def _dot(a, b):
    return jnp.dot(a, b, preferred_element_type=F32)


def _dot_nt(a, b):
    return lax.dot_general(a, b, (((1,), (1,)), ((), ())), preferred_element_type=F32)


def _t_bf16(a):
    return a.astype(F32).T.astype(BF16)


ROW_SLAB = 64


def _slab(x, fn):
    return fn(x.reshape(x.shape[0] // ROW_SLAB, ROW_SLAB, x.shape[1]), axis=0)


def _reduce_rows(x, fn):
    y = _slab(x, fn)
    z = fn(y.reshape(ROW_SLAB // SUBLANES, SUBLANES, y.shape[1]), axis=0)
    return fn(z, axis=0, keepdims=True)


def _proj_kernel(x_ref, g_ref, w_ref, o_ref, xn_ref):
    @pl.when(pl.program_id(1) == 0)
    def _():
        x = x_ref[...]
        ms = jnp.mean(x * x, axis=-1, keepdims=True)
        xn_ref[...] = (x * lax.rsqrt(ms + RMS_EPS) * g_ref[...]).astype(BF16)

    o_ref[...] = _dot(xn_ref[...], w_ref[...]).astype(o_ref.dtype)


def _proj(x2, gain, w_packed, layer, out_dtype, tm, tn, name):
    n = x2.shape[0]
    width = w_packed.shape[-1]
    return pl.pallas_call(
        _proj_kernel,
        out_shape=jax.ShapeDtypeStruct((n, width), out_dtype),
        grid=(n // tm, width // tn),
        in_specs=[
            pl.BlockSpec((tm, D_MODEL), lambda i, j: (i, 0)),
            pl.BlockSpec((1, D_MODEL), lambda i, j: (0, 0)),
            pl.BlockSpec((None, D_MODEL, tn), lambda i, j: (layer, 0, j)),
        ],
        out_specs=pl.BlockSpec((tm, tn), lambda i, j: (i, j)),
        scratch_shapes=[pltpu.VMEM((tm, D_MODEL), BF16)],
        compiler_params=_cparams(("parallel", "arbitrary")),
        name=name,
    )(x2, gain, w_packed)


HG_LEVELS = (32, 16, 8, 4, 2, 1)
HG_UNROLL = 8


def _hgrn_constants():
    c = HG_CHUNK
    t = np.arange(c)
    i = t[:, None]
    tt = t[None, :]
    rows = [tt <= i]
    masks = []
    for m in HG_LEVELS:
        mid = ((t // (2 * m)) * (2 * m) + m)[:, None]
        upper = (i >= mid) & (tt >= mid) & (tt <= i)
        lower = (i < mid) & (tt > i) & (tt <= mid - 1)
        rows.append(upper | lower)
        same = (t[:, None] // (2 * m)) == (t[None, :] // (2 * m))
        masks.append(same & (i >= mid) & (tt < mid))
    masks.append(np.eye(c, dtype=bool))
    m_all = np.concatenate(rows, axis=0).astype(np.float32)
    return m_all, np.stack(masks).astype(np.float32)


def _hgrn_kernel(q_ref, f_ref, i_ref, g_ref, lb_ref, nw_ref, mall_ref, mask_ref, o_ref, st_ref, e_ref, *, n_chunks):
    c = HG_CHUNK
    nl = len(HG_LEVELS)

    @pl.when(pl.program_id(1) == 0)
    def _():
        st_ref[...] = jnp.zeros_like(st_ref)

    lb = lb_ref[...]
    nw = nw_ref[...]

    heads = [slice(h * HG_D, (h + 1) * HG_D) for h in range(HG_HEADS)]

    def intra(r0, u):
        fp = f_ref[pl.ds(r0, c), :]
        f = lb + (1.0 - lb) * _sigmoid(fp)
        logf = jnp.log(jnp.maximum(f, TINY))
        kk = (1.0 - lb) * _sigmoid(-fp)
        g_hi = logf.astype(BF16)
        g_lo = (logf - g_hi.astype(F32)).astype(BF16)
        e_ref[u] = _dot(mall_ref[...], jnp.concatenate([g_hi, g_lo], axis=1))
        q = q_ref[pl.ds(r0, c), :]

        def expo(row):
            return e_ref[u, row * c:(row + 1) * c, 0:HEAD_W] + e_ref[u, row * c:(row + 1) * c, HEAD_W:2 * HEAD_W]

        b = expo(0)
        b_last = b[c - 1:c, :]
        q_in = (q * jnp.exp(b)).astype(BF16)
        k_out = (kk * jnp.exp(b_last - b)).astype(BF16)
        qls, kls = [], []
        for l in range(nl):
            decay = jnp.exp(expo(1 + l))
            qls.append((q * decay).astype(BF16))
            kls.append((kk * decay).astype(BF16))
        qls.append(q.astype(BF16))
        kls.append(kk.astype(BF16))
        sls = [[_dot_nt(qls[l][:, hs], kls[l][:, hs]) for l in range(nl + 1)] for hs in heads]
        ss = []
        for h in range(HG_HEADS):
            s = jnp.zeros((c, c), F32)
            for l in range(nl + 1):
                s = s + jnp.where(mask_ref[l] > 0.0, sls[h][l], 0.0)
            ss.append(s.astype(BF16))
        v = i_ref[pl.ds(r0, c), :].astype(BF16)
        o_intra = [_dot(ss[h], v[:, hs]) for h, hs in enumerate(heads)]
        kv = [_dot(v[:, hs].astype(F32).T.astype(BF16), k_out[:, hs]) for hs in heads]
        return q_in, jnp.exp(b_last), o_intra, kv

    def step(ci, carry):
        parts = [intra(pl.multiple_of((ci * HG_UNROLL + u) * c, c), u) for u in range(HG_UNROLL)]
        for u, (q_in, st_decay, o_intra, kv) in enumerate(parts):
            r0 = pl.multiple_of((ci * HG_UNROLL + u) * c, c)
            og = g_ref[pl.ds(r0, c), :]
            outs = [_dot_nt(q_in[:, hs], st_ref[h].astype(BF16)) + o_intra[h] for h, hs in enumerate(heads)]
            for h, hs in enumerate(heads):
                st_ref[h] = st_ref[h] * st_decay[:, hs] + kv[h]
            for h, hs in enumerate(heads):
                o = outs[h]
                y = o * lax.rsqrt(jnp.mean(o * o, axis=-1, keepdims=True) + RMS_EPS) * nw
                gh = og[:, hs]
                o_ref[pl.ds(r0, c), hs] = (y * (gh * _sigmoid(gh))).astype(o_ref.dtype)
        return carry

    lax.fori_loop(0, n_chunks // HG_UNROLL, step, 0)


def _hgrn(proj_f, lb, norm_w, batch, seq, rows):
    m_all, masks = _hgrn_constants()
    n_rb = seq // rows
    n_chunks = rows // HG_CHUNK
    assert n_chunks % HG_UNROLL == 0

    def col(c0):
        return pl.BlockSpec((rows, HEAD_W), lambda b, r, cidx=c0 // HEAD_W: (b * n_rb + r, cidx))

    return pl.pallas_call(
        functools.partial(_hgrn_kernel, n_chunks=n_chunks),
        out_shape=jax.ShapeDtypeStruct((batch * seq, HEAD_W), BF16),
        grid=(batch, n_rb),
        in_specs=[
            col(FCOL_HQ), col(FCOL_HF), col(FCOL_HI), col(FCOL_HG),
            pl.BlockSpec((1, HEAD_W), lambda b, r: (0, 0)),
            pl.BlockSpec((1, HG_D), lambda b, r: (0, 0)),
            pl.BlockSpec(m_all.shape, lambda b, r: (0, 0)),
            pl.BlockSpec(masks.shape, lambda b, r: (0, 0, 0)),
        ],
        out_specs=pl.BlockSpec((rows, HEAD_W), lambda b, r: (b * n_rb + r, 0)),
        scratch_shapes=[
            pltpu.VMEM((HG_HEADS, HG_D, HG_D), F32),
            pltpu.VMEM((HG_UNROLL, m_all.shape[0], 2 * HEAD_W), F32),
        ],
        compiler_params=_cparams(("parallel", "arbitrary")),
        name="hgrn",
    )(proj_f, proj_f, proj_f, proj_f, lb, norm_w, jnp.asarray(m_all, BF16), jnp.asarray(masks, F32))


def _rel_bucket(dist):
    n = jnp.maximum(dist, 0)
    exact = REL_BUCKETS // 2
    nf = jnp.maximum(n, exact).astype(F32)
    large = exact + (jnp.log(nf / exact) / math.log(REL_MAX_DIST / exact) * (REL_BUCKETS - exact)).astype(jnp.int32)
    large = jnp.minimum(large, REL_BUCKETS - 1)
    return jnp.where(n < exact, n, large)


def _n_bias_tiles(tile):
    d = 0
    while d * tile - (tile - 1) < REL_FAR_DIST:
        d += 1
    return d + 1


def _bias_tiles_t(table, tile):
    n = _n_bias_tiles(tile)
    length = n * tile
    by_dist = (table[_rel_bucket(jnp.arange(length, dtype=jnp.int32))].astype(F32) * LOG2E).T
    heads = by_dist.shape[0]
    w = jnp.concatenate([by_dist, jnp.zeros((heads, 1), F32)], axis=1)
    skew = jnp.broadcast_to(w[:, None, :], (heads, tile, length + 1)).reshape(heads, tile * (length + 1))
    skew = skew[:, :tile * length].reshape(heads, tile, length)
    return skew.reshape(heads, tile, n, tile).transpose(0, 2, 1, 3)


def _moba_kernel(q_ref, k_ref, v_ref, bias_ref, o_ref, vt_ref, km_ref, sel_ref, m_ref, l_ref, acc_ref,
                 *, n_blocks, n_sel, n_bias):
    blk, dh, nh = MOBA_BLOCK, MOBA_DH, MOBA_HEADS
    qb = pl.program_id(1)
    scale2 = dh ** -0.5 * LOG2E

    @pl.when(qb == 0)
    def _():
        for h in range(nh):
            for n in range(n_blocks):
                kblk = k_ref[n * blk:(n + 1) * blk, h * dh:(h + 1) * dh].astype(F32)
                km_ref[h, n:n + 1, :] = jnp.mean(kblk, axis=0, keepdims=True)
                vt_ref[h * dh:(h + 1) * dh, n * blk:(n + 1) * blk] = _t_bf16(v_ref[n * blk:(n + 1) * blk, h * dh:(h + 1) * dh])

    q_h = []
    n_iota = lax.broadcasted_iota(jnp.int32, (n_blocks, blk), 0)
    for h in range(nh):
        qh = q_ref[:, h * dh:(h + 1) * dh]
        q_h.append(qh)
        gate = _dot_nt(km_ref[h].astype(BF16), qh)
        gate = jnp.where(n_iota < qb, gate, NEG_BIG)
        rank = jnp.zeros(gate.shape, F32)
        for n in range(n_blocks):
            row = gate[n:n + 1, :]
            beats = (row > gate) | ((row == gate) & (n < n_iota))
            rank = rank + jnp.where(beats, 1.0, 0.0)
        sel_ref[h] = jnp.where((rank < n_sel) & (n_iota < qb), 0.0, NEG_BIG)
        m_ref[h] = jnp.full((1, blk), -jnp.inf, F32)
        l_ref[h] = jnp.zeros((1, blk), F32)
        acc_ref[h] = jnp.zeros((dh, blk), F32)

    def attend(r0, nk, mask_fn):
        ss = [_dot_nt(k_ref[pl.ds(r0, nk), h * dh:(h + 1) * dh], q_h[h]) for h in range(nh)]
        ps, alphas = [], []
        for h in range(nh):
            s = ss[h] * scale2 + mask_fn(h)
            m_old = m_ref[h]
            m_new = jnp.maximum(m_old, _reduce_rows(s, jnp.max))
            alphas.append(jnp.exp2(m_old - m_new))
            p = jnp.exp2(s - m_new)
            l_ref[h] = alphas[h] * l_ref[h] + _reduce_rows(p, jnp.sum)
            m_ref[h] = m_new
            ps.append(p.astype(BF16))
        for h in range(nh):
            acc_ref[h] = alphas[h] * acc_ref[h] + _dot(vt_ref[h * dh:(h + 1) * dh, pl.ds(r0, nk)], ps[h])

    def past_mask(h, n):
        return bias_ref[h, jnp.minimum(qb - n, n_bias - 1)] + sel_ref[h, pl.ds(n, 1), :]

    causal = jnp.where(lax.broadcasted_iota(jnp.int32, (blk, blk), 0) <= lax.broadcasted_iota(jnp.int32, (blk, blk), 1),
                       0.0, NEG_BIG)

    def own_mask(h):
        return bias_ref[h, 0] + causal

    group = MOBA_STEP_BLOCKS

    def past_group(i, carry):
        n = group * i
        attend(pl.multiple_of(n * blk, group * blk), group * blk,
               lambda h: jnp.concatenate([past_mask(h, n + u) for u in range(group)], axis=0))
        return carry

    lax.fori_loop(0, qb // group, past_group, 0)

    for k in range(group):
        @pl.when(qb % group == k)
        def _(k=k):
            attend(pl.multiple_of((qb - k) * blk, blk), (k + 1) * blk,
                   lambda h: jnp.concatenate([past_mask(h, qb - k + u) for u in range(k)] + [own_mask(h)], axis=0))

    for h in range(nh):
        o_ref[:, h * dh:(h + 1) * dh] = (acc_ref[h] / l_ref[h]).T.astype(o_ref.dtype)


def _moba(proj_b, bias_tiles, batch, seq):
    blk = MOBA_BLOCK
    nb = seq // blk
    n_sel = min(MOBA_TOPK, nb - 1)
    n_bias = bias_tiles.shape[1]
    return pl.pallas_call(
        functools.partial(_moba_kernel, n_blocks=nb, n_sel=n_sel, n_bias=n_bias),
        out_shape=jax.ShapeDtypeStruct((batch * seq, HEAD_W), BF16),
        grid=(batch, nb),
        in_specs=[
            pl.BlockSpec((blk, HEAD_W), lambda b, i: (b * nb + i, BCOL_MQ // HEAD_W)),
            pl.BlockSpec((seq, HEAD_W), lambda b, i: (b, BCOL_MK // HEAD_W)),
            pl.BlockSpec((seq, HEAD_W), lambda b, i: (b, BCOL_MV // HEAD_W)),
            pl.BlockSpec(bias_tiles.shape, lambda b, i: (0, 0, 0, 0)),
        ],
        out_specs=pl.BlockSpec((blk, HEAD_W), lambda b, i: (b * nb + i, 0)),
        scratch_shapes=[
            pltpu.VMEM((HEAD_W, seq), BF16),
            pltpu.VMEM((MOBA_HEADS, nb, MOBA_DH), F32),
            pltpu.VMEM((MOBA_HEADS, nb, blk), F32),
            pltpu.VMEM((MOBA_HEADS, 1, blk), F32),
            pltpu.VMEM((MOBA_HEADS, 1, blk), F32),
            pltpu.VMEM((MOBA_HEADS, MOBA_DH, blk), F32),
        ],
        compiler_params=_cparams(("parallel", "arbitrary")),
        name="moba",
    )(proj_b, proj_b, proj_b, bias_tiles)


def _sort_key(x):
    bits = pltpu.bitcast(x, jnp.int32)
    return bits ^ ((bits >> 31) & 0x7FFFFFFF)


def _dsa_kernel(dq_ref, iq_ref, iwq_ref, dkv_ref, ikw_ref, kvg_ref, lnw_ref, lnb_ref, wuk_ref, wuv_ref, bias_ref,
                o_ref, ckv_ref, ckvt_ref, ikn_ref, key_ref, mask_ref, tau_ref, cge_ref, m_ref, l_ref, acc_ref,
                *, seq, n_keep, n_bias):
    tq, kc, nh, dh = DSA_QB, DSA_KC, DSA_HEADS, DSA_DH
    sub = kc // LANES
    qb = pl.program_id(1)
    t0 = qb * tq
    n_kc = (t0 + tq - 1) // kc + 1
    n_rest = (seq - n_kc * kc).astype(F32)

    @pl.when(qb == 0)
    def _():
        for c in range(seq // kc):
            rows = slice(c * kc, (c + 1) * kc)
            x = dkv_ref[rows, :]
            cn = x * lax.rsqrt(jnp.mean(x * x, axis=-1, keepdims=True) + RMS_EPS) * kvg_ref[...]
            ckv_ref[rows, :] = cn.astype(BF16)
            ckvt_ref[:, rows] = cn.T.astype(BF16)
            ik = ikw_ref[rows, 0:IDX_DIM]
            xc = ik - jnp.mean(ik, axis=-1, keepdims=True)
            ikn = xc * lax.rsqrt(jnp.mean(xc * xc, axis=-1, keepdims=True) + RMS_EPS) * lnw_ref[...] + lnb_ref[...]
            ikn_ref[rows, :] = ikn.astype(BF16)

    t_row = t0 + lax.broadcasted_iota(jnp.int32, (1, tq), 1)
    key_off = lax.broadcasted_iota(jnp.int32, (kc, tq), 0)

    def causal_of(c0):
        return (c0 + key_off) <= t_row

    iq_t = iq_ref[...].astype(F32).T
    iq_cat = jnp.concatenate([iq_t[h * IDX_DIM:(h + 1) * IDX_DIM, :] for h in range(IDX_HEADS)], axis=1).astype(BF16)
    iw_t = iwq_ref[...].T[IDX_DIM:IDX_DIM + IDX_HEADS, :] * (IDX_HEADS ** -0.5)
    iw_t = iw_t * (IDX_DIM ** -0.5)

    hpg = 2 * LANES // tq

    def score_chunks(c, nc):
        c0s = [pl.multiple_of((c + u) * kc, kc) for u in range(nc)]
        s_grp = [[_dot(ikn_ref[pl.ds(c0, kc), :], iq_cat[:, g * hpg * tq:(g + 1) * hpg * tq])
                  for g in range(IDX_HEADS // hpg)] for c0 in c0s]
        for u, c0 in enumerate(c0s):
            sc = jnp.zeros((kc, tq), F32)
            for h in range(IDX_HEADS):
                s_h = s_grp[u][h // hpg][:, (h % hpg) * tq:(h % hpg + 1) * tq]
                sc = sc + iw_t[h:h + 1, :] * jnp.maximum(s_h, 0.0)
            sc = jnp.where(causal_of(c0), sc, NEG_BIG)
            sc = jnp.where(sc == 0.0, 0.0, sc)
            key_ref[pl.ds(c0, kc), :] = _sort_key(sc)

    def score_pair(i, carry):
        score_chunks(2 * i, 2)
        return carry

    lax.fori_loop(0, n_kc // 2, score_pair, 0)

    @pl.when(n_kc % 2 == 1)
    def _():
        score_chunks(n_kc - 1, 1)

    def count(pred_fn):
        def body(c, tot):
            c0 = pl.multiple_of(c * kc, kc)
            hit = pred_fn(key_ref[pl.ds(c0, kc), :], c0)
            return tot + _slab(jnp.where(hit, 1.0, 0.0), jnp.sum)
        part = lax.fori_loop(0, n_kc, body, jnp.zeros((ROW_SLAB, tq), F32))
        return _reduce_rows(part, jnp.sum)

    def bit_step(k, carry):
        base, c_base = carry
        cand = base + jnp.left_shift(jnp.int32(1), 31 - k)
        cnt = count(lambda keys, c0: keys >= cand) + jnp.where(cand <= KEY_NEG, n_rest, 0.0)
        take = cnt >= n_keep
        return jnp.where(take, cand, base), jnp.where(take, cnt, c_base)

    tau, c_ge = lax.fori_loop(0, DSA_SURE_BITS, bit_step, (jnp.full((1, tq), jnp.iinfo(jnp.int32).min, jnp.int32),
                                                           jnp.full((1, tq), float(seq), F32)))
    tau_ref[...] = tau
    cge_ref[...] = c_ge

    @pl.when(jnp.max(jnp.abs(c_ge - n_keep)) > 0.0)
    def _():
        tau_ref[...], cge_ref[...] = lax.fori_loop(DSA_SURE_BITS, 32, bit_step, (tau, c_ge))

    tau = tau_ref[...]
    c_ge = cge_ref[...]

    c_gt = count(lambda keys, c0: keys > tau) + jnp.where(tau < KEY_NEG, n_rest, 0.0)
    need = n_keep - c_gt
    n_eq = (c_ge - c_gt) - jnp.where(tau == KEY_NEG, (seq - 1 - t_row).astype(F32), 0.0)
    tie_break = jnp.max(n_eq - need) > 0.0

    @pl.when(jnp.logical_not(tie_break))
    def _():
        def body(c, carry):
            c0 = pl.multiple_of(c * kc, kc)
            keep = (key_ref[pl.ds(c0, kc), :] >= tau) & causal_of(c0)
            mask_ref[pl.ds(c0, kc), :] = jnp.where(keep, 0.0, NEG_BIG)
            return carry
        lax.fori_loop(0, n_kc, body, 0)

    @pl.when(tie_break)
    def _():
        lower = jnp.where(lax.broadcasted_iota(jnp.int32, (kc, kc), 1)
                          < lax.broadcasted_iota(jnp.int32, (kc, kc), 0), 1.0, 0.0).astype(BF16)

        def body(c, seen):
            c0 = pl.multiple_of(c * kc, kc)
            keys = key_ref[pl.ds(c0, kc), :]
            eq = keys == tau
            eqf = jnp.where(eq, 1.0, 0.0)
            rank = seen + _dot(lower, eqf.astype(BF16))
            keep = ((keys > tau) | (eq & (rank < need))) & causal_of(c0)
            mask_ref[pl.ds(c0, kc), :] = jnp.where(keep, 0.0, NEG_BIG)
            return seen + _reduce_rows(eqf, jnp.sum)
        lax.fori_loop(0, n_kc, body, jnp.zeros((1, tq), F32))

    scale2 = dh ** -0.5 * LOG2E
    qa_cat = jnp.concatenate(
        [_dot_nt(wuk_ref[h], dq_ref[:, h * dh:(h + 1) * dh]) for h in range(nh)], axis=1).astype(BF16)
    m_ref[...] = jnp.full(m_ref.shape, -jnp.inf, F32)
    l_ref[...] = jnp.zeros(l_ref.shape, F32)
    acc_ref[...] = jnp.zeros(acc_ref.shape, F32)

    def attend(c, nc):
        c0 = pl.multiple_of(c * kc, kc)
        nk = nc * kc
        msk = mask_ref[pl.ds(c0, nk), :]
        bias = jnp.concatenate(
            [bias_ref[jnp.clip(qb - (c * sub + g), 0, n_bias - 1)] for g in range(nc * sub)], axis=0)
        s = _dot(ckv_ref[pl.ds(c0, nk), :], qa_cat) * scale2 + bias + jnp.concatenate([msk] * nh, axis=1)
        m_old = m_ref[...]
        m_new = jnp.maximum(m_old, _reduce_rows(s, jnp.max))
        alpha = jnp.exp2(m_old - m_new)
        p = jnp.exp2(s - m_new)
        l_ref[...] = alpha * l_ref[...] + _reduce_rows(p, jnp.sum)
        acc_ref[...] = alpha * acc_ref[...] + _dot(ckvt_ref[:, pl.ds(c0, nk)], p.astype(BF16))
        m_ref[...] = m_new

    def att_pair(i, carry):
        attend(2 * i, 2)
        return carry

    lax.fori_loop(0, n_kc // 2, att_pair, 0)

    @pl.when(n_kc % 2 == 1)
    def _():
        attend(n_kc - 1, 1)

    o_lat_t = (acc_ref[...] / l_ref[...]).astype(BF16)
    for h in range(nh):
        o_ref[:, h * dh:(h + 1) * dh] = _dot(wuv_ref[h], o_lat_t[:, h * tq:(h + 1) * tq]).T.astype(o_ref.dtype)


def _dsa(proj_f, proj_b, kv_gain, ln_w, ln_b, w_uk_h, w_uv_h, bias_tiles, batch, seq):
    tq = DSA_QB
    nq = seq // tq
    n_keep = min(DSA_TOPK_MAX, seq // 4)
    n_bias = bias_tiles.shape[0]
    return pl.pallas_call(
        functools.partial(_dsa_kernel, seq=seq, n_keep=float(n_keep), n_bias=n_bias),
        out_shape=jax.ShapeDtypeStruct((batch * seq, HEAD_W), BF16),
        grid=(batch, nq),
        in_specs=[
            pl.BlockSpec((tq, HEAD_W), lambda b, i: (b * nq + i, BCOL_DQ // HEAD_W)),
            pl.BlockSpec((tq, HEAD_W), lambda b, i: (b * nq + i, BCOL_IQ // HEAD_W)),
            pl.BlockSpec((tq, LANES), lambda b, i: (b * nq + i, FCOL_IKW // LANES)),
            pl.BlockSpec((seq, DSA_KV_RANK), lambda b, i: (b, FCOL_DKV // DSA_KV_RANK)),
            pl.BlockSpec((seq, LANES), lambda b, i: (b, FCOL_IKW // LANES)),
            pl.BlockSpec((1, DSA_KV_RANK), lambda b, i: (0, 0)),
            pl.BlockSpec((1, IDX_DIM), lambda b, i: (0, 0)),
            pl.BlockSpec((1, IDX_DIM), lambda b, i: (0, 0)),
            pl.BlockSpec(w_uk_h.shape, lambda b, i: (0, 0, 0)),
            pl.BlockSpec(w_uv_h.shape, lambda b, i: (0, 0, 0)),
            pl.BlockSpec(bias_tiles.shape, lambda b, i: (0, 0, 0)),
        ],
        out_specs=pl.BlockSpec((tq, HEAD_W), lambda b, i: (b * nq + i, 0)),
        scratch_shapes=[
            pltpu.VMEM((seq, DSA_KV_RANK), BF16),
            pltpu.VMEM((DSA_KV_RANK, seq), BF16),
            pltpu.VMEM((seq, IDX_DIM), BF16),
            pltpu.VMEM((seq, tq), jnp.int32),
            pltpu.VMEM((seq, tq), F32),
            pltpu.VMEM((1, tq), jnp.int32),
            pltpu.VMEM((1, tq), F32),
            pltpu.VMEM((1, DSA_HEADS * tq), F32),
            pltpu.VMEM((1, DSA_HEADS * tq), F32),
            pltpu.VMEM((DSA_KV_RANK, DSA_HEADS * tq), F32),
        ],
        compiler_params=_cparams(("parallel", "arbitrary")),
        name="dsa",
    )(proj_b, proj_b, proj_f, proj_f, proj_f, kv_gain, ln_w, ln_b, w_uk_h, w_uv_h, bias_tiles)


def _merge_kernel(x_ref, ya_ref, yb_ref, yc_ref, g0_ref, g1_ref, g2_ref, wa_ref, wb_ref, wc_ref, wo_ref, gain_ref,
                  o_ref):
    merged = (_sigmoid(g0_ref[...]) * _dot(ya_ref[...], wa_ref[...])
              + _sigmoid(g1_ref[...]) * _dot(yb_ref[...], wb_ref[...])
              + _sigmoid(g2_ref[...]) * _dot(yc_ref[...], wc_ref[...]))
    z = _dot(merged.astype(BF16), wo_ref[...])
    zn = z * lax.rsqrt(jnp.mean(z * z, axis=-1, keepdims=True) + RMS_EPS) * gain_ref[...]
    o_ref[...] = x_ref[...] + zn


def _merge(x2, ya, yb, yc, proj_f, wa, wb, wc, wo, layer, gain, tm):
    n = x2.shape[0]
    gcol = FCOL_GATES // D_MODEL

    def rows(w):
        return pl.BlockSpec((tm, w), lambda i: (i, 0))

    def gate(k):
        return pl.BlockSpec((tm, D_MODEL), lambda i, k=k: (i, gcol + k))

    def weight(a):
        return pl.BlockSpec((None,) + a.shape[1:], lambda i: (layer, 0, 0))

    return pl.pallas_call(
        _merge_kernel,
        out_shape=jax.ShapeDtypeStruct((n, D_MODEL), F32),
        grid=(n // tm,),
        in_specs=[rows(D_MODEL), rows(HEAD_W), rows(HEAD_W), rows(HEAD_W), gate(0), gate(1), gate(2),
                  weight(wa), weight(wb), weight(wc), weight(wo), pl.BlockSpec(gain.shape, lambda i: (0, 0))],
        out_specs=rows(D_MODEL),
        compiler_params=_cparams(("parallel",)),
        name="merge",
    )(x2, ya, yb, yc, proj_f, proj_f, proj_f, wa, wb, wc, wo, gain)


def _ffn_kernel(x_ref, g1_ref, wg_ref, wu_ref, wd_ref, g2_ref, o_ref, hn_ref, acc_ref):
    j = pl.program_id(1)

    @pl.when(j == 0)
    def _():
        x = x_ref[...]
        hn_ref[...] = (x * lax.rsqrt(jnp.mean(x * x, axis=-1, keepdims=True) + RMS_EPS) * g1_ref[...]).astype(BF16)
        acc_ref[...] = jnp.zeros_like(acc_ref)

    hn = hn_ref[...]
    gate = _dot(hn, wg_ref[...])
    up = _dot(hn, wu_ref[...])
    act = (gate * _sigmoid(gate)) * up
    acc_ref[...] += _dot(act.astype(BF16), wd_ref[...])

    @pl.when(j == pl.num_programs(1) - 1)
    def _():
        y = acc_ref[...]
        yn = y * lax.rsqrt(jnp.mean(y * y, axis=-1, keepdims=True) + RMS_EPS) * g2_ref[...]
        o_ref[...] = x_ref[...] + yn


def _ffn(x2, g1, w_in, w_out, layer, g2, tm, tf):
    n = x2.shape[0]
    nf = D_FF // tf
    return pl.pallas_call(
        _ffn_kernel,
        out_shape=jax.ShapeDtypeStruct((n, D_MODEL), F32),
        grid=(n // tm, nf),
        in_specs=[
            pl.BlockSpec((tm, D_MODEL), lambda i, j: (i, 0)),
            pl.BlockSpec((1, D_MODEL), lambda i, j: (0, 0)),
            pl.BlockSpec((None, D_MODEL, tf), lambda i, j: (layer, 0, j)),
            pl.BlockSpec((None, D_MODEL, tf), lambda i, j: (layer, 0, nf + j)),
            pl.BlockSpec((None, tf, D_MODEL), lambda i, j: (layer, j, 0)),
            pl.BlockSpec((1, D_MODEL), lambda i, j: (0, 0)),
        ],
        out_specs=pl.BlockSpec((tm, D_MODEL), lambda i, j: (i, 0)),
        scratch_shapes=[pltpu.VMEM((tm, D_MODEL), BF16), pltpu.VMEM((tm, D_MODEL), F32)],
        compiler_params=_cparams(("parallel", "arbitrary")),
        name="ffn",
    )(x2, g1, w_in, w_in, w_out, g2)


def _pack_w_in(w):
    o_mq, o_dq, o_dkv, o_iq, o_ik, o_gates = 2048, 3584, 4096, 4352, 4864, 4936
    w = w.astype(BF16)
    z = lambda k: jnp.zeros(w.shape[:-1] + (k,), w.dtype)
    w_f = jnp.concatenate([
        w[..., :o_mq],
        w[..., o_gates:],
        w[..., o_dkv:o_iq],
        w[..., o_ik:o_gates],
        z(PROJ_F_W - FCOL_IKW - (o_gates - o_ik)),
    ], axis=-1)
    w_b = jnp.concatenate([
        w[..., o_mq:o_dq],
        w[..., o_dq:o_dkv],
        w[..., o_iq:o_ik],
    ], axis=-1)
    assert w_f.shape[-1] == PROJ_F_W and w_b.shape[-1] == PROJ_B_W
    return w_f, w_b


def _tiles(batch, seq):
    n = batch * seq
    return dict(
        proj_tm=min(2048, n), proj_f_tn=PROJ_F_W // 4, proj_b_tm=min(2048, n), proj_b_tn=PROJ_B_W // 2,
        hgrn_rows=min(512, seq),
        merge_tm=min(512, n),
        ffn_tm=min(1024, n), ffn_tf=D_FF // 2,
    )


def kernel(x, w_in, pre_mix_norm, post_mix_norm, pre_ffn_norm, post_ffn_norm, hg_lb_logits, hg_norm_w, dsa_kv_norm,
           dsa_w_uk, dsa_w_uv, idx_k_norm_w, idx_k_norm_b, rel_bias, w_branch_a, w_branch_b, w_branch_c, w_out,
           w_ffn_in, w_ffn_out):
    batch, seq, _ = x.shape
    depth = w_in.shape[0]
    t = _tiles(batch, seq)
    lbp = jax.nn.softmax(hg_lb_logits.astype(F32), axis=0)
    lower_bounds = jnp.cumsum(lbp, axis=0) - lbp[0]
    bias_b = _bias_tiles_t(rel_bias[:, :MOBA_HEADS], MOBA_BLOCK)
    bias_c = _bias_tiles_t(rel_bias[:, MOBA_HEADS:], LANES)
    bias_c = bias_c.transpose(1, 2, 0, 3).reshape(bias_c.shape[1], LANES, DSA_HEADS * LANES)
    row = lambda a: a.reshape(1, -1).astype(F32)

    w_f, w_b = _pack_w_in(w_in)
    w_uk = jnp.transpose(dsa_w_uk, (0, 2, 1, 3)).astype(BF16)
    w_uv = jnp.transpose(dsa_w_uv, (0, 2, 3, 1)).astype(BF16)
    w_a, w_bb, w_c, w_o = (a.astype(BF16) for a in (w_branch_a, w_branch_b, w_branch_c, w_out))
    w_fi, w_fo = w_ffn_in.astype(BF16), w_ffn_out.astype(BF16)

    x2 = x.reshape(batch * seq, D_MODEL)
    for l in range(depth):
        gain = row(pre_mix_norm[l])
        proj_f = _proj(x2, gain, w_f, l, F32, t["proj_tm"], t["proj_f_tn"], "proj_f")
        proj_b = _proj(x2, gain, w_b, l, BF16, t["proj_b_tm"], t["proj_b_tn"], "proj_b")
        y_a = _hgrn(proj_f, row(lower_bounds[l]), row(hg_norm_w[l]), batch, seq, t["hgrn_rows"])
        y_b = _moba(proj_b, bias_b, batch, seq)
        y_c = _dsa(proj_f, proj_b, row(dsa_kv_norm[l]), row(idx_k_norm_w[l]), row(idx_k_norm_b[l]),
                   w_uk[l], w_uv[l], bias_c, batch, seq)
        x2 = _merge(x2, y_a, y_b, y_c, proj_f, w_a, w_bb, w_c, w_o, l, row(post_mix_norm[l]), t["merge_tm"])
        x2 = _ffn(x2, row(pre_ffn_norm[l]), w_fi, w_fo, l, row(post_ffn_norm[l]), t["ffn_tm"], t["ffn_tf"])
    return x2.reshape(batch, seq, D_MODEL)
```

```python
import functools
import math

import numpy as np
import jax
import jax.numpy as jnp
from jax import lax
from jax.experimental import pallas as pl
from jax.experimental.pallas import tpu as pltpu

F32 = jnp.float32
BF16 = jnp.bfloat16

D_MODEL = 1024
RMS_EPS = 1e-6
LOG2E = 1.0 / math.log(2.0)
NEG_BIG = -1e30
TINY = 1e-20
HG_HEADS = 4
HG_D = 128
HG_CHUNK = 64
MOBA_HEADS = 4
MOBA_DH = 128
MOBA_BLOCK = 256
MOBA_TOPK = 3
MOBA_STEP_BLOCKS = 4
DSA_HEADS = 4
DSA_DH = 128
DSA_KV_RANK = 256
IDX_HEADS = 8
IDX_DIM = 64
DSA_TOPK_MAX = 256
DSA_KC = 512
DSA_SURE_BITS = 28
REL_BUCKETS = 32
REL_MAX_DIST = 2048
D_FF = 2816
HEAD_W = 512

LANES = 128
SUBLANES = 8
DSA_QB = LANES

FCOL_HQ, FCOL_HF, FCOL_HI, FCOL_HG = 0, 512, 1024, 1536
FCOL_GATES, FCOL_DKV, FCOL_IKW = 2048, 5120, 5376
PROJ_F_W = 5632
BCOL_MQ, BCOL_MK, BCOL_MV, BCOL_DQ, BCOL_IQ = 0, 512, 1024, 1536, 2048
PROJ_B_W = 2560

VMEM_LIMIT = 56 * 1024 * 1024

_REL_EXACT = REL_BUCKETS // 2
REL_FAR_DIST = int(math.ceil(_REL_EXACT * (REL_MAX_DIST / _REL_EXACT) ** ((REL_BUCKETS - _REL_EXACT - 1) / (REL_BUCKETS - _REL_EXACT)))) + 1

_NEG_BITS = int(np.array(NEG_BIG, np.float32).view(np.int32))
KEY_NEG = int(np.int32(_NEG_BITS ^ 0x7FFFFFFF))


def _cparams(sem):
    return pltpu.CompilerParams(dimension_semantics=sem, vmem_limit_bytes=VMEM_LIMIT)


def _sigmoid(x):
    return 1.0 / (1.0 + jnp.exp(-x))


def _dot(a, b):
    return jnp.dot(a, b, preferred_element_type=F32)


def _dot_nt(a, b):
    return lax.dot_general(a, b, (((1,), (1,)), ((), ())), preferred_element_type=F32)


def _t_bf16(a):
    return a.astype(F32).T.astype(BF16)


ROW_SLAB = 64


def _slab(x, fn):
    return fn(x.reshape(x.shape[0] // ROW_SLAB, ROW_SLAB, x.shape[1]), axis=0)


def _reduce_rows(x, fn):
    y = _slab(x, fn)
    z = fn(y.reshape(ROW_SLAB // SUBLANES, SUBLANES, y.shape[1]), axis=0)
    return fn(z, axis=0, keepdims=True)


def _proj_kernel(x_ref, g_ref, w_ref, o_ref, xn_ref):
    @pl.when(pl.program_id(1) == 0)
    def _():
        x = x_ref[...]
        ms = jnp.mean(x * x, axis=-1, keepdims=True)
        xn_ref[...] = (x * lax.rsqrt(ms + RMS_EPS) * g_ref[...]).astype(BF16)

    o_ref[...] = _dot(xn_ref[...], w_ref[...]).astype(o_ref.dtype)


def _proj(x2, gain, w_packed, layer, out_dtype, tm, tn, name):
    n = x2.shape[0]
    width = w_packed.shape[-1]
    return pl.pallas_call(
        _proj_kernel,
        out_shape=jax.ShapeDtypeStruct((n, width), out_dtype),
        grid=(n // tm, width // tn),
        in_specs=[
            pl.BlockSpec((tm, D_MODEL), lambda i, j: (i, 0)),
            pl.BlockSpec((1, D_MODEL), lambda i, j: (0, 0)),
            pl.BlockSpec((None, D_MODEL, tn), lambda i, j: (layer, 0, j)),
        ],
        out_specs=pl.BlockSpec((tm, tn), lambda i, j: (i, j)),
        scratch_shapes=[pltpu.VMEM((tm, D_MODEL), BF16)],
        compiler_params=_cparams(("parallel", "arbitrary")),
        name=name,
    )(x2, gain, w_packed)


HG_LEVELS = (32, 16, 8, 4, 2, 1)
HG_UNROLL = 8


def _hgrn_constants():
    c = HG_CHUNK
    t = np.arange(c)
    i = t[:, None]
    tt = t[None, :]
    rows = [tt <= i]
    masks = []
    for m in HG_LEVELS:
        mid = ((t // (2 * m)) * (2 * m) + m)[:, None]
        upper = (i >= mid) & (tt >= mid) & (tt <= i)
        lower = (i < mid) & (tt > i) & (tt <= mid - 1)
        rows.append(upper | lower)
        same = (t[:, None] // (2 * m)) == (t[None, :] // (2 * m))
        masks.append(same & (i >= mid) & (tt < mid))
    masks.append(np.eye(c, dtype=bool))
    m_all = np.concatenate(rows, axis=0).astype(np.float32)
    return m_all, np.stack(masks).astype(np.float32)


def _hgrn_kernel(q_ref, f_ref, i_ref, g_ref, lb_ref, nw_ref, mall_ref, mask_ref, o_ref, st_ref, e_ref, *, n_chunks):
    c = HG_CHUNK
    nl = len(HG_LEVELS)

    @pl.when(pl.program_id(1) == 0)
    def _():
        st_ref[...] = jnp.zeros_like(st_ref)

    lb = lb_ref[...]
    nw = nw_ref[...]

    heads = [slice(h * HG_D, (h + 1) * HG_D) for h in range(HG_HEADS)]

    def intra(r0, u):
        fp = f_ref[pl.ds(r0, c), :]
        f = lb + (1.0 - lb) * _sigmoid(fp)
        logf = jnp.log(jnp.maximum(f, TINY))
        kk = (1.0 - lb) * _sigmoid(-fp)
        g_hi = logf.astype(BF16)
        g_lo = (logf - g_hi.astype(F32)).astype(BF16)
        e_ref[u] = _dot(mall_ref[...], jnp.concatenate([g_hi, g_lo], axis=1))
        q = q_ref[pl.ds(r0, c), :]

        def expo(row):
            return e_ref[u, row * c:(row + 1) * c, 0:HEAD_W] + e_ref[u, row * c:(row + 1) * c, HEAD_W:2 * HEAD_W]

        b = expo(0)
        b_last = b[c - 1:c, :]
        q_in = (q * jnp.exp(b)).astype(BF16)
        k_out = (kk * jnp.exp(b_last - b)).astype(BF16)
        qls, kls = [], []
        for l in range(nl):
            decay = jnp.exp(expo(1 + l))
            qls.append((q * decay).astype(BF16))
            kls.append((kk * decay).astype(BF16))
        qls.append(q.astype(BF16))
        kls.append(kk.astype(BF16))
        sls = [[_dot_nt(qls[l][:, hs], kls[l][:, hs]) for l in range(nl + 1)] for hs in heads]
        ss = []
        for h in range(HG_HEADS):
            s = jnp.zeros((c, c), F32)
            for l in range(nl + 1):
                s = s + jnp.where(mask_ref[l] > 0.0, sls[h][l], 0.0)
            ss.append(s.astype(BF16))
        v = i_ref[pl.ds(r0, c), :].astype(BF16)
        o_intra = [_dot(ss[h], v[:, hs]) for h, hs in enumerate(heads)]
        kv = [_dot(v[:, hs].astype(F32).T.astype(BF16), k_out[:, hs]) for hs in heads]
        return q_in, jnp.exp(b_last), o_intra, kv

    def step(ci, carry):
        parts = [intra(pl.multiple_of((ci * HG_UNROLL + u) * c, c), u) for u in range(HG_UNROLL)]
        for u, (q_in, st_decay, o_intra, kv) in enumerate(parts):
            r0 = pl.multiple_of((ci * HG_UNROLL + u) * c, c)
            og = g_ref[pl.ds(r0, c), :]
            outs = [_dot_nt(q_in[:, hs], st_ref[h].astype(BF16)) + o_intra[h] for h, hs in enumerate(heads)]
            for h, hs in enumerate(heads):
                st_ref[h] = st_ref[h] * st_decay[:, hs] + kv[h]
            for h, hs in enumerate(heads):
                o = outs[h]
                y = o * lax.rsqrt(jnp.mean(o * o, axis=-1, keepdims=True) + RMS_EPS) * nw
                gh = og[:, hs]
                o_ref[pl.ds(r0, c), hs] = (y * (gh * _sigmoid(gh))).astype(o_ref.dtype)
        return carry

    lax.fori_loop(0, n_chunks // HG_UNROLL, step, 0)


def _hgrn(proj_f, lb, norm_w, batch, seq, rows):
    m_all, masks = _hgrn_constants()
    n_rb = seq // rows
    n_chunks = rows // HG_CHUNK
    assert n_chunks % HG_UNROLL == 0

    def col(c0):
        return pl.BlockSpec((rows, HEAD_W), lambda b, r, cidx=c0 // HEAD_W: (b * n_rb + r, cidx))

    return pl.pallas_call(
        functools.partial(_hgrn_kernel, n_chunks=n_chunks),
        out_shape=jax.ShapeDtypeStruct((batch * seq, HEAD_W), BF16),
        grid=(batch, n_rb),
        in_specs=[
            col(FCOL_HQ), col(FCOL_HF), col(FCOL_HI), col(FCOL_HG),
            pl.BlockSpec((1, HEAD_W), lambda b, r: (0, 0)),
            pl.BlockSpec((1, HG_D), lambda b, r: (0, 0)),
            pl.BlockSpec(m_all.shape, lambda b, r: (0, 0)),
            pl.BlockSpec(masks.shape, lambda b, r: (0, 0, 0)),
        ],
        out_specs=pl.BlockSpec((rows, HEAD_W), lambda b, r: (b * n_rb + r, 0)),
        scratch_shapes=[
            pltpu.VMEM((HG_HEADS, HG_D, HG_D), F32),
            pltpu.VMEM((HG_UNROLL, m_all.shape[0], 2 * HEAD_W), F32),
        ],
        compiler_params=_cparams(("parallel", "arbitrary")),
        name="hgrn",
    )(proj_f, proj_f, proj_f, proj_f, lb, norm_w, jnp.asarray(m_all, BF16), jnp.asarray(masks, F32))


def _rel_bucket(dist):
    n = jnp.maximum(dist, 0)
    exact = REL_BUCKETS // 2
    nf = jnp.maximum(n, exact).astype(F32)
    large = exact + (jnp.log(nf / exact) / math.log(REL_MAX_DIST / exact) * (REL_BUCKETS - exact)).astype(jnp.int32)
    large = jnp.minimum(large, REL_BUCKETS - 1)
    return jnp.where(n < exact, n, large)


def _n_bias_tiles(tile):
    d = 0
    while d * tile - (tile - 1) < REL_FAR_DIST:
        d += 1
    return d + 1


def _bias_tiles_t(table, tile):
    n = _n_bias_tiles(tile)
    length = n * tile
    by_dist = (table[_rel_bucket(jnp.arange(length, dtype=jnp.int32))].astype(F32) * LOG2E).T
    heads = by_dist.shape[0]
    w = jnp.concatenate([by_dist, jnp.zeros((heads, 1), F32)], axis=1)
    skew = jnp.broadcast_to(w[:, None, :], (heads, tile, length + 1)).reshape(heads, tile * (length + 1))
    skew = skew[:, :tile * length].reshape(heads, tile, length)
    return skew.reshape(heads, tile, n, tile).transpose(0, 2, 1, 3)


def _moba_kernel(q_ref, k_ref, v_ref, bias_ref, o_ref, vt_ref, km_ref, sel_ref, m_ref, l_ref, acc_ref,
                 *, n_blocks, n_sel, n_bias):
    blk, dh, nh = MOBA_BLOCK, MOBA_DH, MOBA_HEADS
    qb = pl.program_id(1)
    scale2 = dh ** -0.5 * LOG2E

    @pl.when(qb == 0)
    def _():
        for h in range(nh):
            for n in range(n_blocks):
                kblk = k_ref[n * blk:(n + 1) * blk, h * dh:(h + 1) * dh].astype(F32)
                km_ref[h, n:n + 1, :] = jnp.mean(kblk, axis=0, keepdims=True)
                vt_ref[h * dh:(h + 1) * dh, n * blk:(n + 1) * blk] = _t_bf16(v_ref[n * blk:(n + 1) * blk, h * dh:(h + 1) * dh])

    q_t = []
    n_iota = lax.broadcasted_iota(jnp.int32, (n_blocks, blk), 0)
    for h in range(nh):
        qt = _t_bf16(q_ref[:, h * dh:(h + 1) * dh])
        q_t.append(qt)
        gate = _dot(km_ref[h].astype(BF16), qt)
        gate = jnp.where(n_iota < qb, gate, NEG_BIG)
        rank = jnp.zeros(gate.shape, F32)
        for n in range(n_blocks):
            row = gate[n:n + 1, :]
            beats = (row > gate) | ((row == gate) & (n < n_iota))
            rank = rank + jnp.where(beats, 1.0, 0.0)
        sel_ref[h] = jnp.where((rank < n_sel) & (n_iota < qb), 0.0, NEG_BIG)
        m_ref[h] = jnp.full((1, blk), -jnp.inf, F32)
        l_ref[h] = jnp.zeros((1, blk), F32)
        acc_ref[h] = jnp.zeros((dh, blk), F32)

    def attend(r0, nk, mask_fn):
        ss = [_dot(k_ref[pl.ds(r0, nk), h * dh:(h + 1) * dh], q_t[h]) for h in range(nh)]
        ps, alphas = [], []
        for h in range(nh):
            s = ss[h] * scale2 + mask_fn(h)
            m_old = m_ref[h]
            m_new = jnp.maximum(m_old, _reduce_rows(s, jnp.max))
            alphas.append(jnp.exp2(m_old - m_new))
            p = jnp.exp2(s - m_new)
            l_ref[h] = alphas[h] * l_ref[h] + _reduce_rows(p, jnp.sum)
            m_ref[h] = m_new
            ps.append(p.astype(BF16))
        for h in range(nh):
            acc_ref[h] = alphas[h] * acc_ref[h] + _dot(vt_ref[h * dh:(h + 1) * dh, pl.ds(r0, nk)], ps[h])

    def past_mask(h, n):
        return bias_ref[h, jnp.minimum(qb - n, n_bias - 1)] + sel_ref[h, pl.ds(n, 1), :]

    causal = jnp.where(lax.broadcasted_iota(jnp.int32, (blk, blk), 0) <= lax.broadcasted_iota(jnp.int32, (blk, blk), 1),
                       0.0, NEG_BIG)

    def own_mask(h):
        return bias_ref[h, 0] + causal

    group = MOBA_STEP_BLOCKS

    def past_group(i, carry):
        n = group * i
        attend(pl.multiple_of(n * blk, group * blk), group * blk,
               lambda h: jnp.concatenate([past_mask(h, n + u) for u in range(group)], axis=0))
        return carry

    lax.fori_loop(0, qb // group, past_group, 0)

    for k in range(group):
        @pl.when(qb % group == k)
        def _(k=k):
            attend(pl.multiple_of((qb - k) * blk, blk), (k + 1) * blk,
                   lambda h: jnp.concatenate([past_mask(h, qb - k + u) for u in range(k)] + [own_mask(h)], axis=0))

    for h in range(nh):
        o_ref[:, h * dh:(h + 1) * dh] = (acc_ref[h] / l_ref[h]).T.astype(o_ref.dtype)


def _moba(proj_b, bias_tiles, batch, seq):
    blk = MOBA_BLOCK
    nb = seq // blk
    n_sel = min(MOBA_TOPK, nb - 1)
    n_bias = bias_tiles.shape[1]
    return pl.pallas_call(
        functools.partial(_moba_kernel, n_blocks=nb, n_sel=n_sel, n_bias=n_bias),
        out_shape=jax.ShapeDtypeStruct((batch * seq, HEAD_W), BF16),
        grid=(batch, nb),
        in_specs=[
            pl.BlockSpec((blk, HEAD_W), lambda b, i: (b * nb + i, BCOL_MQ // HEAD_W)),
            pl.BlockSpec((seq, HEAD_W), lambda b, i: (b, BCOL_MK // HEAD_W)),
            pl.BlockSpec((seq, HEAD_W), lambda b, i: (b, BCOL_MV // HEAD_W)),
            pl.BlockSpec(bias_tiles.shape, lambda b, i: (0, 0, 0, 0)),
        ],
        out_specs=pl.BlockSpec((blk, HEAD_W), lambda b, i: (b * nb + i, 0)),
        scratch_shapes=[
            pltpu.VMEM((HEAD_W, seq), BF16),
            pltpu.VMEM((MOBA_HEADS, nb, MOBA_DH), F32),
            pltpu.VMEM((MOBA_HEADS, nb, blk), F32),
            pltpu.VMEM((MOBA_HEADS, 1, blk), F32),
            pltpu.VMEM((MOBA_HEADS, 1, blk), F32),
            pltpu.VMEM((MOBA_HEADS, MOBA_DH, blk), F32),
        ],
        compiler_params=_cparams(("parallel", "arbitrary")),
        name="moba",
    )(proj_b, proj_b, proj_b, bias_tiles)


def _sort_key(x):
    bits = pltpu.bitcast(x, jnp.int32)
    return bits ^ ((bits >> 31) & 0x7FFFFFFF)


def _dsa_kernel(dq_ref, iq_ref, iwq_ref, dkv_ref, ikw_ref, kvg_ref, lnw_ref, lnb_ref, wuk_ref, wuv_ref, bias_ref,
                o_ref, ckv_ref, ckvt_ref, ikn_ref, key_ref, mask_ref, tau_ref, cge_ref, m_ref, l_ref, acc_ref,
                *, seq, n_keep, n_bias):
    tq, kc, nh, dh = DSA_QB, DSA_KC, DSA_HEADS, DSA_DH
    sub = kc // LANES
    qb = pl.program_id(1)
    t0 = qb * tq
    n_kc = (t0 + tq - 1) // kc + 1
    n_rest = (seq - n_kc * kc).astype(F32)

    @pl.when(qb == 0)
    def _():
        for c in range(seq // kc):
            rows = slice(c * kc, (c + 1) * kc)
            x = dkv_ref[rows, :]
            cn = x * lax.rsqrt(jnp.mean(x * x, axis=-1, keepdims=True) + RMS_EPS) * kvg_ref[...]
            ckv_ref[rows, :] = cn.astype(BF16)
            ckvt_ref[:, rows] = cn.T.astype(BF16)
            ik = ikw_ref[rows, 0:IDX_DIM]
            xc = ik - jnp.mean(ik, axis=-1, keepdims=True)
            ikn = xc * lax.rsqrt(jnp.mean(xc * xc, axis=-1, keepdims=True) + RMS_EPS) * lnw_ref[...] + lnb_ref[...]
            ikn_ref[rows, :] = ikn.astype(BF16)

    t_row = t0 + lax.broadcasted_iota(jnp.int32, (1, tq), 1)
    key_off = lax.broadcasted_iota(jnp.int32, (kc, tq), 0)

    def causal_of(c0):
        return (c0 + key_off) <= t_row

    iq_t = iq_ref[...].astype(F32).T
    iq_cat = jnp.concatenate([iq_t[h * IDX_DIM:(h + 1) * IDX_DIM, :] for h in range(IDX_HEADS)], axis=1).astype(BF16)
    iw_t = iwq_ref[...].T[IDX_DIM:IDX_DIM + IDX_HEADS, :] * (IDX_HEADS ** -0.5)
    iw_t = iw_t * (IDX_DIM ** -0.5)

    hpg = 2 * LANES // tq

    def score_chunks(c, nc):
        c0s = [pl.multiple_of((c + u) * kc, kc) for u in range(nc)]
        s_grp = [[_dot(ikn_ref[pl.ds(c0, kc), :], iq_cat[:, g * hpg * tq:(g + 1) * hpg * tq])
                  for g in range(IDX_HEADS // hpg)] for c0 in c0s]
        for u, c0 in enumerate(c0s):
            sc = jnp.zeros((kc, tq), F32)
            for h in range(IDX_HEADS):
                s_h = s_grp[u][h // hpg][:, (h % hpg) * tq:(h % hpg + 1) * tq]
                sc = sc + iw_t[h:h + 1, :] * jnp.maximum(s_h, 0.0)
            sc = jnp.where(causal_of(c0), sc, NEG_BIG)
            sc = jnp.where(sc == 0.0, 0.0, sc)
            key_ref[pl.ds(c0, kc), :] = _sort_key(sc)

    def score_pair(i, carry):
        score_chunks(2 * i, 2)
        return carry

    lax.fori_loop(0, n_kc // 2, score_pair, 0)

    @pl.when(n_kc % 2 == 1)
    def _():
        score_chunks(n_kc - 1, 1)

    def count(pred_fn):
        def body(c, tot):
            c0 = pl.multiple_of(c * kc, kc)
            hit = pred_fn(key_ref[pl.ds(c0, kc), :], c0)
            return tot + _slab(jnp.where(hit, 1.0, 0.0), jnp.sum)
        part = lax.fori_loop(0, n_kc, body, jnp.zeros((ROW_SLAB, tq), F32))
        return _reduce_rows(part, jnp.sum)

    def bit_step(k, carry):
        base, c_base = carry
        cand = base + jnp.left_shift(jnp.int32(1), 31 - k)
        cnt = count(lambda keys, c0: keys >= cand) + jnp.where(cand <= KEY_NEG, n_rest, 0.0)
        take = cnt >= n_keep
        return jnp.where(take, cand, base), jnp.where(take, cnt, c_base)

    tau, c_ge = lax.fori_loop(0, DSA_SURE_BITS, bit_step, (jnp.full((1, tq), jnp.iinfo(jnp.int32).min, jnp.int32),
                                                           jnp.full((1, tq), float(seq), F32)))
    tau_ref[...] = tau
    cge_ref[...] = c_ge

    @pl.when(jnp.max(jnp.abs(c_ge - n_keep)) > 0.0)
    def _():
        tau_ref[...], cge_ref[...] = lax.fori_loop(DSA_SURE_BITS, 32, bit_step, (tau, c_ge))

    tau = tau_ref[...]
    c_ge = cge_ref[...]

    c_gt = count(lambda keys, c0: keys > tau) + jnp.where(tau < KEY_NEG, n_rest, 0.0)
    need = n_keep - c_gt
    n_eq = (c_ge - c_gt) - jnp.where(tau == KEY_NEG, (seq - 1 - t_row).astype(F32), 0.0)
    tie_break = jnp.max(n_eq - need) > 0.0

    @pl.when(jnp.logical_not(tie_break))
    def _():
        def body(c, carry):
            c0 = pl.multiple_of(c * kc, kc)
            keep = (key_ref[pl.ds(c0, kc), :] >= tau) & causal_of(c0)
            mask_ref[pl.ds(c0, kc), :] = jnp.where(keep, 0.0, NEG_BIG)
            return carry
        lax.fori_loop(0, n_kc, body, 0)

    @pl.when(tie_break)
    def _():
        lower = jnp.where(lax.broadcasted_iota(jnp.int32, (kc, kc), 1)
                          < lax.broadcasted_iota(jnp.int32, (kc, kc), 0), 1.0, 0.0).astype(BF16)

        def body(c, seen):
            c0 = pl.multiple_of(c * kc, kc)
            keys = key_ref[pl.ds(c0, kc), :]
            eq = keys == tau
            eqf = jnp.where(eq, 1.0, 0.0)
            rank = seen + _dot(lower, eqf.astype(BF16))
            keep = ((keys > tau) | (eq & (rank < need))) & causal_of(c0)
            mask_ref[pl.ds(c0, kc), :] = jnp.where(keep, 0.0, NEG_BIG)
            return seen + _reduce_rows(eqf, jnp.sum)
        lax.fori_loop(0, n_kc, body, jnp.zeros((1, tq), F32))

    scale2 = dh ** -0.5 * LOG2E
    qa_cat = jnp.concatenate(
        [_dot_nt(wuk_ref[h], dq_ref[:, h * dh:(h + 1) * dh]) for h in range(nh)], axis=1).astype(BF16)
    m_ref[...] = jnp.full(m_ref.shape, -jnp.inf, F32)
    l_ref[...] = jnp.zeros(l_ref.shape, F32)
    acc_ref[...] = jnp.zeros(acc_ref.shape, F32)

    def attend(c, nc):
        c0 = pl.multiple_of(c * kc, kc)
        nk = nc * kc
        msk = mask_ref[pl.ds(c0, nk), :]
        bias = jnp.concatenate(
            [bias_ref[jnp.clip(qb - (c * sub + g), 0, n_bias - 1)] for g in range(nc * sub)], axis=0)
        s = _dot(ckv_ref[pl.ds(c0, nk), :], qa_cat) * scale2 + bias + jnp.concatenate([msk] * nh, axis=1)
        m_old = m_ref[...]
        m_new = jnp.maximum(m_old, _reduce_rows(s, jnp.max))
        alpha = jnp.exp2(m_old - m_new)
        p = jnp.exp2(s - m_new)
        l_ref[...] = alpha * l_ref[...] + _reduce_rows(p, jnp.sum)
        acc_ref[...] = alpha * acc_ref[...] + _dot(ckvt_ref[:, pl.ds(c0, nk)], p.astype(BF16))
        m_ref[...] = m_new

    def att_pair(i, carry):
        attend(2 * i, 2)
        return carry

    lax.fori_loop(0, n_kc // 2, att_pair, 0)

    @pl.when(n_kc % 2 == 1)
    def _():
        attend(n_kc - 1, 1)

    o_lat_t = (acc_ref[...] / l_ref[...]).astype(BF16)
    for h in range(nh):
        o_ref[:, h * dh:(h + 1) * dh] = _dot(wuv_ref[h], o_lat_t[:, h * tq:(h + 1) * tq]).T.astype(o_ref.dtype)


def _dsa(proj_f, proj_b, kv_gain, ln_w, ln_b, w_uk_h, w_uv_h, bias_tiles, batch, seq):
    tq = DSA_QB
    nq = seq // tq
    n_keep = min(DSA_TOPK_MAX, seq // 4)
    n_bias = bias_tiles.shape[0]
    return pl.pallas_call(
        functools.partial(_dsa_kernel, seq=seq, n_keep=float(n_keep), n_bias=n_bias),
        out_shape=jax.ShapeDtypeStruct((batch * seq, HEAD_W), BF16),
        grid=(batch, nq),
        in_specs=[
            pl.BlockSpec((tq, HEAD_W), lambda b, i: (b * nq + i, BCOL_DQ // HEAD_W)),
            pl.BlockSpec((tq, HEAD_W), lambda b, i: (b * nq + i, BCOL_IQ // HEAD_W)),
            pl.BlockSpec((tq, LANES), lambda b, i: (b * nq + i, FCOL_IKW // LANES)),
            pl.BlockSpec((seq, DSA_KV_RANK), lambda b, i: (b, FCOL_DKV // DSA_KV_RANK)),
            pl.BlockSpec((seq, LANES), lambda b, i: (b, FCOL_IKW // LANES)),
            pl.BlockSpec((1, DSA_KV_RANK), lambda b, i: (0, 0)),
            pl.BlockSpec((1, IDX_DIM), lambda b, i: (0, 0)),
            pl.BlockSpec((1, IDX_DIM), lambda b, i: (0, 0)),
            pl.BlockSpec(w_uk_h.shape, lambda b, i: (0, 0, 0)),
            pl.BlockSpec(w_uv_h.shape, lambda b, i: (0, 0, 0)),
            pl.BlockSpec(bias_tiles.shape, lambda b, i: (0, 0, 0)),
        ],
        out_specs=pl.BlockSpec((tq, HEAD_W), lambda b, i: (b * nq + i, 0)),
        scratch_shapes=[
            pltpu.VMEM((seq, DSA_KV_RANK), BF16),
            pltpu.VMEM((DSA_KV_RANK, seq), BF16),
            pltpu.VMEM((seq, IDX_DIM), BF16),
            pltpu.VMEM((seq, tq), jnp.int32),
            pltpu.VMEM((seq, tq), F32),
            pltpu.VMEM((1, tq), jnp.int32),
            pltpu.VMEM((1, tq), F32),
            pltpu.VMEM((1, DSA_HEADS * tq), F32),
            pltpu.VMEM((1, DSA_HEADS * tq), F32),
            pltpu.VMEM((DSA_KV_RANK, DSA_HEADS * tq), F32),
        ],
        compiler_params=_cparams(("parallel", "arbitrary")),
        name="dsa",
    )(proj_b, proj_b, proj_f, proj_f, proj_f, kv_gain, ln_w, ln_b, w_uk_h, w_uv_h, bias_tiles)


def _merge_kernel(x_ref, ya_ref, yb_ref, yc_ref, g0_ref, g1_ref, g2_ref, wa_ref, wb_ref, wc_ref, wo_ref, gain_ref,
                  o_ref):
    merged = (_sigmoid(g0_ref[...]) * _dot(ya_ref[...], wa_ref[...])
              + _sigmoid(g1_ref[...]) * _dot(yb_ref[...], wb_ref[...])
              + _sigmoid(g2_ref[...]) * _dot(yc_ref[...], wc_ref[...]))
    z = _dot(merged.astype(BF16), wo_ref[...])
    zn = z * lax.rsqrt(jnp.mean(z * z, axis=-1, keepdims=True) + RMS_EPS) * gain_ref[...]
    o_ref[...] = x_ref[...] + zn


def _merge(x2, ya, yb, yc, proj_f, wa, wb, wc, wo, layer, gain, tm):
    n = x2.shape[0]
    gcol = FCOL_GATES // D_MODEL

    def rows(w):
        return pl.BlockSpec((tm, w), lambda i: (i, 0))

    def gate(k):
        return pl.BlockSpec((tm, D_MODEL), lambda i, k=k: (i, gcol + k))

    def weight(a):
        return pl.BlockSpec((None,) + a.shape[1:], lambda i: (layer, 0, 0))

    return pl.pallas_call(
        _merge_kernel,
        out_shape=jax.ShapeDtypeStruct((n, D_MODEL), F32),
        grid=(n // tm,),
        in_specs=[rows(D_MODEL), rows(HEAD_W), rows(HEAD_W), rows(HEAD_W), gate(0), gate(1), gate(2),
                  weight(wa), weight(wb), weight(wc), weight(wo), pl.BlockSpec(gain.shape, lambda i: (0, 0))],
        out_specs=rows(D_MODEL),
        compiler_params=_cparams(("parallel",)),
        name="merge",
    )(x2, ya, yb, yc, proj_f, proj_f, proj_f, wa, wb, wc, wo, gain)


def _ffn_kernel(x_ref, g1_ref, wg_ref, wu_ref, wd_ref, g2_ref, o_ref, hn_ref, acc_ref):
    j = pl.program_id(1)

    @pl.when(j == 0)
    def _():
        x = x_ref[...]
        hn_ref[...] = (x * lax.rsqrt(jnp.mean(x * x, axis=-1, keepdims=True) + RMS_EPS) * g1_ref[...]).astype(BF16)
        acc_ref[...] = jnp.zeros_like(acc_ref)

    hn = hn_ref[...]
    gate = _dot(hn, wg_ref[...])
    up = _dot(hn, wu_ref[...])
    act = (gate * _sigmoid(gate)) * up
    acc_ref[...] += _dot(act.astype(BF16), wd_ref[...])

    @pl.when(j == pl.num_programs(1) - 1)
    def _():
        y = acc_ref[...]
        yn = y * lax.rsqrt(jnp.mean(y * y, axis=-1, keepdims=True) + RMS_EPS) * g2_ref[...]
        o_ref[...] = x_ref[...] + yn


def _ffn(x2, g1, w_in, w_out, layer, g2, tm, tf):
    n = x2.shape[0]
    nf = D_FF // tf
    return pl.pallas_call(
        _ffn_kernel,
        out_shape=jax.ShapeDtypeStruct((n, D_MODEL), F32),
        grid=(n // tm, nf),
        in_specs=[
            pl.BlockSpec((tm, D_MODEL), lambda i, j: (i, 0)),
            pl.BlockSpec((1, D_MODEL), lambda i, j: (0, 0)),
            pl.BlockSpec((None, D_MODEL, tf), lambda i, j: (layer, 0, j)),
            pl.BlockSpec((None, D_MODEL, tf), lambda i, j: (layer, 0, nf + j)),
            pl.BlockSpec((None, tf, D_MODEL), lambda i, j: (layer, j, 0)),
            pl.BlockSpec((1, D_MODEL), lambda i, j: (0, 0)),
        ],
        out_specs=pl.BlockSpec((tm, D_MODEL), lambda i, j: (i, 0)),
        scratch_shapes=[pltpu.VMEM((tm, D_MODEL), BF16), pltpu.VMEM((tm, D_MODEL), F32)],
        compiler_params=_cparams(("parallel", "arbitrary")),
        name="ffn",
    )(x2, g1, w_in, w_in, w_out, g2)


def _pack_w_in(w):
    o_mq, o_dq, o_dkv, o_iq, o_ik, o_gates = 2048, 3584, 4096, 4352, 4864, 4936
    w = w.astype(BF16)
    z = lambda k: jnp.zeros(w.shape[:-1] + (k,), w.dtype)
    w_f = jnp.concatenate([
        w[..., :o_mq],
        w[..., o_gates:],
        w[..., o_dkv:o_iq],
        w[..., o_ik:o_gates],
        z(PROJ_F_W - FCOL_IKW - (o_gates - o_ik)),
    ], axis=-1)
    w_b = jnp.concatenate([
        w[..., o_mq:o_dq],
        w[..., o_dq:o_dkv],
        w[..., o_iq:o_ik],
    ], axis=-1)
    assert w_f.shape[-1] == PROJ_F_W and w_b.shape[-1] == PROJ_B_W
    return w_f, w_b


def _tiles(batch, seq):
    n = batch * seq
    return dict(
        proj_tm=min(2048, n), proj_f_tn=PROJ_F_W // 4, proj_b_tm=min(2048, n), proj_b_tn=PROJ_B_W,
        hgrn_rows=min(512, seq),
        merge_tm=min(512, n),
        ffn_tm=min(1024, n), ffn_tf=D_FF // 2,
    )


def kernel(x, w_in, pre_mix_norm, post_mix_norm, pre_ffn_norm, post_ffn_norm, hg_lb_logits, hg_norm_w, dsa_kv_norm,
           dsa_w_uk, dsa_w_uv, idx_k_norm_w, idx_k_norm_b, rel_bias, w_branch_a, w_branch_b, w_branch_c, w_out,
           w_ffn_in, w_ffn_out):
    batch, seq, _ = x.shape
    depth = w_in.shape[0]
    t = _tiles(batch, seq)
    lbp = jax.nn.softmax(hg_lb_logits.astype(F32), axis=0)
    lower_bounds = jnp.cumsum(lbp, axis=0) - lbp[0]
    bias_b = _bias_tiles_t(rel_bias[:, :MOBA_HEADS], MOBA_BLOCK)
    bias_c = _bias_tiles_t(rel_bias[:, MOBA_HEADS:], LANES)
    bias_c = bias_c.transpose(1, 2, 0, 3).reshape(bias_c.shape[1], LANES, DSA_HEADS * LANES)
    row = lambda a: a.reshape(1, -1).astype(F32)

    w_f, w_b = _pack_w_in(w_in)
    w_uk = jnp.transpose(dsa_w_uk, (0, 2, 1, 3)).astype(BF16)
    w_uv = jnp.transpose(dsa_w_uv, (0, 2, 3, 1)).astype(BF16)
    w_a, w_bb, w_c, w_o = (a.astype(BF16) for a in (w_branch_a, w_branch_b, w_branch_c, w_out))
    w_fi, w_fo = w_ffn_in.astype(BF16), w_ffn_out.astype(BF16)

    x2 = x.reshape(batch * seq, D_MODEL)
    for l in range(depth):
        gain = row(pre_mix_norm[l])
        proj_f = _proj(x2, gain, w_f, l, F32, t["proj_tm"], t["proj_f_tn"], "proj_f")
        proj_b = _proj(x2, gain, w_b, l, BF16, t["proj_b_tm"], t["proj_b_tn"], "proj_b")
        y_a = _hgrn(proj_f, row(lower_bounds[l]), row(hg_norm_w[l]), batch, seq, t["hgrn_rows"])
        y_b = _moba(proj_b, bias_b, batch, seq)
        y_c = _dsa(proj_f, proj_b, row(dsa_kv_norm[l]), row(idx_k_norm_w[l]), row(idx_k_norm_b[l]),
                   w_uk[l], w_uv[l], bias_c, batch, seq)
        x2 = _merge(x2, y_a, y_b, y_c, proj_f, w_a, w_bb, w_c, w_o, l, row(post_mix_norm[l]), t["merge_tm"])
        x2 = _ffn(x2, row(pre_ffn_norm[l]), w_fi, w_fo, l, row(post_ffn_norm[l]), t["ffn_tm"], t["ffn_tf"])
    return x2.reshape(batch, seq, D_MODEL)
```

```python
import functools
import math

import numpy as np
import jax
import jax.numpy as jnp
from jax import lax
from jax.experimental import pallas as pl
from jax.experimental.pallas import tpu as pltpu

F32 = jnp.float32
BF16 = jnp.bfloat16

D_MODEL = 1024
RMS_EPS = 1e-6
LOG2E = 1.0 / math.log(2.0)
NEG_BIG = -1e30
TINY = 1e-20
HG_HEADS = 4
HG_D = 128
HG_CHUNK = 64
MOBA_HEADS = 4
MOBA_DH = 128
MOBA_BLOCK = 256
MOBA_TOPK = 3
MOBA_STEP_BLOCKS = 4
DSA_HEADS = 4
DSA_DH = 128
DSA_KV_RANK = 256
IDX_HEADS = 8
IDX_DIM = 64
DSA_TOPK_MAX = 256
DSA_KC = 512
DSA_SURE_BITS = 28
REL_BUCKETS = 32
REL_MAX_DIST = 2048
D_FF = 2816
HEAD_W = 512

LANES = 128
SUBLANES = 8
DSA_QB = LANES

FCOL_HQ, FCOL_HF, FCOL_HI, FCOL_HG = 0, 512, 1024, 1536
FCOL_GATES, FCOL_DKV, FCOL_IKW = 2048, 5120, 5376
PROJ_F_W = 5632
BCOL_MQ, BCOL_MK, BCOL_MV, BCOL_DQ, BCOL_IQ = 0, 512, 1024, 1536, 2048
PROJ_B_W = 2560

VMEM_LIMIT = 56 * 1024 * 1024

_REL_EXACT = REL_BUCKETS // 2
REL_FAR_DIST = int(math.ceil(_REL_EXACT * (REL_MAX_DIST / _REL_EXACT) ** ((REL_BUCKETS - _REL_EXACT - 1) / (REL_BUCKETS - _REL_EXACT)))) + 1

_NEG_BITS = int(np.array(NEG_BIG, np.float32).view(np.int32))
KEY_NEG = int(np.int32(_NEG_BITS ^ 0x7FFFFFFF))


def _cparams(sem):
    return pltpu.CompilerParams(dimension_semantics=sem, vmem_limit_bytes=VMEM_LIMIT)


def _sigmoid(x):
    return 1.0 / (1.0 + jnp.exp(-x))


def _dot(a, b):
    return jnp.dot(a, b, preferred_element_type=F32)


def _dot_nt(a, b):
    return lax.dot_general(a, b, (((1,), (1,)), ((), ())), preferred_element_type=F32)


def _t_bf16(a):
    return a.astype(F32).T.astype(BF16)


ROW_SLAB = 64


def _slab(x, fn):
    return fn(x.reshape(x.shape[0] // ROW_SLAB, ROW_SLAB, x.shape[1]), axis=0)


def _reduce_rows(x, fn):
    y = _slab(x, fn)
    z = fn(y.reshape(ROW_SLAB // SUBLANES, SUBLANES, y.shape[1]), axis=0)
    return fn(z, axis=0, keepdims=True)


def _proj_kernel(x_ref, g_ref, w_ref, o_ref, xn_ref):
    @pl.when(pl.program_id(1) == 0)
    def _():
        x = x_ref[...]
        ms = jnp.mean(x * x, axis=-1, keepdims=True)
        xn_ref[...] = (x * lax.rsqrt(ms + RMS_EPS) * g_ref[...]).astype(BF16)

    o_ref[...] = _dot(xn_ref[...], w_ref[...]).astype(o_ref.dtype)


def _proj(x2, gain, w_packed, layer, out_dtype, tm, tn, name):
    n = x2.shape[0]
    width = w_packed.shape[-1]
    return pl.pallas_call(
        _proj_kernel,
        out_shape=jax.ShapeDtypeStruct((n, width), out_dtype),
        grid=(n // tm, width // tn),
        in_specs=[
            pl.BlockSpec((tm, D_MODEL), lambda i, j: (i, 0)),
            pl.BlockSpec((1, D_MODEL), lambda i, j: (0, 0)),
            pl.BlockSpec((None, D_MODEL, tn), lambda i, j: (layer, 0, j)),
        ],
        out_specs=pl.BlockSpec((tm, tn), lambda i, j: (i, j)),
        scratch_shapes=[pltpu.VMEM((tm, D_MODEL), BF16)],
        compiler_params=_cparams(("parallel", "arbitrary")),
        name=name,
    )(x2, gain, w_packed)


HG_LEVELS = (32, 16, 8, 4, 2, 1)
HG_UNROLL = 8


def _hgrn_constants():
    c = HG_CHUNK
    t = np.arange(c)
    i = t[:, None]
    tt = t[None, :]
    rows = [tt <= i]
    masks = []
    for m in HG_LEVELS:
        mid = ((t // (2 * m)) * (2 * m) + m)[:, None]
        upper = (i >= mid) & (tt >= mid) & (tt <= i)
        lower = (i < mid) & (tt > i) & (tt <= mid - 1)
        rows.append(upper | lower)
        same = (t[:, None] // (2 * m)) == (t[None, :] // (2 * m))
        masks.append(same & (i >= mid) & (tt < mid))
    masks.append(np.eye(c, dtype=bool))
    m_all = np.concatenate(rows, axis=0).astype(np.float32)
    return m_all, np.stack(masks).astype(np.float32)


def _hgrn_kernel(q_ref, f_ref, i_ref, g_ref, lb_ref, nw_ref, mall_ref, mask_ref, o_ref, st_ref, e_ref, *, n_chunks):
    c = HG_CHUNK
    nl = len(HG_LEVELS)

    @pl.when(pl.program_id(1) == 0)
    def _():
        st_ref[...] = jnp.zeros_like(st_ref)

    lb = lb_ref[...]
    nw = nw_ref[...]

    heads = [slice(h * HG_D, (h + 1) * HG_D) for h in range(HG_HEADS)]

    def intra(r0, u):
        fp = f_ref[pl.ds(r0, c), :]
        f = lb + (1.0 - lb) * _sigmoid(fp)
        logf = jnp.log(jnp.maximum(f, TINY))
        kk = (1.0 - lb) * _sigmoid(-fp)
        g_hi = logf.astype(BF16)
        g_lo = (logf - g_hi.astype(F32)).astype(BF16)
        e_ref[u] = _dot(mall_ref[...], jnp.concatenate([g_hi, g_lo], axis=1))
        q = q_ref[pl.ds(r0, c), :]

        def expo(row):
            return e_ref[u, row * c:(row + 1) * c, 0:HEAD_W] + e_ref[u, row * c:(row + 1) * c, HEAD_W:2 * HEAD_W]

        b = expo(0)
        b_last = b[c - 1:c, :]
        q_in = (q * jnp.exp(b)).astype(BF16)
        k_out = (kk * jnp.exp(b_last - b)).astype(BF16)
        qls, kls = [], []
        for l in range(nl):
            decay = jnp.exp(expo(1 + l))
            qls.append((q * decay).astype(BF16))
            kls.append((kk * decay).astype(BF16))
        qls.append(q.astype(BF16))
        kls.append(kk.astype(BF16))
        sls = [[_dot_nt(qls[l][:, hs], kls[l][:, hs]) for l in range(nl + 1)] for hs in heads]
        ss = []
        for h in range(HG_HEADS):
            s = jnp.zeros((c, c), F32)
            for l in range(nl + 1):
                s = s + jnp.where(mask_ref[l] > 0.0, sls[h][l], 0.0)
            ss.append(s.astype(BF16))
        v = i_ref[pl.ds(r0, c), :].astype(BF16)
        o_intra = [_dot(ss[h], v[:, hs]) for h, hs in enumerate(heads)]
        kv = [_dot(v[:, hs].astype(F32).T.astype(BF16), k_out[:, hs]) for hs in heads]
        return q_in, jnp.exp(b_last), o_intra, kv

    def step(ci, carry):
        parts = [intra(pl.multiple_of((ci * HG_UNROLL + u) * c, c), u) for u in range(HG_UNROLL)]
        for u, (q_in, st_decay, o_intra, kv) in enumerate(parts):
            r0 = pl.multiple_of((ci * HG_UNROLL + u) * c, c)
            og = g_ref[pl.ds(r0, c), :]
            outs = [_dot_nt(q_in[:, hs], st_ref[h].astype(BF16)) + o_intra[h] for h, hs in enumerate(heads)]
            for h, hs in enumerate(heads):
                st_ref[h] = st_ref[h] * st_decay[:, hs] + kv[h]
            for h, hs in enumerate(heads):
                o = outs[h]
                y = o * lax.rsqrt(jnp.mean(o * o, axis=-1, keepdims=True) + RMS_EPS) * nw
                gh = og[:, hs]
                o_ref[pl.ds(r0, c), hs] = (y * (gh * _sigmoid(gh))).astype(o_ref.dtype)
        return carry

    lax.fori_loop(0, n_chunks // HG_UNROLL, step, 0)


def _hgrn(proj_f, lb, norm_w, batch, seq, rows):
    m_all, masks = _hgrn_constants()
    n_rb = seq // rows
    n_chunks = rows // HG_CHUNK
    assert n_chunks % HG_UNROLL == 0

    def col(c0):
        return pl.BlockSpec((rows, HEAD_W), lambda b, r, cidx=c0 // HEAD_W: (b * n_rb + r, cidx))

    return pl.pallas_call(
        functools.partial(_hgrn_kernel, n_chunks=n_chunks),
        out_shape=jax.ShapeDtypeStruct((batch * seq, HEAD_W), BF16),
        grid=(batch, n_rb),
        in_specs=[
            col(FCOL_HQ), col(FCOL_HF), col(FCOL_HI), col(FCOL_HG),
            pl.BlockSpec((1, HEAD_W), lambda b, r: (0, 0)),
            pl.BlockSpec((1, HG_D), lambda b, r: (0, 0)),
            pl.BlockSpec(m_all.shape, lambda b, r: (0, 0)),
            pl.BlockSpec(masks.shape, lambda b, r: (0, 0, 0)),
        ],
        out_specs=pl.BlockSpec((rows, HEAD_W), lambda b, r: (b * n_rb + r, 0)),
        scratch_shapes=[
            pltpu.VMEM((HG_HEADS, HG_D, HG_D), F32),
            pltpu.VMEM((HG_UNROLL, m_all.shape[0], 2 * HEAD_W), F32),
        ],
        compiler_params=_cparams(("parallel", "arbitrary")),
        name="hgrn",
    )(proj_f, proj_f, proj_f, proj_f, lb, norm_w, jnp.asarray(m_all, BF16), jnp.asarray(masks, F32))


def _rel_bucket(dist):
    n = jnp.maximum(dist, 0)
    exact = REL_BUCKETS // 2
    nf = jnp.maximum(n, exact).astype(F32)
    large = exact + (jnp.log(nf / exact) / math.log(REL_MAX_DIST / exact) * (REL_BUCKETS - exact)).astype(jnp.int32)
    large = jnp.minimum(large, REL_BUCKETS - 1)
    return jnp.where(n < exact, n, large)


def _n_bias_tiles(tile):
    d = 0
    while d * tile - (tile - 1) < REL_FAR_DIST:
        d += 1
    return d + 1


def _bias_tiles_t(table, tile):
    n = _n_bias_tiles(tile)
    length = n * tile
    by_dist = (table[_rel_bucket(jnp.arange(length, dtype=jnp.int32))].astype(F32) * LOG2E).T
    heads = by_dist.shape[0]
    w = jnp.concatenate([by_dist, jnp.zeros((heads, 1), F32)], axis=1)
    skew = jnp.broadcast_to(w[:, None, :], (heads, tile, length + 1)).reshape(heads, tile * (length + 1))
    skew = skew[:, :tile * length].reshape(heads, tile, length)
    return skew.reshape(heads, tile, n, tile).transpose(0, 2, 1, 3)


def _moba_kernel(q_ref, k_ref, v_ref, bias_ref, o_ref, vt_ref, km_ref, sel_ref, m_ref, l_ref, acc_ref,
                 *, n_blocks, n_sel, n_bias):
    blk, dh, nh = MOBA_BLOCK, MOBA_DH, MOBA_HEADS
    qb = pl.program_id(1)
    scale2 = dh ** -0.5 * LOG2E

    @pl.when(qb == 0)
    def _():
        for h in range(nh):
            for n in range(n_blocks):
                kblk = k_ref[n * blk:(n + 1) * blk, h * dh:(h + 1) * dh].astype(F32)
                km_ref[h, n:n + 1, :] = jnp.mean(kblk, axis=0, keepdims=True)
                vt_ref[h * dh:(h + 1) * dh, n * blk:(n + 1) * blk] = _t_bf16(v_ref[n * blk:(n + 1) * blk, h * dh:(h + 1) * dh])

    q_t = []
    n_iota = lax.broadcasted_iota(jnp.int32, (n_blocks, blk), 0)
    for h in range(nh):
        qt = _t_bf16(q_ref[:, h * dh:(h + 1) * dh])
        q_t.append(qt)
        gate = _dot(km_ref[h].astype(BF16), qt)
        gate = jnp.where(n_iota < qb, gate, NEG_BIG)
        rank = jnp.zeros(gate.shape, F32)
        for n in range(n_blocks):
            row = gate[n:n + 1, :]
            beats = (row > gate) | ((row == gate) & (n < n_iota))
            rank = rank + jnp.where(beats, 1.0, 0.0)
        sel_ref[h] = jnp.where((rank < n_sel) & (n_iota < qb), 0.0, NEG_BIG)
        m_ref[h] = jnp.full((1, blk), -jnp.inf, F32)
        l_ref[h] = jnp.zeros((1, blk), F32)
        acc_ref[h] = jnp.zeros((dh, blk), F32)

    def attend(r0, nk, mask_fn):
        ss = [_dot(k_ref[pl.ds(r0, nk), h * dh:(h + 1) * dh], q_t[h]) for h in range(nh)]
        ps, alphas = [], []
        for h in range(nh):
            s = ss[h] * scale2 + mask_fn(h)
            m_old = m_ref[h]
            m_new = jnp.maximum(m_old, _reduce_rows(s, jnp.max))
            alphas.append(jnp.exp2(m_old - m_new))
            p = jnp.exp2(s - m_new)
            l_ref[h] = alphas[h] * l_ref[h] + _reduce_rows(p, jnp.sum)
            m_ref[h] = m_new
            ps.append(p.astype(BF16))
        for h in range(nh):
            acc_ref[h] = alphas[h] * acc_ref[h] + _dot(vt_ref[h * dh:(h + 1) * dh, pl.ds(r0, nk)], ps[h])

    def past_mask(h, n):
        return bias_ref[h, jnp.minimum(qb - n, n_bias - 1)] + sel_ref[h, pl.ds(n, 1), :]

    causal = jnp.where(lax.broadcasted_iota(jnp.int32, (blk, blk), 0) <= lax.broadcasted_iota(jnp.int32, (blk, blk), 1),
                       0.0, NEG_BIG)

    def own_mask(h):
        return bias_ref[h, 0] + causal

    group = MOBA_STEP_BLOCKS

    def past_group(i, carry):
        n = group * i
        attend(pl.multiple_of(n * blk, group * blk), group * blk,
               lambda h: jnp.concatenate([past_mask(h, n + u) for u in range(group)], axis=0))
        return carry

    lax.fori_loop(0, qb // group, past_group, 0)

    for k in range(group):
        @pl.when(qb % group == k)
        def _(k=k):
            attend(pl.multiple_of((qb - k) * blk, blk), (k + 1) * blk,
                   lambda h: jnp.concatenate([past_mask(h, qb - k + u) for u in range(k)] + [own_mask(h)], axis=0))

    for h in range(nh):
        o_ref[:, h * dh:(h + 1) * dh] = (acc_ref[h] / l_ref[h]).T.astype(o_ref.dtype)


def _moba(proj_b, bias_tiles, batch, seq):
    blk = MOBA_BLOCK
    nb = seq // blk
    n_sel = min(MOBA_TOPK, nb - 1)
    n_bias = bias_tiles.shape[1]
    return pl.pallas_call(
        functools.partial(_moba_kernel, n_blocks=nb, n_sel=n_sel, n_bias=n_bias),
        out_shape=jax.ShapeDtypeStruct((batch * seq, HEAD_W), BF16),
        grid=(batch, nb),
        in_specs=[
            pl.BlockSpec((blk, HEAD_W), lambda b, i: (b * nb + i, BCOL_MQ // HEAD_W)),
            pl.BlockSpec((seq, HEAD_W), lambda b, i: (b, BCOL_MK // HEAD_W)),
            pl.BlockSpec((seq, HEAD_W), lambda b, i: (b, BCOL_MV // HEAD_W)),
            pl.BlockSpec(bias_tiles.shape, lambda b, i: (0, 0, 0, 0)),
        ],
        out_specs=pl.BlockSpec((blk, HEAD_W), lambda b, i: (b * nb + i, 0)),
        scratch_shapes=[
            pltpu.VMEM((HEAD_W, seq), BF16),
            pltpu.VMEM((MOBA_HEADS, nb, MOBA_DH), F32),
            pltpu.VMEM((MOBA_HEADS, nb, blk), F32),
            pltpu.VMEM((MOBA_HEADS, 1, blk), F32),
            pltpu.VMEM((MOBA_HEADS, 1, blk), F32),
            pltpu.VMEM((MOBA_HEADS, MOBA_DH, blk), F32),
        ],
        compiler_params=_cparams(("parallel", "arbitrary")),
        name="moba",
    )(proj_b, proj_b, proj_b, bias_tiles)


def _sort_key(x):
    bits = pltpu.bitcast(x, jnp.int32)
    return bits ^ ((bits >> 31) & 0x7FFFFFFF)


def _dsa_kernel(dq_ref, iq_ref, iwq_ref, dkv_ref, ikw_ref, kvg_ref, lnw_ref, lnb_ref, wuk_ref, wuv_ref, bias_ref,
                o_ref, ckv_ref, ckvt_ref, ikn_ref, key_ref, mask_ref, tau_ref, cge_ref, m_ref, l_ref, acc_ref,
                *, seq, n_keep, n_bias):
    tq, kc, nh, dh = DSA_QB, DSA_KC, DSA_HEADS, DSA_DH
    sub = kc // LANES
    qb = pl.program_id(1)
    t0 = qb * tq
    n_kc = (t0 + tq - 1) // kc + 1
    n_rest = (seq - n_kc * kc).astype(F32)

    @pl.when(qb == 0)
    def _():
        for c in range(seq // kc):
            rows = slice(c * kc, (c + 1) * kc)
            x = dkv_ref[rows, :]
            cn = x * lax.rsqrt(jnp.mean(x * x, axis=-1, keepdims=True) + RMS_EPS) * kvg_ref[...]
            ckv_ref[rows, :] = cn.astype(BF16)
            ckvt_ref[:, rows] = cn.T.astype(BF16)
            ik = ikw_ref[rows, 0:IDX_DIM]
            xc = ik - jnp.mean(ik, axis=-1, keepdims=True)
            ikn = xc * lax.rsqrt(jnp.mean(xc * xc, axis=-1, keepdims=True) + RMS_EPS) * lnw_ref[...] + lnb_ref[...]
            ikn_ref[rows, :] = ikn.astype(BF16)

    t_row = t0 + lax.broadcasted_iota(jnp.int32, (1, tq), 1)
    key_off = lax.broadcasted_iota(jnp.int32, (kc, tq), 0)

    def causal_of(c0):
        return (c0 + key_off) <= t_row

    iq_t = iq_ref[...].astype(F32).T
    iq_cat = jnp.concatenate([iq_t[h * IDX_DIM:(h + 1) * IDX_DIM, :] for h in range(IDX_HEADS)], axis=1).astype(BF16)
    iw_t = iwq_ref[...].T[IDX_DIM:IDX_DIM + IDX_HEADS, :] * (IDX_HEADS ** -0.5)
    iw_t = iw_t * (IDX_DIM ** -0.5)

    hpg = 2 * LANES // tq

    def score_chunks(c, nc):
        c0s = [pl.multiple_of((c + u) * kc, kc) for u in range(nc)]
        s_grp = [[_dot(ikn_ref[pl.ds(c0, kc), :], iq_cat[:, g * hpg * tq:(g + 1) * hpg * tq])
                  for g in range(IDX_HEADS // hpg)] for c0 in c0s]
        for u, c0 in enumerate(c0s):
            sc = jnp.zeros((kc, tq), F32)
            for h in range(IDX_HEADS):
                s_h = s_grp[u][h // hpg][:, (h % hpg) * tq:(h % hpg + 1) * tq]
                sc = sc + iw_t[h:h + 1, :] * jnp.maximum(s_h, 0.0)
            sc = jnp.where(causal_of(c0), sc, NEG_BIG)
            sc = jnp.where(sc == 0.0, 0.0, sc)
            key_ref[pl.ds(c0, kc), :] = _sort_key(sc)

    def score_pair(i, carry):
        score_chunks(2 * i, 2)
        return carry

    lax.fori_loop(0, n_kc // 2, score_pair, 0)

    @pl.when(n_kc % 2 == 1)
    def _():
        score_chunks(n_kc - 1, 1)

    def count(pred_fn):
        def body(c, tot):
            c0 = pl.multiple_of(c * kc, kc)
            hit = pred_fn(key_ref[pl.ds(c0, kc), :], c0)
            return tot + _slab(jnp.where(hit, 1.0, 0.0), jnp.sum)
        part = lax.fori_loop(0, n_kc, body, jnp.zeros((ROW_SLAB, tq), F32))
        return _reduce_rows(part, jnp.sum)

    def bit_step(k, carry):
        base, c_base = carry
        cand = base + jnp.left_shift(jnp.int32(1), 31 - k)
        cnt = count(lambda keys, c0: keys >= cand) + jnp.where(cand <= KEY_NEG, n_rest, 0.0)
        take = cnt >= n_keep
        return jnp.where(take, cand, base), jnp.where(take, cnt, c_base)

    tau, c_ge = lax.fori_loop(0, DSA_SURE_BITS, bit_step, (jnp.full((1, tq), jnp.iinfo(jnp.int32).min, jnp.int32),
                                                           jnp.full((1, tq), float(seq), F32)))
    tau_ref[...] = tau
    cge_ref[...] = c_ge

    @pl.when(jnp.max(jnp.abs(c_ge - n_keep)) > 0.0)
    def _():
        tau_ref[...], cge_ref[...] = lax.fori_loop(DSA_SURE_BITS, 32, bit_step, (tau, c_ge))

    tau = tau_ref[...]
    c_ge = cge_ref[...]

    def gt_and_mask(c, tot):
        c0 = pl.multiple_of(c * kc, kc)
        keys = key_ref[pl.ds(c0, kc), :]
        mask_ref[pl.ds(c0, kc), :] = jnp.where((keys >= tau) & causal_of(c0), 0.0, NEG_BIG)
        return tot + _slab(jnp.where(keys > tau, 1.0, 0.0), jnp.sum)

    c_gt = _reduce_rows(lax.fori_loop(0, n_kc, gt_and_mask, jnp.zeros((ROW_SLAB, tq), F32)), jnp.sum)
    c_gt = c_gt + jnp.where(tau < KEY_NEG, n_rest, 0.0)
    need = n_keep - c_gt
    n_eq = (c_ge - c_gt) - jnp.where(tau == KEY_NEG, (seq - 1 - t_row).astype(F32), 0.0)
    tie_break = jnp.max(n_eq - need) > 0.0

    @pl.when(tie_break)
    def _():
        lower = jnp.where(lax.broadcasted_iota(jnp.int32, (kc, kc), 1)
                          < lax.broadcasted_iota(jnp.int32, (kc, kc), 0), 1.0, 0.0).astype(BF16)

        def body(c, seen):
            c0 = pl.multiple_of(c * kc, kc)
            keys = key_ref[pl.ds(c0, kc), :]
            eq = keys == tau
            eqf = jnp.where(eq, 1.0, 0.0)
            rank = seen + _dot(lower, eqf.astype(BF16))
            keep = ((keys > tau) | (eq & (rank < need))) & causal_of(c0)
            mask_ref[pl.ds(c0, kc), :] = jnp.where(keep, 0.0, NEG_BIG)
            return seen + _reduce_rows(eqf, jnp.sum)
        lax.fori_loop(0, n_kc, body, jnp.zeros((1, tq), F32))

    scale2 = dh ** -0.5 * LOG2E
    qa_cat = jnp.concatenate(
        [_dot_nt(wuk_ref[h], dq_ref[:, h * dh:(h + 1) * dh]) for h in range(nh)], axis=1).astype(BF16)
    m_ref[...] = jnp.full(m_ref.shape, -jnp.inf, F32)
    l_ref[...] = jnp.zeros(l_ref.shape, F32)
    acc_ref[...] = jnp.zeros(acc_ref.shape, F32)

    def attend(c, nc):
        c0 = pl.multiple_of(c * kc, kc)
        nk = nc * kc
        msk = mask_ref[pl.ds(c0, nk), :]
        bias = jnp.concatenate(
            [bias_ref[jnp.clip(qb - (c * sub + g), 0, n_bias - 1)] for g in range(nc * sub)], axis=0)
        s = _dot(ckv_ref[pl.ds(c0, nk), :], qa_cat) * scale2 + bias + jnp.concatenate([msk] * nh, axis=1)
        m_old = m_ref[...]
        m_new = jnp.maximum(m_old, _reduce_rows(s, jnp.max))
        alpha = jnp.exp2(m_old - m_new)
        p = jnp.exp2(s - m_new)
        l_ref[...] = alpha * l_ref[...] + _reduce_rows(p, jnp.sum)
        acc_ref[...] = alpha * acc_ref[...] + _dot(ckvt_ref[:, pl.ds(c0, nk)], p.astype(BF16))
        m_ref[...] = m_new

    def att_pair(i, carry):
        attend(2 * i, 2)
        return carry

    lax.fori_loop(0, n_kc // 2, att_pair, 0)

    @pl.when(n_kc % 2 == 1)
    def _():
        attend(n_kc - 1, 1)

    o_lat_t = (acc_ref[...] / l_ref[...]).astype(BF16)
    for h in range(nh):
        o_ref[:, h * dh:(h + 1) * dh] = _dot(wuv_ref[h], o_lat_t[:, h * tq:(h + 1) * tq]).T.astype(o_ref.dtype)


def _dsa(proj_f, proj_b, kv_gain, ln_w, ln_b, w_uk_h, w_uv_h, bias_tiles, batch, seq):
    tq = DSA_QB
    nq = seq // tq
    n_keep = min(DSA_TOPK_MAX, seq // 4)
    n_bias = bias_tiles.shape[0]
    return pl.pallas_call(
        functools.partial(_dsa_kernel, seq=seq, n_keep=float(n_keep), n_bias=n_bias),
        out_shape=jax.ShapeDtypeStruct((batch * seq, HEAD_W), BF16),
        grid=(batch, nq),
        in_specs=[
            pl.BlockSpec((tq, HEAD_W), lambda b, i: (b * nq + i, BCOL_DQ // HEAD_W)),
            pl.BlockSpec((tq, HEAD_W), lambda b, i: (b * nq + i, BCOL_IQ // HEAD_W)),
            pl.BlockSpec((tq, LANES), lambda b, i: (b * nq + i, FCOL_IKW // LANES)),
            pl.BlockSpec((seq, DSA_KV_RANK), lambda b, i: (b, FCOL_DKV // DSA_KV_RANK)),
            pl.BlockSpec((seq, LANES), lambda b, i: (b, FCOL_IKW // LANES)),
            pl.BlockSpec((1, DSA_KV_RANK), lambda b, i: (0, 0)),
            pl.BlockSpec((1, IDX_DIM), lambda b, i: (0, 0)),
            pl.BlockSpec((1, IDX_DIM), lambda b, i: (0, 0)),
            pl.BlockSpec(w_uk_h.shape, lambda b, i: (0, 0, 0)),
            pl.BlockSpec(w_uv_h.shape, lambda b, i: (0, 0, 0)),
            pl.BlockSpec(bias_tiles.shape, lambda b, i: (0, 0, 0)),
        ],
        out_specs=pl.BlockSpec((tq, HEAD_W), lambda b, i: (b * nq + i, 0)),
        scratch_shapes=[
            pltpu.VMEM((seq, DSA_KV_RANK), BF16),
            pltpu.VMEM((DSA_KV_RANK, seq), BF16),
            pltpu.VMEM((seq, IDX_DIM), BF16),
            pltpu.VMEM((seq, tq), jnp.int32),
            pltpu.VMEM((seq, tq), F32),
            pltpu.VMEM((1, tq), jnp.int32),
            pltpu.VMEM((1, tq), F32),
            pltpu.VMEM((1, DSA_HEADS * tq), F32),
            pltpu.VMEM((1, DSA_HEADS * tq), F32),
            pltpu.VMEM((DSA_KV_RANK, DSA_HEADS * tq), F32),
        ],
        compiler_params=_cparams(("parallel", "arbitrary")),
        name="dsa",
    )(proj_b, proj_b, proj_f, proj_f, proj_f, kv_gain, ln_w, ln_b, w_uk_h, w_uv_h, bias_tiles)


def _merge_kernel(x_ref, ya_ref, yb_ref, yc_ref, g0_ref, g1_ref, g2_ref, wa_ref, wb_ref, wc_ref, wo_ref, gain_ref,
                  o_ref):
    merged = (_sigmoid(g0_ref[...]) * _dot(ya_ref[...], wa_ref[...])
              + _sigmoid(g1_ref[...]) * _dot(yb_ref[...], wb_ref[...])
              + _sigmoid(g2_ref[...]) * _dot(yc_ref[...], wc_ref[...]))
    z = _dot(merged.astype(BF16), wo_ref[...])
    zn = z * lax.rsqrt(jnp.mean(z * z, axis=-1, keepdims=True) + RMS_EPS) * gain_ref[...]
    o_ref[...] = x_ref[...] + zn


def _merge(x2, ya, yb, yc, proj_f, wa, wb, wc, wo, layer, gain, tm):
    n = x2.shape[0]
    gcol = FCOL_GATES // D_MODEL

    def rows(w):
        return pl.BlockSpec((tm, w), lambda i: (i, 0))

    def gate(k):
        return pl.BlockSpec((tm, D_MODEL), lambda i, k=k: (i, gcol + k))

    def weight(a):
        return pl.BlockSpec((None,) + a.shape[1:], lambda i: (layer, 0, 0))

    return pl.pallas_call(
        _merge_kernel,
        out_shape=jax.ShapeDtypeStruct((n, D_MODEL), F32),
        grid=(n // tm,),
        in_specs=[rows(D_MODEL), rows(HEAD_W), rows(HEAD_W), rows(HEAD_W), gate(0), gate(1), gate(2),
                  weight(wa), weight(wb), weight(wc), weight(wo), pl.BlockSpec(gain.shape, lambda i: (0, 0))],
        out_specs=rows(D_MODEL),
        compiler_params=_cparams(("parallel",)),
        name="merge",
    )(x2, ya, yb, yc, proj_f, proj_f, proj_f, wa, wb, wc, wo, gain)


def _ffn_kernel(x_ref, g1_ref, wg_ref, wu_ref, wd_ref, g2_ref, o_ref, hn_ref, acc_ref):
    j = pl.program_id(1)

    @pl.when(j == 0)
    def _():
        x = x_ref[...]
        hn_ref[...] = (x * lax.rsqrt(jnp.mean(x * x, axis=-1, keepdims=True) + RMS_EPS) * g1_ref[...]).astype(BF16)
        acc_ref[...] = jnp.zeros_like(acc_ref)

    hn = hn_ref[...]
    gate = _dot(hn, wg_ref[...])
    up = _dot(hn, wu_ref[...])
    act = (gate * _sigmoid(gate)) * up
    acc_ref[...] += _dot(act.astype(BF16), wd_ref[...])

    @pl.when(j == pl.num_programs(1) - 1)
    def _():
        y = acc_ref[...]
        yn = y * lax.rsqrt(jnp.mean(y * y, axis=-1, keepdims=True) + RMS_EPS) * g2_ref[...]
        o_ref[...] = x_ref[...] + yn


def _ffn(x2, g1, w_in, w_out, layer, g2, tm, tf):
    n = x2.shape[0]
    nf = D_FF // tf
    return pl.pallas_call(
        _ffn_kernel,
        out_shape=jax.ShapeDtypeStruct((n, D_MODEL), F32),
        grid=(n // tm, nf),
        in_specs=[
            pl.BlockSpec((tm, D_MODEL), lambda i, j: (i, 0)),
            pl.BlockSpec((1, D_MODEL), lambda i, j: (0, 0)),
            pl.BlockSpec((None, D_MODEL, tf), lambda i, j: (layer, 0, j)),
            pl.BlockSpec((None, D_MODEL, tf), lambda i, j: (layer, 0, nf + j)),
            pl.BlockSpec((None, tf, D_MODEL), lambda i, j: (layer, j, 0)),
            pl.BlockSpec((1, D_MODEL), lambda i, j: (0, 0)),
        ],
        out_specs=pl.BlockSpec((tm, D_MODEL), lambda i, j: (i, 0)),
        scratch_shapes=[pltpu.VMEM((tm, D_MODEL), BF16), pltpu.VMEM((tm, D_MODEL), F32)],
        compiler_params=_cparams(("parallel", "arbitrary")),
        name="ffn",
    )(x2, g1, w_in, w_in, w_out, g2)


def _pack_w_in(w):
    o_mq, o_dq, o_dkv, o_iq, o_ik, o_gates = 2048, 3584, 4096, 4352, 4864, 4936
    w = w.astype(BF16)
    z = lambda k: jnp.zeros(w.shape[:-1] + (k,), w.dtype)
    w_f = jnp.concatenate([
        w[..., :o_mq],
        w[..., o_gates:],
        w[..., o_dkv:o_iq],
        w[..., o_ik:o_gates],
        z(PROJ_F_W - FCOL_IKW - (o_gates - o_ik)),
    ], axis=-1)
    w_b = jnp.concatenate([
        w[..., o_mq:o_dq],
        w[..., o_dq:o_dkv],
        w[..., o_iq:o_ik],
    ], axis=-1)
    assert w_f.shape[-1] == PROJ_F_W and w_b.shape[-1] == PROJ_B_W
    return w_f, w_b


def _tiles(batch, seq):
    n = batch * seq
    return dict(
        proj_tm=min(2048, n), proj_f_tn=PROJ_F_W // 4, proj_b_tm=min(2048, n), proj_b_tn=PROJ_B_W,
        hgrn_rows=min(512, seq),
        merge_tm=min(512, n),
        ffn_tm=min(1024, n), ffn_tf=D_FF // 2,
    )


def kernel(x, w_in, pre_mix_norm, post_mix_norm, pre_ffn_norm, post_ffn_norm, hg_lb_logits, hg_norm_w, dsa_kv_norm,
           dsa_w_uk, dsa_w_uv, idx_k_norm_w, idx_k_norm_b, rel_bias, w_branch_a, w_branch_b, w_branch_c, w_out,
           w_ffn_in, w_ffn_out):
    batch, seq, _ = x.shape
    depth = w_in.shape[0]
    t = _tiles(batch, seq)
    lbp = jax.nn.softmax(hg_lb_logits.astype(F32), axis=0)
    lower_bounds = jnp.cumsum(lbp, axis=0) - lbp[0]
    bias_b = _bias_tiles_t(rel_bias[:, :MOBA_HEADS], MOBA_BLOCK)
    bias_c = _bias_tiles_t(rel_bias[:, MOBA_HEADS:], LANES)
    bias_c = bias_c.transpose(1, 2, 0, 3).reshape(bias_c.shape[1], LANES, DSA_HEADS * LANES)
    row = lambda a: a.reshape(1, -1).astype(F32)

    w_f, w_b = _pack_w_in(w_in)
    w_uk = jnp.transpose(dsa_w_uk, (0, 2, 1, 3)).astype(BF16)
    w_uv = jnp.transpose(dsa_w_uv, (0, 2, 3, 1)).astype(BF16)
    w_a, w_bb, w_c, w_o = (a.astype(BF16) for a in (w_branch_a, w_branch_b, w_branch_c, w_out))
    w_fi, w_fo = w_ffn_in.astype(BF16), w_ffn_out.astype(BF16)

    x2 = x.reshape(batch * seq, D_MODEL)
    for l in range(depth):
        gain = row(pre_mix_norm[l])
        proj_f = _proj(x2, gain, w_f, l, F32, t["proj_tm"], t["proj_f_tn"], "proj_f")
        proj_b = _proj(x2, gain, w_b, l, BF16, t["proj_b_tm"], t["proj_b_tn"], "proj_b")
        y_a = _hgrn(proj_f, row(lower_bounds[l]), row(hg_norm_w[l]), batch, seq, t["hgrn_rows"])
        y_b = _moba(proj_b, bias_b, batch, seq)
        y_c = _dsa(proj_f, proj_b, row(dsa_kv_norm[l]), row(idx_k_norm_w[l]), row(idx_k_norm_b[l]),
                   w_uk[l], w_uv[l], bias_c, batch, seq)
        x2 = _merge(x2, y_a, y_b, y_c, proj_f, w_a, w_bb, w_c, w_o, l, row(post_mix_norm[l]), t["merge_tm"])
        x2 = _ffn(x2, row(pre_ffn_norm[l]), w_fi, w_fo, l, row(post_ffn_norm[l]), t["ffn_tm"], t["ffn_tf"])
    return x2.reshape(batch, seq, D_MODEL)
```

```python
import functools
import math

import numpy as np
import jax
import jax.numpy as jnp
from jax import lax
from jax.experimental import pallas as pl
from jax.experimental.pallas import tpu as pltpu

F32 = jnp.float32
BF16 = jnp.bfloat16

D_MODEL = 1024
RMS_EPS = 1e-6
LOG2E = 1.0 / math.log(2.0)
NEG_BIG = -1e30
TINY = 1e-20
HG_HEADS = 4
HG_D = 128
HG_CHUNK = 64
MOBA_HEADS = 4
MOBA_DH = 128
MOBA_BLOCK = 256
MOBA_TOPK = 3
MOBA_STEP_BLOCKS = 4
DSA_HEADS = 4
DSA_DH = 128
DSA_KV_RANK = 256
IDX_HEADS = 8
IDX_DIM = 64
DSA_TOPK_MAX = 256
DSA_KC = 512
DSA_SURE_BITS = 28
REL_BUCKETS = 32
REL_MAX_DIST = 2048
D_FF = 2816
HEAD_W = 512

LANES = 128
SUBLANES = 8
DSA_QB = LANES

IN_NAMES = ("hq", "hf", "hi", "hg", "mq", "mk", "mv", "dq", "dkv", "iq", "ik", "iw", "gates")
IN_SIZES = (HEAD_W, HEAD_W, HEAD_W, HEAD_W, HEAD_W, HEAD_W, HEAD_W, HEAD_W, DSA_KV_RANK,
            IDX_HEADS * IDX_DIM, IDX_DIM, IDX_HEADS, 3 * D_MODEL)

FCOL_HQ, FCOL_HF, FCOL_HI, FCOL_HG = 0, 512, 1024, 1536
FCOL_GATES, FCOL_DKV, FCOL_IKW = 2048, 5120, 5376
PROJ_F_W = 5632
BCOL_MQ, BCOL_MK, BCOL_MV, BCOL_DQ, BCOL_IQ = 0, 512, 1024, 1536, 2048
PROJ_B_W = 2560

VMEM_LIMIT = 56 * 1024 * 1024

_REL_EXACT = REL_BUCKETS // 2
REL_FAR_DIST = int(math.ceil(_REL_EXACT * (REL_MAX_DIST / _REL_EXACT) ** ((REL_BUCKETS - _REL_EXACT - 1) / (REL_BUCKETS - _REL_EXACT)))) + 1

_NEG_BITS = int(np.array(NEG_BIG, np.float32).view(np.int32))
KEY_NEG = int(np.int32(_NEG_BITS ^ 0x7FFFFFFF))


def _cparams(sem):
    return pltpu.CompilerParams(dimension_semantics=sem, vmem_limit_bytes=VMEM_LIMIT)


def _sigmoid(x):
    return 1.0 / (1.0 + jnp.exp(-x))


def _dot(a, b):
    return jnp.dot(a, b, preferred_element_type=F32)


def _dot_nt(a, b):
    return lax.dot_general(a, b, (((1,), (1,)), ((), ())), preferred_element_type=F32)


def _t_bf16(a):
    return a.astype(F32).T.astype(BF16)


ROW_SLAB = 64


def _slab(x, fn):
    return fn(x.reshape(x.shape[0] // ROW_SLAB, ROW_SLAB, x.shape[1]), axis=0)


def _reduce_rows(x, fn):
    y = _slab(x, fn)
    z = fn(y.reshape(ROW_SLAB // SUBLANES, SUBLANES, y.shape[1]), axis=0)
    return fn(z, axis=0, keepdims=True)


def _proj_kernel(x_ref, g_ref, w_ref, o_ref, xn_ref):
    @pl.when(pl.program_id(1) == 0)
    def _():
        x = x_ref[...]
        ms = jnp.mean(x * x, axis=-1, keepdims=True)
        xn_ref[...] = (x * lax.rsqrt(ms + RMS_EPS) * g_ref[...]).astype(BF16)

    o_ref[...] = _dot(xn_ref[...], w_ref[...]).astype(o_ref.dtype)


def _proj(x2, gain, w_packed, layer, out_dtype, tm, tn, name):
    n = x2.shape[0]
    width = w_packed.shape[-1]
    return pl.pallas_call(
        _proj_kernel,
        out_shape=jax.ShapeDtypeStruct((n, width), out_dtype),
        grid=(n // tm, width // tn),
        in_specs=[
            pl.BlockSpec((tm, D_MODEL), lambda i, j: (i, 0)),
            pl.BlockSpec((1, D_MODEL), lambda i, j: (0, 0)),
            pl.BlockSpec((None, D_MODEL, tn), lambda i, j: (layer, 0, j)),
        ],
        out_specs=pl.BlockSpec((tm, tn), lambda i, j: (i, j)),
        scratch_shapes=[pltpu.VMEM((tm, D_MODEL), BF16)],
        compiler_params=_cparams(("parallel", "arbitrary")),
        name=name,
    )(x2, gain, w_packed)


HG_LEVELS = (32, 16, 8, 4, 2, 1)
HG_UNROLL = 8


def _hgrn_constants():
    c = HG_CHUNK
    t = np.arange(c)
    i = t[:, None]
    tt = t[None, :]
    rows = [tt <= i]
    masks = []
    for m in HG_LEVELS:
        mid = ((t // (2 * m)) * (2 * m) + m)[:, None]
        upper = (i >= mid) & (tt >= mid) & (tt <= i)
        lower = (i < mid) & (tt > i) & (tt <= mid - 1)
        rows.append(upper | lower)
        same = (t[:, None] // (2 * m)) == (t[None, :] // (2 * m))
        masks.append(same & (i >= mid) & (tt < mid))
    masks.append(np.eye(c, dtype=bool))
    m_all = np.concatenate(rows, axis=0).astype(np.float32)
    return m_all, np.stack(masks).astype(np.float32)


def _hgrn_kernel(q_ref, f_ref, i_ref, g_ref, lb_ref, nw_ref, mall_ref, mask_ref, o_ref, st_ref, e_ref, *, n_chunks):
    c = HG_CHUNK
    nl = len(HG_LEVELS)

    @pl.when(pl.program_id(1) == 0)
    def _():
        st_ref[...] = jnp.zeros_like(st_ref)

    lb = lb_ref[...]
    nw = nw_ref[...]

    heads = [slice(h * HG_D, (h + 1) * HG_D) for h in range(HG_HEADS)]

    def intra(r0, u):
        fp = f_ref[pl.ds(r0, c), :]
        f = lb + (1.0 - lb) * _sigmoid(fp)
        logf = jnp.log(jnp.maximum(f, TINY))
        kk = (1.0 - lb) * _sigmoid(-fp)
        g_hi = logf.astype(BF16)
        g_lo = (logf - g_hi.astype(F32)).astype(BF16)
        e_ref[u] = _dot(mall_ref[...], jnp.concatenate([g_hi, g_lo], axis=1))
        q = q_ref[pl.ds(r0, c), :]

        def expo(row):
            return e_ref[u, row * c:(row + 1) * c, 0:HEAD_W] + e_ref[u, row * c:(row + 1) * c, HEAD_W:2 * HEAD_W]

        b = expo(0)
        b_last = b[c - 1:c, :]
        q_in = (q * jnp.exp(b)).astype(BF16)
        k_out = (kk * jnp.exp(b_last - b)).astype(BF16)
        qls, kls = [], []
        for l in range(nl):
            decay = jnp.exp(expo(1 + l))
            qls.append((q * decay).astype(BF16))
            kls.append((kk * decay).astype(BF16))
        qls.append(q.astype(BF16))
        kls.append(kk.astype(BF16))
        sls = [[_dot_nt(qls[l][:, hs], kls[l][:, hs]) for l in range(nl + 1)] for hs in heads]
        ss = []
        for h in range(HG_HEADS):
            s = jnp.zeros((c, c), F32)
            for l in range(nl + 1):
                s = s + jnp.where(mask_ref[l] > 0.0, sls[h][l], 0.0)
            ss.append(s.astype(BF16))
        v = i_ref[pl.ds(r0, c), :].astype(BF16)
        o_intra = [_dot(ss[h], v[:, hs]) for h, hs in enumerate(heads)]
        kv = [_dot(v[:, hs].astype(F32).T.astype(BF16), k_out[:, hs]) for hs in heads]
        return q_in, jnp.exp(b_last), o_intra, kv

    def step(ci, carry):
        parts = [intra(pl.multiple_of((ci * HG_UNROLL + u) * c, c), u) for u in range(HG_UNROLL)]
        for u, (q_in, st_decay, o_intra, kv) in enumerate(parts):
            r0 = pl.multiple_of((ci * HG_UNROLL + u) * c, c)
            og = g_ref[pl.ds(r0, c), :]
            outs = [_dot_nt(q_in[:, hs], st_ref[h].astype(BF16)) + o_intra[h] for h, hs in enumerate(heads)]
            for h, hs in enumerate(heads):
                st_ref[h] = st_ref[h] * st_decay[:, hs] + kv[h]
            for h, hs in enumerate(heads):
                o = outs[h]
                y = o * lax.rsqrt(jnp.mean(o * o, axis=-1, keepdims=True) + RMS_EPS) * nw
                gh = og[:, hs]
                o_ref[pl.ds(r0, c), hs] = (y * (gh * _sigmoid(gh))).astype(o_ref.dtype)
        return carry

    lax.fori_loop(0, n_chunks // HG_UNROLL, step, 0)


def _hgrn(proj_f, lb, norm_w, batch, seq, rows):
    m_all, masks = _hgrn_constants()
    n_rb = seq // rows
    n_chunks = rows // HG_CHUNK
    assert n_chunks % HG_UNROLL == 0

    def col(c0):
        return pl.BlockSpec((rows, HEAD_W), lambda b, r, cidx=c0 // HEAD_W: (b * n_rb + r, cidx))

    return pl.pallas_call(
        functools.partial(_hgrn_kernel, n_chunks=n_chunks),
        out_shape=jax.ShapeDtypeStruct((batch * seq, HEAD_W), BF16),
        grid=(batch, n_rb),
        in_specs=[
            col(FCOL_HQ), col(FCOL_HF), col(FCOL_HI), col(FCOL_HG),
            pl.BlockSpec((1, HEAD_W), lambda b, r: (0, 0)),
            pl.BlockSpec((1, HG_D), lambda b, r: (0, 0)),
            pl.BlockSpec(m_all.shape, lambda b, r: (0, 0)),
            pl.BlockSpec(masks.shape, lambda b, r: (0, 0, 0)),
        ],
        out_specs=pl.BlockSpec((rows, HEAD_W), lambda b, r: (b * n_rb + r, 0)),
        scratch_shapes=[
            pltpu.VMEM((HG_HEADS, HG_D, HG_D), F32),
            pltpu.VMEM((HG_UNROLL, m_all.shape[0], 2 * HEAD_W), F32),
        ],
        compiler_params=_cparams(("parallel", "arbitrary")),
        name="hgrn",
    )(proj_f, proj_f, proj_f, proj_f, lb, norm_w, jnp.asarray(m_all, BF16), jnp.asarray(masks, F32))


def _rel_bucket(dist):
    n = jnp.maximum(dist, 0)
    exact = REL_BUCKETS // 2
    nf = jnp.maximum(n, exact).astype(F32)
    large = exact + (jnp.log(nf / exact) / math.log(REL_MAX_DIST / exact) * (REL_BUCKETS - exact)).astype(jnp.int32)
    large = jnp.minimum(large, REL_BUCKETS - 1)
    return jnp.where(n < exact, n, large)


def _n_bias_tiles(tile):
    d = 0
    while d * tile - (tile - 1) < REL_FAR_DIST:
        d += 1
    return d + 1


def _bias_tiles_t(table, tile):
    n = _n_bias_tiles(tile)
    length = n * tile
    by_dist = (table[_rel_bucket(jnp.arange(length, dtype=jnp.int32))].astype(F32) * LOG2E).T
    heads = by_dist.shape[0]
    w = jnp.concatenate([by_dist, jnp.zeros((heads, 1), F32)], axis=1)
    skew = jnp.broadcast_to(w[:, None, :], (heads, tile, length + 1)).reshape(heads, tile * (length + 1))
    skew = skew[:, :tile * length].reshape(heads, tile, length)
    return skew.reshape(heads, tile, n, tile).transpose(0, 2, 1, 3)


def _moba_kernel(q_ref, k_ref, v_ref, bias_ref, o_ref, vt_ref, km_ref, sel_ref, m_ref, l_ref, acc_ref,
                 *, n_blocks, n_sel, n_bias):
    blk, dh, nh = MOBA_BLOCK, MOBA_DH, MOBA_HEADS
    qb = pl.program_id(1)
    scale2 = dh ** -0.5 * LOG2E

    @pl.when(qb == 0)
    def _():
        for h in range(nh):
            for n in range(n_blocks):
                kblk = k_ref[n * blk:(n + 1) * blk, h * dh:(h + 1) * dh].astype(F32)
                km_ref[h, n:n + 1, :] = jnp.mean(kblk, axis=0, keepdims=True)
                vt_ref[h * dh:(h + 1) * dh, n * blk:(n + 1) * blk] = _t_bf16(v_ref[n * blk:(n + 1) * blk, h * dh:(h + 1) * dh])

    q_t = []
    n_iota = lax.broadcasted_iota(jnp.int32, (n_blocks, blk), 0)
    for h in range(nh):
        qt = _t_bf16(q_ref[:, h * dh:(h + 1) * dh])
        q_t.append(qt)
        gate = _dot(km_ref[h].astype(BF16), qt)
        gate = jnp.where(n_iota < qb, gate, NEG_BIG)
        rank = jnp.zeros(gate.shape, F32)
        for n in range(n_blocks):
            row = gate[n:n + 1, :]
            beats = (row > gate) | ((row == gate) & (n < n_iota))
            rank = rank + jnp.where(beats, 1.0, 0.0)
        sel_ref[h] = jnp.where((rank < n_sel) & (n_iota < qb), 0.0, NEG_BIG)
        m_ref[h] = jnp.full((1, blk), -jnp.inf, F32)
        l_ref[h] = jnp.zeros((1, blk), F32)
        acc_ref[h] = jnp.zeros((dh, blk), F32)

    def attend(r0, nk, mask_fn):
        ss = [_dot(k_ref[pl.ds(r0, nk), h * dh:(h + 1) * dh], q_t[h]) for h in range(nh)]
        ps, alphas = [], []
        for h in range(nh):
            s = ss[h] * scale2 + mask_fn(h)
            m_old = m_ref[h]
            m_new = jnp.maximum(m_old, _reduce_rows(s, jnp.max))
            alphas.append(jnp.exp2(m_old - m_new))
            p = jnp.exp2(s - m_new)
            l_ref[h] = alphas[h] * l_ref[h] + _reduce_rows(p, jnp.sum)
            m_ref[h] = m_new
            ps.append(p.astype(BF16))
        for h in range(nh):
            acc_ref[h] = alphas[h] * acc_ref[h] + _dot(vt_ref[h * dh:(h + 1) * dh, pl.ds(r0, nk)], ps[h])

    def past_mask(h, n):
        return bias_ref[h, jnp.minimum(qb - n, n_bias - 1)] + sel_ref[h, pl.ds(n, 1), :]

    causal = jnp.where(lax.broadcasted_iota(jnp.int32, (blk, blk), 0) <= lax.broadcasted_iota(jnp.int32, (blk, blk), 1),
                       0.0, NEG_BIG)

    def own_mask(h):
        return bias_ref[h, 0] + causal

    group = MOBA_STEP_BLOCKS

    def past_group(i, carry):
        n = group * i
        attend(pl.multiple_of(n * blk, group * blk), group * blk,
               lambda h: jnp.concatenate([past_mask(h, n + u) for u in range(group)], axis=0))
        return carry

    lax.fori_loop(0, qb // group, past_group, 0)

    for k in range(group):
        @pl.when(qb % group == k)
        def _(k=k):
            attend(pl.multiple_of((qb - k) * blk, blk), (k + 1) * blk,
                   lambda h: jnp.concatenate([past_mask(h, qb - k + u) for u in range(k)] + [own_mask(h)], axis=0))

    for h in range(nh):
        o_ref[:, h * dh:(h + 1) * dh] = (acc_ref[h] / l_ref[h]).T.astype(o_ref.dtype)


def _moba(proj_b, bias_tiles, batch, seq):
    blk = MOBA_BLOCK
    nb = seq // blk
    n_sel = min(MOBA_TOPK, nb - 1)
    n_bias = bias_tiles.shape[1]
    return pl.pallas_call(
        functools.partial(_moba_kernel, n_blocks=nb, n_sel=n_sel, n_bias=n_bias),
        out_shape=jax.ShapeDtypeStruct((batch * seq, HEAD_W), BF16),
        grid=(batch, nb),
        in_specs=[
            pl.BlockSpec((blk, HEAD_W), lambda b, i: (b * nb + i, BCOL_MQ // HEAD_W)),
            pl.BlockSpec((seq, HEAD_W), lambda b, i: (b, BCOL_MK // HEAD_W)),
            pl.BlockSpec((seq, HEAD_W), lambda b, i: (b, BCOL_MV // HEAD_W)),
            pl.BlockSpec(bias_tiles.shape, lambda b, i: (0, 0, 0, 0)),
        ],
        out_specs=pl.BlockSpec((blk, HEAD_W), lambda b, i: (b * nb + i, 0)),
        scratch_shapes=[
            pltpu.VMEM((HEAD_W, seq), BF16),
            pltpu.VMEM((MOBA_HEADS, nb, MOBA_DH), F32),
            pltpu.VMEM((MOBA_HEADS, nb, blk), F32),
            pltpu.VMEM((MOBA_HEADS, 1, blk), F32),
            pltpu.VMEM((MOBA_HEADS, 1, blk), F32),
            pltpu.VMEM((MOBA_HEADS, MOBA_DH, blk), F32),
        ],
        compiler_params=_cparams(("parallel", "arbitrary")),
        name="moba",
    )(proj_b, proj_b, proj_b, bias_tiles)


def _sort_key(x):
    bits = pltpu.bitcast(x, jnp.int32)
    return bits ^ ((bits >> 31) & 0x7FFFFFFF)


def _dsa_kernel(dq_ref, iq_ref, iwq_ref, dkv_ref, ikw_ref, kvg_ref, lnw_ref, lnb_ref, wuk_ref, wuv_ref, bias_ref,
                o_ref, ckv_ref, ckvt_ref, ikn_ref, key_ref, mask_ref, tau_ref, cge_ref, m_ref, l_ref, acc_ref,
                *, seq, n_keep, n_bias):
    tq, kc, nh, dh = DSA_QB, DSA_KC, DSA_HEADS, DSA_DH
    sub = kc // LANES
    qb = pl.program_id(1)
    t0 = qb * tq
    n_kc = (t0 + tq - 1) // kc + 1
    n_rest = (seq - n_kc * kc).astype(F32)

    @pl.when(qb == 0)
    def _():
        for c in range(seq // kc):
            rows = slice(c * kc, (c + 1) * kc)
            x = dkv_ref[rows, :]
            cn = x * lax.rsqrt(jnp.mean(x * x, axis=-1, keepdims=True) + RMS_EPS) * kvg_ref[...]
            ckv_ref[rows, :] = cn.astype(BF16)
            ckvt_ref[:, rows] = cn.T.astype(BF16)
            ik = ikw_ref[rows, 0:IDX_DIM]
            xc = ik - jnp.mean(ik, axis=-1, keepdims=True)
            ikn = xc * lax.rsqrt(jnp.mean(xc * xc, axis=-1, keepdims=True) + RMS_EPS) * lnw_ref[...] + lnb_ref[...]
            ikn_ref[rows, :] = ikn.astype(BF16)

    t_row = t0 + lax.broadcasted_iota(jnp.int32, (1, tq), 1)
    key_off = lax.broadcasted_iota(jnp.int32, (kc, tq), 0)

    def causal_of(c0):
        return (c0 + key_off) <= t_row

    iq_t = iq_ref[...].astype(F32).T
    iq_cat = jnp.concatenate([iq_t[h * IDX_DIM:(h + 1) * IDX_DIM, :] for h in range(IDX_HEADS)], axis=1).astype(BF16)
    iw_t = iwq_ref[...].T[IDX_DIM:IDX_DIM + IDX_HEADS, :] * (IDX_HEADS ** -0.5)
    iw_t = iw_t * (IDX_DIM ** -0.5)

    hpg = 2 * LANES // tq

    def score_chunks(c, nc):
        c0s = [pl.multiple_of((c + u) * kc, kc) for u in range(nc)]
        s_grp = [[_dot(ikn_ref[pl.ds(c0, kc), :], iq_cat[:, g * hpg * tq:(g + 1) * hpg * tq])
                  for g in range(IDX_HEADS // hpg)] for c0 in c0s]
        for u, c0 in enumerate(c0s):
            sc = jnp.zeros((kc, tq), F32)
            for h in range(IDX_HEADS):
                s_h = s_grp[u][h // hpg][:, (h % hpg) * tq:(h % hpg + 1) * tq]
                sc = sc + iw_t[h:h + 1, :] * jnp.maximum(s_h, 0.0)
            sc = jnp.where(causal_of(c0), sc, NEG_BIG)
            sc = jnp.where(sc == 0.0, 0.0, sc)
            key_ref[pl.ds(c0, kc), :] = _sort_key(sc)

    def score_pair(i, carry):
        score_chunks(2 * i, 2)
        return carry

    lax.fori_loop(0, n_kc // 2, score_pair, 0)

    @pl.when(n_kc % 2 == 1)
    def _():
        score_chunks(n_kc - 1, 1)

    def count(pred_fn):
        def body(c, tot):
            c0 = pl.multiple_of(c * kc, kc)
            hit = pred_fn(key_ref[pl.ds(c0, kc), :], c0)
            return tot + _slab(jnp.where(hit, 1.0, 0.0), jnp.sum)
        part = lax.fori_loop(0, n_kc, body, jnp.zeros((ROW_SLAB, tq), F32))
        return _reduce_rows(part, jnp.sum)

    def bit_step(k, carry):
        base, c_base = carry
        cand = base + jnp.left_shift(jnp.int32(1), 31 - k)
        cnt = count(lambda keys, c0: keys >= cand) + jnp.where(cand <= KEY_NEG, n_rest, 0.0)
        take = cnt >= n_keep
        return jnp.where(take, cand, base), jnp.where(take, cnt, c_base)

    tau, c_ge = lax.fori_loop(0, DSA_SURE_BITS, bit_step, (jnp.full((1, tq), jnp.iinfo(jnp.int32).min, jnp.int32),
                                                           jnp.full((1, tq), float(seq), F32)))
    tau_ref[...] = tau
    cge_ref[...] = c_ge

    @pl.when(jnp.max(jnp.abs(c_ge - n_keep)) > 0.0)
    def _():
        tau_ref[...], cge_ref[...] = lax.fori_loop(DSA_SURE_BITS, 32, bit_step, (tau, c_ge))

    tau = tau_ref[...]
    c_ge = cge_ref[...]

    c_gt = count(lambda keys, c0: keys > tau) + jnp.where(tau < KEY_NEG, n_rest, 0.0)
    need = n_keep - c_gt
    n_eq = (c_ge - c_gt) - jnp.where(tau == KEY_NEG, (seq - 1 - t_row).astype(F32), 0.0)
    tie_break = jnp.max(n_eq - need) > 0.0

    @pl.when(jnp.logical_not(tie_break))
    def _():
        def body(c, carry):
            c0 = pl.multiple_of(c * kc, kc)
            keep = (key_ref[pl.ds(c0, kc), :] >= tau) & causal_of(c0)
            mask_ref[pl.ds(c0, kc), :] = jnp.where(keep, 0.0, NEG_BIG)
            return carry
        lax.fori_loop(0, n_kc, body, 0)

    @pl.when(tie_break)
    def _():
        lower = jnp.where(lax.broadcasted_iota(jnp.int32, (kc, kc), 1)
                          < lax.broadcasted_iota(jnp.int32, (kc, kc), 0), 1.0, 0.0).astype(BF16)

        def body(c, seen):
            c0 = pl.multiple_of(c * kc, kc)
            keys = key_ref[pl.ds(c0, kc), :]
            eq = keys == tau
            eqf = jnp.where(eq, 1.0, 0.0)
            rank = seen + _dot(lower, eqf.astype(BF16))
            keep = ((keys > tau) | (eq & (rank < need))) & causal_of(c0)
            mask_ref[pl.ds(c0, kc), :] = jnp.where(keep, 0.0, NEG_BIG)
            return seen + _reduce_rows(eqf, jnp.sum)
        lax.fori_loop(0, n_kc, body, jnp.zeros((1, tq), F32))

    scale2 = dh ** -0.5 * LOG2E
    qa_cat = jnp.concatenate(
        [_dot_nt(wuk_ref[h], dq_ref[:, h * dh:(h + 1) * dh]) for h in range(nh)], axis=1).astype(BF16)
    m_ref[...] = jnp.full(m_ref.shape, -jnp.inf, F32)
    l_ref[...] = jnp.zeros(l_ref.shape, F32)
    acc_ref[...] = jnp.zeros(acc_ref.shape, F32)

    def attend(c, nc):
        c0 = pl.multiple_of(c * kc, kc)
        nk = nc * kc
        msk = mask_ref[pl.ds(c0, nk), :]
        bias = jnp.concatenate(
            [bias_ref[jnp.clip(qb - (c * sub + g), 0, n_bias - 1)] for g in range(nc * sub)], axis=0)
        s = _dot(ckv_ref[pl.ds(c0, nk), :], qa_cat) * scale2 + bias + jnp.concatenate([msk] * nh, axis=1)
        m_old = m_ref[...]
        m_new = jnp.maximum(m_old, _reduce_rows(s, jnp.max))
        alpha = jnp.exp2(m_old - m_new)
        p = jnp.exp2(s - m_new)
        l_ref[...] = alpha * l_ref[...] + _reduce_rows(p, jnp.sum)
        acc_ref[...] = alpha * acc_ref[...] + _dot(ckvt_ref[:, pl.ds(c0, nk)], p.astype(BF16))
        m_ref[...] = m_new

    def att_pair(i, carry):
        attend(2 * i, 2)
        return carry

    lax.fori_loop(0, n_kc // 2, att_pair, 0)

    @pl.when(n_kc % 2 == 1)
    def _():
        attend(n_kc - 1, 1)

    o_lat_t = (acc_ref[...] / l_ref[...]).astype(BF16)
    for h in range(nh):
        o_ref[:, h * dh:(h + 1) * dh] = _dot(wuv_ref[h], o_lat_t[:, h * tq:(h + 1) * tq]).T.astype(o_ref.dtype)


def _dsa(proj_f, proj_b, kv_gain, ln_w, ln_b, w_uk_h, w_uv_h, bias_tiles, batch, seq):
    tq = DSA_QB
    nq = seq // tq
    n_keep = min(DSA_TOPK_MAX, seq // 4)
    n_bias = bias_tiles.shape[0]
    return pl.pallas_call(
        functools.partial(_dsa_kernel, seq=seq, n_keep=float(n_keep), n_bias=n_bias),
        out_shape=jax.ShapeDtypeStruct((batch * seq, HEAD_W), BF16),
        grid=(batch, nq),
        in_specs=[
            pl.BlockSpec((tq, HEAD_W), lambda b, i: (b * nq + i, BCOL_DQ // HEAD_W)),
            pl.BlockSpec((tq, HEAD_W), lambda b, i: (b * nq + i, BCOL_IQ // HEAD_W)),
            pl.BlockSpec((tq, LANES), lambda b, i: (b * nq + i, FCOL_IKW // LANES)),
            pl.BlockSpec((seq, DSA_KV_RANK), lambda b, i: (b, FCOL_DKV // DSA_KV_RANK)),
            pl.BlockSpec((seq, LANES), lambda b, i: (b, FCOL_IKW // LANES)),
            pl.BlockSpec((1, DSA_KV_RANK), lambda b, i: (0, 0)),
            pl.BlockSpec((1, IDX_DIM), lambda b, i: (0, 0)),
            pl.BlockSpec((1, IDX_DIM), lambda b, i: (0, 0)),
            pl.BlockSpec(w_uk_h.shape, lambda b, i: (0, 0, 0)),
            pl.BlockSpec(w_uv_h.shape, lambda b, i: (0, 0, 0)),
            pl.BlockSpec(bias_tiles.shape, lambda b, i: (0, 0, 0)),
        ],
        out_specs=pl.BlockSpec((tq, HEAD_W), lambda b, i: (b * nq + i, 0)),
        scratch_shapes=[
            pltpu.VMEM((seq, DSA_KV_RANK), BF16),
            pltpu.VMEM((DSA_KV_RANK, seq), BF16),
            pltpu.VMEM((seq, IDX_DIM), BF16),
            pltpu.VMEM((seq, tq), jnp.int32),
            pltpu.VMEM((seq, tq), F32),
            pltpu.VMEM((1, tq), jnp.int32),
            pltpu.VMEM((1, tq), F32),
            pltpu.VMEM((1, DSA_HEADS * tq), F32),
            pltpu.VMEM((1, DSA_HEADS * tq), F32),
            pltpu.VMEM((DSA_KV_RANK, DSA_HEADS * tq), F32),
        ],
        compiler_params=_cparams(("parallel", "arbitrary")),
        name="dsa",
    )(proj_b, proj_b, proj_f, proj_f, proj_f, kv_gain, ln_w, ln_b, w_uk_h, w_uv_h, bias_tiles)


def _merge_kernel(x_ref, ya_ref, yb_ref, yc_ref, g0_ref, g1_ref, g2_ref, wa_ref, wb_ref, wc_ref, wo_ref, gain_ref,
                  o_ref):
    merged = (_sigmoid(g0_ref[...]) * _dot(ya_ref[...], wa_ref[...])
              + _sigmoid(g1_ref[...]) * _dot(yb_ref[...], wb_ref[...])
              + _sigmoid(g2_ref[...]) * _dot(yc_ref[...], wc_ref[...]))
    z = _dot(merged.astype(BF16), wo_ref[...])
    zn = z * lax.rsqrt(jnp.mean(z * z, axis=-1, keepdims=True) + RMS_EPS) * gain_ref[...]
    o_ref[...] = x_ref[...] + zn


def _merge(x2, ya, yb, yc, proj_f, wa, wb, wc, wo, layer, gain, tm):
    n = x2.shape[0]
    gcol = FCOL_GATES // D_MODEL

    def rows(w):
        return pl.BlockSpec((tm, w), lambda i: (i, 0))

    def gate(k):
        return pl.BlockSpec((tm, D_MODEL), lambda i, k=k: (i, gcol + k))

    def weight(a):
        return pl.BlockSpec((None,) + a.shape[1:], lambda i: (layer, 0, 0))

    return pl.pallas_call(
        _merge_kernel,
        out_shape=jax.ShapeDtypeStruct((n, D_MODEL), F32),
        grid=(n // tm,),
        in_specs=[rows(D_MODEL), rows(HEAD_W), rows(HEAD_W), rows(HEAD_W), gate(0), gate(1), gate(2),
                  weight(wa), weight(wb), weight(wc), weight(wo), pl.BlockSpec(gain.shape, lambda i: (0, 0))],
        out_specs=rows(D_MODEL),
        compiler_params=_cparams(("parallel",)),
        name="merge",
    )(x2, ya, yb, yc, proj_f, proj_f, proj_f, wa, wb, wc, wo, gain)


def _ffn_kernel(x_ref, g1_ref, wg_ref, wu_ref, wd_ref, g2_ref, o_ref, hn_ref, acc_ref):
    j = pl.program_id(1)

    @pl.when(j == 0)
    def _():
        x = x_ref[...]
        hn_ref[...] = (x * lax.rsqrt(jnp.mean(x * x, axis=-1, keepdims=True) + RMS_EPS) * g1_ref[...]).astype(BF16)
        acc_ref[...] = jnp.zeros_like(acc_ref)

    hn = hn_ref[...]
    gate = _dot(hn, wg_ref[...])
    up = _dot(hn, wu_ref[...])
    act = (gate * _sigmoid(gate)) * up
    acc_ref[...] += _dot(act.astype(BF16), wd_ref[...])

    @pl.when(j == pl.num_programs(1) - 1)
    def _():
        y = acc_ref[...]
        yn = y * lax.rsqrt(jnp.mean(y * y, axis=-1, keepdims=True) + RMS_EPS) * g2_ref[...]
        o_ref[...] = x_ref[...] + yn


def _ffn(x2, g1, w_in, w_out, layer, g2, tm, tf):
    n = x2.shape[0]
    nf = D_FF // tf
    return pl.pallas_call(
        _ffn_kernel,
        out_shape=jax.ShapeDtypeStruct((n, D_MODEL), F32),
        grid=(n // tm, nf),
        in_specs=[
            pl.BlockSpec((tm, D_MODEL), lambda i, j: (i, 0)),
            pl.BlockSpec((1, D_MODEL), lambda i, j: (0, 0)),
            pl.BlockSpec((None, D_MODEL, tf), lambda i, j: (layer, 0, j)),
            pl.BlockSpec((None, D_MODEL, tf), lambda i, j: (layer, 0, nf + j)),
            pl.BlockSpec((None, tf, D_MODEL), lambda i, j: (layer, j, 0)),
            pl.BlockSpec((1, D_MODEL), lambda i, j: (0, 0)),
        ],
        out_specs=pl.BlockSpec((tm, D_MODEL), lambda i, j: (i, 0)),
        scratch_shapes=[pltpu.VMEM((tm, D_MODEL), BF16), pltpu.VMEM((tm, D_MODEL), F32)],
        compiler_params=_cparams(("parallel", "arbitrary")),
        name="ffn",
    )(x2, g1, w_in, w_in, w_out, g2)


def _pack_w_in(w):
    starts = dict(zip(IN_NAMES, np.cumsum((0,) + IN_SIZES[:-1]).tolist()))
    o_mq, o_dq, o_dkv, o_iq, o_ik, o_gates = (starts[k] for k in ("mq", "dq", "dkv", "iq", "ik", "gates"))
    assert w.shape[-1] == sum(IN_SIZES)
    w = w.astype(BF16)
    z = lambda k: jnp.zeros(w.shape[:-1] + (k,), w.dtype)
    w_f = jnp.concatenate([
        w[..., :o_mq],
        w[..., o_gates:],
        w[..., o_dkv:o_iq],
        w[..., o_ik:o_gates],
        z(PROJ_F_W - FCOL_IKW - (o_gates - o_ik)),
    ], axis=-1)
    w_b = jnp.concatenate([
        w[..., o_mq:o_dq],
        w[..., o_dq:o_dkv],
        w[..., o_iq:o_ik],
    ], axis=-1)
    assert w_f.shape[-1] == PROJ_F_W and w_b.shape[-1] == PROJ_B_W
    return w_f, w_b


def _tiles(batch, seq):
    n = batch * seq
    return dict(
        proj_tm=min(2048, n), proj_f_tn=PROJ_F_W // 4, proj_b_tm=min(2048, n), proj_b_tn=PROJ_B_W,
        hgrn_rows=min(512, seq),
        merge_tm=min(512, n),
        ffn_tm=min(1024, n), ffn_tf=D_FF // 2,
    )


def kernel(x, w_in, pre_mix_norm, post_mix_norm, pre_ffn_norm, post_ffn_norm, hg_lb_logits, hg_norm_w, dsa_kv_norm,
           dsa_w_uk, dsa_w_uv, idx_k_norm_w, idx_k_norm_b, rel_bias, w_branch_a, w_branch_b, w_branch_c, w_out,
           w_ffn_in, w_ffn_out):
    batch, seq, _ = x.shape
    depth = w_in.shape[0]
    t = _tiles(batch, seq)
    lbp = jax.nn.softmax(hg_lb_logits.astype(F32), axis=0)
    lower_bounds = jnp.cumsum(lbp, axis=0) - lbp[0]
    bias_b = _bias_tiles_t(rel_bias[:, :MOBA_HEADS], MOBA_BLOCK)
    bias_c = _bias_tiles_t(rel_bias[:, MOBA_HEADS:], LANES)
    bias_c = bias_c.transpose(1, 2, 0, 3).reshape(bias_c.shape[1], LANES, DSA_HEADS * LANES)
    row = lambda a: a.reshape(1, -1).astype(F32)

    w_f, w_b = _pack_w_in(w_in)
    w_uk = jnp.transpose(dsa_w_uk, (0, 2, 1, 3)).astype(BF16)
    w_uv = jnp.transpose(dsa_w_uv, (0, 2, 3, 1)).astype(BF16)
    w_a, w_bb, w_c, w_o = (a.astype(BF16) for a in (w_branch_a, w_branch_b, w_branch_c, w_out))
    w_fi, w_fo = w_ffn_in.astype(BF16), w_ffn_out.astype(BF16)

    x2 = x.reshape(batch * seq, D_MODEL)
    for l in range(depth):
        gain = row(pre_mix_norm[l])
        proj_f = _proj(x2, gain, w_f, l, F32, t["proj_tm"], t["proj_f_tn"], "proj_f")
        proj_b = _proj(x2, gain, w_b, l, BF16, t["proj_b_tm"], t["proj_b_tn"], "proj_b")
        y_a = _hgrn(proj_f, row(lower_bounds[l]), row(hg_norm_w[l]), batch, seq, t["hgrn_rows"])
        y_b = _moba(proj_b, bias_b, batch, seq)
        y_c = _dsa(proj_f, proj_b, row(dsa_kv_norm[l]), row(idx_k_norm_w[l]), row(idx_k_norm_b[l]),
                   w_uk[l], w_uv[l], bias_c, batch, seq)
        x2 = _merge(x2, y_a, y_b, y_c, proj_f, w_a, w_bb, w_c, w_o, l, row(post_mix_norm[l]), t["merge_tm"])
        x2 = _ffn(x2, row(pre_ffn_norm[l]), w_fi, w_fo, l, row(post_ffn_norm[l]), t["ffn_tm"], t["ffn_tf"])
    return x2.reshape(batch, seq, D_MODEL)
```

```python
import functools
import math

import numpy as np
import jax
import jax.numpy as jnp
from jax import lax
from jax.experimental import pallas as pl
from jax.experimental.pallas import tpu as pltpu

F32 = jnp.float32
BF16 = jnp.bfloat16

D_MODEL = 1024
RMS_EPS = 1e-6
LOG2E = 1.0 / math.log(2.0)
NEG_BIG = -1e30
TINY = 1e-20
HG_HEADS = 4
HG_D = 128
HG_CHUNK = 64
MOBA_HEADS = 4
MOBA_DH = 128
MOBA_BLOCK = 256
MOBA_TOPK = 3
MOBA_STEP_BLOCKS = 4
DSA_HEADS = 4
DSA_DH = 128
DSA_KV_RANK = 256
IDX_HEADS = 8
IDX_DIM = 64
DSA_TOPK_MAX = 256
DSA_KC = 512
DSA_SURE_BITS = 28
REL_BUCKETS = 32
REL_MAX_DIST = 2048
D_FF = 2816
HEAD_W = 512

LANES = 128
SUBLANES = 8
DSA_QB = 2 * LANES

IN_NAMES = ("hq", "hf", "hi", "hg", "mq", "mk", "mv", "dq", "dkv", "iq", "ik", "iw", "gates")
IN_SIZES = (HEAD_W, HEAD_W, HEAD_W, HEAD_W, HEAD_W, HEAD_W, HEAD_W, HEAD_W, DSA_KV_RANK,
            IDX_HEADS * IDX_DIM, IDX_DIM, IDX_HEADS, 3 * D_MODEL)

FCOL_HQ, FCOL_HF, FCOL_HI, FCOL_HG = 0, 512, 1024, 1536
FCOL_GATES, FCOL_DKV, FCOL_IKW = 2048, 5120, 5376
PROJ_F_W = 5632
BCOL_MQ, BCOL_MK, BCOL_MV, BCOL_DQ, BCOL_IQ = 0, 512, 1024, 1536, 2048
PROJ_B_W = 2560

VMEM_LIMIT = 56 * 1024 * 1024

_REL_EXACT = REL_BUCKETS // 2
REL_FAR_DIST = int(math.ceil(_REL_EXACT * (REL_MAX_DIST / _REL_EXACT) ** ((REL_BUCKETS - _REL_EXACT - 1) / (REL_BUCKETS - _REL_EXACT)))) + 1

_NEG_BITS = int(np.array(NEG_BIG, np.float32).view(np.int32))
KEY_NEG = int(np.int32(_NEG_BITS ^ 0x7FFFFFFF))


def _cparams(sem):
    return pltpu.CompilerParams(dimension_semantics=sem, vmem_limit_bytes=VMEM_LIMIT)


def _sigmoid(x):
    return 1.0 / (1.0 + jnp.exp(-x))


def _dot(a, b):
    return jnp.dot(a, b, preferred_element_type=F32)


def _dot_nt(a, b):
    return lax.dot_general(a, b, (((1,), (1,)), ((), ())), preferred_element_type=F32)


def _t_bf16(a):
    return a.astype(F32).T.astype(BF16)


ROW_SLAB = 64


def _slab(x, fn):
    return fn(x.reshape(x.shape[0] // ROW_SLAB, ROW_SLAB, x.shape[1]), axis=0)


def _reduce_rows(x, fn):
    y = _slab(x, fn)
    z = fn(y.reshape(ROW_SLAB // SUBLANES, SUBLANES, y.shape[1]), axis=0)
    return fn(z, axis=0, keepdims=True)


def _proj_kernel(x_ref, g_ref, w_ref, o_ref, xn_ref):
    @pl.when(pl.program_id(1) == 0)
    def _():
        x = x_ref[...]
        ms = jnp.mean(x * x, axis=-1, keepdims=True)
        xn_ref[...] = (x * lax.rsqrt(ms + RMS_EPS) * g_ref[...]).astype(BF16)

    o_ref[...] = _dot(xn_ref[...], w_ref[...]).astype(o_ref.dtype)


def _proj(x2, gain, w_packed, layer, out_dtype, tm, tn, name):
    n = x2.shape[0]
    width = w_packed.shape[-1]
    return pl.pallas_call(
        _proj_kernel,
        out_shape=jax.ShapeDtypeStruct((n, width), out_dtype),
        grid=(n // tm, width // tn),
        in_specs=[
            pl.BlockSpec((tm, D_MODEL), lambda i, j: (i, 0)),
            pl.BlockSpec((1, D_MODEL), lambda i, j: (0, 0)),
            pl.BlockSpec((None, D_MODEL, tn), lambda i, j: (layer, 0, j)),
        ],
        out_specs=pl.BlockSpec((tm, tn), lambda i, j: (i, j)),
        scratch_shapes=[pltpu.VMEM((tm, D_MODEL), BF16)],
        compiler_params=_cparams(("parallel", "arbitrary")),
        name=name,
    )(x2, gain, w_packed)


HG_LEVELS = (32, 16, 8, 4, 2, 1)
HG_UNROLL = 8


def _hgrn_constants():
    c = HG_CHUNK
    t = np.arange(c)
    i = t[:, None]
    tt = t[None, :]
    rows = [tt <= i]
    masks = []
    for m in HG_LEVELS:
        mid = ((t // (2 * m)) * (2 * m) + m)[:, None]
        upper = (i >= mid) & (tt >= mid) & (tt <= i)
        lower = (i < mid) & (tt > i) & (tt <= mid - 1)
        rows.append(upper | lower)
        same = (t[:, None] // (2 * m)) == (t[None, :] // (2 * m))
        masks.append(same & (i >= mid) & (tt < mid))
    masks.append(np.eye(c, dtype=bool))
    m_all = np.concatenate(rows, axis=0).astype(np.float32)
    return m_all, np.stack(masks).astype(np.float32)


def _hgrn_kernel(q_ref, f_ref, i_ref, g_ref, lb_ref, nw_ref, mall_ref, mask_ref, o_ref, st_ref, e_ref, *, n_chunks):
    c = HG_CHUNK
    nl = len(HG_LEVELS)

    @pl.when(pl.program_id(1) == 0)
    def _():
        st_ref[...] = jnp.zeros_like(st_ref)

    lb = lb_ref[...]
    nw = nw_ref[...]

    heads = [slice(h * HG_D, (h + 1) * HG_D) for h in range(HG_HEADS)]

    def intra(r0, u):
        fp = f_ref[pl.ds(r0, c), :]
        f = lb + (1.0 - lb) * _sigmoid(fp)
        logf = jnp.log(jnp.maximum(f, TINY))
        kk = (1.0 - lb) * _sigmoid(-fp)
        g_hi = logf.astype(BF16)
        g_lo = (logf - g_hi.astype(F32)).astype(BF16)
        e_ref[u] = _dot(mall_ref[...], jnp.concatenate([g_hi, g_lo], axis=1))
        q = q_ref[pl.ds(r0, c), :]

        def expo(row):
            return e_ref[u, row * c:(row + 1) * c, 0:HEAD_W] + e_ref[u, row * c:(row + 1) * c, HEAD_W:2 * HEAD_W]

        b = expo(0)
        b_last = b[c - 1:c, :]
        q_in = (q * jnp.exp(b)).astype(BF16)
        k_out = (kk * jnp.exp(b_last - b)).astype(BF16)
        qls, kls = [], []
        for l in range(nl):
            decay = jnp.exp(expo(1 + l))
            qls.append((q * decay).astype(BF16))
            kls.append((kk * decay).astype(BF16))
        qls.append(q.astype(BF16))
        kls.append(kk.astype(BF16))
        sls = [[_dot_nt(qls[l][:, hs], kls[l][:, hs]) for l in range(nl + 1)] for hs in heads]
        ss = []
        for h in range(HG_HEADS):
            s = jnp.zeros((c, c), F32)
            for l in range(nl + 1):
                s = s + jnp.where(mask_ref[l] > 0.0, sls[h][l], 0.0)
            ss.append(s.astype(BF16))
        v = i_ref[pl.ds(r0, c), :].astype(BF16)
        o_intra = [_dot(ss[h], v[:, hs]) for h, hs in enumerate(heads)]
        kv = [_dot(v[:, hs].astype(F32).T.astype(BF16), k_out[:, hs]) for hs in heads]
        return q_in, jnp.exp(b_last), o_intra, kv

    def step(ci, carry):
        parts = [intra(pl.multiple_of((ci * HG_UNROLL + u) * c, c), u) for u in range(HG_UNROLL)]
        for u, (q_in, st_decay, o_intra, kv) in enumerate(parts):
            r0 = pl.multiple_of((ci * HG_UNROLL + u) * c, c)
            og = g_ref[pl.ds(r0, c), :]
            outs = [_dot_nt(q_in[:, hs], st_ref[h].astype(BF16)) + o_intra[h] for h, hs in enumerate(heads)]
            for h, hs in enumerate(heads):
                st_ref[h] = st_ref[h] * st_decay[:, hs] + kv[h]
            for h, hs in enumerate(heads):
                o = outs[h]
                y = o * lax.rsqrt(jnp.mean(o * o, axis=-1, keepdims=True) + RMS_EPS) * nw
                gh = og[:, hs]
                o_ref[pl.ds(r0, c), hs] = (y * (gh * _sigmoid(gh))).astype(o_ref.dtype)
        return carry

    lax.fori_loop(0, n_chunks // HG_UNROLL, step, 0)


def _hgrn(proj_f, lb, norm_w, batch, seq, rows):
    m_all, masks = _hgrn_constants()
    n_rb = seq // rows
    n_chunks = rows // HG_CHUNK
    assert n_chunks % HG_UNROLL == 0

    def col(c0):
        return pl.BlockSpec((rows, HEAD_W), lambda b, r, cidx=c0 // HEAD_W: (b * n_rb + r, cidx))

    return pl.pallas_call(
        functools.partial(_hgrn_kernel, n_chunks=n_chunks),
        out_shape=jax.ShapeDtypeStruct((batch * seq, HEAD_W), BF16),
        grid=(batch, n_rb),
        in_specs=[
            col(FCOL_HQ), col(FCOL_HF), col(FCOL_HI), col(FCOL_HG),
            pl.BlockSpec((1, HEAD_W), lambda b, r: (0, 0)),
            pl.BlockSpec((1, HG_D), lambda b, r: (0, 0)),
            pl.BlockSpec(m_all.shape, lambda b, r: (0, 0)),
            pl.BlockSpec(masks.shape, lambda b, r: (0, 0, 0)),
        ],
        out_specs=pl.BlockSpec((rows, HEAD_W), lambda b, r: (b * n_rb + r, 0)),
        scratch_shapes=[
            pltpu.VMEM((HG_HEADS, HG_D, HG_D), F32),
            pltpu.VMEM((HG_UNROLL, m_all.shape[0], 2 * HEAD_W), F32),
        ],
        compiler_params=_cparams(("parallel", "arbitrary")),
        name="hgrn",
    )(proj_f, proj_f, proj_f, proj_f, lb, norm_w, jnp.asarray(m_all, BF16), jnp.asarray(masks, F32))


def _rel_bucket(dist):
    n = jnp.maximum(dist, 0)
    exact = REL_BUCKETS // 2
    nf = jnp.maximum(n, exact).astype(F32)
    large = exact + (jnp.log(nf / exact) / math.log(REL_MAX_DIST / exact) * (REL_BUCKETS - exact)).astype(jnp.int32)
    large = jnp.minimum(large, REL_BUCKETS - 1)
    return jnp.where(n < exact, n, large)


def _n_bias_tiles(tile):
    d = 0
    while d * tile - (tile - 1) < REL_FAR_DIST:
        d += 1
    return d + 1


def _bias_tiles_t(table, tile):
    n = _n_bias_tiles(tile)
    length = n * tile
    by_dist = (table[_rel_bucket(jnp.arange(length, dtype=jnp.int32))].astype(F32) * LOG2E).T
    heads = by_dist.shape[0]
    w = jnp.concatenate([by_dist, jnp.zeros((heads, 1), F32)], axis=1)
    skew = jnp.broadcast_to(w[:, None, :], (heads, tile, length + 1)).reshape(heads, tile * (length + 1))
    skew = skew[:, :tile * length].reshape(heads, tile, length)
    return skew.reshape(heads, tile, n, tile).transpose(0, 2, 1, 3)


def _bias_tiles_rect(table, rows, cols):
    n = 1
    while (n - 2) * rows - (rows - 1) < REL_FAR_DIST:
        n += 1
    length = n * rows + cols
    by_dist = (table[_rel_bucket(jnp.arange(length - rows, dtype=jnp.int32))].astype(F32) * LOG2E).T
    heads = by_dist.shape[0]
    w = jnp.concatenate([jnp.zeros((heads, rows), F32), by_dist, jnp.zeros((heads, 1), F32)], axis=1)
    skew = jnp.broadcast_to(w[:, None, :], (heads, rows, length + 1)).reshape(heads, rows * (length + 1))
    skew = skew[:, :rows * length].reshape(heads, rows, length)
    tiles = jnp.stack([skew[:, :, d * rows:d * rows + cols] for d in range(n)], axis=0)
    return tiles.transpose(0, 2, 1, 3).reshape(n, rows, heads * cols)


def _moba_kernel(q_ref, k_ref, v_ref, bias_ref, o_ref, vt_ref, km_ref, sel_ref, m_ref, l_ref, acc_ref,
                 *, n_blocks, n_sel, n_bias):
    blk, dh, nh = MOBA_BLOCK, MOBA_DH, MOBA_HEADS
    qb = pl.program_id(1)
    scale2 = dh ** -0.5 * LOG2E

    @pl.when(qb == 0)
    def _():
        for h in range(nh):
            for n in range(n_blocks):
                kblk = k_ref[n * blk:(n + 1) * blk, h * dh:(h + 1) * dh].astype(F32)
                km_ref[h, n:n + 1, :] = jnp.mean(kblk, axis=0, keepdims=True)
                vt_ref[h * dh:(h + 1) * dh, n * blk:(n + 1) * blk] = _t_bf16(v_ref[n * blk:(n + 1) * blk, h * dh:(h + 1) * dh])

    q_t = []
    n_iota = lax.broadcasted_iota(jnp.int32, (n_blocks, blk), 0)
    for h in range(nh):
        qt = _t_bf16(q_ref[:, h * dh:(h + 1) * dh])
        q_t.append(qt)
        gate = _dot(km_ref[h].astype(BF16), qt)
        gate = jnp.where(n_iota < qb, gate, NEG_BIG)
        rank = jnp.zeros(gate.shape, F32)
        for n in range(n_blocks):
            row = gate[n:n + 1, :]
            beats = (row > gate) | ((row == gate) & (n < n_iota))
            rank = rank + jnp.where(beats, 1.0, 0.0)
        sel_ref[h] = jnp.where((rank < n_sel) & (n_iota < qb), 0.0, NEG_BIG)
        m_ref[h] = jnp.full((1, blk), -jnp.inf, F32)
        l_ref[h] = jnp.zeros((1, blk), F32)
        acc_ref[h] = jnp.zeros((dh, blk), F32)

    def attend(r0, nk, mask_fn):
        ss = [_dot(k_ref[pl.ds(r0, nk), h * dh:(h + 1) * dh], q_t[h]) for h in range(nh)]
        ps, alphas = [], []
        for h in range(nh):
            s = ss[h] * scale2 + mask_fn(h)
            m_old = m_ref[h]
            m_new = jnp.maximum(m_old, _reduce_rows(s, jnp.max))
            alphas.append(jnp.exp2(m_old - m_new))
            p = jnp.exp2(s - m_new)
            l_ref[h] = alphas[h] * l_ref[h] + _reduce_rows(p, jnp.sum)
            m_ref[h] = m_new
            ps.append(p.astype(BF16))
        for h in range(nh):
            acc_ref[h] = alphas[h] * acc_ref[h] + _dot(vt_ref[h * dh:(h + 1) * dh, pl.ds(r0, nk)], ps[h])

    def past_mask(h, n):
        return bias_ref[h, jnp.minimum(qb - n, n_bias - 1)] + sel_ref[h, pl.ds(n, 1), :]

    causal = jnp.where(lax.broadcasted_iota(jnp.int32, (blk, blk), 0) <= lax.broadcasted_iota(jnp.int32, (blk, blk), 1),
                       0.0, NEG_BIG)

    def own_mask(h):
        return bias_ref[h, 0] + causal

    group = MOBA_STEP_BLOCKS

    def past_group(i, carry):
        n = group * i
        attend(pl.multiple_of(n * blk, group * blk), group * blk,
               lambda h: jnp.concatenate([past_mask(h, n + u) for u in range(group)], axis=0))
        return carry

    lax.fori_loop(0, qb // group, past_group, 0)

    for k in range(group):
        @pl.when(qb % group == k)
        def _(k=k):
            attend(pl.multiple_of((qb - k) * blk, blk), (k + 1) * blk,
                   lambda h: jnp.concatenate([past_mask(h, qb - k + u) for u in range(k)] + [own_mask(h)], axis=0))

    for h in range(nh):
        o_ref[:, h * dh:(h + 1) * dh] = (acc_ref[h] / l_ref[h]).T.astype(o_ref.dtype)


def _moba(proj_b, bias_tiles, batch, seq):
    blk = MOBA_BLOCK
    nb = seq // blk
    n_sel = min(MOBA_TOPK, nb - 1)
    n_bias = bias_tiles.shape[1]
    return pl.pallas_call(
        functools.partial(_moba_kernel, n_blocks=nb, n_sel=n_sel, n_bias=n_bias),
        out_shape=jax.ShapeDtypeStruct((batch * seq, HEAD_W), BF16),
        grid=(batch, nb),
        in_specs=[
            pl.BlockSpec((blk, HEAD_W), lambda b, i: (b * nb + i, BCOL_MQ // HEAD_W)),
            pl.BlockSpec((seq, HEAD_W), lambda b, i: (b, BCOL_MK // HEAD_W)),
            pl.BlockSpec((seq, HEAD_W), lambda b, i: (b, BCOL_MV // HEAD_W)),
            pl.BlockSpec(bias_tiles.shape, lambda b, i: (0, 0, 0, 0)),
        ],
        out_specs=pl.BlockSpec((blk, HEAD_W), lambda b, i: (b * nb + i, 0)),
        scratch_shapes=[
            pltpu.VMEM((HEAD_W, seq), BF16),
            pltpu.VMEM((MOBA_HEADS, nb, MOBA_DH), F32),
            pltpu.VMEM((MOBA_HEADS, nb, blk), F32),
            pltpu.VMEM((MOBA_HEADS, 1, blk), F32),
            pltpu.VMEM((MOBA_HEADS, 1, blk), F32),
            pltpu.VMEM((MOBA_HEADS, MOBA_DH, blk), F32),
        ],
        compiler_params=_cparams(("parallel", "arbitrary")),
        name="moba",
    )(proj_b, proj_b, proj_b, bias_tiles)


def _sort_key(x):
    bits = pltpu.bitcast(x, jnp.int32)
    return bits ^ ((bits >> 31) & 0x7FFFFFFF)


def _dsa_kernel(dq_ref, iq_ref, iwq_ref, dkv_ref, ikw_ref, kvg_ref, lnw_ref, lnb_ref, wuk_ref, wuv_ref, bias_ref,
                o_ref, ckv_ref, ckvt_ref, ikn_ref, key_ref, mask_ref, tau_ref, cge_ref, m_ref, l_ref, acc_ref,
                *, seq, n_keep, n_bias):
    tq, kc, nh, dh = DSA_QB, DSA_KC, DSA_HEADS, DSA_DH
    sub = kc // LANES
    qb = pl.program_id(1)
    t0 = qb * tq
    n_kc = (t0 + tq - 1) // kc + 1
    n_rest = (seq - n_kc * kc).astype(F32)

    @pl.when(qb == 0)
    def _():
        for c in range(seq // kc):
            rows = slice(c * kc, (c + 1) * kc)
            x = dkv_ref[rows, :]
            cn = x * lax.rsqrt(jnp.mean(x * x, axis=-1, keepdims=True) + RMS_EPS) * kvg_ref[...]
            ckv_ref[rows, :] = cn.astype(BF16)
            ckvt_ref[:, rows] = cn.T.astype(BF16)
            ik = ikw_ref[rows, 0:IDX_DIM]
            xc = ik - jnp.mean(ik, axis=-1, keepdims=True)
            ikn = xc * lax.rsqrt(jnp.mean(xc * xc, axis=-1, keepdims=True) + RMS_EPS) * lnw_ref[...] + lnb_ref[...]
            ikn_ref[rows, :] = ikn.astype(BF16)

    t_row = t0 + lax.broadcasted_iota(jnp.int32, (1, tq), 1)
    key_off = lax.broadcasted_iota(jnp.int32, (kc, tq), 0)

    def causal_of(c0):
        return (c0 + key_off) <= t_row

    iq_t = iq_ref[...].astype(F32).T
    iq_cat = jnp.concatenate([iq_t[h * IDX_DIM:(h + 1) * IDX_DIM, :] for h in range(IDX_HEADS)], axis=1).astype(BF16)
    iw_t = iwq_ref[...].T[IDX_DIM:IDX_DIM + IDX_HEADS, :] * (IDX_HEADS ** -0.5)
    iw_t = iw_t * (IDX_DIM ** -0.5)

    hpg = 2 * LANES // tq

    def score_chunks(c, nc):
        c0s = [pl.multiple_of((c + u) * kc, kc) for u in range(nc)]
        s_grp = [[_dot(ikn_ref[pl.ds(c0, kc), :], iq_cat[:, g * hpg * tq:(g + 1) * hpg * tq])
                  for g in range(IDX_HEADS // hpg)] for c0 in c0s]
        for u, c0 in enumerate(c0s):
            sc = jnp.zeros((kc, tq), F32)
            for h in range(IDX_HEADS):
                s_h = s_grp[u][h // hpg][:, (h % hpg) * tq:(h % hpg + 1) * tq]
                sc = sc + iw_t[h:h + 1, :] * jnp.maximum(s_h, 0.0)
            sc = jnp.where(causal_of(c0), sc, NEG_BIG)
            sc = jnp.where(sc == 0.0, 0.0, sc)
            key_ref[pl.ds(c0, kc), :] = _sort_key(sc)

    def score_pair(i, carry):
        score_chunks(2 * i, 2)
        return carry

    lax.fori_loop(0, n_kc // 2, score_pair, 0)

    @pl.when(n_kc % 2 == 1)
    def _():
        score_chunks(n_kc - 1, 1)

    def count(pred_fn):
        def body(c, tot):
            c0 = pl.multiple_of(c * kc, kc)
            hit = pred_fn(key_ref[pl.ds(c0, kc), :], c0)
            return tot + _slab(jnp.where(hit, 1.0, 0.0), jnp.sum)
        part = lax.fori_loop(0, n_kc, body, jnp.zeros((ROW_SLAB, tq), F32))
        return _reduce_rows(part, jnp.sum)

    def bit_step(k, carry):
        base, c_base = carry
        cand = base + jnp.left_shift(jnp.int32(1), 31 - k)
        cnt = count(lambda keys, c0: keys >= cand) + jnp.where(cand <= KEY_NEG, n_rest, 0.0)
        take = cnt >= n_keep
        return jnp.where(take, cand, base), jnp.where(take, cnt, c_base)

    tau, c_ge = lax.fori_loop(0, DSA_SURE_BITS, bit_step, (jnp.full((1, tq), jnp.iinfo(jnp.int32).min, jnp.int32),
                                                           jnp.full((1, tq), float(seq), F32)))
    tau_ref[...] = tau
    cge_ref[...] = c_ge

    @pl.when(jnp.max(jnp.abs(c_ge - n_keep)) > 0.0)
    def _():
        tau_ref[...], cge_ref[...] = lax.fori_loop(DSA_SURE_BITS, 32, bit_step, (tau, c_ge))

    tau = tau_ref[...]
    c_ge = cge_ref[...]

    c_gt = count(lambda keys, c0: keys > tau) + jnp.where(tau < KEY_NEG, n_rest, 0.0)
    need = n_keep - c_gt
    n_eq = (c_ge - c_gt) - jnp.where(tau == KEY_NEG, (seq - 1 - t_row).astype(F32), 0.0)
    tie_break = jnp.max(n_eq - need) > 0.0

    @pl.when(jnp.logical_not(tie_break))
    def _():
        def body(c, carry):
            c0 = pl.multiple_of(c * kc, kc)
            keep = (key_ref[pl.ds(c0, kc), :] >= tau) & causal_of(c0)
            mask_ref[pl.ds(c0, kc), :] = jnp.where(keep, 0.0, NEG_BIG)
            return carry
        lax.fori_loop(0, n_kc, body, 0)

    @pl.when(tie_break)
    def _():
        lower = jnp.where(lax.broadcasted_iota(jnp.int32, (kc, kc), 1)
                          < lax.broadcasted_iota(jnp.int32, (kc, kc), 0), 1.0, 0.0).astype(BF16)

        def body(c, seen):
            c0 = pl.multiple_of(c * kc, kc)
            keys = key_ref[pl.ds(c0, kc), :]
            eq = keys == tau
            eqf = jnp.where(eq, 1.0, 0.0)
            rank = seen + _dot(lower, eqf.astype(BF16))
            keep = ((keys > tau) | (eq & (rank < need))) & causal_of(c0)
            mask_ref[pl.ds(c0, kc), :] = jnp.where(keep, 0.0, NEG_BIG)
            return seen + _reduce_rows(eqf, jnp.sum)
        lax.fori_loop(0, n_kc, body, jnp.zeros((1, tq), F32))

    scale2 = dh ** -0.5 * LOG2E
    qa_cat = jnp.concatenate(
        [_dot_nt(wuk_ref[h], dq_ref[:, h * dh:(h + 1) * dh]) for h in range(nh)], axis=1).astype(BF16)
    m_ref[...] = jnp.full(m_ref.shape, -jnp.inf, F32)
    l_ref[...] = jnp.zeros(l_ref.shape, F32)
    acc_ref[...] = jnp.zeros(acc_ref.shape, F32)

    def attend(c, nc):
        c0 = pl.multiple_of(c * kc, kc)
        nk = nc * kc
        msk = mask_ref[pl.ds(c0, nk), :]
        bias = jnp.concatenate(
            [bias_ref[jnp.clip(qb * (tq // LANES) - (c * sub + g) + 1, 0, n_bias - 1)] for g in range(nc * sub)],
            axis=0)
        s = _dot(ckv_ref[pl.ds(c0, nk), :], qa_cat) * scale2 + bias + jnp.concatenate([msk] * nh, axis=1)
        m_old = m_ref[...]
        m_new = jnp.maximum(m_old, _reduce_rows(s, jnp.max))
        alpha = jnp.exp2(m_old - m_new)
        p = jnp.exp2(s - m_new)
        l_ref[...] = alpha * l_ref[...] + _reduce_rows(p, jnp.sum)
        acc_ref[...] = alpha * acc_ref[...] + _dot(ckvt_ref[:, pl.ds(c0, nk)], p.astype(BF16))
        m_ref[...] = m_new

    def att_pair(i, carry):
        attend(2 * i, 2)
        return carry

    lax.fori_loop(0, n_kc // 2, att_pair, 0)

    @pl.when(n_kc % 2 == 1)
    def _():
        attend(n_kc - 1, 1)

    o_lat_t = (acc_ref[...] / l_ref[...]).astype(BF16)
    for h in range(nh):
        o_ref[:, h * dh:(h + 1) * dh] = _dot(wuv_ref[h], o_lat_t[:, h * tq:(h + 1) * tq]).T.astype(o_ref.dtype)


def _dsa(proj_f, proj_b, kv_gain, ln_w, ln_b, w_uk_h, w_uv_h, bias_tiles, batch, seq):
    tq = DSA_QB
    nq = seq // tq
    n_keep = min(DSA_TOPK_MAX, seq // 4)
    n_bias = bias_tiles.shape[0]
    return pl.pallas_call(
        functools.partial(_dsa_kernel, seq=seq, n_keep=float(n_keep), n_bias=n_bias),
        out_shape=jax.ShapeDtypeStruct((batch * seq, HEAD_W), BF16),
        grid=(batch, nq),
        in_specs=[
            pl.BlockSpec((tq, HEAD_W), lambda b, i: (b * nq + i, BCOL_DQ // HEAD_W)),
            pl.BlockSpec((tq, HEAD_W), lambda b, i: (b * nq + i, BCOL_IQ // HEAD_W)),
            pl.BlockSpec((tq, LANES), lambda b, i: (b * nq + i, FCOL_IKW // LANES)),
            pl.BlockSpec((seq, DSA_KV_RANK), lambda b, i: (b, FCOL_DKV // DSA_KV_RANK)),
            pl.BlockSpec((seq, LANES), lambda b, i: (b, FCOL_IKW // LANES)),
            pl.BlockSpec((1, DSA_KV_RANK), lambda b, i: (0, 0)),
            pl.BlockSpec((1, IDX_DIM), lambda b, i: (0, 0)),
            pl.BlockSpec((1, IDX_DIM), lambda b, i: (0, 0)),
            pl.BlockSpec(w_uk_h.shape, lambda b, i: (0, 0, 0)),
            pl.BlockSpec(w_uv_h.shape, lambda b, i: (0, 0, 0)),
            pl.BlockSpec(bias_tiles.shape, lambda b, i: (0, 0, 0)),
        ],
        out_specs=pl.BlockSpec((tq, HEAD_W), lambda b, i: (b * nq + i, 0)),
        scratch_shapes=[
            pltpu.VMEM((seq, DSA_KV_RANK), BF16),
            pltpu.VMEM((DSA_KV_RANK, seq), BF16),
            pltpu.VMEM((seq, IDX_DIM), BF16),
            pltpu.VMEM((seq, tq), jnp.int32),
            pltpu.VMEM((seq, tq), F32),
            pltpu.VMEM((1, tq), jnp.int32),
            pltpu.VMEM((1, tq), F32),
            pltpu.VMEM((1, DSA_HEADS * tq), F32),
            pltpu.VMEM((1, DSA_HEADS * tq), F32),
            pltpu.VMEM((DSA_KV_RANK, DSA_HEADS * tq), F32),
        ],
        compiler_params=_cparams(("parallel", "arbitrary")),
        name="dsa",
    )(proj_b, proj_b, proj_f, proj_f, proj_f, kv_gain, ln_w, ln_b, w_uk_h, w_uv_h, bias_tiles)


def _merge_kernel(x_ref, ya_ref, yb_ref, yc_ref, g0_ref, g1_ref, g2_ref, wa_ref, wb_ref, wc_ref, wo_ref, gain_ref,
                  o_ref):
    merged = (_sigmoid(g0_ref[...]) * _dot(ya_ref[...], wa_ref[...])
              + _sigmoid(g1_ref[...]) * _dot(yb_ref[...], wb_ref[...])
              + _sigmoid(g2_ref[...]) * _dot(yc_ref[...], wc_ref[...]))
    z = _dot(merged.astype(BF16), wo_ref[...])
    zn = z * lax.rsqrt(jnp.mean(z * z, axis=-1, keepdims=True) + RMS_EPS) * gain_ref[...]
    o_ref[...] = x_ref[...] + zn


def _merge(x2, ya, yb, yc, proj_f, wa, wb, wc, wo, layer, gain, tm):
    n = x2.shape[0]
    gcol = FCOL_GATES // D_MODEL

    def rows(w):
        return pl.BlockSpec((tm, w), lambda i: (i, 0))

    def gate(k):
        return pl.BlockSpec((tm, D_MODEL), lambda i, k=k: (i, gcol + k))

    def weight(a):
        return pl.BlockSpec((None,) + a.shape[1:], lambda i: (layer, 0, 0))

    return pl.pallas_call(
        _merge_kernel,
        out_shape=jax.ShapeDtypeStruct((n, D_MODEL), F32),
        grid=(n // tm,),
        in_specs=[rows(D_MODEL), rows(HEAD_W), rows(HEAD_W), rows(HEAD_W), gate(0), gate(1), gate(2),
                  weight(wa), weight(wb), weight(wc), weight(wo), pl.BlockSpec(gain.shape, lambda i: (0, 0))],
        out_specs=rows(D_MODEL),
        compiler_params=_cparams(("parallel",)),
        name="merge",
    )(x2, ya, yb, yc, proj_f, proj_f, proj_f, wa, wb, wc, wo, gain)


def _ffn_kernel(x_ref, g1_ref, wg_ref, wu_ref, wd_ref, g2_ref, o_ref, hn_ref, acc_ref):
    j = pl.program_id(1)

    @pl.when(j == 0)
    def _():
        x = x_ref[...]
        hn_ref[...] = (x * lax.rsqrt(jnp.mean(x * x, axis=-1, keepdims=True) + RMS_EPS) * g1_ref[...]).astype(BF16)
        acc_ref[...] = jnp.zeros_like(acc_ref)

    hn = hn_ref[...]
    gate = _dot(hn, wg_ref[...])
    up = _dot(hn, wu_ref[...])
    act = (gate * _sigmoid(gate)) * up
    acc_ref[...] += _dot(act.astype(BF16), wd_ref[...])

    @pl.when(j == pl.num_programs(1) - 1)
    def _():
        y = acc_ref[...]
        yn = y * lax.rsqrt(jnp.mean(y * y, axis=-1, keepdims=True) + RMS_EPS) * g2_ref[...]
        o_ref[...] = x_ref[...] + yn


def _ffn(x2, g1, w_in, w_out, layer, g2, tm, tf):
    n = x2.shape[0]
    nf = D_FF // tf
    return pl.pallas_call(
        _ffn_kernel,
        out_shape=jax.ShapeDtypeStruct((n, D_MODEL), F32),
        grid=(n // tm, nf),
        in_specs=[
            pl.BlockSpec((tm, D_MODEL), lambda i, j: (i, 0)),
            pl.BlockSpec((1, D_MODEL), lambda i, j: (0, 0)),
            pl.BlockSpec((None, D_MODEL, tf), lambda i, j: (layer, 0, j)),
            pl.BlockSpec((None, D_MODEL, tf), lambda i, j: (layer, 0, nf + j)),
            pl.BlockSpec((None, tf, D_MODEL), lambda i, j: (layer, j, 0)),
            pl.BlockSpec((1, D_MODEL), lambda i, j: (0, 0)),
        ],
        out_specs=pl.BlockSpec((tm, D_MODEL), lambda i, j: (i, 0)),
        scratch_shapes=[pltpu.VMEM((tm, D_MODEL), BF16), pltpu.VMEM((tm, D_MODEL), F32)],
        compiler_params=_cparams(("parallel", "arbitrary")),
        name="ffn",
    )(x2, g1, w_in, w_in, w_out, g2)


def _pack_w_in(w):
    starts = dict(zip(IN_NAMES, np.cumsum((0,) + IN_SIZES[:-1]).tolist()))
    o_mq, o_dq, o_dkv, o_iq, o_ik, o_gates = (starts[k] for k in ("mq", "dq", "dkv", "iq", "ik", "gates"))
    assert w.shape[-1] == sum(IN_SIZES)
    w = w.astype(BF16)
    z = lambda k: jnp.zeros(w.shape[:-1] + (k,), w.dtype)
    w_f = jnp.concatenate([
        w[..., :o_mq],
        w[..., o_gates:],
        w[..., o_dkv:o_iq],
        w[..., o_ik:o_gates],
        z(PROJ_F_W - FCOL_IKW - (o_gates - o_ik)),
    ], axis=-1)
    w_b = jnp.concatenate([
        w[..., o_mq:o_dq],
        w[..., o_dq:o_dkv],
        w[..., o_iq:o_ik],
    ], axis=-1)
    assert w_f.shape[-1] == PROJ_F_W and w_b.shape[-1] == PROJ_B_W
    return w_f, w_b


def _tiles(batch, seq):
    n = batch * seq
    return dict(
        proj_tm=min(2048, n), proj_f_tn=PROJ_F_W // 4, proj_b_tm=min(2048, n), proj_b_tn=PROJ_B_W,
        hgrn_rows=min(512, seq),
        merge_tm=min(512, n),
        ffn_tm=min(1024, n), ffn_tf=D_FF // 2,
    )


def kernel(x, w_in, pre_mix_norm, post_mix_norm, pre_ffn_norm, post_ffn_norm, hg_lb_logits, hg_norm_w, dsa_kv_norm,
           dsa_w_uk, dsa_w_uv, idx_k_norm_w, idx_k_norm_b, rel_bias, w_branch_a, w_branch_b, w_branch_c, w_out,
           w_ffn_in, w_ffn_out):
    batch, seq, _ = x.shape
    depth = w_in.shape[0]
    t = _tiles(batch, seq)
    lbp = jax.nn.softmax(hg_lb_logits.astype(F32), axis=0)
    lower_bounds = jnp.cumsum(lbp, axis=0) - lbp[0]
    bias_b = _bias_tiles_t(rel_bias[:, :MOBA_HEADS], MOBA_BLOCK)
    bias_c = _bias_tiles_rect(rel_bias[:, MOBA_HEADS:], LANES, DSA_QB)
    row = lambda a: a.reshape(1, -1).astype(F32)

    w_f, w_b = _pack_w_in(w_in)
    w_uk = jnp.transpose(dsa_w_uk, (0, 2, 1, 3)).astype(BF16)
    w_uv = jnp.transpose(dsa_w_uv, (0, 2, 3, 1)).astype(BF16)
    w_a, w_bb, w_c, w_o = (a.astype(BF16) for a in (w_branch_a, w_branch_b, w_branch_c, w_out))
    w_fi, w_fo = w_ffn_in.astype(BF16), w_ffn_out.astype(BF16)

    x2 = x.reshape(batch * seq, D_MODEL)
    for l in range(depth):
        gain = row(pre_mix_norm[l])
        proj_f = _proj(x2, gain, w_f, l, F32, t["proj_tm"], t["proj_f_tn"], "proj_f")
        proj_b = _proj(x2, gain, w_b, l, BF16, t["proj_b_tm"], t["proj_b_tn"], "proj_b")
        y_a = _hgrn(proj_f, row(lower_bounds[l]), row(hg_norm_w[l]), batch, seq, t["hgrn_rows"])
        y_b = _moba(proj_b, bias_b, batch, seq)
        y_c = _dsa(proj_f, proj_b, row(dsa_kv_norm[l]), row(idx_k_norm_w[l]), row(idx_k_norm_b[l]),
                   w_uk[l], w_uv[l], bias_c, batch, seq)
        x2 = _merge(x2, y_a, y_b, y_c, proj_f, w_a, w_bb, w_c, w_o, l, row(post_mix_norm[l]), t["merge_tm"])
        x2 = _ffn(x2, row(pre_ffn_norm[l]), w_fi, w_fo, l, row(post_ffn_norm[l]), t["ffn_tm"], t["ffn_tf"])
    return x2.reshape(batch, seq, D_MODEL)
```

```python
import functools
import math

import numpy as np
import jax
import jax.numpy as jnp
from jax import lax
from jax.experimental import pallas as pl
from jax.experimental.pallas import tpu as pltpu

F32 = jnp.float32
BF16 = jnp.bfloat16

D_MODEL = 1024
RMS_EPS = 1e-6
LOG2E = 1.0 / math.log(2.0)
NEG_BIG = -1e30
TINY = 1e-20
HG_HEADS = 4
HG_D = 128
HG_CHUNK = 64
MOBA_HEADS = 4
MOBA_DH = 128
MOBA_BLOCK = 256
MOBA_TOPK = 3
MOBA_STEP_BLOCKS = 4
DSA_HEADS = 4
DSA_DH = 128
DSA_KV_RANK = 256
IDX_HEADS = 8
IDX_DIM = 64
DSA_TOPK_MAX = 256
DSA_KC = 512
DSA_SURE_BITS = 28
REL_BUCKETS = 32
REL_MAX_DIST = 2048
D_FF = 2816
HEAD_W = 512

LANES = 128
SUBLANES = 8
DSA_QB = 2 * LANES

IN_NAMES = ("hq", "hf", "hi", "hg", "mq", "mk", "mv", "dq", "dkv", "iq", "ik", "iw", "gates")
IN_SIZES = (HEAD_W, HEAD_W, HEAD_W, HEAD_W, HEAD_W, HEAD_W, HEAD_W, HEAD_W, DSA_KV_RANK,
            IDX_HEADS * IDX_DIM, IDX_DIM, IDX_HEADS, 3 * D_MODEL)

FCOL_HQ, FCOL_HF, FCOL_HI, FCOL_HG = 0, 512, 1024, 1536
FCOL_GATES, FCOL_DKV, FCOL_IKW = 2048, 5120, 5376
PROJ_F_W = 5632
BCOL_MQ, BCOL_MK, BCOL_MV, BCOL_DQ, BCOL_IQ = 0, 512, 1024, 1536, 2048
PROJ_B_W = 2560

VMEM_LIMIT = 56 * 1024 * 1024

_REL_EXACT = REL_BUCKETS // 2
REL_FAR_DIST = int(math.ceil(_REL_EXACT * (REL_MAX_DIST / _REL_EXACT) ** ((REL_BUCKETS - _REL_EXACT - 1) / (REL_BUCKETS - _REL_EXACT)))) + 1

_NEG_BITS = int(np.array(NEG_BIG, np.float32).view(np.int32))
KEY_NEG = int(np.int32(_NEG_BITS ^ 0x7FFFFFFF))


def _cparams(sem):
    return pltpu.CompilerParams(dimension_semantics=sem, vmem_limit_bytes=VMEM_LIMIT)


def _sigmoid(x):
    return 1.0 / (1.0 + jnp.exp(-x))


def _dot(a, b):
    return jnp.dot(a, b, preferred_element_type=F32)


def _dot_nt(a, b):
    return lax.dot_general(a, b, (((1,), (1,)), ((), ())), preferred_element_type=F32)


def _t_bf16(a):
    return a.astype(F32).T.astype(BF16)


ROW_SLAB = 32


def _slab(x, fn):
    return fn(x.reshape(x.shape[0] // ROW_SLAB, ROW_SLAB, x.shape[1]), axis=0)


def _reduce_rows(x, fn):
    y = _slab(x, fn)
    z = fn(y.reshape(ROW_SLAB // SUBLANES, SUBLANES, y.shape[1]), axis=0)
    return fn(z, axis=0, keepdims=True)


def _proj_kernel(x_ref, g_ref, w_ref, o_ref, xn_ref):
    @pl.when(pl.program_id(1) == 0)
    def _():
        x = x_ref[...]
        ms = jnp.mean(x * x, axis=-1, keepdims=True)
        xn_ref[...] = (x * lax.rsqrt(ms + RMS_EPS) * g_ref[...]).astype(BF16)

    o_ref[...] = _dot(xn_ref[...], w_ref[...]).astype(o_ref.dtype)


def _proj(x2, gain, w_packed, layer, out_dtype, tm, tn, name):
    n = x2.shape[0]
    width = w_packed.shape[-1]
    return pl.pallas_call(
        _proj_kernel,
        out_shape=jax.ShapeDtypeStruct((n, width), out_dtype),
        grid=(n // tm, width // tn),
        in_specs=[
            pl.BlockSpec((tm, D_MODEL), lambda i, j: (i, 0)),
            pl.BlockSpec((1, D_MODEL), lambda i, j: (0, 0)),
            pl.BlockSpec((None, D_MODEL, tn), lambda i, j: (layer, 0, j)),
        ],
        out_specs=pl.BlockSpec((tm, tn), lambda i, j: (i, j)),
        scratch_shapes=[pltpu.VMEM((tm, D_MODEL), BF16)],
        compiler_params=_cparams(("parallel", "arbitrary")),
        name=name,
    )(x2, gain, w_packed)


HG_LEVELS = (32, 16, 8, 4, 2, 1)
HG_UNROLL = 8


def _hgrn_constants():
    c = HG_CHUNK
    t = np.arange(c)
    i = t[:, None]
    tt = t[None, :]
    rows = [tt <= i]
    masks = []
    for m in HG_LEVELS:
        mid = ((t // (2 * m)) * (2 * m) + m)[:, None]
        upper = (i >= mid) & (tt >= mid) & (tt <= i)
        lower = (i < mid) & (tt > i) & (tt <= mid - 1)
        rows.append(upper | lower)
        same = (t[:, None] // (2 * m)) == (t[None, :] // (2 * m))
        masks.append(same & (i >= mid) & (tt < mid))
    masks.append(np.eye(c, dtype=bool))
    m_all = np.concatenate(rows, axis=0).astype(np.float32)
    return m_all, np.stack(masks).astype(np.float32)


def _hgrn_kernel(q_ref, f_ref, i_ref, g_ref, lb_ref, nw_ref, mall_ref, mask_ref, o_ref, st_ref, e_ref, *, n_chunks):
    c = HG_CHUNK
    nl = len(HG_LEVELS)

    @pl.when(pl.program_id(1) == 0)
    def _():
        st_ref[...] = jnp.zeros_like(st_ref)

    lb = lb_ref[...]
    nw = nw_ref[...]

    heads = [slice(h * HG_D, (h + 1) * HG_D) for h in range(HG_HEADS)]

    def intra(r0, u):
        fp = f_ref[pl.ds(r0, c), :]
        f = lb + (1.0 - lb) * _sigmoid(fp)
        logf = jnp.log(jnp.maximum(f, TINY))
        kk = (1.0 - lb) * _sigmoid(-fp)
        g_hi = logf.astype(BF16)
        g_lo = (logf - g_hi.astype(F32)).astype(BF16)
        e_ref[u] = _dot(mall_ref[...], jnp.concatenate([g_hi, g_lo], axis=1))
        q = q_ref[pl.ds(r0, c), :]

        def expo(row):
            return e_ref[u, row * c:(row + 1) * c, 0:HEAD_W] + e_ref[u, row * c:(row + 1) * c, HEAD_W:2 * HEAD_W]

        b = expo(0)
        b_last = b[c - 1:c, :]
        q_in = (q * jnp.exp(b)).astype(BF16)
        k_out = (kk * jnp.exp(b_last - b)).astype(BF16)
        qls, kls = [], []
        for l in range(nl):
            decay = jnp.exp(expo(1 + l))
            qls.append((q * decay).astype(BF16))
            kls.append((kk * decay).astype(BF16))
        qls.append(q.astype(BF16))
        kls.append(kk.astype(BF16))
        sls = [[_dot_nt(qls[l][:, hs], kls[l][:, hs]) for l in range(nl + 1)] for hs in heads]
        ss = []
        for h in range(HG_HEADS):
            s = jnp.zeros((c, c), F32)
            for l in range(nl + 1):
                s = s + jnp.where(mask_ref[l] > 0.0, sls[h][l], 0.0)
            ss.append(s.astype(BF16))
        v = i_ref[pl.ds(r0, c), :].astype(BF16)
        o_intra = [_dot(ss[h], v[:, hs]) for h, hs in enumerate(heads)]
        kv = [_dot(v[:, hs].astype(F32).T.astype(BF16), k_out[:, hs]) for hs in heads]
        return q_in, jnp.exp(b_last), o_intra, kv

    def step(ci, carry):
        parts = [intra(pl.multiple_of((ci * HG_UNROLL + u) * c, c), u) for u in range(HG_UNROLL)]
        for u, (q_in, st_decay, o_intra, kv) in enumerate(parts):
            r0 = pl.multiple_of((ci * HG_UNROLL + u) * c, c)
            og = g_ref[pl.ds(r0, c), :]
            outs = [_dot_nt(q_in[:, hs], st_ref[h].astype(BF16)) + o_intra[h] for h, hs in enumerate(heads)]
            for h, hs in enumerate(heads):
                st_ref[h] = st_ref[h] * st_decay[:, hs] + kv[h]
            for h, hs in enumerate(heads):
                o = outs[h]
                y = o * lax.rsqrt(jnp.mean(o * o, axis=-1, keepdims=True) + RMS_EPS) * nw
                gh = og[:, hs]
                o_ref[pl.ds(r0, c), hs] = (y * (gh * _sigmoid(gh))).astype(o_ref.dtype)
        return carry

    lax.fori_loop(0, n_chunks // HG_UNROLL, step, 0)


def _hgrn(proj_f, lb, norm_w, batch, seq, rows):
    m_all, masks = _hgrn_constants()
    n_rb = seq // rows
    n_chunks = rows // HG_CHUNK
    assert n_chunks % HG_UNROLL == 0

    def col(c0):
        return pl.BlockSpec((rows, HEAD_W), lambda b, r, cidx=c0 // HEAD_W: (b * n_rb + r, cidx))

    return pl.pallas_call(
        functools.partial(_hgrn_kernel, n_chunks=n_chunks),
        out_shape=jax.ShapeDtypeStruct((batch * seq, HEAD_W), BF16),
        grid=(batch, n_rb),
        in_specs=[
            col(FCOL_HQ), col(FCOL_HF), col(FCOL_HI), col(FCOL_HG),
            pl.BlockSpec((1, HEAD_W), lambda b, r: (0, 0)),
            pl.BlockSpec((1, HG_D), lambda b, r: (0, 0)),
            pl.BlockSpec(m_all.shape, lambda b, r: (0, 0)),
            pl.BlockSpec(masks.shape, lambda b, r: (0, 0, 0)),
        ],
        out_specs=pl.BlockSpec((rows, HEAD_W), lambda b, r: (b * n_rb + r, 0)),
        scratch_shapes=[
            pltpu.VMEM((HG_HEADS, HG_D, HG_D), F32),
            pltpu.VMEM((HG_UNROLL, m_all.shape[0], 2 * HEAD_W), F32),
        ],
        compiler_params=_cparams(("parallel", "arbitrary")),
        name="hgrn",
    )(proj_f, proj_f, proj_f, proj_f, lb, norm_w, jnp.asarray(m_all, BF16), jnp.asarray(masks, F32))


def _rel_bucket(dist):
    n = jnp.maximum(dist, 0)
    exact = REL_BUCKETS // 2
    nf = jnp.maximum(n, exact).astype(F32)
    large = exact + (jnp.log(nf / exact) / math.log(REL_MAX_DIST / exact) * (REL_BUCKETS - exact)).astype(jnp.int32)
    large = jnp.minimum(large, REL_BUCKETS - 1)
    return jnp.where(n < exact, n, large)


def _n_bias_tiles(tile):
    d = 0
    while d * tile - (tile - 1) < REL_FAR_DIST:
        d += 1
    return d + 1


def _bias_tiles_t(table, tile):
    n = _n_bias_tiles(tile)
    length = n * tile
    by_dist = (table[_rel_bucket(jnp.arange(length, dtype=jnp.int32))].astype(F32) * LOG2E).T
    heads = by_dist.shape[0]
    w = jnp.concatenate([by_dist, jnp.zeros((heads, 1), F32)], axis=1)
    skew = jnp.broadcast_to(w[:, None, :], (heads, tile, length + 1)).reshape(heads, tile * (length + 1))
    skew = skew[:, :tile * length].reshape(heads, tile, length)
    return skew.reshape(heads, tile, n, tile).transpose(0, 2, 1, 3)


def _bias_tiles_rect(table, rows, cols):
    n = 1
    while (n - 2) * rows - (rows - 1) < REL_FAR_DIST:
        n += 1
    length = n * rows + cols
    by_dist = (table[_rel_bucket(jnp.arange(length - rows, dtype=jnp.int32))].astype(F32) * LOG2E).T
    heads = by_dist.shape[0]
    w = jnp.concatenate([jnp.zeros((heads, rows), F32), by_dist, jnp.zeros((heads, 1), F32)], axis=1)
    skew = jnp.broadcast_to(w[:, None, :], (heads, rows, length + 1)).reshape(heads, rows * (length + 1))
    skew = skew[:, :rows * length].reshape(heads, rows, length)
    tiles = jnp.stack([skew[:, :, d * rows:d * rows + cols] for d in range(n)], axis=0)
    return tiles.transpose(0, 2, 1, 3).reshape(n, rows, heads * cols)


def _moba_kernel(q_ref, k_ref, v_ref, bias_ref, o_ref, vt_ref, km_ref, sel_ref, m_ref, l_ref, acc_ref,
                 *, n_blocks, n_sel, n_bias):
    blk, dh, nh = MOBA_BLOCK, MOBA_DH, MOBA_HEADS
    qb = pl.program_id(1)
    scale2 = dh ** -0.5 * LOG2E

    @pl.when(qb == 0)
    def _():
        for h in range(nh):
            for n in range(n_blocks):
                kblk = k_ref[n * blk:(n + 1) * blk, h * dh:(h + 1) * dh].astype(F32)
                km_ref[h, n:n + 1, :] = jnp.mean(kblk, axis=0, keepdims=True)
                vt_ref[h * dh:(h + 1) * dh, n * blk:(n + 1) * blk] = _t_bf16(v_ref[n * blk:(n + 1) * blk, h * dh:(h + 1) * dh])

    q_t = []
    n_iota = lax.broadcasted_iota(jnp.int32, (n_blocks, blk), 0)
    for h in range(nh):
        qt = _t_bf16(q_ref[:, h * dh:(h + 1) * dh])
        q_t.append(qt)
        gate = _dot(km_ref[h].astype(BF16), qt)
        gate = jnp.where(n_iota < qb, gate, NEG_BIG)
        rank = jnp.zeros(gate.shape, F32)
        for n in range(n_blocks):
            row = gate[n:n + 1, :]
            beats = (row > gate) | ((row == gate) & (n < n_iota))
            rank = rank + jnp.where(beats, 1.0, 0.0)
        sel_ref[h] = jnp.where((rank < n_sel) & (n_iota < qb), 0.0, NEG_BIG)
        m_ref[h] = jnp.full((1, blk), -jnp.inf, F32)
        l_ref[h] = jnp.zeros((1, blk), F32)
        acc_ref[h] = jnp.zeros((dh, blk), F32)

    def attend(r0, nk, mask_fn):
        ss = [_dot(k_ref[pl.ds(r0, nk), h * dh:(h + 1) * dh], q_t[h]) for h in range(nh)]
        ps, alphas = [], []
        for h in range(nh):
            s = ss[h] * scale2 + mask_fn(h)
            m_old = m_ref[h]
            m_new = jnp.maximum(m_old, _reduce_rows(s, jnp.max))
            alphas.append(jnp.exp2(m_old - m_new))
            p = jnp.exp2(s - m_new)
            l_ref[h] = alphas[h] * l_ref[h] + _reduce_rows(p, jnp.sum)
            m_ref[h] = m_new
            ps.append(p.astype(BF16))
        for h in range(nh):
            acc_ref[h] = alphas[h] * acc_ref[h] + _dot(vt_ref[h * dh:(h + 1) * dh, pl.ds(r0, nk)], ps[h])

    def past_mask(h, n):
        return bias_ref[h, jnp.minimum(qb - n, n_bias - 1)] + sel_ref[h, pl.ds(n, 1), :]

    causal = jnp.where(lax.broadcasted_iota(jnp.int32, (blk, blk), 0) <= lax.broadcasted_iota(jnp.int32, (blk, blk), 1),
                       0.0, NEG_BIG)

    def own_mask(h):
        return bias_ref[h, 0] + causal

    group = MOBA_STEP_BLOCKS

    def past_group(i, carry):
        n = group * i
        attend(pl.multiple_of(n * blk, group * blk), group * blk,
               lambda h: jnp.concatenate([past_mask(h, n + u) for u in range(group)], axis=0))
        return carry

    lax.fori_loop(0, qb // group, past_group, 0)

    for k in range(group):
        @pl.when(qb % group == k)
        def _(k=k):
            attend(pl.multiple_of((qb - k) * blk, blk), (k + 1) * blk,
                   lambda h: jnp.concatenate([past_mask(h, qb - k + u) for u in range(k)] + [own_mask(h)], axis=0))

    for h in range(nh):
        o_ref[:, h * dh:(h + 1) * dh] = (acc_ref[h] / l_ref[h]).T.astype(o_ref.dtype)


def _moba(proj_b, bias_tiles, batch, seq):
    blk = MOBA_BLOCK
    nb = seq // blk
    n_sel = min(MOBA_TOPK, nb - 1)
    n_bias = bias_tiles.shape[1]
    return pl.pallas_call(
        functools.partial(_moba_kernel, n_blocks=nb, n_sel=n_sel, n_bias=n_bias),
        out_shape=jax.ShapeDtypeStruct((batch * seq, HEAD_W), BF16),
        grid=(batch, nb),
        in_specs=[
            pl.BlockSpec((blk, HEAD_W), lambda b, i: (b * nb + i, BCOL_MQ // HEAD_W)),
            pl.BlockSpec((seq, HEAD_W), lambda b, i: (b, BCOL_MK // HEAD_W)),
            pl.BlockSpec((seq, HEAD_W), lambda b, i: (b, BCOL_MV // HEAD_W)),
            pl.BlockSpec(bias_tiles.shape, lambda b, i: (0, 0, 0, 0)),
        ],
        out_specs=pl.BlockSpec((blk, HEAD_W), lambda b, i: (b * nb + i, 0)),
        scratch_shapes=[
            pltpu.VMEM((HEAD_W, seq), BF16),
            pltpu.VMEM((MOBA_HEADS, nb, MOBA_DH), F32),
            pltpu.VMEM((MOBA_HEADS, nb, blk), F32),
            pltpu.VMEM((MOBA_HEADS, 1, blk), F32),
            pltpu.VMEM((MOBA_HEADS, 1, blk), F32),
            pltpu.VMEM((MOBA_HEADS, MOBA_DH, blk), F32),
        ],
        compiler_params=_cparams(("parallel", "arbitrary")),
        name="moba",
    )(proj_b, proj_b, proj_b, bias_tiles)


def _sort_key(x):
    bits = pltpu.bitcast(x, jnp.int32)
    return bits ^ ((bits >> 31) & 0x7FFFFFFF)


def _dsa_kernel(dq_ref, iq_ref, iwq_ref, dkv_ref, ikw_ref, kvg_ref, lnw_ref, lnb_ref, wuk_ref, wuv_ref, bias_ref,
                o_ref, ckv_ref, ckvt_ref, ikn_ref, key_ref, mask_ref, tau_ref, cge_ref, m_ref, l_ref, acc_ref,
                *, seq, n_keep, n_bias):
    tq, kc, nh, dh = DSA_QB, DSA_KC, DSA_HEADS, DSA_DH
    sub = kc // LANES
    qb = pl.program_id(1)
    t0 = qb * tq
    n_kc = (t0 + tq - 1) // kc + 1
    n_rest = (seq - n_kc * kc).astype(F32)

    @pl.when(qb == 0)
    def _():
        for c in range(seq // kc):
            rows = slice(c * kc, (c + 1) * kc)
            x = dkv_ref[rows, :]
            cn = x * lax.rsqrt(jnp.mean(x * x, axis=-1, keepdims=True) + RMS_EPS) * kvg_ref[...]
            ckv_ref[rows, :] = cn.astype(BF16)
            ckvt_ref[:, rows] = cn.T.astype(BF16)
            ik = ikw_ref[rows, 0:IDX_DIM]
            xc = ik - jnp.mean(ik, axis=-1, keepdims=True)
            ikn = xc * lax.rsqrt(jnp.mean(xc * xc, axis=-1, keepdims=True) + RMS_EPS) * lnw_ref[...] + lnb_ref[...]
            ikn_ref[rows, :] = ikn.astype(BF16)

    t_row = t0 + lax.broadcasted_iota(jnp.int32, (1, tq), 1)
    key_off = lax.broadcasted_iota(jnp.int32, (kc, tq), 0)

    def causal_of(c0):
        return (c0 + key_off) <= t_row

    iq_t = iq_ref[...].astype(F32).T
    iq_cat = jnp.concatenate([iq_t[h * IDX_DIM:(h + 1) * IDX_DIM, :] for h in range(IDX_HEADS)], axis=1).astype(BF16)
    iw_t = iwq_ref[...].T[IDX_DIM:IDX_DIM + IDX_HEADS, :] * (IDX_HEADS ** -0.5)
    iw_t = iw_t * (IDX_DIM ** -0.5)

    hpg = 2 * LANES // tq

    def score_chunks(c, nc):
        c0s = [pl.multiple_of((c + u) * kc, kc) for u in range(nc)]
        s_grp = [[_dot(ikn_ref[pl.ds(c0, kc), :], iq_cat[:, g * hpg * tq:(g + 1) * hpg * tq])
                  for g in range(IDX_HEADS // hpg)] for c0 in c0s]
        for u, c0 in enumerate(c0s):
            sc = jnp.zeros((kc, tq), F32)
            for h in range(IDX_HEADS):
                s_h = s_grp[u][h // hpg][:, (h % hpg) * tq:(h % hpg + 1) * tq]
                sc = sc + iw_t[h:h + 1, :] * jnp.maximum(s_h, 0.0)
            sc = jnp.where(causal_of(c0), sc, NEG_BIG)
            sc = jnp.where(sc == 0.0, 0.0, sc)
            key_ref[pl.ds(c0, kc), :] = _sort_key(sc)

    def score_pair(i, carry):
        score_chunks(2 * i, 2)
        return carry

    lax.fori_loop(0, n_kc // 2, score_pair, 0)

    @pl.when(n_kc % 2 == 1)
    def _():
        score_chunks(n_kc - 1, 1)

    def count(pred_fn):
        def body(c, tot):
            c0 = pl.multiple_of(c * kc, kc)
            hit = pred_fn(key_ref[pl.ds(c0, kc), :], c0)
            return tot + _slab(jnp.where(hit, 1.0, 0.0), jnp.sum)
        part = lax.fori_loop(0, n_kc, body, jnp.zeros((ROW_SLAB, tq), F32))
        return _reduce_rows(part, jnp.sum)

    def bit_step(k, carry):
        base, c_base = carry
        cand = base + jnp.left_shift(jnp.int32(1), 31 - k)
        cnt = count(lambda keys, c0: keys >= cand) + jnp.where(cand <= KEY_NEG, n_rest, 0.0)
        take = cnt >= n_keep
        return jnp.where(take, cand, base), jnp.where(take, cnt, c_base)

    tau, c_ge = lax.fori_loop(0, DSA_SURE_BITS, bit_step, (jnp.full((1, tq), jnp.iinfo(jnp.int32).min, jnp.int32),
                                                           jnp.full((1, tq), float(seq), F32)))
    tau_ref[...] = tau
    cge_ref[...] = c_ge

    @pl.when(jnp.max(jnp.abs(c_ge - n_keep)) > 0.0)
    def _():
        tau_ref[...], cge_ref[...] = lax.fori_loop(DSA_SURE_BITS, 32, bit_step, (tau, c_ge))

    tau = tau_ref[...]
    c_ge = cge_ref[...]

    c_gt = count(lambda keys, c0: keys > tau) + jnp.where(tau < KEY_NEG, n_rest, 0.0)
    need = n_keep - c_gt
    n_eq = (c_ge - c_gt) - jnp.where(tau == KEY_NEG, (seq - 1 - t_row).astype(F32), 0.0)
    tie_break = jnp.max(n_eq - need) > 0.0

    @pl.when(jnp.logical_not(tie_break))
    def _():
        def body(c, carry):
            c0 = pl.multiple_of(c * kc, kc)
            keep = (key_ref[pl.ds(c0, kc), :] >= tau) & causal_of(c0)
            mask_ref[pl.ds(c0, kc), :] = jnp.where(keep, 0.0, NEG_BIG)
            return carry
        lax.fori_loop(0, n_kc, body, 0)

    @pl.when(tie_break)
    def _():
        lower = jnp.where(lax.broadcasted_iota(jnp.int32, (kc, kc), 1)
                          < lax.broadcasted_iota(jnp.int32, (kc, kc), 0), 1.0, 0.0).astype(BF16)

        def body(c, seen):
            c0 = pl.multiple_of(c * kc, kc)
            keys = key_ref[pl.ds(c0, kc), :]
            eq = keys == tau
            eqf = jnp.where(eq, 1.0, 0.0)
            rank = seen + _dot(lower, eqf.astype(BF16))
            keep = ((keys > tau) | (eq & (rank < need))) & causal_of(c0)
            mask_ref[pl.ds(c0, kc), :] = jnp.where(keep, 0.0, NEG_BIG)
            return seen + _reduce_rows(eqf, jnp.sum)
        lax.fori_loop(0, n_kc, body, jnp.zeros((1, tq), F32))

    scale2 = dh ** -0.5 * LOG2E
    qa_cat = jnp.concatenate(
        [_dot_nt(wuk_ref[h], dq_ref[:, h * dh:(h + 1) * dh]) for h in range(nh)], axis=1).astype(BF16)
    m_ref[...] = jnp.full(m_ref.shape, -jnp.inf, F32)
    l_ref[...] = jnp.zeros(l_ref.shape, F32)
    acc_ref[...] = jnp.zeros(acc_ref.shape, F32)

    def attend(c, nc):
        c0 = pl.multiple_of(c * kc, kc)
        nk = nc * kc
        msk = mask_ref[pl.ds(c0, nk), :]
        bias = jnp.concatenate(
            [bias_ref[jnp.clip(qb * (tq // LANES) - (c * sub + g) + 1, 0, n_bias - 1)] for g in range(nc * sub)],
            axis=0)
        s = _dot(ckv_ref[pl.ds(c0, nk), :], qa_cat) * scale2 + bias + jnp.concatenate([msk] * nh, axis=1)
        m_old = m_ref[...]
        m_new = jnp.maximum(m_old, _reduce_rows(s, jnp.max))
        alpha = jnp.exp2(m_old - m_new)
        p = jnp.exp2(s - m_new)
        l_ref[...] = alpha * l_ref[...] + _reduce_rows(p, jnp.sum)
        acc_ref[...] = alpha * acc_ref[...] + _dot(ckvt_ref[:, pl.ds(c0, nk)], p.astype(BF16))
        m_ref[...] = m_new

    def att_pair(i, carry):
        attend(2 * i, 2)
        return carry

    lax.fori_loop(0, n_kc // 2, att_pair, 0)

    @pl.when(n_kc % 2 == 1)
    def _():
        attend(n_kc - 1, 1)

    o_lat_t = (acc_ref[...] / l_ref[...]).astype(BF16)
    for h in range(nh):
        o_ref[:, h * dh:(h + 1) * dh] = _dot(wuv_ref[h], o_lat_t[:, h * tq:(h + 1) * tq]).T.astype(o_ref.dtype)


def _dsa(proj_f, proj_b, kv_gain, ln_w, ln_b, w_uk_h, w_uv_h, bias_tiles, batch, seq):
    tq = DSA_QB
    nq = seq // tq
    n_keep = min(DSA_TOPK_MAX, seq // 4)
    n_bias = bias_tiles.shape[0]
    return pl.pallas_call(
        functools.partial(_dsa_kernel, seq=seq, n_keep=float(n_keep), n_bias=n_bias),
        out_shape=jax.ShapeDtypeStruct((batch * seq, HEAD_W), BF16),
        grid=(batch, nq),
        in_specs=[
            pl.BlockSpec((tq, HEAD_W), lambda b, i: (b * nq + i, BCOL_DQ // HEAD_W)),
            pl.BlockSpec((tq, HEAD_W), lambda b, i: (b * nq + i, BCOL_IQ // HEAD_W)),
            pl.BlockSpec((tq, LANES), lambda b, i: (b * nq + i, FCOL_IKW // LANES)),
            pl.BlockSpec((seq, DSA_KV_RANK), lambda b, i: (b, FCOL_DKV // DSA_KV_RANK)),
            pl.BlockSpec((seq, LANES), lambda b, i: (b, FCOL_IKW // LANES)),
            pl.BlockSpec((1, DSA_KV_RANK), lambda b, i: (0, 0)),
            pl.BlockSpec((1, IDX_DIM), lambda b, i: (0, 0)),
            pl.BlockSpec((1, IDX_DIM), lambda b, i: (0, 0)),
            pl.BlockSpec(w_uk_h.shape, lambda b, i: (0, 0, 0)),
            pl.BlockSpec(w_uv_h.shape, lambda b, i: (0, 0, 0)),
            pl.BlockSpec(bias_tiles.shape, lambda b, i: (0, 0, 0)),
        ],
        out_specs=pl.BlockSpec((tq, HEAD_W), lambda b, i: (b * nq + i, 0)),
        scratch_shapes=[
            pltpu.VMEM((seq, DSA_KV_RANK), BF16),
            pltpu.VMEM((DSA_KV_RANK, seq), BF16),
            pltpu.VMEM((seq, IDX_DIM), BF16),
            pltpu.VMEM((seq, tq), jnp.int32),
            pltpu.VMEM((seq, tq), F32),
            pltpu.VMEM((1, tq), jnp.int32),
            pltpu.VMEM((1, tq), F32),
            pltpu.VMEM((1, DSA_HEADS * tq), F32),
            pltpu.VMEM((1, DSA_HEADS * tq), F32),
            pltpu.VMEM((DSA_KV_RANK, DSA_HEADS * tq), F32),
        ],
        compiler_params=_cparams(("parallel", "arbitrary")),
        name="dsa",
    )(proj_b, proj_b, proj_f, proj_f, proj_f, kv_gain, ln_w, ln_b, w_uk_h, w_uv_h, bias_tiles)


def _merge_kernel(x_ref, ya_ref, yb_ref, yc_ref, g0_ref, g1_ref, g2_ref, wa_ref, wb_ref, wc_ref, wo_ref, gain_ref,
                  o_ref):
    merged = (_sigmoid(g0_ref[...]) * _dot(ya_ref[...], wa_ref[...])
              + _sigmoid(g1_ref[...]) * _dot(yb_ref[...], wb_ref[...])
              + _sigmoid(g2_ref[...]) * _dot(yc_ref[...], wc_ref[...]))
    z = _dot(merged.astype(BF16), wo_ref[...])
    zn = z * lax.rsqrt(jnp.mean(z * z, axis=-1, keepdims=True) + RMS_EPS) * gain_ref[...]
    o_ref[...] = x_ref[...] + zn


def _merge(x2, ya, yb, yc, proj_f, wa, wb, wc, wo, layer, gain, tm):
    n = x2.shape[0]
    gcol = FCOL_GATES // D_MODEL

    def rows(w):
        return pl.BlockSpec((tm, w), lambda i: (i, 0))

    def gate(k):
        return pl.BlockSpec((tm, D_MODEL), lambda i, k=k: (i, gcol + k))

    def weight(a):
        return pl.BlockSpec((None,) + a.shape[1:], lambda i: (layer, 0, 0))

    return pl.pallas_call(
        _merge_kernel,
        out_shape=jax.ShapeDtypeStruct((n, D_MODEL), F32),
        grid=(n // tm,),
        in_specs=[rows(D_MODEL), rows(HEAD_W), rows(HEAD_W), rows(HEAD_W), gate(0), gate(1), gate(2),
                  weight(wa), weight(wb), weight(wc), weight(wo), pl.BlockSpec(gain.shape, lambda i: (0, 0))],
        out_specs=rows(D_MODEL),
        compiler_params=_cparams(("parallel",)),
        name="merge",
    )(x2, ya, yb, yc, proj_f, proj_f, proj_f, wa, wb, wc, wo, gain)


def _ffn_kernel(x_ref, g1_ref, wg_ref, wu_ref, wd_ref, g2_ref, o_ref, hn_ref, acc_ref):
    j = pl.program_id(1)

    @pl.when(j == 0)
    def _():
        x = x_ref[...]
        hn_ref[...] = (x * lax.rsqrt(jnp.mean(x * x, axis=-1, keepdims=True) + RMS_EPS) * g1_ref[...]).astype(BF16)
        acc_ref[...] = jnp.zeros_like(acc_ref)

    hn = hn_ref[...]
    gate = _dot(hn, wg_ref[...])
    up = _dot(hn, wu_ref[...])
    act = (gate * _sigmoid(gate)) * up
    acc_ref[...] += _dot(act.astype(BF16), wd_ref[...])

    @pl.when(j == pl.num_programs(1) - 1)
    def _():
        y = acc_ref[...]
        yn = y * lax.rsqrt(jnp.mean(y * y, axis=-1, keepdims=True) + RMS_EPS) * g2_ref[...]
        o_ref[...] = x_ref[...] + yn


def _ffn(x2, g1, w_in, w_out, layer, g2, tm, tf):
    n = x2.shape[0]
    nf = D_FF // tf
    return pl.pallas_call(
        _ffn_kernel,
        out_shape=jax.ShapeDtypeStruct((n, D_MODEL), F32),
        grid=(n // tm, nf),
        in_specs=[
            pl.BlockSpec((tm, D_MODEL), lambda i, j: (i, 0)),
            pl.BlockSpec((1, D_MODEL), lambda i, j: (0, 0)),
            pl.BlockSpec((None, D_MODEL, tf), lambda i, j: (layer, 0, j)),
            pl.BlockSpec((None, D_MODEL, tf), lambda i, j: (layer, 0, nf + j)),
            pl.BlockSpec((None, tf, D_MODEL), lambda i, j: (layer, j, 0)),
            pl.BlockSpec((1, D_MODEL), lambda i, j: (0, 0)),
        ],
        out_specs=pl.BlockSpec((tm, D_MODEL), lambda i, j: (i, 0)),
        scratch_shapes=[pltpu.VMEM((tm, D_MODEL), BF16), pltpu.VMEM((tm, D_MODEL), F32)],
        compiler_params=_cparams(("parallel", "arbitrary")),
        name="ffn",
    )(x2, g1, w_in, w_in, w_out, g2)


def _pack_w_in(w):
    starts = dict(zip(IN_NAMES, np.cumsum((0,) + IN_SIZES[:-1]).tolist()))
    o_mq, o_dq, o_dkv, o_iq, o_ik, o_gates = (starts[k] for k in ("mq", "dq", "dkv", "iq", "ik", "gates"))
    assert w.shape[-1] == sum(IN_SIZES)
    w = w.astype(BF16)
    z = lambda k: jnp.zeros(w.shape[:-1] + (k,), w.dtype)
    w_f = jnp.concatenate([
        w[..., :o_mq],
        w[..., o_gates:],
        w[..., o_dkv:o_iq],
        w[..., o_ik:o_gates],
        z(PROJ_F_W - FCOL_IKW - (o_gates - o_ik)),
    ], axis=-1)
    w_b = jnp.concatenate([
        w[..., o_mq:o_dq],
        w[..., o_dq:o_dkv],
        w[..., o_iq:o_ik],
    ], axis=-1)
    assert w_f.shape[-1] == PROJ_F_W and w_b.shape[-1] == PROJ_B_W
    return w_f, w_b


def _tiles(batch, seq):
    n = batch * seq
    return dict(
        proj_tm=min(2048, n), proj_f_tn=PROJ_F_W // 4, proj_b_tm=min(2048, n), proj_b_tn=PROJ_B_W,
        hgrn_rows=min(512, seq),
        merge_tm=min(512, n),
        ffn_tm=min(1024, n), ffn_tf=D_FF // 2,
    )


def kernel(x, w_in, pre_mix_norm, post_mix_norm, pre_ffn_norm, post_ffn_norm, hg_lb_logits, hg_norm_w, dsa_kv_norm,
           dsa_w_uk, dsa_w_uv, idx_k_norm_w, idx_k_norm_b, rel_bias, w_branch_a, w_branch_b, w_branch_c, w_out,
           w_ffn_in, w_ffn_out):
    batch, seq, _ = x.shape
    depth = w_in.shape[0]
    t = _tiles(batch, seq)
    lbp = jax.nn.softmax(hg_lb_logits.astype(F32), axis=0)
    lower_bounds = jnp.cumsum(lbp, axis=0) - lbp[0]
    bias_b = _bias_tiles_t(rel_bias[:, :MOBA_HEADS], MOBA_BLOCK)
    bias_c = _bias_tiles_rect(rel_bias[:, MOBA_HEADS:], LANES, DSA_QB)
    row = lambda a: a.reshape(1, -1).astype(F32)

    w_f, w_b = _pack_w_in(w_in)
    w_uk = jnp.transpose(dsa_w_uk, (0, 2, 1, 3)).astype(BF16)
    w_uv = jnp.transpose(dsa_w_uv, (0, 2, 3, 1)).astype(BF16)
    w_a, w_bb, w_c, w_o = (a.astype(BF16) for a in (w_branch_a, w_branch_b, w_branch_c, w_out))
    w_fi, w_fo = w_ffn_in.astype(BF16), w_ffn_out.astype(BF16)

    x2 = x.reshape(batch * seq, D_MODEL)
    for l in range(depth):
        gain = row(pre_mix_norm[l])
        proj_f = _proj(x2, gain, w_f, l, F32, t["proj_tm"], t["proj_f_tn"], "proj_f")
        proj_b = _proj(x2, gain, w_b, l, BF16, t["proj_b_tm"], t["proj_b_tn"], "proj_b")
        y_a = _hgrn(proj_f, row(lower_bounds[l]), row(hg_norm_w[l]), batch, seq, t["hgrn_rows"])
        y_b = _moba(proj_b, bias_b, batch, seq)
        y_c = _dsa(proj_f, proj_b, row(dsa_kv_norm[l]), row(idx_k_norm_w[l]), row(idx_k_norm_b[l]),
                   w_uk[l], w_uv[l], bias_c, batch, seq)
        x2 = _merge(x2, y_a, y_b, y_c, proj_f, w_a, w_bb, w_c, w_o, l, row(post_mix_norm[l]), t["merge_tm"])
        x2 = _ffn(x2, row(pre_ffn_norm[l]), w_fi, w_fo, l, row(post_ffn_norm[l]), t["ffn_tm"], t["ffn_tf"])
    return x2.reshape(batch, seq, D_MODEL)
```

```python
import functools
import math

import numpy as np
import jax
import jax.numpy as jnp
from jax import lax
from jax.experimental import pallas as pl
from jax.experimental.pallas import tpu as pltpu

F32 = jnp.float32
BF16 = jnp.bfloat16

D_MODEL = 1024
RMS_EPS = 1e-6
LOG2E = 1.0 / math.log(2.0)
NEG_BIG = -1e30
TINY = 1e-20
HG_HEADS = 4
HG_D = 128
HG_CHUNK = 64
MOBA_HEADS = 4
MOBA_DH = 128
MOBA_BLOCK = 256
MOBA_TOPK = 3
MOBA_STEP_BLOCKS = 4
DSA_HEADS = 4
DSA_DH = 128
DSA_KV_RANK = 256
IDX_HEADS = 8
IDX_DIM = 64
DSA_TOPK_MAX = 256
DSA_KC = 512
DSA_SURE_BITS = 28
REL_BUCKETS = 32
REL_MAX_DIST = 2048
D_FF = 2816
HEAD_W = 512

LANES = 128
SUBLANES = 8
DSA_QB = 2 * LANES

IN_NAMES = ("hq", "hf", "hi", "hg", "mq", "mk", "mv", "dq", "dkv", "iq", "ik", "iw", "gates")
IN_SIZES = (HEAD_W, HEAD_W, HEAD_W, HEAD_W, HEAD_W, HEAD_W, HEAD_W, HEAD_W, DSA_KV_RANK,
            IDX_HEADS * IDX_DIM, IDX_DIM, IDX_HEADS, 3 * D_MODEL)

FCOL_HQ, FCOL_HF, FCOL_HI, FCOL_HG = 0, 512, 1024, 1536
FCOL_GATES, FCOL_DKV, FCOL_IKW = 2048, 5120, 5376
PROJ_F_W = 5632
BCOL_MQ, BCOL_MK, BCOL_MV, BCOL_DQ, BCOL_IQ = 0, 512, 1024, 1536, 2048
PROJ_B_W = 2560

VMEM_LIMIT = 56 * 1024 * 1024

_REL_EXACT = REL_BUCKETS // 2
REL_FAR_DIST = int(math.ceil(_REL_EXACT * (REL_MAX_DIST / _REL_EXACT) ** ((REL_BUCKETS - _REL_EXACT - 1) / (REL_BUCKETS - _REL_EXACT)))) + 1

_NEG_BITS = int(np.array(NEG_BIG, np.float32).view(np.int32))
KEY_NEG = int(np.int32(_NEG_BITS ^ 0x7FFFFFFF))


def _cparams(sem):
    return pltpu.CompilerParams(dimension_semantics=sem, vmem_limit_bytes=VMEM_LIMIT)


def _sigmoid(x):
    return 1.0 / (1.0 + jnp.exp(-x))


def _dot(a, b):
    return jnp.dot(a, b, preferred_element_type=F32)


def _dot_nt(a, b):
    return lax.dot_general(a, b, (((1,), (1,)), ((), ())), preferred_element_type=F32)


def _t_bf16(a):
    return a.astype(F32).T.astype(BF16)


ROW_SLAB = 16


def _slab(x, fn):
    return fn(x.reshape(x.shape[0] // ROW_SLAB, ROW_SLAB, x.shape[1]), axis=0)


def _reduce_rows(x, fn):
    y = _slab(x, fn)
    z = fn(y.reshape(ROW_SLAB // SUBLANES, SUBLANES, y.shape[1]), axis=0)
    return fn(z, axis=0, keepdims=True)


def _proj_kernel(x_ref, g_ref, w_ref, o_ref, xn_ref):
    @pl.when(pl.program_id(1) == 0)
    def _():
        x = x_ref[...]
        ms = jnp.mean(x * x, axis=-1, keepdims=True)
        xn_ref[...] = (x * lax.rsqrt(ms + RMS_EPS) * g_ref[...]).astype(BF16)

    o_ref[...] = _dot(xn_ref[...], w_ref[...]).astype(o_ref.dtype)


def _proj(x2, gain, w_packed, layer, out_dtype, tm, tn, name):
    n = x2.shape[0]
    width = w_packed.shape[-1]
    return pl.pallas_call(
        _proj_kernel,
        out_shape=jax.ShapeDtypeStruct((n, width), out_dtype),
        grid=(n // tm, width // tn),
        in_specs=[
            pl.BlockSpec((tm, D_MODEL), lambda i, j: (i, 0)),
            pl.BlockSpec((1, D_MODEL), lambda i, j: (0, 0)),
            pl.BlockSpec((None, D_MODEL, tn), lambda i, j: (layer, 0, j)),
        ],
        out_specs=pl.BlockSpec((tm, tn), lambda i, j: (i, j)),
        scratch_shapes=[pltpu.VMEM((tm, D_MODEL), BF16)],
        compiler_params=_cparams(("parallel", "arbitrary")),
        name=name,
    )(x2, gain, w_packed)


HG_LEVELS = (32, 16, 8, 4, 2, 1)
HG_UNROLL = 8


def _hgrn_constants():
    c = HG_CHUNK
    t = np.arange(c)
    i = t[:, None]
    tt = t[None, :]
    rows = [tt <= i]
    masks = []
    for m in HG_LEVELS:
        mid = ((t // (2 * m)) * (2 * m) + m)[:, None]
        upper = (i >= mid) & (tt >= mid) & (tt <= i)
        lower = (i < mid) & (tt > i) & (tt <= mid - 1)
        rows.append(upper | lower)
        same = (t[:, None] // (2 * m)) == (t[None, :] // (2 * m))
        masks.append(same & (i >= mid) & (tt < mid))
    masks.append(np.eye(c, dtype=bool))
    m_all = np.concatenate(rows, axis=0).astype(np.float32)
    return m_all, np.stack(masks).astype(np.float32)


def _hgrn_kernel(q_ref, f_ref, i_ref, g_ref, lb_ref, nw_ref, mall_ref, mask_ref, o_ref, st_ref, e_ref, *, n_chunks):
    c = HG_CHUNK
    nl = len(HG_LEVELS)

    @pl.when(pl.program_id(1) == 0)
    def _():
        st_ref[...] = jnp.zeros_like(st_ref)

    lb = lb_ref[...]
    nw = nw_ref[...]

    heads = [slice(h * HG_D, (h + 1) * HG_D) for h in range(HG_HEADS)]

    def intra(r0, u):
        fp = f_ref[pl.ds(r0, c), :]
        f = lb + (1.0 - lb) * _sigmoid(fp)
        logf = jnp.log(jnp.maximum(f, TINY))
        kk = (1.0 - lb) * _sigmoid(-fp)
        g_hi = logf.astype(BF16)
        g_lo = (logf - g_hi.astype(F32)).astype(BF16)
        e_ref[u] = _dot(mall_ref[...], jnp.concatenate([g_hi, g_lo], axis=1))
        q = q_ref[pl.ds(r0, c), :]

        def expo(row):
            return e_ref[u, row * c:(row + 1) * c, 0:HEAD_W] + e_ref[u, row * c:(row + 1) * c, HEAD_W:2 * HEAD_W]

        b = expo(0)
        b_last = b[c - 1:c, :]
        q_in = (q * jnp.exp(b)).astype(BF16)
        k_out = (kk * jnp.exp(b_last - b)).astype(BF16)
        qls, kls = [], []
        for l in range(nl):
            decay = jnp.exp(expo(1 + l))
            qls.append((q * decay).astype(BF16))
            kls.append((kk * decay).astype(BF16))
        qls.append(q.astype(BF16))
        kls.append(kk.astype(BF16))
        sls = [[_dot_nt(qls[l][:, hs], kls[l][:, hs]) for l in range(nl + 1)] for hs in heads]
        ss = []
        for h in range(HG_HEADS):
            s = jnp.zeros((c, c), F32)
            for l in range(nl + 1):
                s = s + jnp.where(mask_ref[l] > 0.0, sls[h][l], 0.0)
            ss.append(s.astype(BF16))
        v = i_ref[pl.ds(r0, c), :].astype(BF16)
        o_intra = [_dot(ss[h], v[:, hs]) for h, hs in enumerate(heads)]
        kv = [_dot(v[:, hs].astype(F32).T.astype(BF16), k_out[:, hs]) for hs in heads]
        return q_in, jnp.exp(b_last), o_intra, kv

    def step(ci, carry):
        parts = [intra(pl.multiple_of((ci * HG_UNROLL + u) * c, c), u) for u in range(HG_UNROLL)]
        for u, (q_in, st_decay, o_intra, kv) in enumerate(parts):
            r0 = pl.multiple_of((ci * HG_UNROLL + u) * c, c)
            og = g_ref[pl.ds(r0, c), :]
            outs = [_dot_nt(q_in[:, hs], st_ref[h].astype(BF16)) + o_intra[h] for h, hs in enumerate(heads)]
            for h, hs in enumerate(heads):
                st_ref[h] = st_ref[h] * st_decay[:, hs] + kv[h]
            for h, hs in enumerate(heads):
                o = outs[h]
                y = o * lax.rsqrt(jnp.mean(o * o, axis=-1, keepdims=True) + RMS_EPS) * nw
                gh = og[:, hs]
                o_ref[pl.ds(r0, c), hs] = (y * (gh * _sigmoid(gh))).astype(o_ref.dtype)
        return carry

    lax.fori_loop(0, n_chunks // HG_UNROLL, step, 0)


def _hgrn(proj_f, lb, norm_w, batch, seq, rows):
    m_all, masks = _hgrn_constants()
    n_rb = seq // rows
    n_chunks = rows // HG_CHUNK
    assert n_chunks % HG_UNROLL == 0

    def col(c0):
        return pl.BlockSpec((rows, HEAD_W), lambda b, r, cidx=c0 // HEAD_W: (b * n_rb + r, cidx))

    return pl.pallas_call(
        functools.partial(_hgrn_kernel, n_chunks=n_chunks),
        out_shape=jax.ShapeDtypeStruct((batch * seq, HEAD_W), BF16),
        grid=(batch, n_rb),
        in_specs=[
            col(FCOL_HQ), col(FCOL_HF), col(FCOL_HI), col(FCOL_HG),
            pl.BlockSpec((1, HEAD_W), lambda b, r: (0, 0)),
            pl.BlockSpec((1, HG_D), lambda b, r: (0, 0)),
            pl.BlockSpec(m_all.shape, lambda b, r: (0, 0)),
            pl.BlockSpec(masks.shape, lambda b, r: (0, 0, 0)),
        ],
        out_specs=pl.BlockSpec((rows, HEAD_W), lambda b, r: (b * n_rb + r, 0)),
        scratch_shapes=[
            pltpu.VMEM((HG_HEADS, HG_D, HG_D), F32),
            pltpu.VMEM((HG_UNROLL, m_all.shape[0], 2 * HEAD_W), F32),
        ],
        compiler_params=_cparams(("parallel", "arbitrary")),
        name="hgrn",
    )(proj_f, proj_f, proj_f, proj_f, lb, norm_w, jnp.asarray(m_all, BF16), jnp.asarray(masks, F32))


def _rel_bucket(dist):
    n = jnp.maximum(dist, 0)
    exact = REL_BUCKETS // 2
    nf = jnp.maximum(n, exact).astype(F32)
    large = exact + (jnp.log(nf / exact) / math.log(REL_MAX_DIST / exact) * (REL_BUCKETS - exact)).astype(jnp.int32)
    large = jnp.minimum(large, REL_BUCKETS - 1)
    return jnp.where(n < exact, n, large)


def _n_bias_tiles(tile):
    d = 0
    while d * tile - (tile - 1) < REL_FAR_DIST:
        d += 1
    return d + 1


def _bias_tiles_t(table, tile):
    n = _n_bias_tiles(tile)
    length = n * tile
    by_dist = (table[_rel_bucket(jnp.arange(length, dtype=jnp.int32))].astype(F32) * LOG2E).T
    heads = by_dist.shape[0]
    w = jnp.concatenate([by_dist, jnp.zeros((heads, 1), F32)], axis=1)
    skew = jnp.broadcast_to(w[:, None, :], (heads, tile, length + 1)).reshape(heads, tile * (length + 1))
    skew = skew[:, :tile * length].reshape(heads, tile, length)
    return skew.reshape(heads, tile, n, tile).transpose(0, 2, 1, 3)


def _bias_tiles_rect(table, rows, cols):
    n = 1
    while (n - 2) * rows - (rows - 1) < REL_FAR_DIST:
        n += 1
    length = n * rows + cols
    by_dist = (table[_rel_bucket(jnp.arange(length - rows, dtype=jnp.int32))].astype(F32) * LOG2E).T
    heads = by_dist.shape[0]
    w = jnp.concatenate([jnp.zeros((heads, rows), F32), by_dist, jnp.zeros((heads, 1), F32)], axis=1)
    skew = jnp.broadcast_to(w[:, None, :], (heads, rows, length + 1)).reshape(heads, rows * (length + 1))
    skew = skew[:, :rows * length].reshape(heads, rows, length)
    tiles = jnp.stack([skew[:, :, d * rows:d * rows + cols] for d in range(n)], axis=0)
    return tiles.transpose(0, 2, 1, 3).reshape(n, rows, heads * cols)


def _moba_kernel(q_ref, k_ref, v_ref, bias_ref, o_ref, vt_ref, km_ref, sel_ref, m_ref, l_ref, acc_ref,
                 *, n_blocks, n_sel, n_bias):
    blk, dh, nh = MOBA_BLOCK, MOBA_DH, MOBA_HEADS
    qb = pl.program_id(1)
    scale2 = dh ** -0.5 * LOG2E

    @pl.when(qb == 0)
    def _():
        for h in range(nh):
            for n in range(n_blocks):
                kblk = k_ref[n * blk:(n + 1) * blk, h * dh:(h + 1) * dh].astype(F32)
                km_ref[h, n:n + 1, :] = jnp.mean(kblk, axis=0, keepdims=True)
                vt_ref[h * dh:(h + 1) * dh, n * blk:(n + 1) * blk] = _t_bf16(v_ref[n * blk:(n + 1) * blk, h * dh:(h + 1) * dh])

    q_t = []
    n_iota = lax.broadcasted_iota(jnp.int32, (n_blocks, blk), 0)
    for h in range(nh):
        qt = _t_bf16(q_ref[:, h * dh:(h + 1) * dh])
        q_t.append(qt)
        gate = _dot(km_ref[h].astype(BF16), qt)
        gate = jnp.where(n_iota < qb, gate, NEG_BIG)
        rank = jnp.zeros(gate.shape, F32)
        for n in range(n_blocks):
            row = gate[n:n + 1, :]
            beats = (row > gate) | ((row == gate) & (n < n_iota))
            rank = rank + jnp.where(beats, 1.0, 0.0)
        sel_ref[h] = jnp.where((rank < n_sel) & (n_iota < qb), 0.0, NEG_BIG)
        m_ref[h] = jnp.full((1, blk), -jnp.inf, F32)
        l_ref[h] = jnp.zeros((1, blk), F32)
        acc_ref[h] = jnp.zeros((dh, blk), F32)

    def attend(r0, nk, mask_fn):
        ss = [_dot(k_ref[pl.ds(r0, nk), h * dh:(h + 1) * dh], q_t[h]) for h in range(nh)]
        ps, alphas = [], []
        for h in range(nh):
            s = ss[h] * scale2 + mask_fn(h)
            m_old = m_ref[h]
            m_new = jnp.maximum(m_old, _reduce_rows(s, jnp.max))
            alphas.append(jnp.exp2(m_old - m_new))
            p = jnp.exp2(s - m_new)
            l_ref[h] = alphas[h] * l_ref[h] + _reduce_rows(p, jnp.sum)
            m_ref[h] = m_new
            ps.append(p.astype(BF16))
        for h in range(nh):
            acc_ref[h] = alphas[h] * acc_ref[h] + _dot(vt_ref[h * dh:(h + 1) * dh, pl.ds(r0, nk)], ps[h])

    def past_mask(h, n):
        return bias_ref[h, jnp.minimum(qb - n, n_bias - 1)] + sel_ref[h, pl.ds(n, 1), :]

    causal = jnp.where(lax.broadcasted_iota(jnp.int32, (blk, blk), 0) <= lax.broadcasted_iota(jnp.int32, (blk, blk), 1),
                       0.0, NEG_BIG)

    def own_mask(h):
        return bias_ref[h, 0] + causal

    group = MOBA_STEP_BLOCKS

    def past_group(i, carry):
        n = group * i
        attend(pl.multiple_of(n * blk, group * blk), group * blk,
               lambda h: jnp.concatenate([past_mask(h, n + u) for u in range(group)], axis=0))
        return carry

    lax.fori_loop(0, qb // group, past_group, 0)

    for k in range(group):
        @pl.when(qb % group == k)
        def _(k=k):
            attend(pl.multiple_of((qb - k) * blk, blk), (k + 1) * blk,
                   lambda h: jnp.concatenate([past_mask(h, qb - k + u) for u in range(k)] + [own_mask(h)], axis=0))

    for h in range(nh):
        o_ref[:, h * dh:(h + 1) * dh] = (acc_ref[h] / l_ref[h]).T.astype(o_ref.dtype)


def _moba(proj_b, bias_tiles, batch, seq):
    blk = MOBA_BLOCK
    nb = seq // blk
    n_sel = min(MOBA_TOPK, nb - 1)
    n_bias = bias_tiles.shape[1]
    return pl.pallas_call(
        functools.partial(_moba_kernel, n_blocks=nb, n_sel=n_sel, n_bias=n_bias),
        out_shape=jax.ShapeDtypeStruct((batch * seq, HEAD_W), BF16),
        grid=(batch, nb),
        in_specs=[
            pl.BlockSpec((blk, HEAD_W), lambda b, i: (b * nb + i, BCOL_MQ // HEAD_W)),
            pl.BlockSpec((seq, HEAD_W), lambda b, i: (b, BCOL_MK // HEAD_W)),
            pl.BlockSpec((seq, HEAD_W), lambda b, i: (b, BCOL_MV // HEAD_W)),
            pl.BlockSpec(bias_tiles.shape, lambda b, i: (0, 0, 0, 0)),
        ],
        out_specs=pl.BlockSpec((blk, HEAD_W), lambda b, i: (b * nb + i, 0)),
        scratch_shapes=[
            pltpu.VMEM((HEAD_W, seq), BF16),
            pltpu.VMEM((MOBA_HEADS, nb, MOBA_DH), F32),
            pltpu.VMEM((MOBA_HEADS, nb, blk), F32),
            pltpu.VMEM((MOBA_HEADS, 1, blk), F32),
            pltpu.VMEM((MOBA_HEADS, 1, blk), F32),
            pltpu.VMEM((MOBA_HEADS, MOBA_DH, blk), F32),
        ],
        compiler_params=_cparams(("parallel", "arbitrary")),
        name="moba",
    )(proj_b, proj_b, proj_b, bias_tiles)


def _sort_key(x):
    bits = pltpu.bitcast(x, jnp.int32)
    return bits ^ ((bits >> 31) & 0x7FFFFFFF)


def _dsa_kernel(dq_ref, iq_ref, iwq_ref, dkv_ref, ikw_ref, kvg_ref, lnw_ref, lnb_ref, wuk_ref, wuv_ref, bias_ref,
                o_ref, ckv_ref, ckvt_ref, ikn_ref, key_ref, mask_ref, tau_ref, cge_ref, m_ref, l_ref, acc_ref,
                *, seq, n_keep, n_bias):
    tq, kc, nh, dh = DSA_QB, DSA_KC, DSA_HEADS, DSA_DH
    sub = kc // LANES
    qb = pl.program_id(1)
    t0 = qb * tq
    n_kc = (t0 + tq - 1) // kc + 1
    n_rest = (seq - n_kc * kc).astype(F32)

    @pl.when(qb == 0)
    def _():
        for c in range(seq // kc):
            rows = slice(c * kc, (c + 1) * kc)
            x = dkv_ref[rows, :]
            cn = x * lax.rsqrt(jnp.mean(x * x, axis=-1, keepdims=True) + RMS_EPS) * kvg_ref[...]
            ckv_ref[rows, :] = cn.astype(BF16)
            ckvt_ref[:, rows] = cn.T.astype(BF16)
            ik = ikw_ref[rows, 0:IDX_DIM]
            xc = ik - jnp.mean(ik, axis=-1, keepdims=True)
            ikn = xc * lax.rsqrt(jnp.mean(xc * xc, axis=-1, keepdims=True) + RMS_EPS) * lnw_ref[...] + lnb_ref[...]
            ikn_ref[rows, :] = ikn.astype(BF16)

    t_row = t0 + lax.broadcasted_iota(jnp.int32, (1, tq), 1)
    key_off = lax.broadcasted_iota(jnp.int32, (kc, tq), 0)

    def causal_of(c0):
        return (c0 + key_off) <= t_row

    iq_t = iq_ref[...].astype(F32).T
    iq_cat = jnp.concatenate([iq_t[h * IDX_DIM:(h + 1) * IDX_DIM, :] for h in range(IDX_HEADS)], axis=1).astype(BF16)
    iw_t = iwq_ref[...].T[IDX_DIM:IDX_DIM + IDX_HEADS, :] * (IDX_HEADS ** -0.5)
    iw_t = iw_t * (IDX_DIM ** -0.5)

    hpg = 2 * LANES // tq

    def score_chunks(c, nc):
        c0s = [pl.multiple_of((c + u) * kc, kc) for u in range(nc)]
        s_grp = [[_dot(ikn_ref[pl.ds(c0, kc), :], iq_cat[:, g * hpg * tq:(g + 1) * hpg * tq])
                  for g in range(IDX_HEADS // hpg)] for c0 in c0s]
        for u, c0 in enumerate(c0s):
            sc = jnp.zeros((kc, tq), F32)
            for h in range(IDX_HEADS):
                s_h = s_grp[u][h // hpg][:, (h % hpg) * tq:(h % hpg + 1) * tq]
                sc = sc + iw_t[h:h + 1, :] * jnp.maximum(s_h, 0.0)
            sc = jnp.where(causal_of(c0), sc, NEG_BIG)
            sc = jnp.where(sc == 0.0, 0.0, sc)
            key_ref[pl.ds(c0, kc), :] = _sort_key(sc)

    def score_pair(i, carry):
        score_chunks(2 * i, 2)
        return carry

    lax.fori_loop(0, n_kc // 2, score_pair, 0)

    @pl.when(n_kc % 2 == 1)
    def _():
        score_chunks(n_kc - 1, 1)

    def count(pred_fn):
        def body(c, tot):
            c0 = pl.multiple_of(c * kc, kc)
            hit = pred_fn(key_ref[pl.ds(c0, kc), :], c0)
            return tot + _slab(jnp.where(hit, 1.0, 0.0), jnp.sum)
        part = lax.fori_loop(0, n_kc, body, jnp.zeros((ROW_SLAB, tq), F32))
        return _reduce_rows(part, jnp.sum)

    def bit_step(k, carry):
        base, c_base = carry
        cand = base + jnp.left_shift(jnp.int32(1), 31 - k)
        cnt = count(lambda keys, c0: keys >= cand) + jnp.where(cand <= KEY_NEG, n_rest, 0.0)
        take = cnt >= n_keep
        return jnp.where(take, cand, base), jnp.where(take, cnt, c_base)

    tau, c_ge = lax.fori_loop(0, DSA_SURE_BITS, bit_step, (jnp.full((1, tq), jnp.iinfo(jnp.int32).min, jnp.int32),
                                                           jnp.full((1, tq), float(seq), F32)))
    tau_ref[...] = tau
    cge_ref[...] = c_ge

    @pl.when(jnp.max(jnp.abs(c_ge - n_keep)) > 0.0)
    def _():
        tau_ref[...], cge_ref[...] = lax.fori_loop(DSA_SURE_BITS, 32, bit_step, (tau, c_ge))

    tau = tau_ref[...]
    c_ge = cge_ref[...]

    c_gt = count(lambda keys, c0: keys > tau) + jnp.where(tau < KEY_NEG, n_rest, 0.0)
    need = n_keep - c_gt
    n_eq = (c_ge - c_gt) - jnp.where(tau == KEY_NEG, (seq - 1 - t_row).astype(F32), 0.0)
    tie_break = jnp.max(n_eq - need) > 0.0

    @pl.when(jnp.logical_not(tie_break))
    def _():
        def body(c, carry):
            c0 = pl.multiple_of(c * kc, kc)
            keep = (key_ref[pl.ds(c0, kc), :] >= tau) & causal_of(c0)
            mask_ref[pl.ds(c0, kc), :] = jnp.where(keep, 0.0, NEG_BIG)
            return carry
        lax.fori_loop(0, n_kc, body, 0)

    @pl.when(tie_break)
    def _():
        lower = jnp.where(lax.broadcasted_iota(jnp.int32, (kc, kc), 1)
                          < lax.broadcasted_iota(jnp.int32, (kc, kc), 0), 1.0, 0.0).astype(BF16)

        def body(c, seen):
            c0 = pl.multiple_of(c * kc, kc)
            keys = key_ref[pl.ds(c0, kc), :]
            eq = keys == tau
            eqf = jnp.where(eq, 1.0, 0.0)
            rank = seen + _dot(lower, eqf.astype(BF16))
            keep = ((keys > tau) | (eq & (rank < need))) & causal_of(c0)
            mask_ref[pl.ds(c0, kc), :] = jnp.where(keep, 0.0, NEG_BIG)
            return seen + _reduce_rows(eqf, jnp.sum)
        lax.fori_loop(0, n_kc, body, jnp.zeros((1, tq), F32))

    scale2 = dh ** -0.5 * LOG2E
    qa_cat = jnp.concatenate(
        [_dot_nt(wuk_ref[h], dq_ref[:, h * dh:(h + 1) * dh]) for h in range(nh)], axis=1).astype(BF16)
    m_ref[...] = jnp.full(m_ref.shape, -jnp.inf, F32)
    l_ref[...] = jnp.zeros(l_ref.shape, F32)
    acc_ref[...] = jnp.zeros(acc_ref.shape, F32)

    def attend(c, nc):
        c0 = pl.multiple_of(c * kc, kc)
        nk = nc * kc
        msk = mask_ref[pl.ds(c0, nk), :]
        bias = jnp.concatenate(
            [bias_ref[jnp.clip(qb * (tq // LANES) - (c * sub + g) + 1, 0, n_bias - 1)] for g in range(nc * sub)],
            axis=0)
        s = _dot(ckv_ref[pl.ds(c0, nk), :], qa_cat) * scale2 + bias + jnp.concatenate([msk] * nh, axis=1)
        m_old = m_ref[...]
        m_new = jnp.maximum(m_old, _reduce_rows(s, jnp.max))
        alpha = jnp.exp2(m_old - m_new)
        p = jnp.exp2(s - m_new)
        l_ref[...] = alpha * l_ref[...] + _reduce_rows(p, jnp.sum)
        acc_ref[...] = alpha * acc_ref[...] + _dot(ckvt_ref[:, pl.ds(c0, nk)], p.astype(BF16))
        m_ref[...] = m_new

    def att_pair(i, carry):
        attend(2 * i, 2)
        return carry

    lax.fori_loop(0, n_kc // 2, att_pair, 0)

    @pl.when(n_kc % 2 == 1)
    def _():
        attend(n_kc - 1, 1)

    o_lat_t = (acc_ref[...] / l_ref[...]).astype(BF16)
    for h in range(nh):
        o_ref[:, h * dh:(h + 1) * dh] = _dot(wuv_ref[h], o_lat_t[:, h * tq:(h + 1) * tq]).T.astype(o_ref.dtype)


def _dsa(proj_f, proj_b, kv_gain, ln_w, ln_b, w_uk_h, w_uv_h, bias_tiles, batch, seq):
    tq = DSA_QB
    nq = seq // tq
    n_keep = min(DSA_TOPK_MAX, seq // 4)
    n_bias = bias_tiles.shape[0]
    return pl.pallas_call(
        functools.partial(_dsa_kernel, seq=seq, n_keep=float(n_keep), n_bias=n_bias),
        out_shape=jax.ShapeDtypeStruct((batch * seq, HEAD_W), BF16),
        grid=(batch, nq),
        in_specs=[
            pl.BlockSpec((tq, HEAD_W), lambda b, i: (b * nq + i, BCOL_DQ // HEAD_W)),
            pl.BlockSpec((tq, HEAD_W), lambda b, i: (b * nq + i, BCOL_IQ // HEAD_W)),
            pl.BlockSpec((tq, LANES), lambda b, i: (b * nq + i, FCOL_IKW // LANES)),
            pl.BlockSpec((seq, DSA_KV_RANK), lambda b, i: (b, FCOL_DKV // DSA_KV_RANK)),
            pl.BlockSpec((seq, LANES), lambda b, i: (b, FCOL_IKW // LANES)),
            pl.BlockSpec((1, DSA_KV_RANK), lambda b, i: (0, 0)),
            pl.BlockSpec((1, IDX_DIM), lambda b, i: (0, 0)),
            pl.BlockSpec((1, IDX_DIM), lambda b, i: (0, 0)),
            pl.BlockSpec(w_uk_h.shape, lambda b, i: (0, 0, 0)),
            pl.BlockSpec(w_uv_h.shape, lambda b, i: (0, 0, 0)),
            pl.BlockSpec(bias_tiles.shape, lambda b, i: (0, 0, 0)),
        ],
        out_specs=pl.BlockSpec((tq, HEAD_W), lambda b, i: (b * nq + i, 0)),
        scratch_shapes=[
            pltpu.VMEM((seq, DSA_KV_RANK), BF16),
            pltpu.VMEM((DSA_KV_RANK, seq), BF16),
            pltpu.VMEM((seq, IDX_DIM), BF16),
            pltpu.VMEM((seq, tq), jnp.int32),
            pltpu.VMEM((seq, tq), F32),
            pltpu.VMEM((1, tq), jnp.int32),
            pltpu.VMEM((1, tq), F32),
            pltpu.VMEM((1, DSA_HEADS * tq), F32),
            pltpu.VMEM((1, DSA_HEADS * tq), F32),
            pltpu.VMEM((DSA_KV_RANK, DSA_HEADS * tq), F32),
        ],
        compiler_params=_cparams(("parallel", "arbitrary")),
        name="dsa",
    )(proj_b, proj_b, proj_f, proj_f, proj_f, kv_gain, ln_w, ln_b, w_uk_h, w_uv_h, bias_tiles)


def _merge_kernel(x_ref, ya_ref, yb_ref, yc_ref, g0_ref, g1_ref, g2_ref, wa_ref, wb_ref, wc_ref, wo_ref, gain_ref,
                  o_ref):
    merged = (_sigmoid(g0_ref[...]) * _dot(ya_ref[...], wa_ref[...])
              + _sigmoid(g1_ref[...]) * _dot(yb_ref[...], wb_ref[...])
              + _sigmoid(g2_ref[...]) * _dot(yc_ref[...], wc_ref[...]))
    z = _dot(merged.astype(BF16), wo_ref[...])
    zn = z * lax.rsqrt(jnp.mean(z * z, axis=-1, keepdims=True) + RMS_EPS) * gain_ref[...]
    o_ref[...] = x_ref[...] + zn


def _merge(x2, ya, yb, yc, proj_f, wa, wb, wc, wo, layer, gain, tm):
    n = x2.shape[0]
    gcol = FCOL_GATES // D_MODEL

    def rows(w):
        return pl.BlockSpec((tm, w), lambda i: (i, 0))

    def gate(k):
        return pl.BlockSpec((tm, D_MODEL), lambda i, k=k: (i, gcol + k))

    def weight(a):
        return pl.BlockSpec((None,) + a.shape[1:], lambda i: (layer, 0, 0))

    return pl.pallas_call(
        _merge_kernel,
        out_shape=jax.ShapeDtypeStruct((n, D_MODEL), F32),
        grid=(n // tm,),
        in_specs=[rows(D_MODEL), rows(HEAD_W), rows(HEAD_W), rows(HEAD_W), gate(0), gate(1), gate(2),
                  weight(wa), weight(wb), weight(wc), weight(wo), pl.BlockSpec(gain.shape, lambda i: (0, 0))],
        out_specs=rows(D_MODEL),
        compiler_params=_cparams(("parallel",)),
        name="merge",
    )(x2, ya, yb, yc, proj_f, proj_f, proj_f, wa, wb, wc, wo, gain)


def _ffn_kernel(x_ref, g1_ref, wg_ref, wu_ref, wd_ref, g2_ref, o_ref, hn_ref, acc_ref):
    j = pl.program_id(1)

    @pl.when(j == 0)
    def _():
        x = x_ref[...]
        hn_ref[...] = (x * lax.rsqrt(jnp.mean(x * x, axis=-1, keepdims=True) + RMS_EPS) * g1_ref[...]).astype(BF16)
        acc_ref[...] = jnp.zeros_like(acc_ref)

    hn = hn_ref[...]
    gate = _dot(hn, wg_ref[...])
    up = _dot(hn, wu_ref[...])
    act = (gate * _sigmoid(gate)) * up
    acc_ref[...] += _dot(act.astype(BF16), wd_ref[...])

    @pl.when(j == pl.num_programs(1) - 1)
    def _():
        y = acc_ref[...]
        yn = y * lax.rsqrt(jnp.mean(y * y, axis=-1, keepdims=True) + RMS_EPS) * g2_ref[...]
        o_ref[...] = x_ref[...] + yn


def _ffn(x2, g1, w_in, w_out, layer, g2, tm, tf):
    n = x2.shape[0]
    nf = D_FF // tf
    return pl.pallas_call(
        _ffn_kernel,
        out_shape=jax.ShapeDtypeStruct((n, D_MODEL), F32),
        grid=(n // tm, nf),
        in_specs=[
            pl.BlockSpec((tm, D_MODEL), lambda i, j: (i, 0)),
            pl.BlockSpec((1, D_MODEL), lambda i, j: (0, 0)),
            pl.BlockSpec((None, D_MODEL, tf), lambda i, j: (layer, 0, j)),
            pl.BlockSpec((None, D_MODEL, tf), lambda i, j: (layer, 0, nf + j)),
            pl.BlockSpec((None, tf, D_MODEL), lambda i, j: (layer, j, 0)),
            pl.BlockSpec((1, D_MODEL), lambda i, j: (0, 0)),
        ],
        out_specs=pl.BlockSpec((tm, D_MODEL), lambda i, j: (i, 0)),
        scratch_shapes=[pltpu.VMEM((tm, D_MODEL), BF16), pltpu.VMEM((tm, D_MODEL), F32)],
        compiler_params=_cparams(("parallel", "arbitrary")),
        name="ffn",
    )(x2, g1, w_in, w_in, w_out, g2)


def _pack_w_in(w):
    starts = dict(zip(IN_NAMES, np.cumsum((0,) + IN_SIZES[:-1]).tolist()))
    o_mq, o_dq, o_dkv, o_iq, o_ik, o_gates = (starts[k] for k in ("mq", "dq", "dkv", "iq", "ik", "gates"))
    assert w.shape[-1] == sum(IN_SIZES)
    w = w.astype(BF16)
    z = lambda k: jnp.zeros(w.shape[:-1] + (k,), w.dtype)
    w_f = jnp.concatenate([
        w[..., :o_mq],
        w[..., o_gates:],
        w[..., o_dkv:o_iq],
        w[..., o_ik:o_gates],
        z(PROJ_F_W - FCOL_IKW - (o_gates - o_ik)),
    ], axis=-1)
    w_b = jnp.concatenate([
        w[..., o_mq:o_dq],
        w[..., o_dq:o_dkv],
        w[..., o_iq:o_ik],
    ], axis=-1)
    assert w_f.shape[-1] == PROJ_F_W and w_b.shape[-1] == PROJ_B_W
    return w_f, w_b


def _tiles(batch, seq):
    n = batch * seq
    return dict(
        proj_tm=min(2048, n), proj_f_tn=PROJ_F_W // 4, proj_b_tm=min(2048, n), proj_b_tn=PROJ_B_W,
        hgrn_rows=min(512, seq),
        merge_tm=min(512, n),
        ffn_tm=min(1024, n), ffn_tf=D_FF // 2,
    )


def kernel(x, w_in, pre_mix_norm, post_mix_norm, pre_ffn_norm, post_ffn_norm, hg_lb_logits, hg_norm_w, dsa_kv_norm,
           dsa_w_uk, dsa_w_uv, idx_k_norm_w, idx_k_norm_b, rel_bias, w_branch_a, w_branch_b, w_branch_c, w_out,
           w_ffn_in, w_ffn_out):
    batch, seq, _ = x.shape
    depth = w_in.shape[0]
    t = _tiles(batch, seq)
    lbp = jax.nn.softmax(hg_lb_logits.astype(F32), axis=0)
    lower_bounds = jnp.cumsum(lbp, axis=0) - lbp[0]
    bias_b = _bias_tiles_t(rel_bias[:, :MOBA_HEADS], MOBA_BLOCK)
    bias_c = _bias_tiles_rect(rel_bias[:, MOBA_HEADS:], LANES, DSA_QB)
    row = lambda a: a.reshape(1, -1).astype(F32)

    w_f, w_b = _pack_w_in(w_in)
    w_uk = jnp.transpose(dsa_w_uk, (0, 2, 1, 3)).astype(BF16)
    w_uv = jnp.transpose(dsa_w_uv, (0, 2, 3, 1)).astype(BF16)
    w_a, w_bb, w_c, w_o = (a.astype(BF16) for a in (w_branch_a, w_branch_b, w_branch_c, w_out))
    w_fi, w_fo = w_ffn_in.astype(BF16), w_ffn_out.astype(BF16)

    x2 = x.reshape(batch * seq, D_MODEL)
    for l in range(depth):
        gain = row(pre_mix_norm[l])
        proj_f = _proj(x2, gain, w_f, l, F32, t["proj_tm"], t["proj_f_tn"], "proj_f")
        proj_b = _proj(x2, gain, w_b, l, BF16, t["proj_b_tm"], t["proj_b_tn"], "proj_b")
        y_a = _hgrn(proj_f, row(lower_bounds[l]), row(hg_norm_w[l]), batch, seq, t["hgrn_rows"])
        y_b = _moba(proj_b, bias_b, batch, seq)
        y_c = _dsa(proj_f, proj_b, row(dsa_kv_norm[l]), row(idx_k_norm_w[l]), row(idx_k_norm_b[l]),
                   w_uk[l], w_uv[l], bias_c, batch, seq)
        x2 = _merge(x2, y_a, y_b, y_c, proj_f, w_a, w_bb, w_c, w_o, l, row(post_mix_norm[l]), t["merge_tm"])
        x2 = _ffn(x2, row(pre_ffn_norm[l]), w_fi, w_fo, l, row(post_ffn_norm[l]), t["ffn_tm"], t["ffn_tf"])
    return x2.reshape(batch, seq, D_MODEL)
```

```python
import functools
import math

import numpy as np
import jax
import jax.numpy as jnp
from jax import lax
from jax.experimental import pallas as pl
from jax.experimental.pallas import tpu as pltpu

F32 = jnp.float32
BF16 = jnp.bfloat16

D_MODEL = 1024
RMS_EPS = 1e-6
LOG2E = 1.0 / math.log(2.0)
NEG_BIG = -1e30
TINY = 1e-20
HG_HEADS = 4
HG_D = 128
HG_CHUNK = 64
MOBA_HEADS = 4
MOBA_DH = 128
MOBA_BLOCK = 256
MOBA_TOPK = 3
MOBA_STEP_BLOCKS = 4
DSA_HEADS = 4
DSA_DH = 128
DSA_KV_RANK = 256
IDX_HEADS = 8
IDX_DIM = 64
DSA_TOPK_MAX = 256
DSA_KC = 512
DSA_SURE_BITS = 28
REL_BUCKETS = 32
REL_MAX_DIST = 2048
D_FF = 2816
HEAD_W = 512

LANES = 128
SUBLANES = 8
DSA_QB = 2 * LANES

IN_NAMES = ("hq", "hf", "hi", "hg", "mq", "mk", "mv", "dq", "dkv", "iq", "ik", "iw", "gates")
IN_SIZES = (HEAD_W, HEAD_W, HEAD_W, HEAD_W, HEAD_W, HEAD_W, HEAD_W, HEAD_W, DSA_KV_RANK,
            IDX_HEADS * IDX_DIM, IDX_DIM, IDX_HEADS, 3 * D_MODEL)

FCOL_HQ, FCOL_HF, FCOL_HI, FCOL_HG = 0, 512, 1024, 1536
FCOL_GATES, FCOL_DKV, FCOL_IKW = 2048, 5120, 5376
PROJ_F_W = 5632
BCOL_MQ, BCOL_MK, BCOL_MV, BCOL_DQ, BCOL_IQ = 0, 512, 1024, 1536, 2048
PROJ_B_W = 2560

VMEM_LIMIT = 56 * 1024 * 1024

_REL_EXACT = REL_BUCKETS // 2
REL_FAR_DIST = int(math.ceil(_REL_EXACT * (REL_MAX_DIST / _REL_EXACT) ** ((REL_BUCKETS - _REL_EXACT - 1) / (REL_BUCKETS - _REL_EXACT)))) + 1

_NEG_BITS = int(np.array(NEG_BIG, np.float32).view(np.int32))
KEY_NEG = int(np.int32(_NEG_BITS ^ 0x7FFFFFFF))


def _cparams(sem):
    return pltpu.CompilerParams(dimension_semantics=sem, vmem_limit_bytes=VMEM_LIMIT)


def _sigmoid(x):
    return 1.0 / (1.0 + jnp.exp(-x))


def _dot(a, b):
    return jnp.dot(a, b, preferred_element_type=F32)


def _dot_nt(a, b):
    return lax.dot_general(a, b, (((1,), (1,)), ((), ())), preferred_element_type=F32)


def _t_bf16(a):
    return a.astype(F32).T.astype(BF16)


ROW_SLAB = 8
COUNT_SLAB = 16


def _slab(x, fn, rows=ROW_SLAB):
    return fn(x.reshape(x.shape[0] // rows, rows, x.shape[1]), axis=0)


def _reduce_rows(x, fn, rows=ROW_SLAB):
    y = _slab(x, fn, rows)
    z = fn(y.reshape(rows // SUBLANES, SUBLANES, y.shape[1]), axis=0)
    return fn(z, axis=0, keepdims=True)


def _proj_kernel(x_ref, g_ref, w_ref, o_ref, xn_ref):
    @pl.when(pl.program_id(1) == 0)
    def _():
        x = x_ref[...]
        ms = jnp.mean(x * x, axis=-1, keepdims=True)
        xn_ref[...] = (x * lax.rsqrt(ms + RMS_EPS) * g_ref[...]).astype(BF16)

    o_ref[...] = _dot(xn_ref[...], w_ref[...]).astype(o_ref.dtype)


def _proj(x2, gain, w_packed, layer, out_dtype, tm, tn, name):
    n = x2.shape[0]
    width = w_packed.shape[-1]
    return pl.pallas_call(
        _proj_kernel,
        out_shape=jax.ShapeDtypeStruct((n, width), out_dtype),
        grid=(n // tm, width // tn),
        in_specs=[
            pl.BlockSpec((tm, D_MODEL), lambda i, j: (i, 0)),
            pl.BlockSpec((1, D_MODEL), lambda i, j: (0, 0)),
            pl.BlockSpec((None, D_MODEL, tn), lambda i, j: (layer, 0, j)),
        ],
        out_specs=pl.BlockSpec((tm, tn), lambda i, j: (i, j)),
        scratch_shapes=[pltpu.VMEM((tm, D_MODEL), BF16)],
        compiler_params=_cparams(("parallel", "arbitrary")),
        name=name,
    )(x2, gain, w_packed)


HG_LEVELS = (32, 16, 8, 4, 2, 1)
HG_UNROLL = 8


def _hgrn_constants():
    c = HG_CHUNK
    t = np.arange(c)
    i = t[:, None]
    tt = t[None, :]
    rows = [tt <= i]
    masks = []
    for m in HG_LEVELS:
        mid = ((t // (2 * m)) * (2 * m) + m)[:, None]
        upper = (i >= mid) & (tt >= mid) & (tt <= i)
        lower = (i < mid) & (tt > i) & (tt <= mid - 1)
        rows.append(upper | lower)
        same = (t[:, None] // (2 * m)) == (t[None, :] // (2 * m))
        masks.append(same & (i >= mid) & (tt < mid))
    masks.append(np.eye(c, dtype=bool))
    m_all = np.concatenate(rows, axis=0).astype(np.float32)
    return m_all, np.stack(masks).astype(np.float32)


def _hgrn_kernel(q_ref, f_ref, i_ref, g_ref, lb_ref, nw_ref, mall_ref, mask_ref, o_ref, st_ref, e_ref, *, n_chunks):
    c = HG_CHUNK
    nl = len(HG_LEVELS)

    @pl.when(pl.program_id(1) == 0)
    def _():
        st_ref[...] = jnp.zeros_like(st_ref)

    lb = lb_ref[...]
    nw = nw_ref[...]

    heads = [slice(h * HG_D, (h + 1) * HG_D) for h in range(HG_HEADS)]

    def intra(r0, u):
        fp = f_ref[pl.ds(r0, c), :]
        f = lb + (1.0 - lb) * _sigmoid(fp)
        logf = jnp.log(jnp.maximum(f, TINY))
        kk = (1.0 - lb) * _sigmoid(-fp)
        g_hi = logf.astype(BF16)
        g_lo = (logf - g_hi.astype(F32)).astype(BF16)
        e_ref[u] = _dot(mall_ref[...], jnp.concatenate([g_hi, g_lo], axis=1))
        q = q_ref[pl.ds(r0, c), :]

        def expo(row):
            return e_ref[u, row * c:(row + 1) * c, 0:HEAD_W] + e_ref[u, row * c:(row + 1) * c, HEAD_W:2 * HEAD_W]

        b = expo(0)
        b_last = b[c - 1:c, :]
        q_in = (q * jnp.exp(b)).astype(BF16)
        k_out = (kk * jnp.exp(b_last - b)).astype(BF16)
        qls, kls = [], []
        for l in range(nl):
            decay = jnp.exp(expo(1 + l))
            qls.append((q * decay).astype(BF16))
            kls.append((kk * decay).astype(BF16))
        qls.append(q.astype(BF16))
        kls.append(kk.astype(BF16))
        sls = [[_dot_nt(qls[l][:, hs], kls[l][:, hs]) for l in range(nl + 1)] for hs in heads]
        ss = []
        for h in range(HG_HEADS):
            s = jnp.zeros((c, c), F32)
            for l in range(nl + 1):
                s = s + jnp.where(mask_ref[l] > 0.0, sls[h][l], 0.0)
            ss.append(s.astype(BF16))
        v = i_ref[pl.ds(r0, c), :].astype(BF16)
        o_intra = [_dot(ss[h], v[:, hs]) for h, hs in enumerate(heads)]
        kv = [_dot(v[:, hs].astype(F32).T.astype(BF16), k_out[:, hs]) for hs in heads]
        return q_in, jnp.exp(b_last), o_intra, kv

    def step(ci, carry):
        parts = [intra(pl.multiple_of((ci * HG_UNROLL + u) * c, c), u) for u in range(HG_UNROLL)]
        for u, (q_in, st_decay, o_intra, kv) in enumerate(parts):
            r0 = pl.multiple_of((ci * HG_UNROLL + u) * c, c)
            og = g_ref[pl.ds(r0, c), :]
            outs = [_dot_nt(q_in[:, hs], st_ref[h].astype(BF16)) + o_intra[h] for h, hs in enumerate(heads)]
            for h, hs in enumerate(heads):
                st_ref[h] = st_ref[h] * st_decay[:, hs] + kv[h]
            for h, hs in enumerate(heads):
                o = outs[h]
                y = o * lax.rsqrt(jnp.mean(o * o, axis=-1, keepdims=True) + RMS_EPS) * nw
                gh = og[:, hs]
                o_ref[pl.ds(r0, c), hs] = (y * (gh * _sigmoid(gh))).astype(o_ref.dtype)
        return carry

    lax.fori_loop(0, n_chunks // HG_UNROLL, step, 0)


def _hgrn(proj_f, lb, norm_w, batch, seq, rows):
    m_all, masks = _hgrn_constants()
    n_rb = seq // rows
    n_chunks = rows // HG_CHUNK
    assert n_chunks % HG_UNROLL == 0

    def col(c0):
        return pl.BlockSpec((rows, HEAD_W), lambda b, r, cidx=c0 // HEAD_W: (b * n_rb + r, cidx))

    return pl.pallas_call(
        functools.partial(_hgrn_kernel, n_chunks=n_chunks),
        out_shape=jax.ShapeDtypeStruct((batch * seq, HEAD_W), BF16),
        grid=(batch, n_rb),
        in_specs=[
            col(FCOL_HQ), col(FCOL_HF), col(FCOL_HI), col(FCOL_HG),
            pl.BlockSpec((1, HEAD_W), lambda b, r: (0, 0)),
            pl.BlockSpec((1, HG_D), lambda b, r: (0, 0)),
            pl.BlockSpec(m_all.shape, lambda b, r: (0, 0)),
            pl.BlockSpec(masks.shape, lambda b, r: (0, 0, 0)),
        ],
        out_specs=pl.BlockSpec((rows, HEAD_W), lambda b, r: (b * n_rb + r, 0)),
        scratch_shapes=[
            pltpu.VMEM((HG_HEADS, HG_D, HG_D), F32),
            pltpu.VMEM((HG_UNROLL, m_all.shape[0], 2 * HEAD_W), F32),
        ],
        compiler_params=_cparams(("parallel", "arbitrary")),
        name="hgrn",
    )(proj_f, proj_f, proj_f, proj_f, lb, norm_w, jnp.asarray(m_all, BF16), jnp.asarray(masks, F32))


def _rel_bucket(dist):
    n = jnp.maximum(dist, 0)
    exact = REL_BUCKETS // 2
    nf = jnp.maximum(n, exact).astype(F32)
    large = exact + (jnp.log(nf / exact) / math.log(REL_MAX_DIST / exact) * (REL_BUCKETS - exact)).astype(jnp.int32)
    large = jnp.minimum(large, REL_BUCKETS - 1)
    return jnp.where(n < exact, n, large)


def _n_bias_tiles(tile):
    d = 0
    while d * tile - (tile - 1) < REL_FAR_DIST:
        d += 1
    return d + 1


def _bias_tiles_t(table, tile):
    n = _n_bias_tiles(tile)
    length = n * tile
    by_dist = (table[_rel_bucket(jnp.arange(length, dtype=jnp.int32))].astype(F32) * LOG2E).T
    heads = by_dist.shape[0]
    w = jnp.concatenate([by_dist, jnp.zeros((heads, 1), F32)], axis=1)
    skew = jnp.broadcast_to(w[:, None, :], (heads, tile, length + 1)).reshape(heads, tile * (length + 1))
    skew = skew[:, :tile * length].reshape(heads, tile, length)
    return skew.reshape(heads, tile, n, tile).transpose(0, 2, 1, 3)


def _bias_tiles_rect(table, rows, cols):
    n = 1
    while (n - 2) * rows - (rows - 1) < REL_FAR_DIST:
        n += 1
    length = n * rows + cols
    by_dist = (table[_rel_bucket(jnp.arange(length - rows, dtype=jnp.int32))].astype(F32) * LOG2E).T
    heads = by_dist.shape[0]
    w = jnp.concatenate([jnp.zeros((heads, rows), F32), by_dist, jnp.zeros((heads, 1), F32)], axis=1)
    skew = jnp.broadcast_to(w[:, None, :], (heads, rows, length + 1)).reshape(heads, rows * (length + 1))
    skew = skew[:, :rows * length].reshape(heads, rows, length)
    tiles = jnp.stack([skew[:, :, d * rows:d * rows + cols] for d in range(n)], axis=0)
    return tiles.transpose(0, 2, 1, 3).reshape(n, rows, heads * cols)


def _moba_kernel(q_ref, k_ref, v_ref, bias_ref, o_ref, vt_ref, km_ref, sel_ref, m_ref, l_ref, acc_ref,
                 *, n_blocks, n_sel, n_bias):
    blk, dh, nh = MOBA_BLOCK, MOBA_DH, MOBA_HEADS
    qb = pl.program_id(1)
    scale2 = dh ** -0.5 * LOG2E

    @pl.when(qb == 0)
    def _():
        for h in range(nh):
            for n in range(n_blocks):
                kblk = k_ref[n * blk:(n + 1) * blk, h * dh:(h + 1) * dh].astype(F32)
                km_ref[h, n:n + 1, :] = jnp.mean(kblk, axis=0, keepdims=True)
                vt_ref[h * dh:(h + 1) * dh, n * blk:(n + 1) * blk] = _t_bf16(v_ref[n * blk:(n + 1) * blk, h * dh:(h + 1) * dh])

    q_t = []
    n_iota = lax.broadcasted_iota(jnp.int32, (n_blocks, blk), 0)
    for h in range(nh):
        qt = _t_bf16(q_ref[:, h * dh:(h + 1) * dh])
        q_t.append(qt)
        gate = _dot(km_ref[h].astype(BF16), qt)
        gate = jnp.where(n_iota < qb, gate, NEG_BIG)
        rank = jnp.zeros(gate.shape, F32)
        for n in range(n_blocks):
            row = gate[n:n + 1, :]
            beats = (row > gate) | ((row == gate) & (n < n_iota))
            rank = rank + jnp.where(beats, 1.0, 0.0)
        sel_ref[h] = jnp.where((rank < n_sel) & (n_iota < qb), 0.0, NEG_BIG)
        m_ref[h] = jnp.full((1, blk), -jnp.inf, F32)
        l_ref[h] = jnp.zeros((1, blk), F32)
        acc_ref[h] = jnp.zeros((dh, blk), F32)

    def attend(r0, nk, mask_fn):
        ss = [_dot(k_ref[pl.ds(r0, nk), h * dh:(h + 1) * dh], q_t[h]) for h in range(nh)]
        ps, alphas = [], []
        for h in range(nh):
            s = ss[h] * scale2 + mask_fn(h)
            m_old = m_ref[h]
            m_new = jnp.maximum(m_old, _reduce_rows(s, jnp.max))
            alphas.append(jnp.exp2(m_old - m_new))
            p = jnp.exp2(s - m_new)
            l_ref[h] = alphas[h] * l_ref[h] + _reduce_rows(p, jnp.sum)
            m_ref[h] = m_new
            ps.append(p.astype(BF16))
        for h in range(nh):
            acc_ref[h] = alphas[h] * acc_ref[h] + _dot(vt_ref[h * dh:(h + 1) * dh, pl.ds(r0, nk)], ps[h])

    def past_mask(h, n):
        return bias_ref[h, jnp.minimum(qb - n, n_bias - 1)] + sel_ref[h, pl.ds(n, 1), :]

    causal = jnp.where(lax.broadcasted_iota(jnp.int32, (blk, blk), 0) <= lax.broadcasted_iota(jnp.int32, (blk, blk), 1),
                       0.0, NEG_BIG)

    def own_mask(h):
        return bias_ref[h, 0] + causal

    group = MOBA_STEP_BLOCKS

    def past_group(i, carry):
        n = group * i
        attend(pl.multiple_of(n * blk, group * blk), group * blk,
               lambda h: jnp.concatenate([past_mask(h, n + u) for u in range(group)], axis=0))
        return carry

    lax.fori_loop(0, qb // group, past_group, 0)

    for k in range(group):
        @pl.when(qb % group == k)
        def _(k=k):
            attend(pl.multiple_of((qb - k) * blk, blk), (k + 1) * blk,
                   lambda h: jnp.concatenate([past_mask(h, qb - k + u) for u in range(k)] + [own_mask(h)], axis=0))

    for h in range(nh):
        o_ref[:, h * dh:(h + 1) * dh] = (acc_ref[h] / l_ref[h]).T.astype(o_ref.dtype)


def _moba(proj_b, bias_tiles, batch, seq):
    blk = MOBA_BLOCK
    nb = seq // blk
    n_sel = min(MOBA_TOPK, nb - 1)
    n_bias = bias_tiles.shape[1]
    return pl.pallas_call(
        functools.partial(_moba_kernel, n_blocks=nb, n_sel=n_sel, n_bias=n_bias),
        out_shape=jax.ShapeDtypeStruct((batch * seq, HEAD_W), BF16),
        grid=(batch, nb),
        in_specs=[
            pl.BlockSpec((blk, HEAD_W), lambda b, i: (b * nb + i, BCOL_MQ // HEAD_W)),
            pl.BlockSpec((seq, HEAD_W), lambda b, i: (b, BCOL_MK // HEAD_W)),
            pl.BlockSpec((seq, HEAD_W), lambda b, i: (b, BCOL_MV // HEAD_W)),
            pl.BlockSpec(bias_tiles.shape, lambda b, i: (0, 0, 0, 0)),
        ],
        out_specs=pl.BlockSpec((blk, HEAD_W), lambda b, i: (b * nb + i, 0)),
        scratch_shapes=[
            pltpu.VMEM((HEAD_W, seq), BF16),
            pltpu.VMEM((MOBA_HEADS, nb, MOBA_DH), F32),
            pltpu.VMEM((MOBA_HEADS, nb, blk), F32),
            pltpu.VMEM((MOBA_HEADS, 1, blk), F32),
            pltpu.VMEM((MOBA_HEADS, 1, blk), F32),
            pltpu.VMEM((MOBA_HEADS, MOBA_DH, blk), F32),
        ],
        compiler_params=_cparams(("parallel", "arbitrary")),
        name="moba",
    )(proj_b, proj_b, proj_b, bias_tiles)


def _sort_key(x):
    bits = pltpu.bitcast(x, jnp.int32)
    return bits ^ ((bits >> 31) & 0x7FFFFFFF)


def _dsa_kernel(dq_ref, iq_ref, iwq_ref, dkv_ref, ikw_ref, kvg_ref, lnw_ref, lnb_ref, wuk_ref, wuv_ref, bias_ref,
                o_ref, ckv_ref, ckvt_ref, ikn_ref, key_ref, mask_ref, tau_ref, cge_ref, m_ref, l_ref, acc_ref,
                *, seq, n_keep, n_bias):
    tq, kc, nh, dh = DSA_QB, DSA_KC, DSA_HEADS, DSA_DH
    sub = kc // LANES
    qb = pl.program_id(1)
    t0 = qb * tq
    n_kc = (t0 + tq - 1) // kc + 1
    n_rest = (seq - n_kc * kc).astype(F32)

    @pl.when(qb == 0)
    def _():
        for c in range(seq // kc):
            rows = slice(c * kc, (c + 1) * kc)
            x = dkv_ref[rows, :]
            cn = x * lax.rsqrt(jnp.mean(x * x, axis=-1, keepdims=True) + RMS_EPS) * kvg_ref[...]
            ckv_ref[rows, :] = cn.astype(BF16)
            ckvt_ref[:, rows] = cn.T.astype(BF16)
            ik = ikw_ref[rows, 0:IDX_DIM]
            xc = ik - jnp.mean(ik, axis=-1, keepdims=True)
            ikn = xc * lax.rsqrt(jnp.mean(xc * xc, axis=-1, keepdims=True) + RMS_EPS) * lnw_ref[...] + lnb_ref[...]
            ikn_ref[rows, :] = ikn.astype(BF16)

    t_row = t0 + lax.broadcasted_iota(jnp.int32, (1, tq), 1)
    key_off = lax.broadcasted_iota(jnp.int32, (kc, tq), 0)

    def causal_of(c0):
        return (c0 + key_off) <= t_row

    iq_t = iq_ref[...].astype(F32).T
    iq_cat = jnp.concatenate([iq_t[h * IDX_DIM:(h + 1) * IDX_DIM, :] for h in range(IDX_HEADS)], axis=1).astype(BF16)
    iw_t = iwq_ref[...].T[IDX_DIM:IDX_DIM + IDX_HEADS, :] * (IDX_HEADS ** -0.5)
    iw_t = iw_t * (IDX_DIM ** -0.5)

    hpg = 2 * LANES // tq

    def score_chunks(c, nc):
        c0s = [pl.multiple_of((c + u) * kc, kc) for u in range(nc)]
        s_grp = [[_dot(ikn_ref[pl.ds(c0, kc), :], iq_cat[:, g * hpg * tq:(g + 1) * hpg * tq])
                  for g in range(IDX_HEADS // hpg)] for c0 in c0s]
        for u, c0 in enumerate(c0s):
            sc = jnp.zeros((kc, tq), F32)
            for h in range(IDX_HEADS):
                s_h = s_grp[u][h // hpg][:, (h % hpg) * tq:(h % hpg + 1) * tq]
                sc = sc + iw_t[h:h + 1, :] * jnp.maximum(s_h, 0.0)
            sc = jnp.where(causal_of(c0), sc, NEG_BIG)
            sc = jnp.where(sc == 0.0, 0.0, sc)
            key_ref[pl.ds(c0, kc), :] = _sort_key(sc)

    def score_pair(i, carry):
        score_chunks(2 * i, 2)
        return carry

    lax.fori_loop(0, n_kc // 2, score_pair, 0)

    @pl.when(n_kc % 2 == 1)
    def _():
        score_chunks(n_kc - 1, 1)

    def count(pred_fn):
        def body(c, tot):
            c0 = pl.multiple_of(c * kc, kc)
            hit = pred_fn(key_ref[pl.ds(c0, kc), :], c0)
            return tot + _slab(jnp.where(hit, 1.0, 0.0), jnp.sum, COUNT_SLAB)
        part = lax.fori_loop(0, n_kc, body, jnp.zeros((COUNT_SLAB, tq), F32))
        return _reduce_rows(part, jnp.sum, COUNT_SLAB)

    def bit_step(k, carry):
        base, c_base = carry
        cand = base + jnp.left_shift(jnp.int32(1), 31 - k)
        cnt = count(lambda keys, c0: keys >= cand) + jnp.where(cand <= KEY_NEG, n_rest, 0.0)
        take = cnt >= n_keep
        return jnp.where(take, cand, base), jnp.where(take, cnt, c_base)

    tau, c_ge = lax.fori_loop(0, DSA_SURE_BITS, bit_step, (jnp.full((1, tq), jnp.iinfo(jnp.int32).min, jnp.int32),
                                                           jnp.full((1, tq), float(seq), F32)))
    tau_ref[...] = tau
    cge_ref[...] = c_ge

    @pl.when(jnp.max(jnp.abs(c_ge - n_keep)) > 0.0)
    def _():
        tau_ref[...], cge_ref[...] = lax.fori_loop(DSA_SURE_BITS, 32, bit_step, (tau, c_ge))

    tau = tau_ref[...]
    c_ge = cge_ref[...]

    c_gt = count(lambda keys, c0: keys > tau) + jnp.where(tau < KEY_NEG, n_rest, 0.0)
    need = n_keep - c_gt
    n_eq = (c_ge - c_gt) - jnp.where(tau == KEY_NEG, (seq - 1 - t_row).astype(F32), 0.0)
    tie_break = jnp.max(n_eq - need) > 0.0

    @pl.when(jnp.logical_not(tie_break))
    def _():
        def body(c, carry):
            c0 = pl.multiple_of(c * kc, kc)
            keep = (key_ref[pl.ds(c0, kc), :] >= tau) & causal_of(c0)
            mask_ref[pl.ds(c0, kc), :] = jnp.where(keep, 0.0, NEG_BIG)
            return carry
        lax.fori_loop(0, n_kc, body, 0)

    @pl.when(tie_break)
    def _():
        lower = jnp.where(lax.broadcasted_iota(jnp.int32, (kc, kc), 1)
                          < lax.broadcasted_iota(jnp.int32, (kc, kc), 0), 1.0, 0.0).astype(BF16)

        def body(c, seen):
            c0 = pl.multiple_of(c * kc, kc)
            keys = key_ref[pl.ds(c0, kc), :]
            eq = keys == tau
            eqf = jnp.where(eq, 1.0, 0.0)
            rank = seen + _dot(lower, eqf.astype(BF16))
            keep = ((keys > tau) | (eq & (rank < need))) & causal_of(c0)
            mask_ref[pl.ds(c0, kc), :] = jnp.where(keep, 0.0, NEG_BIG)
            return seen + _reduce_rows(eqf, jnp.sum)
        lax.fori_loop(0, n_kc, body, jnp.zeros((1, tq), F32))

    scale2 = dh ** -0.5 * LOG2E
    qa_cat = jnp.concatenate(
        [_dot_nt(wuk_ref[h], dq_ref[:, h * dh:(h + 1) * dh]) for h in range(nh)], axis=1).astype(BF16)
    m_ref[...] = jnp.full(m_ref.shape, -jnp.inf, F32)
    l_ref[...] = jnp.zeros(l_ref.shape, F32)
    acc_ref[...] = jnp.zeros(acc_ref.shape, F32)

    def attend(c, nc):
        c0 = pl.multiple_of(c * kc, kc)
        nk = nc * kc
        msk = mask_ref[pl.ds(c0, nk), :]
        bias = jnp.concatenate(
            [bias_ref[jnp.clip(qb * (tq // LANES) - (c * sub + g) + 1, 0, n_bias - 1)] for g in range(nc * sub)],
            axis=0)
        s = _dot(ckv_ref[pl.ds(c0, nk), :], qa_cat) * scale2 + bias + jnp.concatenate([msk] * nh, axis=1)
        m_old = m_ref[...]
        m_new = jnp.maximum(m_old, _reduce_rows(s, jnp.max))
        alpha = jnp.exp2(m_old - m_new)
        p = jnp.exp2(s - m_new)
        l_ref[...] = alpha * l_ref[...] + _reduce_rows(p, jnp.sum)
        acc_ref[...] = alpha * acc_ref[...] + _dot(ckvt_ref[:, pl.ds(c0, nk)], p.astype(BF16))
        m_ref[...] = m_new

    def att_pair(i, carry):
        attend(2 * i, 2)
        return carry

    lax.fori_loop(0, n_kc // 2, att_pair, 0)

    @pl.when(n_kc % 2 == 1)
    def _():
        attend(n_kc - 1, 1)

    o_lat_t = (acc_ref[...] / l_ref[...]).astype(BF16)
    for h in range(nh):
        o_ref[:, h * dh:(h + 1) * dh] = _dot(wuv_ref[h], o_lat_t[:, h * tq:(h + 1) * tq]).T.astype(o_ref.dtype)


def _dsa(proj_f, proj_b, kv_gain, ln_w, ln_b, w_uk_h, w_uv_h, bias_tiles, batch, seq):
    tq = DSA_QB
    nq = seq // tq
    n_keep = min(DSA_TOPK_MAX, seq // 4)
    n_bias = bias_tiles.shape[0]
    return pl.pallas_call(
        functools.partial(_dsa_kernel, seq=seq, n_keep=float(n_keep), n_bias=n_bias),
        out_shape=jax.ShapeDtypeStruct((batch * seq, HEAD_W), BF16),
        grid=(batch, nq),
        in_specs=[
            pl.BlockSpec((tq, HEAD_W), lambda b, i: (b * nq + i, BCOL_DQ // HEAD_W)),
            pl.BlockSpec((tq, HEAD_W), lambda b, i: (b * nq + i, BCOL_IQ // HEAD_W)),
            pl.BlockSpec((tq, LANES), lambda b, i: (b * nq + i, FCOL_IKW // LANES)),
            pl.BlockSpec((seq, DSA_KV_RANK), lambda b, i: (b, FCOL_DKV // DSA_KV_RANK)),
            pl.BlockSpec((seq, LANES), lambda b, i: (b, FCOL_IKW // LANES)),
            pl.BlockSpec((1, DSA_KV_RANK), lambda b, i: (0, 0)),
            pl.BlockSpec((1, IDX_DIM), lambda b, i: (0, 0)),
            pl.BlockSpec((1, IDX_DIM), lambda b, i: (0, 0)),
            pl.BlockSpec(w_uk_h.shape, lambda b, i: (0, 0, 0)),
            pl.BlockSpec(w_uv_h.shape, lambda b, i: (0, 0, 0)),
            pl.BlockSpec(bias_tiles.shape, lambda b, i: (0, 0, 0)),
        ],
        out_specs=pl.BlockSpec((tq, HEAD_W), lambda b, i: (b * nq + i, 0)),
        scratch_shapes=[
            pltpu.VMEM((seq, DSA_KV_RANK), BF16),
            pltpu.VMEM((DSA_KV_RANK, seq), BF16),
            pltpu.VMEM((seq, IDX_DIM), BF16),
            pltpu.VMEM((seq, tq), jnp.int32),
            pltpu.VMEM((seq, tq), F32),
            pltpu.VMEM((1, tq), jnp.int32),
            pltpu.VMEM((1, tq), F32),
            pltpu.VMEM((1, DSA_HEADS * tq), F32),
            pltpu.VMEM((1, DSA_HEADS * tq), F32),
            pltpu.VMEM((DSA_KV_RANK, DSA_HEADS * tq), F32),
        ],
        compiler_params=_cparams(("parallel", "arbitrary")),
        name="dsa",
    )(proj_b, proj_b, proj_f, proj_f, proj_f, kv_gain, ln_w, ln_b, w_uk_h, w_uv_h, bias_tiles)


def _merge_kernel(x_ref, ya_ref, yb_ref, yc_ref, g0_ref, g1_ref, g2_ref, wa_ref, wb_ref, wc_ref, wo_ref, gain_ref,
                  o_ref):
    merged = (_sigmoid(g0_ref[...]) * _dot(ya_ref[...], wa_ref[...])
              + _sigmoid(g1_ref[...]) * _dot(yb_ref[...], wb_ref[...])
              + _sigmoid(g2_ref[...]) * _dot(yc_ref[...], wc_ref[...]))
    z = _dot(merged.astype(BF16), wo_ref[...])
    zn = z * lax.rsqrt(jnp.mean(z * z, axis=-1, keepdims=True) + RMS_EPS) * gain_ref[...]
    o_ref[...] = x_ref[...] + zn


def _merge(x2, ya, yb, yc, proj_f, wa, wb, wc, wo, layer, gain, tm):
    n = x2.shape[0]
    gcol = FCOL_GATES // D_MODEL

    def rows(w):
        return pl.BlockSpec((tm, w), lambda i: (i, 0))

    def gate(k):
        return pl.BlockSpec((tm, D_MODEL), lambda i, k=k: (i, gcol + k))

    def weight(a):
        return pl.BlockSpec((None,) + a.shape[1:], lambda i: (layer, 0, 0))

    return pl.pallas_call(
        _merge_kernel,
        out_shape=jax.ShapeDtypeStruct((n, D_MODEL), F32),
        grid=(n // tm,),
        in_specs=[rows(D_MODEL), rows(HEAD_W), rows(HEAD_W), rows(HEAD_W), gate(0), gate(1), gate(2),
                  weight(wa), weight(wb), weight(wc), weight(wo), pl.BlockSpec(gain.shape, lambda i: (0, 0))],
        out_specs=rows(D_MODEL),
        compiler_params=_cparams(("parallel",)),
        name="merge",
    )(x2, ya, yb, yc, proj_f, proj_f, proj_f, wa, wb, wc, wo, gain)


def _ffn_kernel(x_ref, g1_ref, wg_ref, wu_ref, wd_ref, g2_ref, o_ref, hn_ref, acc_ref):
    j = pl.program_id(1)

    @pl.when(j == 0)
    def _():
        x = x_ref[...]
        hn_ref[...] = (x * lax.rsqrt(jnp.mean(x * x, axis=-1, keepdims=True) + RMS_EPS) * g1_ref[...]).astype(BF16)
        acc_ref[...] = jnp.zeros_like(acc_ref)

    hn = hn_ref[...]
    gate = _dot(hn, wg_ref[...])
    up = _dot(hn, wu_ref[...])
    act = (gate * _sigmoid(gate)) * up
    acc_ref[...] += _dot(act.astype(BF16), wd_ref[...])

    @pl.when(j == pl.num_programs(1) - 1)
    def _():
        y = acc_ref[...]
        yn = y * lax.rsqrt(jnp.mean(y * y, axis=-1, keepdims=True) + RMS_EPS) * g2_ref[...]
        o_ref[...] = x_ref[...] + yn


def _ffn(x2, g1, w_in, w_out, layer, g2, tm, tf):
    n = x2.shape[0]
    nf = D_FF // tf
    return pl.pallas_call(
        _ffn_kernel,
        out_shape=jax.ShapeDtypeStruct((n, D_MODEL), F32),
        grid=(n // tm, nf),
        in_specs=[
            pl.BlockSpec((tm, D_MODEL), lambda i, j: (i, 0)),
            pl.BlockSpec((1, D_MODEL), lambda i, j: (0, 0)),
            pl.BlockSpec((None, D_MODEL, tf), lambda i, j: (layer, 0, j)),
            pl.BlockSpec((None, D_MODEL, tf), lambda i, j: (layer, 0, nf + j)),
            pl.BlockSpec((None, tf, D_MODEL), lambda i, j: (layer, j, 0)),
            pl.BlockSpec((1, D_MODEL), lambda i, j: (0, 0)),
        ],
        out_specs=pl.BlockSpec((tm, D_MODEL), lambda i, j: (i, 0)),
        scratch_shapes=[pltpu.VMEM((tm, D_MODEL), BF16), pltpu.VMEM((tm, D_MODEL), F32)],
        compiler_params=_cparams(("parallel", "arbitrary")),
        name="ffn",
    )(x2, g1, w_in, w_in, w_out, g2)


def _pack_w_in(w):
    starts = dict(zip(IN_NAMES, np.cumsum((0,) + IN_SIZES[:-1]).tolist()))
    o_mq, o_dq, o_dkv, o_iq, o_ik, o_gates = (starts[k] for k in ("mq", "dq", "dkv", "iq", "ik", "gates"))
    assert w.shape[-1] == sum(IN_SIZES)
    w = w.astype(BF16)
    z = lambda k: jnp.zeros(w.shape[:-1] + (k,), w.dtype)
    w_f = jnp.concatenate([
        w[..., :o_mq],
        w[..., o_gates:],
        w[..., o_dkv:o_iq],
        w[..., o_ik:o_gates],
        z(PROJ_F_W - FCOL_IKW - (o_gates - o_ik)),
    ], axis=-1)
    w_b = jnp.concatenate([
        w[..., o_mq:o_dq],
        w[..., o_dq:o_dkv],
        w[..., o_iq:o_ik],
    ], axis=-1)
    assert w_f.shape[-1] == PROJ_F_W and w_b.shape[-1] == PROJ_B_W
    return w_f, w_b


def _tiles(batch, seq):
    n = batch * seq
    return dict(
        proj_tm=min(2048, n), proj_f_tn=PROJ_F_W // 4, proj_b_tm=min(2048, n), proj_b_tn=PROJ_B_W,
        hgrn_rows=min(512, seq),
        merge_tm=min(512, n),
        ffn_tm=min(1024, n), ffn_tf=D_FF // 2,
    )


def kernel(x, w_in, pre_mix_norm, post_mix_norm, pre_ffn_norm, post_ffn_norm, hg_lb_logits, hg_norm_w, dsa_kv_norm,
           dsa_w_uk, dsa_w_uv, idx_k_norm_w, idx_k_norm_b, rel_bias, w_branch_a, w_branch_b, w_branch_c, w_out,
           w_ffn_in, w_ffn_out):
    batch, seq, _ = x.shape
    depth = w_in.shape[0]
    t = _tiles(batch, seq)
    lbp = jax.nn.softmax(hg_lb_logits.astype(F32), axis=0)
    lower_bounds = jnp.cumsum(lbp, axis=0) - lbp[0]
    bias_b = _bias_tiles_t(rel_bias[:, :MOBA_HEADS], MOBA_BLOCK)
    bias_c = _bias_tiles_rect(rel_bias[:, MOBA_HEADS:], LANES, DSA_QB)
    row = lambda a: a.reshape(1, -1).astype(F32)

    w_f, w_b = _pack_w_in(w_in)
    w_uk = jnp.transpose(dsa_w_uk, (0, 2, 1, 3)).astype(BF16)
    w_uv = jnp.transpose(dsa_w_uv, (0, 2, 3, 1)).astype(BF16)
    w_a, w_bb, w_c, w_o = (a.astype(BF16) for a in (w_branch_a, w_branch_b, w_branch_c, w_out))
    w_fi, w_fo = w_ffn_in.astype(BF16), w_ffn_out.astype(BF16)

    x2 = x.reshape(batch * seq, D_MODEL)
    for l in range(depth):
        gain = row(pre_mix_norm[l])
        proj_f = _proj(x2, gain, w_f, l, F32, t["proj_tm"], t["proj_f_tn"], "proj_f")
        proj_b = _proj(x2, gain, w_b, l, BF16, t["proj_b_tm"], t["proj_b_tn"], "proj_b")
        y_a = _hgrn(proj_f, row(lower_bounds[l]), row(hg_norm_w[l]), batch, seq, t["hgrn_rows"])
        y_b = _moba(proj_b, bias_b, batch, seq)
        y_c = _dsa(proj_f, proj_b, row(dsa_kv_norm[l]), row(idx_k_norm_w[l]), row(idx_k_norm_b[l]),
                   w_uk[l], w_uv[l], bias_c, batch, seq)
        x2 = _merge(x2, y_a, y_b, y_c, proj_f, w_a, w_bb, w_c, w_o, l, row(post_mix_norm[l]), t["merge_tm"])
        x2 = _ffn(x2, row(pre_ffn_norm[l]), w_fi, w_fo, l, row(post_ffn_norm[l]), t["ffn_tm"], t["ffn_tf"])
    return x2.reshape(batch, seq, D_MODEL)
```

```python
import functools
import math

import numpy as np
import jax
import jax.numpy as jnp
from jax import lax
from jax.experimental import pallas as pl
from jax.experimental.pallas import tpu as pltpu

F32 = jnp.float32
BF16 = jnp.bfloat16

D_MODEL = 1024
RMS_EPS = 1e-6
LOG2E = 1.0 / math.log(2.0)
NEG_BIG = -1e30
TINY = 1e-20
HG_HEADS = 4
HG_D = 128
HG_CHUNK = 64
MOBA_HEADS = 4
MOBA_DH = 128
MOBA_BLOCK = 256
MOBA_TOPK = 3
MOBA_STEP_BLOCKS = 4
DSA_HEADS = 4
DSA_DH = 128
DSA_KV_RANK = 256
IDX_HEADS = 8
IDX_DIM = 64
DSA_TOPK_MAX = 256
DSA_KC = 512
DSA_SURE_BITS = 28
REL_BUCKETS = 32
REL_MAX_DIST = 2048
D_FF = 2816
HEAD_W = 512

LANES = 128
SUBLANES = 8
DSA_QB = 2 * LANES

IN_NAMES = ("hq", "hf", "hi", "hg", "mq", "mk", "mv", "dq", "dkv", "iq", "ik", "iw", "gates")
IN_SIZES = (HEAD_W, HEAD_W, HEAD_W, HEAD_W, HEAD_W, HEAD_W, HEAD_W, HEAD_W, DSA_KV_RANK,
            IDX_HEADS * IDX_DIM, IDX_DIM, IDX_HEADS, 3 * D_MODEL)

FCOL_HQ, FCOL_HF, FCOL_HI, FCOL_HG = 0, 512, 1024, 1536
FCOL_GATES, FCOL_DKV, FCOL_IKW = 2048, 5120, 5376
PROJ_F_W = 5632
BCOL_MQ, BCOL_MK, BCOL_MV, BCOL_DQ, BCOL_IQ = 0, 512, 1024, 1536, 2048
PROJ_B_W = 2560

VMEM_LIMIT = 56 * 1024 * 1024

_REL_EXACT = REL_BUCKETS // 2
REL_FAR_DIST = int(math.ceil(_REL_EXACT * (REL_MAX_DIST / _REL_EXACT) ** ((REL_BUCKETS - _REL_EXACT - 1) / (REL_BUCKETS - _REL_EXACT)))) + 1

_NEG_BITS = int(np.array(NEG_BIG, np.float32).view(np.int32))
KEY_NEG = int(np.int32(_NEG_BITS ^ 0x7FFFFFFF))


def _cparams(sem):
    return pltpu.CompilerParams(dimension_semantics=sem, vmem_limit_bytes=VMEM_LIMIT)


def _sigmoid(x):
    return 1.0 / (1.0 + jnp.exp(-x))


def _dot(a, b):
    return jnp.dot(a, b, preferred_element_type=F32)


def _dot_nt(a, b):
    return lax.dot_general(a, b, (((1,), (1,)), ((), ())), preferred_element_type=F32)


def _t_bf16(a):
    return a.astype(F32).T.astype(BF16)


ROW_SLAB = 8
COUNT_SLAB = 16


def _slab(x, fn, rows=ROW_SLAB):
    return fn(x.reshape(x.shape[0] // rows, rows, x.shape[1]), axis=0)


def _reduce_rows(x, fn, rows=ROW_SLAB):
    y = _slab(x, fn, rows)
    z = fn(y.reshape(rows // SUBLANES, SUBLANES, y.shape[1]), axis=0)
    return fn(z, axis=0, keepdims=True)


def _proj_kernel(x_ref, g_ref, w_ref, o_ref, xn_ref):
    @pl.when(pl.program_id(1) == 0)
    def _():
        x = x_ref[...]
        ms = jnp.mean(x * x, axis=-1, keepdims=True)
        xn_ref[...] = (x * lax.rsqrt(ms + RMS_EPS) * g_ref[...]).astype(BF16)

    o_ref[...] = _dot(xn_ref[...], w_ref[...]).astype(o_ref.dtype)


def _proj(x2, gain, w_packed, layer, out_dtype, tm, tn, name):
    n = x2.shape[0]
    width = w_packed.shape[-1]
    return pl.pallas_call(
        _proj_kernel,
        out_shape=jax.ShapeDtypeStruct((n, width), out_dtype),
        grid=(n // tm, width // tn),
        in_specs=[
            pl.BlockSpec((tm, D_MODEL), lambda i, j: (i, 0)),
            pl.BlockSpec((1, D_MODEL), lambda i, j: (0, 0)),
            pl.BlockSpec((None, D_MODEL, tn), lambda i, j: (layer, 0, j)),
        ],
        out_specs=pl.BlockSpec((tm, tn), lambda i, j: (i, j)),
        scratch_shapes=[pltpu.VMEM((tm, D_MODEL), BF16)],
        compiler_params=_cparams(("parallel", "arbitrary")),
        name=name,
    )(x2, gain, w_packed)


HG_LEVELS = (32, 16, 8, 4, 2, 1)
HG_UNROLL = 8


def _hgrn_constants():
    c = HG_CHUNK
    t = np.arange(c)
    i = t[:, None]
    tt = t[None, :]
    rows = [tt <= i]
    masks = []
    for m in HG_LEVELS:
        mid = ((t // (2 * m)) * (2 * m) + m)[:, None]
        upper = (i >= mid) & (tt >= mid) & (tt <= i)
        lower = (i < mid) & (tt > i) & (tt <= mid - 1)
        rows.append(upper | lower)
        same = (t[:, None] // (2 * m)) == (t[None, :] // (2 * m))
        masks.append(same & (i >= mid) & (tt < mid))
    masks.append(np.eye(c, dtype=bool))
    m_all = np.concatenate(rows, axis=0).astype(np.float32)
    return m_all, np.stack(masks).astype(np.float32)


def _hgrn_kernel(q_ref, f_ref, i_ref, g_ref, lb_ref, nw_ref, mall_ref, mask_ref, o_ref, st_ref, e_ref, *, n_chunks):
    c = HG_CHUNK
    nl = len(HG_LEVELS)

    @pl.when(pl.program_id(1) == 0)
    def _():
        st_ref[...] = jnp.zeros_like(st_ref)

    lb = lb_ref[...]
    nw = nw_ref[...]

    heads = [slice(h * HG_D, (h + 1) * HG_D) for h in range(HG_HEADS)]

    def intra(r0, u):
        fp = f_ref[pl.ds(r0, c), :]
        f = lb + (1.0 - lb) * _sigmoid(fp)
        logf = jnp.log(jnp.maximum(f, TINY))
        kk = (1.0 - lb) * _sigmoid(-fp)
        g_hi = logf.astype(BF16)
        g_lo = (logf - g_hi.astype(F32)).astype(BF16)
        e_ref[u] = _dot(mall_ref[...], jnp.concatenate([g_hi, g_lo], axis=1))
        q = q_ref[pl.ds(r0, c), :]

        def expo(row):
            return e_ref[u, row * c:(row + 1) * c, 0:HEAD_W] + e_ref[u, row * c:(row + 1) * c, HEAD_W:2 * HEAD_W]

        b = expo(0)
        b_last = b[c - 1:c, :]
        q_in = (q * jnp.exp(b)).astype(BF16)
        k_out = (kk * jnp.exp(b_last - b)).astype(BF16)
        qls, kls = [], []
        for l in range(nl):
            decay = jnp.exp(expo(1 + l))
            qls.append((q * decay).astype(BF16))
            kls.append((kk * decay).astype(BF16))
        qls.append(q.astype(BF16))
        kls.append(kk.astype(BF16))
        sls = [[_dot_nt(qls[l][:, hs], kls[l][:, hs]) for l in range(nl + 1)] for hs in heads]
        ss = []
        for h in range(HG_HEADS):
            s = jnp.zeros((c, c), F32)
            for l in range(nl + 1):
                s = s + jnp.where(mask_ref[l] > 0.0, sls[h][l], 0.0)
            ss.append(s.astype(BF16))
        v = i_ref[pl.ds(r0, c), :].astype(BF16)
        o_intra = [_dot(ss[h], v[:, hs]) for h, hs in enumerate(heads)]
        kv = [_dot(v[:, hs].astype(F32).T.astype(BF16), k_out[:, hs]) for hs in heads]
        return q_in, jnp.exp(b_last), o_intra, kv

    def step(ci, carry):
        parts = [intra(pl.multiple_of((ci * HG_UNROLL + u) * c, c), u) for u in range(HG_UNROLL)]
        for u, (q_in, st_decay, o_intra, kv) in enumerate(parts):
            r0 = pl.multiple_of((ci * HG_UNROLL + u) * c, c)
            og = g_ref[pl.ds(r0, c), :]
            outs = [_dot_nt(q_in[:, hs], st_ref[h].astype(BF16)) + o_intra[h] for h, hs in enumerate(heads)]
            for h, hs in enumerate(heads):
                st_ref[h] = st_ref[h] * st_decay[:, hs] + kv[h]
            for h, hs in enumerate(heads):
                o = outs[h]
                y = o * lax.rsqrt(jnp.mean(o * o, axis=-1, keepdims=True) + RMS_EPS) * nw
                gh = og[:, hs]
                o_ref[pl.ds(r0, c), hs] = (y * (gh * _sigmoid(gh))).astype(o_ref.dtype)
        return carry

    lax.fori_loop(0, n_chunks // HG_UNROLL, step, 0)


def _hgrn(proj_f, lb, norm_w, batch, seq, rows):
    m_all, masks = _hgrn_constants()
    n_rb = seq // rows
    n_chunks = rows // HG_CHUNK
    assert n_chunks % HG_UNROLL == 0

    def col(c0):
        return pl.BlockSpec((rows, HEAD_W), lambda b, r, cidx=c0 // HEAD_W: (b * n_rb + r, cidx))

    return pl.pallas_call(
        functools.partial(_hgrn_kernel, n_chunks=n_chunks),
        out_shape=jax.ShapeDtypeStruct((batch * seq, HEAD_W), BF16),
        grid=(batch, n_rb),
        in_specs=[
            col(FCOL_HQ), col(FCOL_HF), col(FCOL_HI), col(FCOL_HG),
            pl.BlockSpec((1, HEAD_W), lambda b, r: (0, 0)),
            pl.BlockSpec((1, HG_D), lambda b, r: (0, 0)),
            pl.BlockSpec(m_all.shape, lambda b, r: (0, 0)),
            pl.BlockSpec(masks.shape, lambda b, r: (0, 0, 0)),
        ],
        out_specs=pl.BlockSpec((rows, HEAD_W), lambda b, r: (b * n_rb + r, 0)),
        scratch_shapes=[
            pltpu.VMEM((HG_HEADS, HG_D, HG_D), F32),
            pltpu.VMEM((HG_UNROLL, m_all.shape[0], 2 * HEAD_W), F32),
        ],
        compiler_params=_cparams(("parallel", "arbitrary")),
        name="hgrn",
    )(proj_f, proj_f, proj_f, proj_f, lb, norm_w, jnp.asarray(m_all, BF16), jnp.asarray(masks, F32))


def _rel_bucket(dist):
    n = jnp.maximum(dist, 0)
    exact = REL_BUCKETS // 2
    nf = jnp.maximum(n, exact).astype(F32)
    large = exact + (jnp.log(nf / exact) / math.log(REL_MAX_DIST / exact) * (REL_BUCKETS - exact)).astype(jnp.int32)
    large = jnp.minimum(large, REL_BUCKETS - 1)
    return jnp.where(n < exact, n, large)


def _n_bias_tiles(tile):
    d = 0
    while d * tile - (tile - 1) < REL_FAR_DIST:
        d += 1
    return d + 1


def _bias_tiles_t(table, tile):
    n = _n_bias_tiles(tile)
    length = n * tile
    by_dist = (table[_rel_bucket(jnp.arange(length, dtype=jnp.int32))].astype(F32) * LOG2E).T
    heads = by_dist.shape[0]
    w = jnp.concatenate([by_dist, jnp.zeros((heads, 1), F32)], axis=1)
    skew = jnp.broadcast_to(w[:, None, :], (heads, tile, length + 1)).reshape(heads, tile * (length + 1))
    skew = skew[:, :tile * length].reshape(heads, tile, length)
    return skew.reshape(heads, tile, n, tile).transpose(0, 2, 1, 3)


def _bias_tiles_rect(table, rows, cols):
    n = 1
    while (n - 2) * rows - (rows - 1) < REL_FAR_DIST:
        n += 1
    length = n * rows + cols
    by_dist = (table[_rel_bucket(jnp.arange(length - rows, dtype=jnp.int32))].astype(F32) * LOG2E).T
    heads = by_dist.shape[0]
    w = jnp.concatenate([jnp.zeros((heads, rows), F32), by_dist, jnp.zeros((heads, 1), F32)], axis=1)
    skew = jnp.broadcast_to(w[:, None, :], (heads, rows, length + 1)).reshape(heads, rows * (length + 1))
    skew = skew[:, :rows * length].reshape(heads, rows, length)
    tiles = jnp.stack([skew[:, :, d * rows:d * rows + cols] for d in range(n)], axis=0)
    return tiles.transpose(0, 2, 1, 3).reshape(n, rows, heads * cols)


def _moba_kernel(q_ref, k_ref, v_ref, bias_ref, o_ref, vt_ref, km_ref, sel_ref, m_ref, l_ref, acc_ref,
                 *, n_blocks, n_sel, n_bias):
    blk, dh, nh = MOBA_BLOCK, MOBA_DH, MOBA_HEADS
    qb = pl.program_id(1)
    scale2 = dh ** -0.5 * LOG2E

    @pl.when(qb == 0)
    def _():
        for h in range(nh):
            for n in range(n_blocks):
                kblk = k_ref[n * blk:(n + 1) * blk, h * dh:(h + 1) * dh].astype(F32)
                km_ref[h, n:n + 1, :] = jnp.mean(kblk, axis=0, keepdims=True)
                vt_ref[h * dh:(h + 1) * dh, n * blk:(n + 1) * blk] = _t_bf16(v_ref[n * blk:(n + 1) * blk, h * dh:(h + 1) * dh])

    q_t = []
    n_iota = lax.broadcasted_iota(jnp.int32, (n_blocks, blk), 0)
    for h in range(nh):
        qt = _t_bf16(q_ref[:, h * dh:(h + 1) * dh])
        q_t.append(qt)
        gate = _dot(km_ref[h].astype(BF16), qt)
        gate = jnp.where(n_iota < qb, gate, NEG_BIG)
        rank = jnp.zeros(gate.shape, F32)
        for n in range(n_blocks):
            row = gate[n:n + 1, :]
            beats = (row > gate) | ((row == gate) & (n < n_iota))
            rank = rank + jnp.where(beats, 1.0, 0.0)
        sel_ref[h] = jnp.where((rank < n_sel) & (n_iota < qb), 0.0, NEG_BIG)
        m_ref[h] = jnp.full((1, blk), -jnp.inf, F32)
        l_ref[h] = jnp.zeros((1, blk), F32)
        acc_ref[h] = jnp.zeros((dh, blk), F32)

    def attend(r0, nk, mask_fn):
        ss = [_dot(k_ref[pl.ds(r0, nk), h * dh:(h + 1) * dh], q_t[h]) for h in range(nh)]
        ps, alphas = [], []
        for h in range(nh):
            s = ss[h] * scale2 + mask_fn(h)
            m_old = m_ref[h]
            m_new = jnp.maximum(m_old, _reduce_rows(s, jnp.max))
            alphas.append(jnp.exp2(m_old - m_new))
            p = jnp.exp2(s - m_new)
            l_ref[h] = alphas[h] * l_ref[h] + _reduce_rows(p, jnp.sum)
            m_ref[h] = m_new
            ps.append(p.astype(BF16))
        for h in range(nh):
            acc_ref[h] = alphas[h] * acc_ref[h] + _dot(vt_ref[h * dh:(h + 1) * dh, pl.ds(r0, nk)], ps[h])

    def past_mask(h, n):
        return bias_ref[h, jnp.minimum(qb - n, n_bias - 1)] + sel_ref[h, pl.ds(n, 1), :]

    causal = jnp.where(lax.broadcasted_iota(jnp.int32, (blk, blk), 0) <= lax.broadcasted_iota(jnp.int32, (blk, blk), 1),
                       0.0, NEG_BIG)

    def own_mask(h):
        return bias_ref[h, 0] + causal

    group = MOBA_STEP_BLOCKS

    def past_group(i, carry):
        n = group * i
        attend(pl.multiple_of(n * blk, group * blk), group * blk,
               lambda h: jnp.concatenate([past_mask(h, n + u) for u in range(group)], axis=0))
        return carry

    lax.fori_loop(0, qb // group, past_group, 0)

    for k in range(group):
        @pl.when(qb % group == k)
        def _(k=k):
            attend(pl.multiple_of((qb - k) * blk, blk), (k + 1) * blk,
                   lambda h: jnp.concatenate([past_mask(h, qb - k + u) for u in range(k)] + [own_mask(h)], axis=0))

    for h in range(nh):
        o_ref[:, h * dh:(h + 1) * dh] = (acc_ref[h] / l_ref[h]).T.astype(o_ref.dtype)


def _moba(proj_b, bias_tiles, batch, seq):
    blk = MOBA_BLOCK
    nb = seq // blk
    n_sel = min(MOBA_TOPK, nb - 1)
    n_bias = bias_tiles.shape[1]
    return pl.pallas_call(
        functools.partial(_moba_kernel, n_blocks=nb, n_sel=n_sel, n_bias=n_bias),
        out_shape=jax.ShapeDtypeStruct((batch * seq, HEAD_W), BF16),
        grid=(batch, nb),
        in_specs=[
            pl.BlockSpec((blk, HEAD_W), lambda b, i: (b * nb + i, BCOL_MQ // HEAD_W)),
            pl.BlockSpec((seq, HEAD_W), lambda b, i: (b, BCOL_MK // HEAD_W)),
            pl.BlockSpec((seq, HEAD_W), lambda b, i: (b, BCOL_MV // HEAD_W)),
            pl.BlockSpec(bias_tiles.shape, lambda b, i: (0, 0, 0, 0)),
        ],
        out_specs=pl.BlockSpec((blk, HEAD_W), lambda b, i: (b * nb + i, 0)),
        scratch_shapes=[
            pltpu.VMEM((HEAD_W, seq), BF16),
            pltpu.VMEM((MOBA_HEADS, nb, MOBA_DH), F32),
            pltpu.VMEM((MOBA_HEADS, nb, blk), F32),
            pltpu.VMEM((MOBA_HEADS, 1, blk), F32),
            pltpu.VMEM((MOBA_HEADS, 1, blk), F32),
            pltpu.VMEM((MOBA_HEADS, MOBA_DH, blk), F32),
        ],
        compiler_params=_cparams(("parallel", "arbitrary")),
        name="moba",
    )(proj_b, proj_b, proj_b, bias_tiles)


def _sort_key(x):
    bits = pltpu.bitcast(x, jnp.int32)
    return bits ^ ((bits >> 31) & 0x7FFFFFFF)


def _dsa_kernel(dq_ref, iq_ref, iwq_ref, dkv_ref, ikw_ref, kvg_ref, lnw_ref, lnb_ref, wuk_ref, wuv_ref, bias_ref,
                o_ref, ckv_ref, ckvt_ref, ikn_ref, key_ref, mask_ref, tau_ref, cge_ref, m_ref, l_ref, acc_ref,
                *, seq, n_keep, n_bias):
    tq, kc, nh, dh = DSA_QB, DSA_KC, DSA_HEADS, DSA_DH
    sub = kc // LANES
    qb = pl.program_id(1)
    t0 = qb * tq
    n_kc = (t0 + tq - 1) // kc + 1
    n_rest = (seq - n_kc * kc).astype(F32)

    @pl.when(qb == 0)
    def _():
        for c in range(seq // kc):
            rows = slice(c * kc, (c + 1) * kc)
            x = dkv_ref[rows, :]
            cn = x * lax.rsqrt(jnp.mean(x * x, axis=-1, keepdims=True) + RMS_EPS) * kvg_ref[...]
            ckv_ref[rows, :] = cn.astype(BF16)
            ckvt_ref[:, rows] = cn.T.astype(BF16)
            ik = ikw_ref[rows, 0:IDX_DIM]
            xc = ik - jnp.mean(ik, axis=-1, keepdims=True)
            ikn = xc * lax.rsqrt(jnp.mean(xc * xc, axis=-1, keepdims=True) + RMS_EPS) * lnw_ref[...] + lnb_ref[...]
            ikn_ref[rows, :] = ikn.astype(BF16)

    t_row = t0 + lax.broadcasted_iota(jnp.int32, (1, tq), 1)
    key_off = lax.broadcasted_iota(jnp.int32, (kc, tq), 0)

    def causal_of(c0):
        return (c0 + key_off) <= t_row

    iq_t = iq_ref[...].astype(F32).T
    iq_cat = jnp.concatenate([iq_t[h * IDX_DIM:(h + 1) * IDX_DIM, :] for h in range(IDX_HEADS)], axis=1).astype(BF16)
    iw_t = iwq_ref[...].T[IDX_DIM:IDX_DIM + IDX_HEADS, :] * (IDX_HEADS ** -0.5)
    iw_t = iw_t * (IDX_DIM ** -0.5)

    hpg = 2 * LANES // tq

    def score_chunks(c, nc):
        c0s = [pl.multiple_of((c + u) * kc, kc) for u in range(nc)]
        s_grp = [[_dot(ikn_ref[pl.ds(c0, kc), :], iq_cat[:, g * hpg * tq:(g + 1) * hpg * tq])
                  for g in range(IDX_HEADS // hpg)] for c0 in c0s]
        for u, c0 in enumerate(c0s):
            sc = jnp.zeros((kc, tq), F32)
            for h in range(IDX_HEADS):
                s_h = s_grp[u][h // hpg][:, (h % hpg) * tq:(h % hpg + 1) * tq]
                sc = sc + iw_t[h:h + 1, :] * jnp.maximum(s_h, 0.0)
            sc = jnp.where(causal_of(c0), sc, NEG_BIG)
            sc = jnp.where(sc == 0.0, 0.0, sc)
            key_ref[pl.ds(c0, kc), :] = _sort_key(sc)

    def score_pair(i, carry):
        score_chunks(2 * i, 2)
        return carry

    lax.fori_loop(0, n_kc // 2, score_pair, 0)

    @pl.when(n_kc % 2 == 1)
    def _():
        score_chunks(n_kc - 1, 1)

    def count(pred_fn):
        def body(c, tot):
            c0 = pl.multiple_of(c * kc, kc)
            hit = pred_fn(key_ref[pl.ds(c0, kc), :], c0)
            return tot + _slab(jnp.where(hit, 1.0, 0.0), jnp.sum, COUNT_SLAB)
        part = lax.fori_loop(0, n_kc, body, jnp.zeros((COUNT_SLAB, tq), F32))
        return _reduce_rows(part, jnp.sum, COUNT_SLAB)

    def bit_step(k, carry):
        base, c_base = carry
        cand = base + jnp.left_shift(jnp.int32(1), 31 - k)
        cnt = count(lambda keys, c0: keys >= cand) + jnp.where(cand <= KEY_NEG, n_rest, 0.0)
        take = cnt >= n_keep
        return jnp.where(take, cand, base), jnp.where(take, cnt, c_base)

    tau, c_ge = lax.fori_loop(0, DSA_SURE_BITS, bit_step, (jnp.full((1, tq), jnp.iinfo(jnp.int32).min, jnp.int32),
                                                           jnp.full((1, tq), float(seq), F32)))
    tau_ref[...] = tau
    cge_ref[...] = c_ge

    @pl.when(jnp.max(jnp.abs(c_ge - n_keep)) > 0.0)
    def _():
        tau_ref[...], cge_ref[...] = lax.fori_loop(DSA_SURE_BITS, 32, bit_step, (tau, c_ge))

    tau = tau_ref[...]
    c_ge = cge_ref[...]

    c_gt = count(lambda keys, c0: keys > tau) + jnp.where(tau < KEY_NEG, n_rest, 0.0)
    need = n_keep - c_gt
    n_eq = (c_ge - c_gt) - jnp.where(tau == KEY_NEG, (seq - 1 - t_row).astype(F32), 0.0)
    tie_break = jnp.max(n_eq - need) > 0.0

    @pl.when(jnp.logical_not(tie_break))
    def _():
        def body(c, carry):
            c0 = pl.multiple_of(c * kc, kc)
            keep = (key_ref[pl.ds(c0, kc), :] >= tau) & causal_of(c0)
            mask_ref[pl.ds(c0, kc), :] = jnp.where(keep, 0.0, NEG_BIG)
            return carry
        lax.fori_loop(0, n_kc, body, 0)

    @pl.when(tie_break)
    def _():
        lower = jnp.where(lax.broadcasted_iota(jnp.int32, (kc, kc), 1)
                          < lax.broadcasted_iota(jnp.int32, (kc, kc), 0), 1.0, 0.0).astype(BF16)

        def body(c, seen):
            c0 = pl.multiple_of(c * kc, kc)
            keys = key_ref[pl.ds(c0, kc), :]
            eq = keys == tau
            eqf = jnp.where(eq, 1.0, 0.0)
            rank = seen + _dot(lower, eqf.astype(BF16))
            keep = ((keys > tau) | (eq & (rank < need))) & causal_of(c0)
            mask_ref[pl.ds(c0, kc), :] = jnp.where(keep, 0.0, NEG_BIG)
            return seen + _reduce_rows(eqf, jnp.sum)
        lax.fori_loop(0, n_kc, body, jnp.zeros((1, tq), F32))

    scale2 = dh ** -0.5 * LOG2E
    qa_cat = jnp.concatenate(
        [_dot_nt(wuk_ref[h], dq_ref[:, h * dh:(h + 1) * dh]) for h in range(nh)], axis=1).astype(BF16)
    m_ref[...] = jnp.full(m_ref.shape, -jnp.inf, F32)
    l_ref[...] = jnp.zeros(l_ref.shape, F32)
    acc_ref[...] = jnp.zeros(acc_ref.shape, F32)

    def attend(c, nc):
        c0 = pl.multiple_of(c * kc, kc)
        nk = nc * kc
        msk = mask_ref[pl.ds(c0, nk), :]
        bias = jnp.concatenate(
            [bias_ref[jnp.clip(qb * (tq // LANES) - (c * sub + g) + 1, 0, n_bias - 1)] for g in range(nc * sub)],
            axis=0)
        s = _dot(ckv_ref[pl.ds(c0, nk), :], qa_cat) * scale2 + bias + jnp.concatenate([msk] * nh, axis=1)
        m_old = m_ref[...]
        m_new = jnp.maximum(m_old, _reduce_rows(s, jnp.max))
        alpha = jnp.exp2(m_old - m_new)
        p = jnp.exp2(s - m_new)
        l_ref[...] = alpha * l_ref[...] + _reduce_rows(p, jnp.sum)
        acc_ref[...] = alpha * acc_ref[...] + _dot(ckvt_ref[:, pl.ds(c0, nk)], p.astype(BF16))
        m_ref[...] = m_new

    def att_pair(i, carry):
        attend(2 * i, 2)
        return carry

    lax.fori_loop(0, n_kc // 2, att_pair, 0)

    @pl.when(n_kc % 2 == 1)
    def _():
        attend(n_kc - 1, 1)

    o_lat_t = (acc_ref[...] / l_ref[...]).astype(BF16)
    for h in range(nh):
        o_ref[:, h * dh:(h + 1) * dh] = lax.dot_general(
            o_lat_t[:, h * tq:(h + 1) * tq], wuv_ref[h], (((0,), (0,)), ((), ())),
            preferred_element_type=F32).astype(o_ref.dtype)


def _dsa(proj_f, proj_b, kv_gain, ln_w, ln_b, w_uk_h, w_uv_h, bias_tiles, batch, seq):
    tq = DSA_QB
    nq = seq // tq
    n_keep = min(DSA_TOPK_MAX, seq // 4)
    n_bias = bias_tiles.shape[0]
    return pl.pallas_call(
        functools.partial(_dsa_kernel, seq=seq, n_keep=float(n_keep), n_bias=n_bias),
        out_shape=jax.ShapeDtypeStruct((batch * seq, HEAD_W), BF16),
        grid=(batch, nq),
        in_specs=[
            pl.BlockSpec((tq, HEAD_W), lambda b, i: (b * nq + i, BCOL_DQ // HEAD_W)),
            pl.BlockSpec((tq, HEAD_W), lambda b, i: (b * nq + i, BCOL_IQ // HEAD_W)),
            pl.BlockSpec((tq, LANES), lambda b, i: (b * nq + i, FCOL_IKW // LANES)),
            pl.BlockSpec((seq, DSA_KV_RANK), lambda b, i: (b, FCOL_DKV // DSA_KV_RANK)),
            pl.BlockSpec((seq, LANES), lambda b, i: (b, FCOL_IKW // LANES)),
            pl.BlockSpec((1, DSA_KV_RANK), lambda b, i: (0, 0)),
            pl.BlockSpec((1, IDX_DIM), lambda b, i: (0, 0)),
            pl.BlockSpec((1, IDX_DIM), lambda b, i: (0, 0)),
            pl.BlockSpec(w_uk_h.shape, lambda b, i: (0, 0, 0)),
            pl.BlockSpec(w_uv_h.shape, lambda b, i: (0, 0, 0)),
            pl.BlockSpec(bias_tiles.shape, lambda b, i: (0, 0, 0)),
        ],
        out_specs=pl.BlockSpec((tq, HEAD_W), lambda b, i: (b * nq + i, 0)),
        scratch_shapes=[
            pltpu.VMEM((seq, DSA_KV_RANK), BF16),
            pltpu.VMEM((DSA_KV_RANK, seq), BF16),
            pltpu.VMEM((seq, IDX_DIM), BF16),
            pltpu.VMEM((seq, tq), jnp.int32),
            pltpu.VMEM((seq, tq), F32),
            pltpu.VMEM((1, tq), jnp.int32),
            pltpu.VMEM((1, tq), F32),
            pltpu.VMEM((1, DSA_HEADS * tq), F32),
            pltpu.VMEM((1, DSA_HEADS * tq), F32),
            pltpu.VMEM((DSA_KV_RANK, DSA_HEADS * tq), F32),
        ],
        compiler_params=_cparams(("parallel", "arbitrary")),
        name="dsa",
    )(proj_b, proj_b, proj_f, proj_f, proj_f, kv_gain, ln_w, ln_b, w_uk_h, w_uv_h, bias_tiles)


def _merge_kernel(x_ref, ya_ref, yb_ref, yc_ref, g0_ref, g1_ref, g2_ref, wa_ref, wb_ref, wc_ref, wo_ref, gain_ref,
                  o_ref):
    merged = (_sigmoid(g0_ref[...]) * _dot(ya_ref[...], wa_ref[...])
              + _sigmoid(g1_ref[...]) * _dot(yb_ref[...], wb_ref[...])
              + _sigmoid(g2_ref[...]) * _dot(yc_ref[...], wc_ref[...]))
    z = _dot(merged.astype(BF16), wo_ref[...])
    zn = z * lax.rsqrt(jnp.mean(z * z, axis=-1, keepdims=True) + RMS_EPS) * gain_ref[...]
    o_ref[...] = x_ref[...] + zn


def _merge(x2, ya, yb, yc, proj_f, wa, wb, wc, wo, layer, gain, tm):
    n = x2.shape[0]
    gcol = FCOL_GATES // D_MODEL

    def rows(w):
        return pl.BlockSpec((tm, w), lambda i: (i, 0))

    def gate(k):
        return pl.BlockSpec((tm, D_MODEL), lambda i, k=k: (i, gcol + k))

    def weight(a):
        return pl.BlockSpec((None,) + a.shape[1:], lambda i: (layer, 0, 0))

    return pl.pallas_call(
        _merge_kernel,
        out_shape=jax.ShapeDtypeStruct((n, D_MODEL), F32),
        grid=(n // tm,),
        in_specs=[rows(D_MODEL), rows(HEAD_W), rows(HEAD_W), rows(HEAD_W), gate(0), gate(1), gate(2),
                  weight(wa), weight(wb), weight(wc), weight(wo), pl.BlockSpec(gain.shape, lambda i: (0, 0))],
        out_specs=rows(D_MODEL),
        compiler_params=_cparams(("parallel",)),
        name="merge",
    )(x2, ya, yb, yc, proj_f, proj_f, proj_f, wa, wb, wc, wo, gain)


def _ffn_kernel(x_ref, g1_ref, wg_ref, wu_ref, wd_ref, g2_ref, o_ref, hn_ref, acc_ref):
    j = pl.program_id(1)

    @pl.when(j == 0)
    def _():
        x = x_ref[...]
        hn_ref[...] = (x * lax.rsqrt(jnp.mean(x * x, axis=-1, keepdims=True) + RMS_EPS) * g1_ref[...]).astype(BF16)
        acc_ref[...] = jnp.zeros_like(acc_ref)

    hn = hn_ref[...]
    gate = _dot(hn, wg_ref[...])
    up = _dot(hn, wu_ref[...])
    act = (gate * _sigmoid(gate)) * up
    acc_ref[...] += _dot(act.astype(BF16), wd_ref[...])

    @pl.when(j == pl.num_programs(1) - 1)
    def _():
        y = acc_ref[...]
        yn = y * lax.rsqrt(jnp.mean(y * y, axis=-1, keepdims=True) + RMS_EPS) * g2_ref[...]
        o_ref[...] = x_ref[...] + yn


def _ffn(x2, g1, w_in, w_out, layer, g2, tm, tf):
    n = x2.shape[0]
    nf = D_FF // tf
    return pl.pallas_call(
        _ffn_kernel,
        out_shape=jax.ShapeDtypeStruct((n, D_MODEL), F32),
        grid=(n // tm, nf),
        in_specs=[
            pl.BlockSpec((tm, D_MODEL), lambda i, j: (i, 0)),
            pl.BlockSpec((1, D_MODEL), lambda i, j: (0, 0)),
            pl.BlockSpec((None, D_MODEL, tf), lambda i, j: (layer, 0, j)),
            pl.BlockSpec((None, D_MODEL, tf), lambda i, j: (layer, 0, nf + j)),
            pl.BlockSpec((None, tf, D_MODEL), lambda i, j: (layer, j, 0)),
            pl.BlockSpec((1, D_MODEL), lambda i, j: (0, 0)),
        ],
        out_specs=pl.BlockSpec((tm, D_MODEL), lambda i, j: (i, 0)),
        scratch_shapes=[pltpu.VMEM((tm, D_MODEL), BF16), pltpu.VMEM((tm, D_MODEL), F32)],
        compiler_params=_cparams(("parallel", "arbitrary")),
        name="ffn",
    )(x2, g1, w_in, w_in, w_out, g2)


def _pack_w_in(w):
    starts = dict(zip(IN_NAMES, np.cumsum((0,) + IN_SIZES[:-1]).tolist()))
    o_mq, o_dq, o_dkv, o_iq, o_ik, o_gates = (starts[k] for k in ("mq", "dq", "dkv", "iq", "ik", "gates"))
    assert w.shape[-1] == sum(IN_SIZES)
    w = w.astype(BF16)
    z = lambda k: jnp.zeros(w.shape[:-1] + (k,), w.dtype)
    w_f = jnp.concatenate([
        w[..., :o_mq],
        w[..., o_gates:],
        w[..., o_dkv:o_iq],
        w[..., o_ik:o_gates],
        z(PROJ_F_W - FCOL_IKW - (o_gates - o_ik)),
    ], axis=-1)
    w_b = jnp.concatenate([
        w[..., o_mq:o_dq],
        w[..., o_dq:o_dkv],
        w[..., o_iq:o_ik],
    ], axis=-1)
    assert w_f.shape[-1] == PROJ_F_W and w_b.shape[-1] == PROJ_B_W
    return w_f, w_b


def _tiles(batch, seq):
    n = batch * seq
    return dict(
        proj_tm=min(2048, n), proj_f_tn=PROJ_F_W // 4, proj_b_tm=min(2048, n), proj_b_tn=PROJ_B_W,
        hgrn_rows=min(512, seq),
        merge_tm=min(512, n),
        ffn_tm=min(1024, n), ffn_tf=D_FF // 2,
    )


def kernel(x, w_in, pre_mix_norm, post_mix_norm, pre_ffn_norm, post_ffn_norm, hg_lb_logits, hg_norm_w, dsa_kv_norm,
           dsa_w_uk, dsa_w_uv, idx_k_norm_w, idx_k_norm_b, rel_bias, w_branch_a, w_branch_b, w_branch_c, w_out,
           w_ffn_in, w_ffn_out):
    batch, seq, _ = x.shape
    depth = w_in.shape[0]
    t = _tiles(batch, seq)
    lbp = jax.nn.softmax(hg_lb_logits.astype(F32), axis=0)
    lower_bounds = jnp.cumsum(lbp, axis=0) - lbp[0]
    bias_b = _bias_tiles_t(rel_bias[:, :MOBA_HEADS], MOBA_BLOCK)
    bias_c = _bias_tiles_rect(rel_bias[:, MOBA_HEADS:], LANES, DSA_QB)
    row = lambda a: a.reshape(1, -1).astype(F32)

    w_f, w_b = _pack_w_in(w_in)
    w_uk = jnp.transpose(dsa_w_uk, (0, 2, 1, 3)).astype(BF16)
    w_uv = jnp.transpose(dsa_w_uv, (0, 2, 1, 3)).astype(BF16)
    w_a, w_bb, w_c, w_o = (a.astype(BF16) for a in (w_branch_a, w_branch_b, w_branch_c, w_out))
    w_fi, w_fo = w_ffn_in.astype(BF16), w_ffn_out.astype(BF16)

    x2 = x.reshape(batch * seq, D_MODEL)
    for l in range(depth):
        gain = row(pre_mix_norm[l])
        proj_f = _proj(x2, gain, w_f, l, F32, t["proj_tm"], t["proj_f_tn"], "proj_f")
        proj_b = _proj(x2, gain, w_b, l, BF16, t["proj_b_tm"], t["proj_b_tn"], "proj_b")
        y_a = _hgrn(proj_f, row(lower_bounds[l]), row(hg_norm_w[l]), batch, seq, t["hgrn_rows"])
        y_b = _moba(proj_b, bias_b, batch, seq)
        y_c = _dsa(proj_f, proj_b, row(dsa_kv_norm[l]), row(idx_k_norm_w[l]), row(idx_k_norm_b[l]),
                   w_uk[l], w_uv[l], bias_c, batch, seq)
        x2 = _merge(x2, y_a, y_b, y_c, proj_f, w_a, w_bb, w_c, w_o, l, row(post_mix_norm[l]), t["merge_tm"])
        x2 = _ffn(x2, row(pre_ffn_norm[l]), w_fi, w_fo, l, row(post_ffn_norm[l]), t["ffn_tm"], t["ffn_tf"])
    return x2.reshape(batch, seq, D_MODEL)
```
